```python
import numpy as np
import jax, jax.numpy as jnp
from jax import lax

D_MODEL = 1024
BATCH = 2
SEQ = 8192
DEPTH = 2
DEC_BATCH = 32
DEC_SEQ = 8
PAST_LEN = 16384
PAGE_SIZE = 128

DSA_HEAD_DIM = 64
DSA_HEADS = (D_MODEL // 4) // DSA_HEAD_DIM
DSA_PATTERNS = ((128, 1), (512, 4), (2048, 16))
DSA_MAX_WINDOW = 2048
ROPE_THETA = 10000.0
Q_BLOCK = 128
RET_HEAD_DIM = 64
RET_HEADS = (D_MODEL // 4) // RET_HEAD_DIM
GDN_HEAD_DIM = 128
GDN_HEADS = (D_MODEL // 2) // GDN_HEAD_DIM
CONV_WIDTH = 4
CHUNK = 64
DSA_W = DSA_HEADS * DSA_HEAD_DIM
RET_W = RET_HEADS * RET_HEAD_DIM
GDN_W = GDN_HEADS * GDN_HEAD_DIM
MIX_W = DSA_W + RET_W + GDN_W
CONV_DIM = 3 * GDN_W
IN_SIZES = (DSA_W, DSA_W, DSA_W, RET_W, RET_W, RET_W, RET_W, CONV_DIM, GDN_W, GDN_HEADS, GDN_HEADS)
IN_COLS = sum(IN_SIZES)
FFN_HIDDEN = ((-(-8 * D_MODEL // 3)) + 255) // 256 * 256
NORM_EPS = 1e-6

kernel_name = 'hybrid_dilated_retention_gdn_adaln_step'

F32 = jnp.float32


def rmsnorm(x, g):
    xf = x.astype(F32)
    y = xf * lax.rsqrt(jnp.mean(xf * xf, axis=-1, keepdims=True) + NORM_EPS)
    return (y * g.astype(F32)).astype(x.dtype)


def l2norm(x):
    xf = x.astype(F32)
    return xf * lax.rsqrt(jnp.sum(xf * xf, axis=-1, keepdims=True) + NORM_EPS)


def rope(x, pos, inv_freq):
    ang = pos.astype(F32)[:, None] * inv_freq[None, :]
    cos = jnp.cos(ang)[None, :, None, :]
    sin = jnp.sin(ang)[None, :, None, :]
    x1, x2 = jnp.split(x.astype(F32), 2, axis=-1)
    return jnp.concatenate([x1 * cos - x2 * sin, x1 * sin + x2 * cos], axis=-1).astype(x.dtype)


def dilated_attention(q, k_buf, v_buf, q_off):
    nbat, tq, nh, hd = q.shape
    qb = Q_BLOCK if tq % Q_BLOCK == 0 else tq
    nblk = tq // qb
    qs = q.reshape(nbat, nblk, qb, nh, hd).transpose(1, 0, 2, 3, 4)

    def block(args):
        qblk, b = args
        qf = qblk.astype(F32)
        rows = q_off + b * qb + jnp.arange(qb)
        lses, outs = [], []
        for window, dil in DSA_PATTERNS:
            idx = rows[:, None] - dil * jnp.arange(window // dil + 1)[None, :]
            valid = idx >= 0
            idx = jnp.maximum(idx, 0)
            kg = k_buf[:, idx].astype(F32)
            vg = v_buf[:, idx].astype(F32)
            s = jnp.einsum('bqhd,bqkhd->bhqk', qf, kg)
            s = jnp.where(valid[None, None], s, -jnp.inf)
            m = jnp.max(s, axis=-1, keepdims=True)
            p = jnp.exp(s - m)
            den = jnp.sum(p, axis=-1)
            o = jnp.einsum('bhqk,bqkhd->bqhd', p / den[..., None], vg)
            lses.append(jnp.transpose(m[..., 0] + jnp.log(den), (0, 2, 1)))
            outs.append(o)
        alpha = jax.nn.softmax(jnp.stack(lses, axis=0), axis=0)
        return jnp.sum(alpha[..., None] * jnp.stack(outs, axis=0), axis=0).astype(q.dtype)

    out = lax.map(block, (qs, jnp.arange(nblk)))
    return out.transpose(1, 0, 2, 3, 4).reshape(nbat, tq, nh, hd)


def retention(q, k, v, state, log_gamma):
    nbat, t, nh, dk = q.shape
    dv = v.shape[-1]
    cs = CHUNK if t % CHUNK == 0 else t
    n = t // cs
    chunks = lambda a: a.astype(F32).reshape(nbat, n, cs, nh, a.shape[-1]).transpose(1, 0, 3, 2, 4)
    qc, kc, vc = chunks(q), chunks(k), chunks(v)
    i = jnp.arange(cs, dtype=F32)
    diff = i[:, None] - i[None, :]
    causal = diff >= 0
    decay = jnp.where(causal[None], jnp.exp(log_gamma[:, None, None] * jnp.where(causal, diff, 0.0)[None]), 0.0)
    q_dec = jnp.exp(log_gamma[:, None] * (i[None, :] + 1.0))[None, :, :, None]
    k_dec = jnp.exp(log_gamma[:, None] * (cs - 1.0 - i)[None, :])[None, :, :, None]
    chunk_dec = jnp.exp(log_gamma * cs)[None, :, None, None]

    def step(s_prev, inp):
        qi, ki, vi = inp
        inner = jnp.einsum('bhqd,bhkd->bhqk', qi, ki) * decay[None]
        o = jnp.einsum('bhqk,bhkv->bhqv', inner, vi) + jnp.einsum('bhqd,bhdv->bhqv', qi, s_prev) * q_dec
        s_new = s_prev * chunk_dec + jnp.einsum('bhkd,bhkv->bhdv', ki * k_dec, vi)
        return s_new, o

    s_fin, o = lax.scan(step, state.astype(F32), (qc, kc, vc))
    o = o.transpose(1, 0, 3, 2, 4).reshape(nbat, t, nh, dv)
    return o.astype(q.dtype), s_fin.astype(state.dtype)


def gated_delta(q, k, v, g, beta, state):
    nbat, t, nh, dk = q.shape
    dv = v.shape[-1]
    cs = CHUNK if t % CHUNK == 0 else t
    n = t // cs
    chunks = lambda a: a.astype(F32).reshape(nbat, n, cs, nh, a.shape[-1]).transpose(1, 0, 3, 2, 4)
    qc = chunks(q) * (dk ** -0.5)
    kc, vc = chunks(k), chunks(v)
    bc = chunks(beta[..., None])[..., 0]
    gc = jnp.cumsum(chunks(g[..., None])[..., 0], axis=-1)
    ar = jnp.arange(cs)
    incl = ar[:, None] >= ar[None, :]
    strict = ar[:, None] > ar[None, :]
    eye = jnp.eye(cs, dtype=F32)

    def step(s_prev, inp):
        qi, ki, vi, bi, gi = inp
        gdiff = gi[..., :, None] - gi[..., None, :]
        dmask = jnp.where(incl, jnp.exp(jnp.where(incl, gdiff, 0.0)), 0.0)
        kb = ki * bi[..., None]
        lower = jnp.where(strict, jnp.einsum('bhid,bhjd->bhij', kb, ki) * dmask, 0.0)
        rhs = jnp.concatenate([vi * bi[..., None], kb * jnp.exp(gi)[..., None]], axis=-1)
        sol = lax.linalg.triangular_solve(eye + lower, rhs, left_side=True, lower=True, unit_diagonal=True)
        u, w = sol[..., :dv], sol[..., dv:]
        v_new = u - jnp.einsum('bhik,bhkv->bhiv', w, s_prev)
        attn = jnp.einsum('bhid,bhjd->bhij', qi, ki) * dmask
        o = jnp.einsum('bhid,bhdv->bhiv', qi * jnp.exp(gi)[..., None], s_prev) + jnp.einsum('bhij,bhjv->bhiv', attn, v_new)
        g_last = gi[..., -1]
        s_new = s_prev * jnp.exp(g_last)[..., None, None] + jnp.einsum(
            'bhjd,bhjv->bhdv', ki * jnp.exp(g_last[..., None] - gi)[..., None], v_new)
        return s_new, o

    s_fin, o = lax.scan(step, state.astype(F32), (qc, kc, vc, bc, gc))
    o = o.transpose(1, 0, 3, 2, 4).reshape(nbat, t, nh, dv)
    return o.astype(v.dtype), s_fin.astype(state.dtype)


def causal_conv(x, hist, w):
    t = x.shape[1]
    xp = jnp.concatenate([hist, x], axis=1)
    y = xp[:, 0:t] * w[0]
    for i in range(1, CONV_WIDTH):
        y = y + xp[:, i:i + t] * w[i]
    return jax.nn.silu(y), xp[:, xp.shape[1] - (CONV_WIDTH - 1):]


def mixer(h, pos, k_hist, v_hist, s_ret, s_gdn, conv_hist,
          w_in, ret_norm, conv_w, a_log, dt_bias, gdn_norm, w_out):
    nbat, t, _ = h.shape
    heads = lambda a, nh, d: a.reshape(nbat, t, nh, d)
    qa, ka, va, qr, kr, vr, gr, qkv_c, zc, beta_raw, a_raw = jnp.split(
        h @ w_in, np.cumsum(IN_SIZES)[:-1].tolist(), axis=-1)
    inv_a = 1.0 / (ROPE_THETA ** (jnp.arange(0, DSA_HEAD_DIM, 2, dtype=F32) / DSA_HEAD_DIM))
    qa = rope(heads(qa, DSA_HEADS, DSA_HEAD_DIM), pos, inv_a) * (DSA_HEAD_DIM ** -0.5)
    ka = rope(heads(ka, DSA_HEADS, DSA_HEAD_DIM), pos, inv_a)
    va = heads(va, DSA_HEADS, DSA_HEAD_DIM)
    oa = dilated_attention(qa, jnp.concatenate([k_hist, ka], axis=1),
                           jnp.concatenate([v_hist, va], axis=1), k_hist.shape[1])
    inv_r = 1.0 / (10000.0 ** jnp.linspace(0.0, 1.0, RET_HEAD_DIM // 2, dtype=F32))
    log_gamma = jnp.log(1.0 - 2.0 ** (-5.0 - jnp.arange(RET_HEADS, dtype=F32)))
    qr = rope(heads(qr, RET_HEADS, RET_HEAD_DIM), pos, inv_r)
    kr = rope(heads(kr, RET_HEADS, RET_HEAD_DIM), pos, inv_r) * (RET_HEAD_DIM ** -0.5)
    orr, s_ret_new = retention(qr, kr, heads(vr, RET_HEADS, RET_HEAD_DIM), s_ret, log_gamma)
    orr = rmsnorm(orr, ret_norm) * jax.nn.silu(heads(gr, RET_HEADS, RET_HEAD_DIM))
    qkv_c, conv_new = causal_conv(qkv_c, conv_hist, conv_w)
    qc, kc, vc = jnp.split(qkv_c, 3, axis=-1)
    qc = l2norm(heads(qc, GDN_HEADS, GDN_HEAD_DIM))
    kc = l2norm(heads(kc, GDN_HEADS, GDN_HEAD_DIM))
    vc = heads(vc, GDN_HEADS, GDN_HEAD_DIM)
    beta = jax.nn.sigmoid(beta_raw.astype(F32))
    g = -jnp.exp(a_log.astype(F32)) * jax.nn.softplus(a_raw.astype(F32) + dt_bias.astype(F32))
    oc, s_gdn_new = gated_delta(qc, kc, vc, g, beta, s_gdn)
    oc = rmsnorm(oc, gdn_norm) * jax.nn.silu(heads(zc, GDN_HEADS, GDN_HEAD_DIM))
    o = jnp.concatenate([oa.reshape(nbat, t, DSA_W), orr.reshape(nbat, t, RET_W),
                         oc.reshape(nbat, t, GDN_W)], axis=-1)
    return o @ w_out, ka, va, s_ret_new, s_gdn_new, conv_new


def trunk(x, c, pos, k_hist, v_hist, s_ret, s_gdn, conv_hist,
          ada_w, ada_b, norm_mix, norm_ffn, w_in, ret_norm, conv_w, a_log, dt_bias,
          gdn_norm, w_out, w_gate, w_up, w_down, final_norm):
    ks, vs, rs, gs, cs = [], [], [], [], []
    for l in range(DEPTH):
        mod = jax.nn.silu(c) @ ada_w[l] + ada_b[l]
        sh1, sc1, g1, sh2, sc2, g2 = [m[:, None, :] for m in jnp.split(mod, 6, axis=-1)]
        h = rmsnorm(x, norm_mix[l]) * (1 + sc1) + sh1
        o, ka, va, sr, sg, cv = mixer(h, pos, k_hist[l], v_hist[l], s_ret[l], s_gdn[l], conv_hist[l],
                                      w_in[l], ret_norm[l], conv_w[l], a_log[l], dt_bias[l],
                                      gdn_norm[l], w_out[l])
        x = x + g1 * o
        h = rmsnorm(x, norm_ffn[l]) * (1 + sc2) + sh2
        x = x + g2 * ((jax.nn.silu(h @ w_gate[l]) * (h @ w_up[l])) @ w_down[l])
        ks.append(ka); vs.append(va); rs.append(sr); gs.append(sg); cs.append(cv)
    return (rmsnorm(x, final_norm), jnp.stack(ks), jnp.stack(vs), jnp.stack(rs),
            jnp.stack(gs), jnp.stack(cs))


def setup_inputs(seed: int = 0) -> dict:
    key = jax.random.key(seed)
    ks = jax.random.split(key, 26)
    nrm = lambda k, shape, s: jax.random.normal(k, shape, F32) * s
    wb = min(DSA_MAX_WINDOW, PAST_LEN)
    dt = jnp.exp(jax.random.uniform(ks[20], (DEPTH, GDN_HEADS), F32, np.log(1e-3), np.log(1e-1)))
    return {
        'x_prompt': nrm(ks[0], (BATCH, SEQ, D_MODEL), 1.0),
        'x_sample': nrm(ks[1], (DEC_BATCH, DEC_SEQ, D_MODEL), 1.0),
        'cache_win_k': nrm(ks[2], (DEPTH, DEC_BATCH, wb, DSA_HEADS, DSA_HEAD_DIM), 1.0),
        'cache_win_v': nrm(ks[3], (DEPTH, DEC_BATCH, wb, DSA_HEADS, DSA_HEAD_DIM), 1.0),
        'state_ret': nrm(ks[4], (DEPTH, DEC_BATCH, RET_HEADS, RET_HEAD_DIM, RET_HEAD_DIM), 0.5),
        'state_gdn': nrm(ks[5], (DEPTH, DEC_BATCH, GDN_HEADS, GDN_HEAD_DIM, GDN_HEAD_DIM), 0.2),
        'state_conv': nrm(ks[6], (DEPTH, DEC_BATCH, CONV_WIDTH - 1, CONV_DIM), 1.0),
        'c_prompt': nrm(ks[7], (BATCH, D_MODEL), 1.0),
        'c_sample': nrm(ks[8], (DEC_BATCH, D_MODEL), 1.0),
        'ada_w': nrm(ks[9], (DEPTH, D_MODEL, 6 * D_MODEL), 0.5 * D_MODEL ** -0.5),
        'ada_b': nrm(ks[10], (DEPTH, 6 * D_MODEL), 0.02),
        'norm_mix': 1.0 + nrm(ks[11], (DEPTH, D_MODEL), 0.05),
        'norm_ffn': 1.0 + nrm(ks[12], (DEPTH, D_MODEL), 0.05),
        'w_in': nrm(ks[13], (DEPTH, D_MODEL, IN_COLS), D_MODEL ** -0.5),
        'ret_norm': 1.0 + nrm(ks[14], (DEPTH, RET_HEAD_DIM), 0.05),
        'conv_w': nrm(ks[15], (DEPTH, CONV_WIDTH, CONV_DIM), CONV_WIDTH ** -0.5),
        'a_log': jnp.log(jax.random.uniform(ks[16], (DEPTH, GDN_HEADS), F32, 1.0, 16.0)),
        'dt_bias': dt + jnp.log(-jnp.expm1(-dt)),
        'gdn_norm': 1.0 + nrm(ks[17], (DEPTH, GDN_HEAD_DIM), 0.05),
        'w_out': nrm(ks[18], (DEPTH, MIX_W, D_MODEL), MIX_W ** -0.5),
        'w_gate': nrm(ks[19], (DEPTH, D_MODEL, FFN_HIDDEN), D_MODEL ** -0.5),
        'w_up': nrm(ks[21], (DEPTH, D_MODEL, FFN_HIDDEN), D_MODEL ** -0.5),
        'w_down': nrm(ks[22], (DEPTH, FFN_HIDDEN, D_MODEL), FFN_HIDDEN ** -0.5),
        'final_norm': 1.0 + nrm(ks[23], (D_MODEL,), 0.05),
    }


def reference(x_prompt, x_sample, cache_win_k, cache_win_v, state_ret, state_gdn, state_conv,
              c_prompt, c_sample, ada_w, ada_b, norm_mix, norm_ffn, w_in, ret_norm, conv_w,
              a_log, dt_bias, gdn_norm, w_out, w_gate, w_up, w_down, final_norm):
    weights = (ada_w, ada_b, norm_mix, norm_ffn, w_in, ret_norm, conv_w, a_log, dt_bias,
               gdn_norm, w_out, w_gate, w_up, w_down, final_norm)
    nb, t_p, _ = x_prompt.shape
    dtp = x_prompt.dtype
    zk = jnp.zeros((DEPTH, nb, 0, DSA_HEADS, DSA_HEAD_DIM), dtp)
    zr = jnp.zeros((DEPTH, nb, RET_HEADS, RET_HEAD_DIM, RET_HEAD_DIM), dtp)
    zg = jnp.zeros((DEPTH, nb, GDN_HEADS, GDN_HEAD_DIM, GDN_HEAD_DIM), dtp)
    zc = jnp.zeros((DEPTH, nb, CONV_WIDTH - 1, CONV_DIM), dtp)
    pos_p = jnp.arange(t_p, dtype=jnp.int32)
    y_prompt, kp, vp, rp, gp, cp = trunk(x_prompt, c_prompt, pos_p, zk, zk, zr, zg, zc, *weights)
    pos_s = PAST_LEN + jnp.arange(x_sample.shape[1], dtype=jnp.int32)
    y_sample, ks, vs, rs, gs, cs = trunk(x_sample, c_sample, pos_s, cache_win_k, cache_win_v,
                                         state_ret, state_gdn, state_conv, *weights)
    return (y_prompt, y_sample, kp[:, :, -DSA_MAX_WINDOW:], vp[:, :, -DSA_MAX_WINDOW:], rp, gp, cp,
            ks, vs, rs, gs, cs)
```

```python
import functools
import math

import jax
import jax.numpy as jnp
from jax import lax
from jax.experimental import pallas as pl
from jax.experimental.pallas import tpu as pltpu

F32 = jnp.float32
BF16 = jnp.bfloat16
HIGHEST = lax.Precision.HIGHEST

DSA_HEAD_DIM = 64
DSA_HEADS = 4
DSA_PATTERNS = ((128, 1), (512, 4), (2048, 16))
DSA_MAX_WINDOW = 2048
ROPE_THETA = 10000.0
RET_HEAD_DIM = 64
RET_HEADS = 4
GDN_HEAD_DIM = 128
GDN_HEADS = 4
CONV_WIDTH = 4
CHUNK = 64
NORM_EPS = 1e-6
PAST_LEN = 16384

DSA_W = DSA_HEADS * DSA_HEAD_DIM
RET_W = RET_HEADS * RET_HEAD_DIM
GDN_W = GDN_HEADS * GDN_HEAD_DIM
CONV_DIM = 3 * GDN_W
LANES = 128
SUBLANES = 8
VMEM_LIMIT = 56 * 1024 * 1024

_C_QA, _C_KA, _C_VA = 0, DSA_W, 2 * DSA_W
_C_QR = 3 * DSA_W
_C_KR, _C_VR, _C_GR = _C_QR + RET_W, _C_QR + 2 * RET_W, _C_QR + 3 * RET_W
_C_CV = _C_QR + 4 * RET_W
_C_Z = _C_CV + CONV_DIM
_C_BA = _C_Z + GDN_W
IN_COLS = _C_BA + 2 * GDN_HEADS
IN_COLS_PAD = _C_BA + LANES


def _cparams(*sem):
    return pltpu.CompilerParams(dimension_semantics=sem, vmem_limit_bytes=VMEM_LIMIT)


def _dot(a, b, **kw):
    return jnp.dot(a, b, preferred_element_type=F32, **kw)


def _dot_nt(a, b, **kw):
    return lax.dot_general(a, b, (((1,), (1,)), ((), ())), preferred_element_type=F32, **kw)


def _dot_tn(a, b, **kw):
    return lax.dot_general(a, b, (((0,), (0,)), ((), ())), preferred_element_type=F32, **kw)


def _silu(x):
    return x * jax.nn.sigmoid(x)


def _mod_kernel(c_ref, w_ref, b_ref, o_ref):
    a = _silu(c_ref[...]).astype(BF16)
    o_ref[...] = _dot(a, w_ref[...].astype(BF16)) + b_ref[...]


def _modulation(c_all, ada_w, ada_b, tn=1536):
    depth, d, n = ada_w.shape
    bp = c_all.shape[0]
    return pl.pallas_call(
        _mod_kernel,
        grid=(depth, n // tn),
        in_specs=[
            pl.BlockSpec((bp, d), lambda l, j: (0, 0)),
            pl.BlockSpec((None, d, tn), lambda l, j: (l, 0, j)),
            pl.BlockSpec((None, 1, tn), lambda l, j: (l, 0, j)),
        ],
        out_specs=pl.BlockSpec((None, bp, tn), lambda l, j: (l, 0, j)),
        out_shape=jax.ShapeDtypeStruct((depth, bp, n), F32),
        compiler_params=_cparams("parallel", "parallel"),
        name="modulation",
    )(c_all, ada_w, ada_b.reshape(depth, 1, n))


def _rms_mod(x, nw, sc, sh):
    ms = jnp.mean(x * x, axis=-1, keepdims=True)
    return (x * lax.rsqrt(ms + NORM_EPS) * nw) * (1.0 + sc) + sh


def _inproj_kernel(x_ref, nw_ref, sh_ref, sc_ref, w_ref, ca_ref, sa_ref, cr_ref, sr_ref,
                   qa_ref, ka_ref, va_ref, qr_ref, kr_ref, vr_ref, gr_ref, cv_ref, z_ref, ba_ref):
    tm = x_ref.shape[0]
    hb = _rms_mod(x_ref[...], nw_ref[...], sc_ref[...], sh_ref[...]).astype(BF16)

    def proj(c0, width):
        return _dot(hb, w_ref[:, c0:c0 + width])

    lane = lax.broadcasted_iota(jnp.int32, (tm, DSA_W), 1)
    first_half = (lane % DSA_HEAD_DIM) < (DSA_HEAD_DIM // 2)

    def rope(y, cos, sin_signed):
        partner = jnp.where(first_half, pltpu.roll(y, DSA_W - DSA_HEAD_DIM // 2, 1),
                            pltpu.roll(y, DSA_HEAD_DIM // 2, 1))
        return y * cos + partner * sin_signed

    ca, sa, cr, sr = ca_ref[...], sa_ref[...], cr_ref[...], sr_ref[...]
    qa_ref[...] = rope(proj(_C_QA, DSA_W), ca, sa) * (DSA_HEAD_DIM ** -0.5)
    ka_ref[...] = rope(proj(_C_KA, DSA_W), ca, sa)
    va_ref[...] = proj(_C_VA, DSA_W)
    qr_ref[...] = rope(proj(_C_QR, RET_W), cr, sr)
    kr_ref[...] = rope(proj(_C_KR, RET_W), cr, sr) * (RET_HEAD_DIM ** -0.5)
    vr_ref[...] = proj(_C_VR, RET_W)
    gr_ref[...] = proj(_C_GR, RET_W)
    for s in range(CONV_DIM // GDN_W):
        cv_ref[:, s * GDN_W:(s + 1) * GDN_W] = proj(_C_CV + s * GDN_W, GDN_W)
    z_ref[...] = proj(_C_Z, GDN_W)
    ba_ref[...] = proj(_C_BA, LANES)


def _mod_spec(tm, t, d, col):
    if t >= tm:
        return pl.BlockSpec((None, 1, d), lambda i: ((i * tm) // t, 0, col))
    return pl.BlockSpec((None, tm, d), lambda i: (i, 0, col))


def _inproj(x2, t, mod3, norm_w, w_in_b, tabs, tm):
    m, d = x2.shape
    nt = tabs[0].shape[0] // tm
    widths = (DSA_W,) * 3 + (RET_W,) * 4 + (CONV_DIM, GDN_W, LANES)
    tab_spec = pl.BlockSpec((tm, DSA_W), lambda i: (i % nt, 0))
    return pl.pallas_call(
        _inproj_kernel,
        grid=(m // tm,),
        in_specs=[
            pl.BlockSpec((tm, d), lambda i: (i, 0)),
            pl.BlockSpec((1, d), lambda i: (0, 0)),
            _mod_spec(tm, t, d, 0),
            _mod_spec(tm, t, d, 1),
            pl.BlockSpec((d, IN_COLS_PAD), lambda i: (0, 0)),
            tab_spec, tab_spec, tab_spec, tab_spec,
        ],
        out_specs=[pl.BlockSpec((tm, w), lambda i: (i, 0)) for w in widths],
        out_shape=[jax.ShapeDtypeStruct((m, w), F32) for w in widths],
        compiler_params=_cparams("parallel"),
        name="inproj",
    )(x2, norm_w.reshape(1, d), mod3, mod3, w_in_b, *tabs)


DSA_BLK = 128


def _dsa_kernel(q_ref, kp_ref, kc_ref, vp_ref, vc_ref, o_ref, l_ref):
    blk = pl.program_id(3)
    nsub = q_ref.shape[0] // DSA_BLK
    row = lax.broadcasted_iota(jnp.int32, (DSA_BLK, DSA_BLK), 0)
    col = lax.broadcasted_iota(jnp.int32, (DSA_BLK, DSA_BLK), 1)
    diag_ok = col <= row
    prev_ok = col >= row
    lo = lax.broadcasted_iota(jnp.int32, (DSA_BLK, LANES), 1) < DSA_HEAD_DIM
    neg = -jnp.inf
    for j in range(nsub):
        rows = slice(j * DSA_BLK, (j + 1) * DSA_BLK)
        q = q_ref[rows, :]
        if j == 0:
            k_a, v_a = kp_ref[...], vp_ref[...]
            ok_a = jnp.logical_and(prev_ok, blk > 0)
        else:
            prev = slice((j - 1) * DSA_BLK, j * DSA_BLK)
            k_a, v_a = kc_ref[prev, :], vc_ref[prev, :]
            ok_a = prev_ok
        k_a, v_a = k_a.astype(BF16), v_a.astype(BF16)
        k_b, v_b = kc_ref[rows, :].astype(BF16), vc_ref[rows, :].astype(BF16)
        outs, lses = [], []
        for hh in range(2):
            qm = jnp.where(lo if hh == 0 else jnp.logical_not(lo), q, 0.0).astype(BF16)
            s_a = jnp.where(ok_a, _dot_nt(qm, k_a), neg)
            s_b = jnp.where(diag_ok, _dot_nt(qm, k_b), neg)
            mx = jnp.maximum(jnp.max(s_a, axis=-1, keepdims=True), jnp.max(s_b, axis=-1, keepdims=True))
            p_a = jnp.exp(s_a - mx)
            p_b = jnp.exp(s_b - mx)
            den = jnp.sum(p_a, axis=-1, keepdims=True) + jnp.sum(p_b, axis=-1, keepdims=True)
            o = _dot(p_a.astype(BF16), v_a) + _dot(p_b.astype(BF16), v_b)
            outs.append(o / den)
            lses.append(mx + jnp.log(den))
        o_ref[rows, :] = jnp.where(lo, outs[0], outs[1])
        l_ref[rows, :] = jnp.where(lo, jnp.broadcast_to(lses[0], (DSA_BLK, LANES)),
                                   jnp.broadcast_to(lses[1], (DSA_BLK, LANES)))


def _dsa_pattern(q, k, v, nb, t, dil):
    tp = t // dil
    tq = min(512, tp)
    ngrp = DSA_W // LANES
    view = lambda a: a.reshape(nb, tp, dil * DSA_W)
    sub = tq // DSA_BLK
    cur = pl.BlockSpec((None, tq, LANES), lambda b, ph, g, i: (b, i, ph * ngrp + g))
    prv = pl.BlockSpec((None, DSA_BLK, LANES),
                       lambda b, ph, g, i: (b, jnp.maximum(i * sub - 1, 0), ph * ngrp + g))
    o, lse = pl.pallas_call(
        _dsa_kernel,
        grid=(nb, dil, ngrp, tp // tq),
        in_specs=[cur, prv, cur, prv, cur],
        out_specs=[cur, cur],
        out_shape=[jax.ShapeDtypeStruct((nb, tp, dil * DSA_W), F32)] * 2,
        compiler_params=_cparams("parallel", "parallel", "parallel", "parallel"),
        name=f"dsa_dil{dil}",
    )(view(q), view(k), view(k), view(v), view(v))
    return o.reshape(nb * t, DSA_W), lse.reshape(nb * t, DSA_W)


def _multiplicity(dist):
    total = jnp.zeros(dist.shape, F32)
    for window, dil in DSA_PATTERNS:
        hit = (dist >= 0) & (dist <= window) & ((dist & (dil - 1)) == 0)
        total = total + hit.astype(F32)
    return total


def _dsa_step_kernel(q_ref, kc_ref, vc_ref, kn_ref, vn_ref, o_ref):
    tq = q_ref.shape[0]
    wb = kc_ref.shape[0]
    qi = lax.broadcasted_iota(jnp.int32, (2 * tq, wb), 0) % tq
    w_c = _multiplicity(wb + qi - lax.broadcasted_iota(jnp.int32, (2 * tq, wb), 1))
    qn = lax.broadcasted_iota(jnp.int32, (2 * tq, LANES), 0) % tq
    nn = lax.broadcasted_iota(jnp.int32, (2 * tq, LANES), 1)
    w_n = jnp.where(nn < tq, _multiplicity(qn - nn), 0.0)
    lo = lax.broadcasted_iota(jnp.int32, (tq, LANES), 1) < DSA_HEAD_DIM
    pad = jnp.zeros((LANES - tq, LANES), F32)
    neg = -jnp.inf
    for g in range(DSA_W // LANES):
        cols = slice(g * LANES, (g + 1) * LANES)
        q = q_ref[:, cols]
        q2 = jnp.concatenate([jnp.where(lo, q, 0.0), jnp.where(lo, 0.0, q)], axis=0).astype(BF16)
        k_c, v_c = kc_ref[:, cols].astype(BF16), vc_ref[:, cols].astype(BF16)
        k_n = jnp.concatenate([kn_ref[:, cols], pad], axis=0).astype(BF16)
        v_n = jnp.concatenate([vn_ref[:, cols], pad], axis=0).astype(BF16)
        s_c = jnp.where(w_c > 0, _dot_nt(q2, k_c), neg)
        s_n = jnp.where(w_n > 0, _dot_nt(q2, k_n), neg)
        mx = jnp.maximum(jnp.max(s_c, axis=-1, keepdims=True), jnp.max(s_n, axis=-1, keepdims=True))
        p_c = w_c * jnp.exp(s_c - mx)
        p_n = w_n * jnp.exp(s_n - mx)
        den = jnp.sum(p_c, axis=-1, keepdims=True) + jnp.sum(p_n, axis=-1, keepdims=True)
        o2 = (_dot(p_c.astype(BF16), v_c) + _dot(p_n.astype(BF16), v_n)) / den
        o_ref[:, cols] = jnp.where(lo, o2[:tq], o2[tq:])


def _dsa_step(q, k_new, v_new, k_cache, v_cache, layer, nb, t):
    wb = k_cache.shape[2]
    new = pl.BlockSpec((None, t, DSA_W), lambda b: (b, 0, 0))
    cache = pl.BlockSpec((None, None, wb, DSA_W), lambda b: (layer, b, 0, 0))
    r3 = lambda a: a.reshape(nb, t, DSA_W)
    return pl.pallas_call(
        _dsa_step_kernel,
        grid=(nb,),
        in_specs=[new, cache, cache, new, new],
        out_specs=new,
        out_shape=jax.ShapeDtypeStruct((nb, t, DSA_W), F32),
        compiler_params=_cparams("parallel"),
        name="dsa_step",
    )(r3(q), k_cache, v_cache, r3(k_new), r3(v_new)).reshape(nb * t, DSA_W)


def _ret_kernel(q_ref, k_ref, v_ref, g_ref, s0_ref, dec_ref, qd_ref, kd_ref, cd_ref, bd_ref, nw_ref,
                o_ref, so_ref, s_scr, *, chunk):
    tb = pl.program_id(1)
    ngrp = RET_W // LANES

    @pl.when(tb == 0)
    def _():
        s_scr[...] = s0_ref[...]

    lo = lax.broadcasted_iota(jnp.int32, (chunk, LANES), 1) < RET_HEAD_DIM
    nw = nw_ref[...]
    bd = bd_ref[...]

    def body(c, carry):
        rows = pl.ds(pl.multiple_of(c * chunk, chunk), chunk)
        for g in range(ngrp):
            cols = slice(g * LANES, (g + 1) * LANES)
            q, k, v = q_ref[rows, cols], k_ref[rows, cols], v_ref[rows, cols]
            kb, vb = k.astype(BF16), v.astype(BF16)
            parts = []
            for hh in range(2):
                qm = jnp.where(lo if hh == 0 else jnp.logical_not(lo), q, 0.0).astype(BF16)
                inner = _dot_nt(qm, kb) * dec_ref[2 * g + hh]
                parts.append(_dot(inner.astype(BF16), vb))
            s_prev = s_scr[g]
            o = jnp.where(lo, parts[0], parts[1]) + _dot(q.astype(BF16), s_prev.astype(BF16)) * qd_ref[g]
            s_scr[g] = s_prev * cd_ref[g] + bd * _dot_tn((k * kd_ref[g]).astype(BF16), vb)
            o2 = o * o
            ms = jnp.where(lo, jnp.sum(jnp.where(lo, o2, 0.0), axis=-1, keepdims=True),
                           jnp.sum(jnp.where(lo, 0.0, o2), axis=-1, keepdims=True)) * (1.0 / RET_HEAD_DIM)
            o_ref[rows, cols] = o * lax.rsqrt(ms + NORM_EPS) * nw * _silu(g_ref[rows, cols])
        return carry

    lax.fori_loop(0, q_ref.shape[0] // chunk, body, 0)

    @pl.when(tb == pl.num_programs(1) - 1)
    def _():
        so_ref[...] = s_scr[...]


def _ret_tables(chunk):
    log_gamma = jnp.log(1.0 - 2.0 ** (-5.0 - jnp.arange(RET_HEADS, dtype=F32)))
    i = jnp.arange(chunk, dtype=F32)
    diff = i[:, None] - i[None, :]
    causal = diff >= 0
    decay = jnp.where(causal[None], jnp.exp(log_gamma[:, None, None] * jnp.where(causal, diff, 0.0)[None]), 0.0)
    per_lane = lambda a: jnp.repeat(a, RET_HEAD_DIM, axis=0).reshape(RET_W // LANES, LANES, -1)
    q_dec = per_lane(jnp.exp(log_gamma[:, None] * (i[None, :] + 1.0))).transpose(0, 2, 1)
    k_dec = per_lane(jnp.exp(log_gamma[:, None] * (chunk - 1.0 - i)[None, :])).transpose(0, 2, 1)
    c_dec = jnp.broadcast_to(per_lane(jnp.exp(log_gamma * chunk)[:, None]), (RET_W // LANES, LANES, LANES))
    head_of = jnp.arange(LANES) // RET_HEAD_DIM
    block_diag = (head_of[:, None] == head_of[None, :]).astype(F32)
    return decay, q_dec, k_dec, c_dec, block_diag


def _retention(q, k, v, gate, state_bd, ret_norm, nb, t):
    chunk = CHUNK if t % CHUNK == 0 else t
    tblk = min(t, 1024)
    ngrp = RET_W // LANES
    decay, q_dec, k_dec, c_dec, block_diag = _ret_tables(chunk)
    r3 = lambda a: a.reshape(nb, t, RET_W)
    tok = pl.BlockSpec((None, tblk, RET_W), lambda b, i: (b, i, 0))
    st = pl.BlockSpec((None, ngrp, LANES, LANES), lambda b, i: (b, 0, 0, 0))
    full = lambda a: pl.BlockSpec(a.shape, lambda b, i: (0,) * a.ndim)
    nw = jnp.tile(ret_norm, LANES // RET_HEAD_DIM).reshape(1, LANES)
    o, s_new = pl.pallas_call(
        functools.partial(_ret_kernel, chunk=chunk),
        grid=(nb, t // tblk),
        in_specs=[tok, tok, tok, tok, st, full(decay), full(q_dec), full(k_dec), full(c_dec),
                  full(block_diag), full(nw)],
        out_specs=[tok, st],
        out_shape=[jax.ShapeDtypeStruct((nb, t, RET_W), F32),
                   jax.ShapeDtypeStruct((nb, ngrp, LANES, LANES), F32)],
        scratch_shapes=[pltpu.VMEM((ngrp, LANES, LANES), F32)],
        compiler_params=_cparams("parallel", "arbitrary"),
        name="retention",
    )(r3(q), r3(k), r3(v), r3(gate), state_bd, decay, q_dec, k_dec, c_dec, block_diag, nw)
    return o.reshape(nb * t, RET_W), s_new


def _to_block_diag(s):
    nb = s.shape[0]
    s = s.reshape(nb, 2, 2, RET_HEAD_DIM, RET_HEAD_DIM)
    z = jnp.zeros_like(s[:, :, 0])
    top = jnp.concatenate([s[:, :, 0], z], axis=-1)
    bot = jnp.concatenate([z, s[:, :, 1]], axis=-1)
    return jnp.concatenate([top, bot], axis=-2)


def _from_block_diag(s):
    h = RET_HEAD_DIM
    return jnp.stack([s[:, :, :h, :h], s[:, :, h:, h:]], axis=2).reshape(s.shape[0], RET_HEADS, h, h)


def _softplus(x):
    return jnp.maximum(x, 0.0) + jnp.log1p(jnp.exp(-jnp.abs(x)))


def _gdn_prep_kernel(cv_ref, halo_ref, hist_ref, ba_ref, cw_ref, ab_ref,
                     qg_ref, kd_ref, u_ref, w_ref, at_ref, el_ref,
                     xp_scr, cs_scr, beta_scr, g_scr, *, chunk):
    tb = pl.program_id(1)
    tblk = cv_ref.shape[0]
    nhist = CONV_WIDTH - 1
    xp_scr[SUBLANES:, :] = cv_ref[...]

    @pl.when(tb == 0)
    def _():
        xp_scr[:SUBLANES, :] = hist_ref[...]

    @pl.when(tb > 0)
    def _():
        xp_scr[:SUBLANES, :] = halo_ref[...]

    for cg in range(CONV_DIM // LANES):
        cols = slice(cg * LANES, (cg + 1) * LANES)
        acc = None
        for i in range(CONV_WIDTH):
            start = SUBLANES - nhist + i
            term = xp_scr[start:start + tblk, cols] * cw_ref[i:i + 1, cols]
            acc = term if acc is None else acc + term
        cs_scr[:, cols] = _silu(acc)

    ba = ba_ref[...]
    ab = ab_ref[...]
    beta_scr[...] = jax.nn.sigmoid(ba)
    g_scr[...] = -jnp.exp(ab[0:1, :]) * _softplus(ba + ab[1:2, :])

    ri = lax.broadcasted_iota(jnp.int32, (chunk, chunk), 0)
    ci = lax.broadcasted_iota(jnp.int32, (chunk, chunk), 1)
    incl = ri >= ci
    strict = ri > ci
    tri = incl.astype(F32)
    pick0 = (lax.broadcasted_iota(jnp.int32, (chunk, LANES), 1) == 0).astype(F32)
    nsq = max(int(math.log2(chunk)) - 1, 0)

    def l2n(x):
        return x * lax.rsqrt(jnp.sum(x * x, axis=-1, keepdims=True) + NORM_EPS)

    def step(c, carry):
        rows = pl.ds(pl.multiple_of(c * chunk, chunk), chunk)
        beta_c = beta_scr[rows, :]
        g_c = g_scr[rows, :]
        for h in range(GDN_HEADS):
            cols = slice(h * LANES, (h + 1) * LANES)
            q = l2n(cs_scr[rows, h * LANES:(h + 1) * LANES])
            k = l2n(cs_scr[rows, GDN_W + h * LANES:GDN_W + (h + 1) * LANES])
            v = cs_scr[rows, 2 * GDN_W + h * LANES:2 * GDN_W + (h + 1) * LANES]
            beta = jnp.broadcast_to(beta_c[:, h:h + 1], (chunk, LANES))
            g = jnp.broadcast_to(g_c[:, GDN_HEADS + h:GDN_HEADS + h + 1], (chunk, LANES))
            gc = _dot(tri, g, precision=HIGHEST)
            gc_row = _dot_nt(pick0, gc, precision=HIGHEST)
            gdiff = gc[:, :chunk] - gc_row
            dmask = jnp.where(incl, jnp.exp(jnp.where(incl, gdiff, 0.0)), 0.0)
            kb = k * beta
            kbf = k.astype(BF16)
            lower = jnp.where(strict, _dot_nt(kb.astype(BF16), kbf) * dmask, 0.0)
            npow = -lower
            qmat = npow
            for _ in range(nsq):
                npow = _dot(npow, npow, precision=HIGHEST)
                qmat = qmat + npow + _dot(qmat, npow, precision=HIGHEST)
            eg = jnp.exp(gc)
            rhs_u = v * beta
            rhs_w = kb * eg
            qb = qmat.astype(BF16)
            qs = q * (GDN_HEAD_DIM ** -0.5)
            g_last = gc[chunk - 1:chunk, :]
            qg_ref[rows, cols] = qs * eg
            kd_ref[rows, cols] = k * jnp.exp(g_last - gc)
            u_ref[rows, cols] = rhs_u + _dot(qb, rhs_u.astype(BF16))
            w_ref[rows, cols] = rhs_w + _dot(qb, rhs_w.astype(BF16))
            at_ref[rows, h * chunk:(h + 1) * chunk] = _dot_nt(qs.astype(BF16), kbf) * dmask
            el_ref[pl.ds(pl.multiple_of(c * SUBLANES, SUBLANES), SUBLANES), cols] = jnp.broadcast_to(
                jnp.exp(g_last), (SUBLANES, LANES))
        return carry

    lax.fori_loop(0, tblk // chunk, step, 0)


def _gdn_scan_kernel(qg_ref, kd_ref, u_ref, w_ref, at_ref, el_ref, z_ref, s0_ref, nw_ref,
                     o_ref, so_ref, s_scr, *, chunk):
    tb = pl.program_id(1)

    @pl.when(tb == 0)
    def _():
        s_scr[...] = s0_ref[...]

    nw = nw_ref[...]

    def step(c, carry):
        rows = pl.ds(pl.multiple_of(c * chunk, chunk), chunk)
        erow = pl.ds(pl.multiple_of(c * SUBLANES, SUBLANES), SUBLANES)
        for h in range(GDN_HEADS):
            cols = slice(h * LANES, (h + 1) * LANES)
            s_prev = s_scr[h]
            sb = s_prev.astype(BF16)
            v_new = u_ref[rows, cols] - _dot(w_ref[rows, cols].astype(BF16), sb)
            vb = v_new.astype(BF16)
            attn = at_ref[rows, h * chunk:(h + 1) * chunk].astype(BF16)
            o = _dot(qg_ref[rows, cols].astype(BF16), sb) + _dot(attn, vb)
            el = el_ref[erow, cols][0:1, :]
            s_scr[h] = s_prev * el + _dot_tn(kd_ref[rows, cols].astype(BF16), vb)
            ms = jnp.mean(o * o, axis=-1, keepdims=True)
            o_ref[rows, cols] = o * lax.rsqrt(ms + NORM_EPS) * nw * _silu(z_ref[rows, cols])
        return carry

    lax.fori_loop(0, qg_ref.shape[0] // chunk, step, 0)

    @pl.when(tb == pl.num_programs(1) - 1)
    def _():
        so_ref[...] = s_scr[...]


def _gated_delta(cv, z, ba, conv_hist, state, conv_w, a_log, dt_bias, gdn_norm, nb, t):
    chunk = CHUNK if t % CHUNK == 0 else t
    tblk = min(t, 512)
    nhist = CONV_WIDTH - 1
    nchunk_blk = tblk // chunk
    hist_pad = jnp.concatenate([jnp.zeros((nb, SUBLANES - nhist, CONV_DIM), F32), conv_hist], axis=1)
    cw_pad = jnp.concatenate([conv_w, jnp.zeros((SUBLANES - CONV_WIDTH, CONV_DIM), F32)], axis=0)
    ab = jnp.zeros((SUBLANES, LANES), F32)
    ab = ab.at[0, GDN_HEADS:2 * GDN_HEADS].set(a_log).at[1, GDN_HEADS:2 * GDN_HEADS].set(dt_bias)
    cv3 = cv.reshape(nb, t, CONV_DIM)
    tok = lambda w: pl.BlockSpec((None, tblk, w), lambda b, i: (b, i, 0))
    full = lambda a: pl.BlockSpec(a.shape, lambda b, i: (0,) * a.ndim)
    halo = pl.BlockSpec((None, SUBLANES, CONV_DIM),
                        lambda b, i: (b, jnp.maximum(i * (tblk // SUBLANES) - 1, 0), 0))
    el_spec = pl.BlockSpec((None, nchunk_blk * SUBLANES, GDN_W), lambda b, i: (b, i, 0))
    tok_shape = lambda w: jax.ShapeDtypeStruct((nb, t, w), F32)
    el_shape = jax.ShapeDtypeStruct((nb, (t // chunk) * SUBLANES, GDN_W), F32)
    qg, kd, u, w, attn, el = pl.pallas_call(
        functools.partial(_gdn_prep_kernel, chunk=chunk),
        grid=(nb, t // tblk),
        in_specs=[tok(CONV_DIM), halo, pl.BlockSpec((None, SUBLANES, CONV_DIM), lambda b, i: (b, 0, 0)),
                  tok(LANES), full(cw_pad), full(ab)],
        out_specs=[tok(GDN_W), tok(GDN_W), tok(GDN_W), tok(GDN_W), tok(GDN_HEADS * chunk), el_spec],
        out_shape=[tok_shape(GDN_W)] * 4 + [tok_shape(GDN_HEADS * chunk), el_shape],
        scratch_shapes=[pltpu.VMEM((tblk + SUBLANES, CONV_DIM), F32), pltpu.VMEM((tblk, CONV_DIM), F32),
                        pltpu.VMEM((tblk, LANES), F32), pltpu.VMEM((tblk, LANES), F32)],
        compiler_params=_cparams("parallel", "parallel"),
        name="gdn_prep",
    )(cv3, cv3, hist_pad, ba.reshape(nb, t, LANES), cw_pad, ab)

    st = pl.BlockSpec((None, GDN_HEADS, LANES, LANES), lambda b, i: (b, 0, 0, 0))
    nw = gdn_norm.reshape(1, LANES)
    o, s_new = pl.pallas_call(
        functools.partial(_gdn_scan_kernel, chunk=chunk),
        grid=(nb, t // tblk),
        in_specs=[tok(GDN_W), tok(GDN_W), tok(GDN_W), tok(GDN_W), tok(GDN_HEADS * chunk), el_spec,
                  tok(GDN_W), st, full(nw)],
        out_specs=[tok(GDN_W), st],
        out_shape=[tok_shape(GDN_W), jax.ShapeDtypeStruct((nb, GDN_HEADS, LANES, LANES), F32)],
        scratch_shapes=[pltpu.VMEM((GDN_HEADS, LANES, LANES), F32)],
        compiler_params=_cparams("parallel", "arbitrary"),
        name="gdn_scan",
    )(qg, kd, u, w, attn, el, z.reshape(nb, t, GDN_W), state, nw)
    conv_new = jnp.concatenate([conv_hist, cv3], axis=1)[:, -nhist:]
    return o.reshape(nb * t, GDN_W), s_new, conv_new


FFN_TILE = 256


def _out_ffn_kernel(*refs, n_parts, final):
    x_ref = refs[0]
    part_refs = refs[1:1 + n_parts]
    (orr_ref, oc_ref, g1_ref, sh2_ref, sc2_ref, g2_ref, nw_ref, wo_ref, wg_ref, wu_ref, wd_ref, fn_ref,
     out_ref) = refs[1 + n_parts:]
    if n_parts == 1:
        oa = part_refs[0][...]
    else:
        outs, lses = part_refs[0::2], part_refs[1::2]
        mx = lses[0][...]
        for l_ref in lses[1:]:
            mx = jnp.maximum(mx, l_ref[...])
        num, den = None, None
        for o_ref, l_ref in zip(outs, lses):
            e = jnp.exp(l_ref[...] - mx)
            num = e * o_ref[...] if num is None else num + e * o_ref[...]
            den = e if den is None else den + e
        oa = num / den
    mix = (_dot(oa.astype(BF16), wo_ref[0:DSA_W, :])
           + _dot(orr_ref[...].astype(BF16), wo_ref[DSA_W:DSA_W + RET_W, :])
           + _dot(oc_ref[...].astype(BF16), wo_ref[DSA_W + RET_W:, :]))
    x1 = x_ref[...] + g1_ref[...] * mix
    hb = _rms_mod(x1, nw_ref[...], sc2_ref[...], sh2_ref[...]).astype(BF16)
    acc = None
    for j in range(wg_ref.shape[1] // FFN_TILE):
        cols = slice(j * FFN_TILE, (j + 1) * FFN_TILE)
        act = (_silu(_dot(hb, wg_ref[:, cols])) * _dot(hb, wu_ref[:, cols])).astype(BF16)
        down = _dot(act, wd_ref[cols, :])
        acc = down if acc is None else acc + down
    x2 = x1 + g2_ref[...] * acc
    if final:
        ms = jnp.mean(x2 * x2, axis=-1, keepdims=True)
        x2 = x2 * lax.rsqrt(ms + NORM_EPS) * fn_ref[...]
    out_ref[...] = x2


def _out_ffn(x2, t, parts, orr, oc, mod3, norm_w, w_out_b, wg_b, wu_b, wd_b, final_norm, final, tm):
    m, d = x2.shape
    row = lambda w: pl.BlockSpec((tm, w), lambda i: (i, 0))
    const = lambda a: pl.BlockSpec(a.shape, lambda i: (0,) * a.ndim)
    nw = norm_w.reshape(1, d)
    fn = final_norm.reshape(1, d)
    return pl.pallas_call(
        functools.partial(_out_ffn_kernel, n_parts=len(parts), final=final),
        grid=(m // tm,),
        in_specs=[row(d)] + [row(DSA_W)] * len(parts) + [row(RET_W), row(GDN_W),
                  _mod_spec(tm, t, d, 2), _mod_spec(tm, t, d, 3), _mod_spec(tm, t, d, 4), _mod_spec(tm, t, d, 5),
                  const(nw), const(w_out_b), const(wg_b), const(wu_b), const(wd_b), const(fn)],
        out_specs=row(d),
        out_shape=jax.ShapeDtypeStruct((m, d), F32),
        compiler_params=_cparams("parallel"),
        name="out_ffn",
    )(x2, *parts, orr, oc, mod3, mod3, mod3, mod3, nw, w_out_b, wg_b, wu_b, wd_b, fn)


def _rope_tables(pos, inv_freq, reps):
    ang = pos.astype(F32)[:, None] * inv_freq[None, :]
    cos, sin = jnp.cos(ang), jnp.sin(ang)
    return (jnp.tile(jnp.concatenate([cos, cos], axis=-1), (1, reps)),
            jnp.tile(jnp.concatenate([-sin, sin], axis=-1), (1, reps)))


def _trunk(x, modp, pos, k_hist, v_hist, s_ret, s_gdn, conv_hist, wts):
    (norm_mix, norm_ffn, w_in_b, ret_norm, conv_w, a_log, dt_bias, gdn_norm, w_out_b, wg_b, wu_b, wd_b,
     final_norm) = wts
    nb, t, d = x.shape
    m = nb * t
    depth = w_in_b.shape[0]
    tm = min(256, m)
    inv_a = 1.0 / (ROPE_THETA ** (jnp.arange(0, DSA_HEAD_DIM, 2, dtype=F32) / DSA_HEAD_DIM))
    inv_r = 1.0 / (10000.0 ** jnp.linspace(0.0, 1.0, RET_HEAD_DIM // 2, dtype=F32))
    tabs = _rope_tables(pos, inv_a, DSA_HEADS) + _rope_tables(pos, inv_r, RET_HEADS)
    if t < tm:
        tabs = tuple(jnp.tile(a, (tm // t, 1)) for a in tabs)
    x2 = x.reshape(m, d)
    ks, vs, rs, gs, cs = [], [], [], [], []
    for l in range(depth):
        if t >= tm:
            mod3 = modp[l].reshape(nb, 1, 6 * d)
        else:
            mod3 = jnp.repeat(modp[l], t, axis=0).reshape(m // tm, tm, 6 * d)
        qa, ka, va, qr, kr, vr, gr, cv, z, ba = _inproj(x2, t, mod3, norm_mix[l], w_in_b[l], tabs, tm)
        if k_hist is None:
            parts = []
            for _, dil in DSA_PATTERNS:
                parts.extend(_dsa_pattern(qa, ka, va, nb, t, dil))
        else:
            parts = [_dsa_step(qa, ka, va, k_hist, v_hist, l, nb, t)]
        orr, sr = _retention(qr, kr, vr, gr, _to_block_diag(s_ret[l]), ret_norm[l], nb, t)
        oc, sg, cvn = _gated_delta(cv, z, ba, conv_hist[l], s_gdn[l], conv_w[l], a_log[l], dt_bias[l],
                                   gdn_norm[l], nb, t)
        x2 = _out_ffn(x2, t, parts, orr, oc, mod3, norm_ffn[l], w_out_b[l], wg_b[l], wu_b[l], wd_b[l],
                      final_norm, l == depth - 1, tm)
        ks.append(ka.reshape(nb, t, DSA_HEADS, DSA_HEAD_DIM))
        vs.append(va.reshape(nb, t, DSA_HEADS, DSA_HEAD_DIM))
        rs.append(_from_block_diag(sr))
        gs.append(sg)
        cs.append(cvn)
    return (x2.reshape(nb, t, d), jnp.stack(ks), jnp.stack(vs), jnp.stack(rs), jnp.stack(gs), jnp.stack(cs))


def kernel(x_prompt, x_sample, cache_win_k, cache_win_v, state_ret, state_gdn, state_conv, c_prompt, c_sample, ada_w, ada_b, norm_mix, norm_ffn, w_in, ret_norm, conv_w, a_log, dt_bias, gdn_norm, w_out, w_gate, w_up, w_down, final_norm):
    nb, t_p, d = x_prompt.shape
    db, t_s, _ = x_sample.shape
    depth = ada_w.shape[0]
    rows = nb + db
    rows_pad = -(-rows // SUBLANES) * SUBLANES
    c_all = jnp.concatenate([c_prompt, c_sample, jnp.zeros((rows_pad - rows, d), F32)], axis=0)
    mod = _modulation(c_all, ada_w, ada_b)
    w_in_b = jnp.pad(w_in, ((0, 0), (0, 0), (0, IN_COLS_PAD - IN_COLS))).astype(BF16)
    wts = (norm_mix, norm_ffn, w_in_b, ret_norm, conv_w, a_log, dt_bias, gdn_norm, w_out.astype(BF16),
           w_gate.astype(BF16), w_up.astype(BF16), w_down.astype(BF16), final_norm)

    zr = jnp.zeros((depth, nb, RET_HEADS, RET_HEAD_DIM, RET_HEAD_DIM), F32)
    zg = jnp.zeros((depth, nb, GDN_HEADS, GDN_HEAD_DIM, GDN_HEAD_DIM), F32)
    zc = jnp.zeros((depth, nb, CONV_WIDTH - 1, CONV_DIM), F32)
    y_p, kp, vp, rp, gp, cp = _trunk(x_prompt, mod[:, :nb], jnp.arange(t_p, dtype=jnp.int32),
                                     None, None, zr, zg, zc, wts)
    wb = cache_win_k.shape[2]
    y_s, ks, vs, rs, gs, cs = _trunk(x_sample, mod[:, nb:rows], PAST_LEN + jnp.arange(t_s, dtype=jnp.int32),
                                     cache_win_k.reshape(depth, db, wb, DSA_W),
                                     cache_win_v.reshape(depth, db, wb, DSA_W),
                                     state_ret, state_gdn, state_conv, wts)
    return (y_p, y_s, kp[:, :, -DSA_MAX_WINDOW:], vp[:, :, -DSA_MAX_WINDOW:], rp, gp, cp, ks, vs, rs, gs, cs)
```

```python
import functools
import math

import jax
import jax.numpy as jnp
from jax import lax
from jax.experimental import pallas as pl
from jax.experimental.pallas import tpu as pltpu

F32 = jnp.float32
BF16 = jnp.bfloat16
HIGHEST = lax.Precision.HIGHEST

DSA_HEAD_DIM = 64
DSA_HEADS = 4
DSA_PATTERNS = ((128, 1), (512, 4), (2048, 16))
DSA_MAX_WINDOW = 2048
ROPE_THETA = 10000.0
RET_HEAD_DIM = 64
RET_HEADS = 4
GDN_HEAD_DIM = 128
GDN_HEADS = 4
CONV_WIDTH = 4
CHUNK = 64
NORM_EPS = 1e-6
PAST_LEN = 16384

DSA_W = DSA_HEADS * DSA_HEAD_DIM
RET_W = RET_HEADS * RET_HEAD_DIM
GDN_W = GDN_HEADS * GDN_HEAD_DIM
CONV_DIM = 3 * GDN_W
LANES = 128
SUBLANES = 8
VMEM_LIMIT = 56 * 1024 * 1024

_C_QA, _C_KA, _C_VA = 0, DSA_W, 2 * DSA_W
_C_QR = 3 * DSA_W
_C_KR, _C_VR, _C_GR = _C_QR + RET_W, _C_QR + 2 * RET_W, _C_QR + 3 * RET_W
_C_CV = _C_QR + 4 * RET_W
_C_Z = _C_CV + CONV_DIM
_C_BA = _C_Z + GDN_W
IN_COLS = _C_BA + 2 * GDN_HEADS
IN_COLS_PAD = _C_BA + LANES


def _cparams(*sem):
    return pltpu.CompilerParams(dimension_semantics=sem, vmem_limit_bytes=VMEM_LIMIT)


def _dot(a, b, **kw):
    return jnp.dot(a, b, preferred_element_type=F32, **kw)


def _dot_nt(a, b, **kw):
    return lax.dot_general(a, b, (((1,), (1,)), ((), ())), preferred_element_type=F32, **kw)


def _dot_tn(a, b, **kw):
    return lax.dot_general(a, b, (((0,), (0,)), ((), ())), preferred_element_type=F32, **kw)


def _silu(x):
    return x * jax.nn.sigmoid(x)


def _seqs_per_step(nb, t, base):
    return math.gcd(nb, base * (4 if t < CHUNK else 1))


def _mod_kernel(c_ref, w_ref, b_ref, o_ref):
    a = _silu(c_ref[...]).astype(BF16)
    o_ref[...] = _dot(a, w_ref[...].astype(BF16)) + b_ref[...]


def _modulation(c_all, ada_w, ada_b, tn=1536):
    depth, d, n = ada_w.shape
    bp = c_all.shape[0]
    return pl.pallas_call(
        _mod_kernel,
        grid=(depth, n // tn),
        in_specs=[
            pl.BlockSpec((bp, d), lambda l, j: (0, 0)),
            pl.BlockSpec((None, d, tn), lambda l, j: (l, 0, j)),
            pl.BlockSpec((None, 1, tn), lambda l, j: (l, 0, j)),
        ],
        out_specs=pl.BlockSpec((None, bp, tn), lambda l, j: (l, 0, j)),
        out_shape=jax.ShapeDtypeStruct((depth, bp, n), F32),
        compiler_params=_cparams("parallel", "parallel"),
        name="modulation",
    )(c_all, ada_w, ada_b.reshape(depth, 1, n))


def _rms_mod(x, nw, sc, sh):
    ms = jnp.mean(x * x, axis=-1, keepdims=True)
    return (x * lax.rsqrt(ms + NORM_EPS) * nw) * (1.0 + sc) + sh


def _inproj_kernel(x_ref, nw_ref, sh_ref, sc_ref, w_ref, ca_ref, sa_ref, cr_ref, sr_ref,
                   qa_ref, ka_ref, va_ref, qr_ref, kr_ref, vr_ref, gr_ref, cv_ref, z_ref, ba_ref):
    tm = x_ref.shape[0]
    hb = _rms_mod(x_ref[...], nw_ref[...], sc_ref[...], sh_ref[...]).astype(BF16)

    def proj(c0, width):
        return _dot(hb, w_ref[:, c0:c0 + width])

    lane = lax.broadcasted_iota(jnp.int32, (tm, DSA_W), 1)
    first_half = (lane % DSA_HEAD_DIM) < (DSA_HEAD_DIM // 2)

    def rope(y, cos, sin_signed):
        partner = jnp.where(first_half, pltpu.roll(y, DSA_W - DSA_HEAD_DIM // 2, 1),
                            pltpu.roll(y, DSA_HEAD_DIM // 2, 1))
        return y * cos + partner * sin_signed

    ca, sa, cr, sr = ca_ref[...], sa_ref[...], cr_ref[...], sr_ref[...]
    qa_ref[...] = rope(proj(_C_QA, DSA_W), ca, sa) * (DSA_HEAD_DIM ** -0.5)
    ka_ref[...] = rope(proj(_C_KA, DSA_W), ca, sa)
    va_ref[...] = proj(_C_VA, DSA_W)
    qr_ref[...] = rope(proj(_C_QR, RET_W), cr, sr)
    kr_ref[...] = rope(proj(_C_KR, RET_W), cr, sr) * (RET_HEAD_DIM ** -0.5)
    vr_ref[...] = proj(_C_VR, RET_W)
    gr_ref[...] = proj(_C_GR, RET_W)
    for s in range(CONV_DIM // GDN_W):
        cv_ref[:, s * GDN_W:(s + 1) * GDN_W] = proj(_C_CV + s * GDN_W, GDN_W)
    z_ref[...] = proj(_C_Z, GDN_W)
    ba_ref[...] = proj(_C_BA, LANES)


def _mod_spec(tm, t, d, col):
    if t >= tm:
        return pl.BlockSpec((None, 1, d), lambda i: ((i * tm) // t, 0, col))
    return pl.BlockSpec((None, tm, d), lambda i: (i, 0, col))


def _inproj(x2, t, mod3, norm_w, w_in_b, tabs, tm):
    m, d = x2.shape
    nt = tabs[0].shape[0] // tm
    widths = (DSA_W,) * 3 + (RET_W,) * 4 + (CONV_DIM, GDN_W, LANES)
    tab_spec = pl.BlockSpec((tm, DSA_W), lambda i: (i % nt, 0))
    return pl.pallas_call(
        _inproj_kernel,
        grid=(m // tm,),
        in_specs=[
            pl.BlockSpec((tm, d), lambda i: (i, 0)),
            pl.BlockSpec((1, d), lambda i: (0, 0)),
            _mod_spec(tm, t, d, 0),
            _mod_spec(tm, t, d, 1),
            pl.BlockSpec((d, IN_COLS_PAD), lambda i: (0, 0)),
            tab_spec, tab_spec, tab_spec, tab_spec,
        ],
        out_specs=[pl.BlockSpec((tm, w), lambda i: (i, 0)) for w in widths],
        out_shape=[jax.ShapeDtypeStruct((m, w), F32) for w in widths],
        compiler_params=_cparams("parallel"),
        name="inproj",
    )(x2, norm_w.reshape(1, d), mod3, mod3, w_in_b, *tabs)


DSA_BLK = 128


def _dsa_kernel(q_ref, kp_ref, kc_ref, vp_ref, vc_ref, o_ref, l_ref):
    blk = pl.program_id(3)
    nsub = q_ref.shape[0] // DSA_BLK
    row = lax.broadcasted_iota(jnp.int32, (DSA_BLK, DSA_BLK), 0)
    col = lax.broadcasted_iota(jnp.int32, (DSA_BLK, DSA_BLK), 1)
    diag_ok = col <= row
    prev_ok = col >= row
    lo = lax.broadcasted_iota(jnp.int32, (DSA_BLK, LANES), 1) < DSA_HEAD_DIM
    neg = -jnp.inf
    for j in range(nsub):
        rows = slice(j * DSA_BLK, (j + 1) * DSA_BLK)
        q = q_ref[rows, :]
        if j == 0:
            k_a, v_a = kp_ref[...], vp_ref[...]
            ok_a = jnp.logical_and(prev_ok, blk > 0)
        else:
            prev = slice((j - 1) * DSA_BLK, j * DSA_BLK)
            k_a, v_a = kc_ref[prev, :], vc_ref[prev, :]
            ok_a = prev_ok
        k_a, v_a = k_a.astype(BF16), v_a.astype(BF16)
        k_b, v_b = kc_ref[rows, :].astype(BF16), vc_ref[rows, :].astype(BF16)
        outs, lses = [], []
        for hh in range(2):
            qm = jnp.where(lo if hh == 0 else jnp.logical_not(lo), q, 0.0).astype(BF16)
            s_a = jnp.where(ok_a, _dot_nt(qm, k_a), neg)
            s_b = jnp.where(diag_ok, _dot_nt(qm, k_b), neg)
            mx = jnp.maximum(jnp.max(s_a, axis=-1, keepdims=True), jnp.max(s_b, axis=-1, keepdims=True))
            p_a = jnp.exp(s_a - mx)
            p_b = jnp.exp(s_b - mx)
            den = jnp.sum(p_a, axis=-1, keepdims=True) + jnp.sum(p_b, axis=-1, keepdims=True)
            o = _dot(p_a.astype(BF16), v_a) + _dot(p_b.astype(BF16), v_b)
            outs.append(o / den)
            lses.append(mx + jnp.log(den))
        o_ref[rows, :] = jnp.where(lo, outs[0], outs[1])
        l_ref[rows, :] = jnp.where(lo, jnp.broadcast_to(lses[0], (DSA_BLK, LANES)),
                                   jnp.broadcast_to(lses[1], (DSA_BLK, LANES)))


def _dsa_pattern(q, k, v, nb, t, dil):
    tp = t // dil
    tq = min(512, tp)
    ngrp = DSA_W // LANES
    view = lambda a: a.reshape(nb, tp, dil * DSA_W)
    sub = tq // DSA_BLK
    cur = pl.BlockSpec((None, tq, LANES), lambda b, ph, g, i: (b, i, ph * ngrp + g))
    prv = pl.BlockSpec((None, DSA_BLK, LANES),
                       lambda b, ph, g, i: (b, jnp.maximum(i * sub - 1, 0), ph * ngrp + g))
    o, lse = pl.pallas_call(
        _dsa_kernel,
        grid=(nb, dil, ngrp, tp // tq),
        in_specs=[cur, prv, cur, prv, cur],
        out_specs=[cur, cur],
        out_shape=[jax.ShapeDtypeStruct((nb, tp, dil * DSA_W), F32)] * 2,
        compiler_params=_cparams("parallel", "parallel", "parallel", "parallel"),
        name=f"dsa_dil{dil}",
    )(view(q), view(k), view(k), view(v), view(v))
    return o.reshape(nb * t, DSA_W), lse.reshape(nb * t, DSA_W)


def _multiplicity(dist):
    total = jnp.zeros(dist.shape, F32)
    for window, dil in DSA_PATTERNS:
        hit = (dist >= 0) & (dist <= window) & ((dist & (dil - 1)) == 0)
        total = total + hit.astype(F32)
    return total


def _dsa_step_kernel(q_ref, kc_ref, vc_ref, kn_ref, vn_ref, o_ref):
    tq = q_ref.shape[0]
    wb = kc_ref.shape[1]
    qi = lax.broadcasted_iota(jnp.int32, (2 * tq, wb), 0) % tq
    w_c = _multiplicity(wb + qi - lax.broadcasted_iota(jnp.int32, (2 * tq, wb), 1))
    qn = lax.broadcasted_iota(jnp.int32, (2 * tq, LANES), 0) % tq
    nn = lax.broadcasted_iota(jnp.int32, (2 * tq, LANES), 1)
    w_n = jnp.where(nn < tq, _multiplicity(qn - nn), 0.0)
    lo = lax.broadcasted_iota(jnp.int32, (tq, LANES), 1) < DSA_HEAD_DIM
    pad = jnp.zeros((LANES - tq, LANES), F32)
    neg = -jnp.inf
    for g in range(DSA_W // LANES):
        cols = slice(g * LANES, (g + 1) * LANES)
        q = q_ref[:, cols]
        q2 = jnp.concatenate([jnp.where(lo, q, 0.0), jnp.where(lo, 0.0, q)], axis=0).astype(BF16)
        kt_c, vt_c = kc_ref[cols, :].astype(BF16), vc_ref[cols, :].astype(BF16)
        k_n = jnp.concatenate([kn_ref[:, cols], pad], axis=0).astype(BF16)
        v_n = jnp.concatenate([vn_ref[:, cols], pad], axis=0).astype(BF16)
        s_c = jnp.where(w_c > 0, _dot(q2, kt_c), neg)
        s_n = jnp.where(w_n > 0, _dot_nt(q2, k_n), neg)
        mx = jnp.maximum(jnp.max(s_c, axis=-1, keepdims=True), jnp.max(s_n, axis=-1, keepdims=True))
        p_c = w_c * jnp.exp(s_c - mx)
        p_n = w_n * jnp.exp(s_n - mx)
        den = jnp.sum(p_c, axis=-1, keepdims=True) + jnp.sum(p_n, axis=-1, keepdims=True)
        o2 = (_dot_nt(p_c.astype(BF16), vt_c) + _dot(p_n.astype(BF16), v_n)) / den
        o_ref[:, cols] = jnp.where(lo, o2[:tq], o2[tq:])


def _dsa_step(q, k_new, v_new, k_cache, v_cache, layer, nb, t):
    wb = k_cache.shape[3]
    new = pl.BlockSpec((None, t, DSA_W), lambda b: (b, 0, 0))
    cache = pl.BlockSpec((None, None, DSA_W, wb), lambda b: (layer, b, 0, 0))
    r3 = lambda a: a.reshape(nb, t, DSA_W)
    return pl.pallas_call(
        _dsa_step_kernel,
        grid=(nb,),
        in_specs=[new, cache, cache, new, new],
        out_specs=new,
        out_shape=jax.ShapeDtypeStruct((nb, t, DSA_W), F32),
        compiler_params=_cparams("parallel"),
        name="dsa_step",
    )(r3(q), k_cache, v_cache, r3(k_new), r3(v_new)).reshape(nb * t, DSA_W)


RET_CHUNK = 256
RET_SEQS_PER_STEP = 2


def _ret_kernel(q_ref, k_ref, v_ref, g_ref, s0_ref, dec_ref, qd_ref, kd_ref, cd_ref, bd_ref, nw_ref,
                o_ref, so_ref, s_scr, *, chunk):
    tb = pl.program_id(1)
    ngrp = RET_W // LANES

    @pl.when(tb == 0)
    def _():
        s_scr[...] = s0_ref[...]

    lo = lax.broadcasted_iota(jnp.int32, (chunk, LANES), 1) < RET_HEAD_DIM
    nw = nw_ref[...]
    bd = bd_ref[...]

    def body(c, carry):
        rows = pl.ds(pl.multiple_of(c * chunk, chunk), chunk)
        for b in range(q_ref.shape[0]):
            for g in range(ngrp):
                cols = slice(g * LANES, (g + 1) * LANES)
                q, k, v = q_ref[b, rows, cols], k_ref[b, rows, cols], v_ref[b, rows, cols]
                kb, vb = k.astype(BF16), v.astype(BF16)
                parts = []
                for hh in range(2):
                    qm = jnp.where(lo if hh == 0 else jnp.logical_not(lo), q, 0.0).astype(BF16)
                    inner = _dot_nt(qm, kb) * dec_ref[2 * g + hh]
                    parts.append(_dot(inner.astype(BF16), vb))
                s_prev = s_scr[b, g]
                o = jnp.where(lo, parts[0], parts[1]) + _dot(q.astype(BF16), s_prev.astype(BF16)) * qd_ref[g]
                s_scr[b, g] = s_prev * cd_ref[g] + bd * _dot_tn((k * kd_ref[g]).astype(BF16), vb)
                o2 = o * o
                ms = jnp.where(lo, jnp.sum(jnp.where(lo, o2, 0.0), axis=-1, keepdims=True),
                               jnp.sum(jnp.where(lo, 0.0, o2), axis=-1, keepdims=True)) * (1.0 / RET_HEAD_DIM)
                o_ref[b, rows, cols] = o * lax.rsqrt(ms + NORM_EPS) * nw * _silu(g_ref[b, rows, cols])
        return carry

    lax.fori_loop(0, q_ref.shape[1] // chunk, body, 0)

    @pl.when(tb == pl.num_programs(1) - 1)
    def _():
        so_ref[...] = s_scr[...]


def _ret_tables(chunk):
    log_gamma = jnp.log(1.0 - 2.0 ** (-5.0 - jnp.arange(RET_HEADS, dtype=F32)))
    i = jnp.arange(chunk, dtype=F32)
    diff = i[:, None] - i[None, :]
    causal = diff >= 0
    decay = jnp.where(causal[None], jnp.exp(log_gamma[:, None, None] * jnp.where(causal, diff, 0.0)[None]), 0.0)
    per_lane = lambda a: jnp.repeat(a, RET_HEAD_DIM, axis=0).reshape(RET_W // LANES, LANES, -1)
    q_dec = per_lane(jnp.exp(log_gamma[:, None] * (i[None, :] + 1.0))).transpose(0, 2, 1)
    k_dec = per_lane(jnp.exp(log_gamma[:, None] * (chunk - 1.0 - i)[None, :])).transpose(0, 2, 1)
    c_dec = jnp.broadcast_to(per_lane(jnp.exp(log_gamma * chunk)[:, None]), (RET_W // LANES, LANES, LANES))
    head_of = jnp.arange(LANES) // RET_HEAD_DIM
    block_diag = (head_of[:, None] == head_of[None, :]).astype(F32)
    return decay, q_dec, k_dec, c_dec, block_diag


def _retention(q, k, v, gate, state_bd, ret_norm, nb, t):
    chunk = RET_CHUNK if t % RET_CHUNK == 0 else t
    tblk = min(t, 1024)
    ngrp = RET_W // LANES
    bblk = _seqs_per_step(nb, t, RET_SEQS_PER_STEP)
    decay, q_dec, k_dec, c_dec, block_diag = _ret_tables(chunk)
    r3 = lambda a: a.reshape(nb, t, RET_W)
    tok = pl.BlockSpec((bblk, tblk, RET_W), lambda b, i: (b, i, 0))
    st = pl.BlockSpec((bblk, ngrp, LANES, LANES), lambda b, i: (b, 0, 0, 0))
    full = lambda a: pl.BlockSpec(a.shape, lambda b, i: (0,) * a.ndim)
    nw = jnp.tile(ret_norm, LANES // RET_HEAD_DIM).reshape(1, LANES)
    o, s_new = pl.pallas_call(
        functools.partial(_ret_kernel, chunk=chunk),
        grid=(nb // bblk, t // tblk),
        in_specs=[tok, tok, tok, tok, st, full(decay), full(q_dec), full(k_dec), full(c_dec),
                  full(block_diag), full(nw)],
        out_specs=[tok, st],
        out_shape=[jax.ShapeDtypeStruct((nb, t, RET_W), F32),
                   jax.ShapeDtypeStruct((nb, ngrp, LANES, LANES), F32)],
        scratch_shapes=[pltpu.VMEM((bblk, ngrp, LANES, LANES), F32)],
        compiler_params=_cparams("parallel", "arbitrary"),
        name="retention",
    )(r3(q), r3(k), r3(v), r3(gate), state_bd, decay, q_dec, k_dec, c_dec, block_diag, nw)
    return o.reshape(nb * t, RET_W), s_new


def _to_block_diag(s):
    nb = s.shape[0]
    s = s.reshape(nb, 2, 2, RET_HEAD_DIM, RET_HEAD_DIM)
    z = jnp.zeros_like(s[:, :, 0])
    top = jnp.concatenate([s[:, :, 0], z], axis=-1)
    bot = jnp.concatenate([z, s[:, :, 1]], axis=-1)
    return jnp.concatenate([top, bot], axis=-2)


def _from_block_diag(s):
    h = RET_HEAD_DIM
    return jnp.stack([s[:, :, :h, :h], s[:, :, h:, h:]], axis=2).reshape(s.shape[0], RET_HEADS, h, h)


GDN_PREP_CHUNKS_PER_ITER = 2
GDN_SCAN_SEQS_PER_STEP = 2


def _softplus(x):
    return jnp.maximum(x, 0.0) + jnp.log1p(jnp.exp(-jnp.abs(x)))


def _gdn_prep_kernel(cv_ref, halo_ref, hist_ref, ba_ref, cw_ref, ab_ref,
                     qg_ref, kd_ref, u_ref, w_ref, at_ref, el_ref,
                     xp_scr, cs_scr, beta_scr, g_scr, *, chunk, cpi):
    tb = pl.program_id(1)
    tblk = cv_ref.shape[0]
    nhist = CONV_WIDTH - 1
    xp_scr[SUBLANES:, :] = cv_ref[...]

    @pl.when(tb == 0)
    def _():
        xp_scr[:SUBLANES, :] = hist_ref[...]

    @pl.when(tb > 0)
    def _():
        xp_scr[:SUBLANES, :] = halo_ref[...]

    for cg in range(CONV_DIM // LANES):
        cols = slice(cg * LANES, (cg + 1) * LANES)
        acc = None
        for i in range(CONV_WIDTH):
            start = SUBLANES - nhist + i
            term = xp_scr[start:start + tblk, cols] * cw_ref[i:i + 1, cols]
            acc = term if acc is None else acc + term
        cs_scr[:, cols] = _silu(acc)

    ba = ba_ref[...]
    ab = ab_ref[...]
    beta_scr[...] = jax.nn.sigmoid(ba)
    g_scr[...] = -jnp.exp(ab[0:1, :]) * _softplus(ba + ab[1:2, :])

    ri = lax.broadcasted_iota(jnp.int32, (chunk, chunk), 0)
    ci = lax.broadcasted_iota(jnp.int32, (chunk, chunk), 1)
    incl = ri >= ci
    strict = ri > ci
    tri = incl.astype(F32)
    lane_pick = (lax.broadcasted_iota(jnp.int32, (SUBLANES, LANES), 0)
                 == lax.broadcasted_iota(jnp.int32, (SUBLANES, LANES), 1)).astype(F32)
    nsq = max(int(math.log2(chunk)) - 1, 0)

    def l2n(x):
        return x * lax.rsqrt(jnp.sum(x * x, axis=-1, keepdims=True) + NORM_EPS)

    def step(it, carry):
        probs = []
        for cc in range(cpi):
            c = it * cpi + cc
            rows = pl.ds(pl.multiple_of(c * chunk, chunk), chunk)
            beta_c = beta_scr[rows, :]
            gcol = _dot(tri, g_scr[rows, :], precision=HIGHEST)
            grow = _dot_nt(lane_pick, gcol, precision=HIGHEST)
            for h in range(GDN_HEADS):
                q = l2n(cs_scr[rows, h * LANES:(h + 1) * LANES])
                k = l2n(cs_scr[rows, GDN_W + h * LANES:GDN_W + (h + 1) * LANES])
                v = cs_scr[rows, 2 * GDN_W + h * LANES:2 * GDN_W + (h + 1) * LANES]
                beta = jnp.broadcast_to(beta_c[:, h:h + 1], (chunk, LANES))
                gc = jnp.broadcast_to(gcol[:, GDN_HEADS + h:GDN_HEADS + h + 1], (chunk, LANES))
                gdiff = gc[:, :chunk] - grow[GDN_HEADS + h:GDN_HEADS + h + 1, :]
                dmask = jnp.where(incl, jnp.exp(jnp.where(incl, gdiff, 0.0)), 0.0)
                kb = k * beta
                kbf = k.astype(BF16)
                lower = jnp.where(strict, _dot_nt(kb.astype(BF16), kbf) * dmask, 0.0)
                probs.append(dict(c=c, rows=rows, h=h, q=q, k=k, v=v, beta=beta, gc=gc, dmask=dmask, kb=kb,
                                  kbf=kbf, npow=-lower, qmat=-lower))
        for _ in range(nsq):
            for p in probs:
                nb = p["npow"].astype(BF16)
                p["npow"] = _dot(nb, nb)
            for p in probs:
                p["qmat"] = p["qmat"] + p["npow"] + _dot(p["qmat"].astype(BF16), p["npow"].astype(BF16))
        for p in probs:
            rows, h, gc = p["rows"], p["h"], p["gc"]
            cols = slice(h * LANES, (h + 1) * LANES)
            eg = jnp.exp(gc)
            rhs_u = p["v"] * p["beta"]
            rhs_w = p["kb"] * eg
            qb = p["qmat"].astype(BF16)
            qs = p["q"] * (GDN_HEAD_DIM ** -0.5)
            g_last = gc[chunk - 1:chunk, :]
            qg_ref[rows, cols] = qs * eg
            kd_ref[rows, cols] = p["k"] * jnp.exp(g_last - gc)
            u_ref[rows, cols] = rhs_u + _dot(qb, rhs_u.astype(BF16))
            w_ref[rows, cols] = rhs_w + _dot(qb, rhs_w.astype(BF16))
            at_ref[rows, h * chunk:(h + 1) * chunk] = _dot_nt(qs.astype(BF16), p["kbf"]) * p["dmask"]
            el_ref[pl.ds(pl.multiple_of(p["c"] * SUBLANES, SUBLANES), SUBLANES), cols] = jnp.broadcast_to(
                jnp.exp(g_last), (SUBLANES, LANES))
        return carry

    lax.fori_loop(0, tblk // (chunk * cpi), step, 0)


def _gdn_scan_kernel(qg_ref, kd_ref, u_ref, w_ref, at_ref, el_ref, z_ref, s0_ref, nw_ref,
                     o_ref, so_ref, s_scr, *, chunk):
    tb = pl.program_id(1)

    @pl.when(tb == 0)
    def _():
        s_scr[...] = s0_ref[...]

    nw = nw_ref[...]
    probs = [(b, h) for b in range(qg_ref.shape[0]) for h in range(GDN_HEADS)]

    def step(c, carry):
        rows = pl.ds(pl.multiple_of(c * chunk, chunk), chunk)
        erow = pl.ds(pl.multiple_of(c * SUBLANES, SUBLANES), SUBLANES)
        cols = lambda h: slice(h * LANES, (h + 1) * LANES)
        s_prev = [s_scr[b, h] for b, h in probs]
        sb = [s.astype(BF16) for s in s_prev]
        v_new = [u_ref[b, rows, cols(h)] - _dot(w_ref[b, rows, cols(h)].astype(BF16), sb[i])
                 for i, (b, h) in enumerate(probs)]
        vb = [v.astype(BF16) for v in v_new]
        for i, (b, h) in enumerate(probs):
            el = el_ref[b, erow, cols(h)][0:1, :]
            s_scr[b, h] = s_prev[i] * el + _dot_tn(kd_ref[b, rows, cols(h)].astype(BF16), vb[i])
        for i, (b, h) in enumerate(probs):
            attn = at_ref[b, rows, h * chunk:(h + 1) * chunk].astype(BF16)
            o = _dot(qg_ref[b, rows, cols(h)].astype(BF16), sb[i]) + _dot(attn, vb[i])
            ms = jnp.mean(o * o, axis=-1, keepdims=True)
            o_ref[b, rows, cols(h)] = o * lax.rsqrt(ms + NORM_EPS) * nw * _silu(z_ref[b, rows, cols(h)])
        return carry

    lax.fori_loop(0, qg_ref.shape[1] // chunk, step, 0)

    @pl.when(tb == pl.num_programs(1) - 1)
    def _():
        so_ref[...] = s_scr[...]


def _gated_delta(cv, z, ba, conv_hist, state, conv_w, a_log, dt_bias, gdn_norm, nb, t):
    chunk = CHUNK if t % CHUNK == 0 else t
    tblk = min(t, 512)
    nhist = CONV_WIDTH - 1
    nchunk_blk = tblk // chunk
    hist_pad = jnp.concatenate([jnp.zeros((nb, SUBLANES - nhist, CONV_DIM), F32), conv_hist], axis=1)
    cw_pad = jnp.concatenate([conv_w, jnp.zeros((SUBLANES - CONV_WIDTH, CONV_DIM), F32)], axis=0)
    ab = jnp.zeros((SUBLANES, LANES), F32)
    ab = ab.at[0, GDN_HEADS:2 * GDN_HEADS].set(a_log).at[1, GDN_HEADS:2 * GDN_HEADS].set(dt_bias)
    cv3 = cv.reshape(nb, t, CONV_DIM)
    tok = lambda w: pl.BlockSpec((None, tblk, w), lambda b, i: (b, i, 0))
    full = lambda a: pl.BlockSpec(a.shape, lambda b, i: (0,) * a.ndim)
    halo = pl.BlockSpec((None, SUBLANES, CONV_DIM),
                        lambda b, i: (b, jnp.maximum(i * (tblk // SUBLANES) - 1, 0), 0))
    el_spec = pl.BlockSpec((None, nchunk_blk * SUBLANES, GDN_W), lambda b, i: (b, i, 0))
    tok_shape = lambda w: jax.ShapeDtypeStruct((nb, t, w), F32)
    el_shape = jax.ShapeDtypeStruct((nb, (t // chunk) * SUBLANES, GDN_W), F32)
    qg, kd, u, w, attn, el = pl.pallas_call(
        functools.partial(_gdn_prep_kernel, chunk=chunk, cpi=min(GDN_PREP_CHUNKS_PER_ITER, nchunk_blk)),
        grid=(nb, t // tblk),
        in_specs=[tok(CONV_DIM), halo, pl.BlockSpec((None, SUBLANES, CONV_DIM), lambda b, i: (b, 0, 0)),
                  tok(LANES), full(cw_pad), full(ab)],
        out_specs=[tok(GDN_W), tok(GDN_W), tok(GDN_W), tok(GDN_W), tok(GDN_HEADS * chunk), el_spec],
        out_shape=[tok_shape(GDN_W)] * 4 + [tok_shape(GDN_HEADS * chunk), el_shape],
        scratch_shapes=[pltpu.VMEM((tblk + SUBLANES, CONV_DIM), F32), pltpu.VMEM((tblk, CONV_DIM), F32),
                        pltpu.VMEM((tblk, LANES), F32), pltpu.VMEM((tblk, LANES), F32)],
        compiler_params=_cparams("parallel", "parallel"),
        name="gdn_prep",
    )(cv3, cv3, hist_pad, ba.reshape(nb, t, LANES), cw_pad, ab)

    bblk = _seqs_per_step(nb, t, GDN_SCAN_SEQS_PER_STEP)
    stok = lambda w: pl.BlockSpec((bblk, tblk, w), lambda b, i: (b, i, 0))
    sel_spec = pl.BlockSpec((bblk, nchunk_blk * SUBLANES, GDN_W), lambda b, i: (b, i, 0))
    st = pl.BlockSpec((bblk, GDN_HEADS, LANES, LANES), lambda b, i: (b, 0, 0, 0))
    nw = gdn_norm.reshape(1, LANES)
    o, s_new = pl.pallas_call(
        functools.partial(_gdn_scan_kernel, chunk=chunk),
        grid=(nb // bblk, t // tblk),
        in_specs=[stok(GDN_W), stok(GDN_W), stok(GDN_W), stok(GDN_W), stok(GDN_HEADS * chunk), sel_spec,
                  stok(GDN_W), st, full(nw)],
        out_specs=[stok(GDN_W), st],
        out_shape=[tok_shape(GDN_W), jax.ShapeDtypeStruct((nb, GDN_HEADS, LANES, LANES), F32)],
        scratch_shapes=[pltpu.VMEM((bblk, GDN_HEADS, LANES, LANES), F32)],
        compiler_params=_cparams("parallel", "arbitrary"),
        name="gdn_scan",
    )(qg, kd, u, w, attn, el, z.reshape(nb, t, GDN_W), state, nw)
    conv_new = jnp.concatenate([conv_hist, cv3], axis=1)[:, -nhist:]
    return o.reshape(nb * t, GDN_W), s_new, conv_new


FFN_TILE = 256


def _out_ffn_kernel(*refs, n_parts, final):
    x_ref = refs[0]
    part_refs = refs[1:1 + n_parts]
    (orr_ref, oc_ref, g1_ref, sh2_ref, sc2_ref, g2_ref, nw_ref, wo_ref, wg_ref, wu_ref, wd_ref, fn_ref,
     out_ref) = refs[1 + n_parts:]
    if n_parts == 1:
        oa = part_refs[0][...]
    else:
        outs, lses = part_refs[0::2], part_refs[1::2]
        mx = lses[0][...]
        for l_ref in lses[1:]:
            mx = jnp.maximum(mx, l_ref[...])
        num, den = None, None
        for o_ref, l_ref in zip(outs, lses):
            e = jnp.exp(l_ref[...] - mx)
            num = e * o_ref[...] if num is None else num + e * o_ref[...]
            den = e if den is None else den + e
        oa = num / den
    mix = (_dot(oa.astype(BF16), wo_ref[0:DSA_W, :])
           + _dot(orr_ref[...].astype(BF16), wo_ref[DSA_W:DSA_W + RET_W, :])
           + _dot(oc_ref[...].astype(BF16), wo_ref[DSA_W + RET_W:, :]))
    x1 = x_ref[...] + g1_ref[...] * mix
    hb = _rms_mod(x1, nw_ref[...], sc2_ref[...], sh2_ref[...]).astype(BF16)
    acc = None
    for j in range(wg_ref.shape[1] // FFN_TILE):
        cols = slice(j * FFN_TILE, (j + 1) * FFN_TILE)
        act = (_silu(_dot(hb, wg_ref[:, cols])) * _dot(hb, wu_ref[:, cols])).astype(BF16)
        down = _dot(act, wd_ref[cols, :])
        acc = down if acc is None else acc + down
    x2 = x1 + g2_ref[...] * acc
    if final:
        ms = jnp.mean(x2 * x2, axis=-1, keepdims=True)
        x2 = x2 * lax.rsqrt(ms + NORM_EPS) * fn_ref[...]
    out_ref[...] = x2


def _out_ffn(x2, t, parts, orr, oc, mod3, norm_w, w_out_b, wg_b, wu_b, wd_b, final_norm, final, tm):
    m, d = x2.shape
    row = lambda w: pl.BlockSpec((tm, w), lambda i: (i, 0))
    const = lambda a: pl.BlockSpec(a.shape, lambda i: (0,) * a.ndim)
    nw = norm_w.reshape(1, d)
    fn = final_norm.reshape(1, d)
    return pl.pallas_call(
        functools.partial(_out_ffn_kernel, n_parts=len(parts), final=final),
        grid=(m // tm,),
        in_specs=[row(d)] + [row(DSA_W)] * len(parts) + [row(RET_W), row(GDN_W),
                  _mod_spec(tm, t, d, 2), _mod_spec(tm, t, d, 3), _mod_spec(tm, t, d, 4), _mod_spec(tm, t, d, 5),
                  const(nw), const(w_out_b), const(wg_b), const(wu_b), const(wd_b), const(fn)],
        out_specs=row(d),
        out_shape=jax.ShapeDtypeStruct((m, d), F32),
        compiler_params=_cparams("parallel"),
        name="out_ffn",
    )(x2, *parts, orr, oc, mod3, mod3, mod3, mod3, nw, w_out_b, wg_b, wu_b, wd_b, fn)


def _rope_tables(pos, inv_freq, reps):
    ang = pos.astype(F32)[:, None] * inv_freq[None, :]
    cos, sin = jnp.cos(ang), jnp.sin(ang)
    return (jnp.tile(jnp.concatenate([cos, cos], axis=-1), (1, reps)),
            jnp.tile(jnp.concatenate([-sin, sin], axis=-1), (1, reps)))


def _trunk(x, modp, pos, k_hist, v_hist, s_ret, s_gdn, conv_hist, wts):
    (norm_mix, norm_ffn, w_in_b, ret_norm, conv_w, a_log, dt_bias, gdn_norm, w_out_b, wg_b, wu_b, wd_b,
     final_norm) = wts
    nb, t, d = x.shape
    m = nb * t
    depth = w_in_b.shape[0]
    tm = min(256, m)
    inv_a = 1.0 / (ROPE_THETA ** (jnp.arange(0, DSA_HEAD_DIM, 2, dtype=F32) / DSA_HEAD_DIM))
    inv_r = 1.0 / (10000.0 ** jnp.linspace(0.0, 1.0, RET_HEAD_DIM // 2, dtype=F32))
    tabs = _rope_tables(pos, inv_a, DSA_HEADS) + _rope_tables(pos, inv_r, RET_HEADS)
    if t < tm:
        tabs = tuple(jnp.tile(a, (tm // t, 1)) for a in tabs)
    x2 = x.reshape(m, d)
    ks, vs, rs, gs, cs = [], [], [], [], []
    for l in range(depth):
        if t >= tm:
            mod3 = modp[l].reshape(nb, 1, 6 * d)
        else:
            mod3 = jnp.repeat(modp[l], t, axis=0).reshape(m // tm, tm, 6 * d)
        qa, ka, va, qr, kr, vr, gr, cv, z, ba = _inproj(x2, t, mod3, norm_mix[l], w_in_b[l], tabs, tm)
        if k_hist is None:
            parts = []
            for _, dil in DSA_PATTERNS:
                parts.extend(_dsa_pattern(qa, ka, va, nb, t, dil))
        else:
            parts = [_dsa_step(qa, ka, va, k_hist, v_hist, l, nb, t)]
        orr, sr = _retention(qr, kr, vr, gr, _to_block_diag(s_ret[l]), ret_norm[l], nb, t)
        oc, sg, cvn = _gated_delta(cv, z, ba, conv_hist[l], s_gdn[l], conv_w[l], a_log[l], dt_bias[l],
                                   gdn_norm[l], nb, t)
        x2 = _out_ffn(x2, t, parts, orr, oc, mod3, norm_ffn[l], w_out_b[l], wg_b[l], wu_b[l], wd_b[l],
                      final_norm, l == depth - 1, tm)
        ks.append(ka.reshape(nb, t, DSA_HEADS, DSA_HEAD_DIM))
        vs.append(va.reshape(nb, t, DSA_HEADS, DSA_HEAD_DIM))
        rs.append(_from_block_diag(sr))
        gs.append(sg)
        cs.append(cvn)
    return (x2.reshape(nb, t, d), jnp.stack(ks), jnp.stack(vs), jnp.stack(rs), jnp.stack(gs), jnp.stack(cs))


def kernel(x_prompt, x_sample, cache_win_k, cache_win_v, state_ret, state_gdn, state_conv, c_prompt, c_sample, ada_w, ada_b, norm_mix, norm_ffn, w_in, ret_norm, conv_w, a_log, dt_bias, gdn_norm, w_out, w_gate, w_up, w_down, final_norm):
    nb, t_p, d = x_prompt.shape
    db, t_s, _ = x_sample.shape
    depth = ada_w.shape[0]
    rows = nb + db
    rows_pad = -(-rows // SUBLANES) * SUBLANES
    c_all = jnp.concatenate([c_prompt, c_sample, jnp.zeros((rows_pad - rows, d), F32)], axis=0)
    mod = _modulation(c_all, ada_w, ada_b)
    w_in_b = jnp.pad(w_in, ((0, 0), (0, 0), (0, IN_COLS_PAD - IN_COLS))).astype(BF16)
    wts = (norm_mix, norm_ffn, w_in_b, ret_norm, conv_w, a_log, dt_bias, gdn_norm, w_out.astype(BF16),
           w_gate.astype(BF16), w_up.astype(BF16), w_down.astype(BF16), final_norm)

    zr = jnp.zeros((depth, nb, RET_HEADS, RET_HEAD_DIM, RET_HEAD_DIM), F32)
    zg = jnp.zeros((depth, nb, GDN_HEADS, GDN_HEAD_DIM, GDN_HEAD_DIM), F32)
    zc = jnp.zeros((depth, nb, CONV_WIDTH - 1, CONV_DIM), F32)
    y_p, kp, vp, rp, gp, cp = _trunk(x_prompt, mod[:, :nb], jnp.arange(t_p, dtype=jnp.int32),
                                     None, None, zr, zg, zc, wts)
    wb = cache_win_k.shape[2]
    y_s, ks, vs, rs, gs, cs = _trunk(x_sample, mod[:, nb:rows], PAST_LEN + jnp.arange(t_s, dtype=jnp.int32),
                                     cache_win_k.reshape(depth, db, wb, DSA_W).transpose(0, 1, 3, 2),
                                     cache_win_v.reshape(depth, db, wb, DSA_W).transpose(0, 1, 3, 2),
                                     state_ret, state_gdn, state_conv, wts)
    return (y_p, y_s, kp[:, :, -DSA_MAX_WINDOW:], vp[:, :, -DSA_MAX_WINDOW:], rp, gp, cp, ks, vs, rs, gs, cs)
```

```python
import functools
import math

import jax
import jax.numpy as jnp
from jax import lax
from jax.experimental import pallas as pl
from jax.experimental.pallas import tpu as pltpu

F32 = jnp.float32
BF16 = jnp.bfloat16
HIGHEST = lax.Precision.HIGHEST

DSA_HEAD_DIM = 64
DSA_HEADS = 4
DSA_PATTERNS = ((128, 1), (512, 4), (2048, 16))
DSA_MAX_WINDOW = 2048
ROPE_THETA = 10000.0
RET_HEAD_DIM = 64
RET_HEADS = 4
GDN_HEAD_DIM = 128
GDN_HEADS = 4
CONV_WIDTH = 4
CHUNK = 64
NORM_EPS = 1e-6
PAST_LEN = 16384

DSA_W = DSA_HEADS * DSA_HEAD_DIM
RET_W = RET_HEADS * RET_HEAD_DIM
GDN_W = GDN_HEADS * GDN_HEAD_DIM
CONV_DIM = 3 * GDN_W
LANES = 128
SUBLANES = 8
VMEM_LIMIT = 56 * 1024 * 1024

_C_QA, _C_KA, _C_VA = 0, DSA_W, 2 * DSA_W
_C_QR = 3 * DSA_W
_C_KR, _C_VR, _C_GR = _C_QR + RET_W, _C_QR + 2 * RET_W, _C_QR + 3 * RET_W
_C_CV = _C_QR + 4 * RET_W
_C_Z = _C_CV + CONV_DIM
_C_BA = _C_Z + GDN_W
IN_COLS = _C_BA + 2 * GDN_HEADS
IN_COLS_PAD = _C_BA + LANES


def _cparams(*sem):
    return pltpu.CompilerParams(dimension_semantics=sem, vmem_limit_bytes=VMEM_LIMIT)


def _dot(a, b, **kw):
    return jnp.dot(a, b, preferred_element_type=F32, **kw)


def _dot_nt(a, b, **kw):
    return lax.dot_general(a, b, (((1,), (1,)), ((), ())), preferred_element_type=F32, **kw)


def _dot_tn(a, b, **kw):
    return lax.dot_general(a, b, (((0,), (0,)), ((), ())), preferred_element_type=F32, **kw)


def _silu(x):
    return x * jax.nn.sigmoid(x)


def _seqs_per_step(nb, t, base):
    return math.gcd(nb, base * (4 if t < CHUNK else 1))


def _mod_kernel(c_ref, w_ref, b_ref, o_ref):
    a = _silu(c_ref[...]).astype(BF16)
    o_ref[...] = _dot(a, w_ref[...].astype(BF16)) + b_ref[...]


def _modulation(c_all, ada_w, ada_b, tn=1536):
    depth, d, n = ada_w.shape
    bp = c_all.shape[0]
    return pl.pallas_call(
        _mod_kernel,
        grid=(depth, n // tn),
        in_specs=[
            pl.BlockSpec((bp, d), lambda l, j: (0, 0)),
            pl.BlockSpec((None, d, tn), lambda l, j: (l, 0, j)),
            pl.BlockSpec((None, 1, tn), lambda l, j: (l, 0, j)),
        ],
        out_specs=pl.BlockSpec((None, bp, tn), lambda l, j: (l, 0, j)),
        out_shape=jax.ShapeDtypeStruct((depth, bp, n), F32),
        compiler_params=_cparams("parallel", "parallel"),
        name="modulation",
    )(c_all, ada_w, ada_b.reshape(depth, 1, n))


def _rms_mod(x, nw, sc, sh):
    ms = jnp.mean(x * x, axis=-1, keepdims=True)
    return (x * lax.rsqrt(ms + NORM_EPS) * nw) * (1.0 + sc) + sh


def _inproj_kernel(x_ref, nw_ref, sh_ref, sc_ref, w_ref, ca_ref, sa_ref, cr_ref, sr_ref,
                   qa_ref, ka_ref, va_ref, qr_ref, kr_ref, vr_ref, gr_ref, cv_ref, z_ref, ba_ref):
    tm = x_ref.shape[0]
    hb = _rms_mod(x_ref[...], nw_ref[...], sc_ref[...], sh_ref[...]).astype(BF16)

    def proj(c0, width):
        return _dot(hb, w_ref[:, c0:c0 + width])

    lane = lax.broadcasted_iota(jnp.int32, (tm, DSA_W), 1)
    first_half = (lane % DSA_HEAD_DIM) < (DSA_HEAD_DIM // 2)

    def rope(y, cos, sin_signed):
        partner = jnp.where(first_half, pltpu.roll(y, DSA_W - DSA_HEAD_DIM // 2, 1),
                            pltpu.roll(y, DSA_HEAD_DIM // 2, 1))
        return y * cos + partner * sin_signed

    ca, sa, cr, sr = ca_ref[...], sa_ref[...], cr_ref[...], sr_ref[...]
    qa_ref[...] = rope(proj(_C_QA, DSA_W), ca, sa) * (DSA_HEAD_DIM ** -0.5)
    ka_ref[...] = rope(proj(_C_KA, DSA_W), ca, sa)
    va_ref[...] = proj(_C_VA, DSA_W)
    qr_ref[...] = rope(proj(_C_QR, RET_W), cr, sr)
    kr_ref[...] = rope(proj(_C_KR, RET_W), cr, sr) * (RET_HEAD_DIM ** -0.5)
    vr_ref[...] = proj(_C_VR, RET_W)
    gr_ref[...] = proj(_C_GR, RET_W)
    for s in range(CONV_DIM // GDN_W):
        cv_ref[:, s * GDN_W:(s + 1) * GDN_W] = proj(_C_CV + s * GDN_W, GDN_W)
    z_ref[...] = proj(_C_Z, GDN_W)
    ba_ref[...] = proj(_C_BA, LANES)


def _mod_spec(tm, t, d, col):
    if t >= tm:
        return pl.BlockSpec((None, 1, d), lambda i: ((i * tm) // t, 0, col))
    return pl.BlockSpec((None, tm, d), lambda i: (i, 0, col))


def _inproj(x2, t, mod3, norm_w, w_in_b, tabs, tm):
    m, d = x2.shape
    nt = tabs[0].shape[0] // tm
    widths = (DSA_W,) * 3 + (RET_W,) * 4 + (CONV_DIM, GDN_W, LANES)
    tab_spec = pl.BlockSpec((tm, DSA_W), lambda i: (i % nt, 0))
    return pl.pallas_call(
        _inproj_kernel,
        grid=(m // tm,),
        in_specs=[
            pl.BlockSpec((tm, d), lambda i: (i, 0)),
            pl.BlockSpec((1, d), lambda i: (0, 0)),
            _mod_spec(tm, t, d, 0),
            _mod_spec(tm, t, d, 1),
            pl.BlockSpec((d, IN_COLS_PAD), lambda i: (0, 0)),
            tab_spec, tab_spec, tab_spec, tab_spec,
        ],
        out_specs=[pl.BlockSpec((tm, w), lambda i: (i, 0)) for w in widths],
        out_shape=[jax.ShapeDtypeStruct((m, w), F32) for w in widths],
        compiler_params=_cparams("parallel"),
        name="inproj",
    )(x2, norm_w.reshape(1, d), mod3, mod3, w_in_b, *tabs)


DSA_BLK = 128


DSA_QBLK = DSA_MAX_WINDOW
DSA_GROUP = 4


def _dsa_kernel(q_ref, kp_ref, kc_ref, vp_ref, vc_ref, o_ref, acc_scr, m_scr, l_scr):
    blk = pl.program_id(2)
    qblk = q_ref.shape[0]
    row = lax.broadcasted_iota(jnp.int32, (DSA_BLK, DSA_BLK), 0)
    col = lax.broadcasted_iota(jnp.int32, (DSA_BLK, DSA_BLK), 1)
    diag_ok = col <= row
    band_ok = col >= row
    band_ok_first = jnp.logical_and(band_ok, blk > 0)
    lo = lax.broadcasted_iota(jnp.int32, (DSA_BLK, LANES), 1) < DSA_HEAD_DIM
    neg = -jnp.inf
    full = lambda x: jnp.broadcast_to(x, (DSA_BLK, LANES))
    both = lambda pair: jnp.where(lo, pair[0], pair[1])

    def rows_of(dil, ph, j):
        start = ph + dil * DSA_BLK * j
        return pl.ds(start, DSA_BLK) if dil == 1 else pl.ds(start, DSA_BLK, stride=dil)

    def group_softmax(dil, subs):
        n = range(len(subs))
        heads = range(2)
        idx = [rows_of(dil, ph, j) for ph, j in subs]
        q = [q_ref[i, :] for i in idx]
        k_d = [kc_ref[i, :].astype(BF16) for i in idx]
        v_d = [vc_ref[i, :].astype(BF16) for i in idx]
        k_b, v_b, ok_b = [], [], []
        for ph, j in subs:
            if j == 0:
                i = pl.ds(qblk - dil * DSA_BLK + ph, DSA_BLK) if dil == 1 else pl.ds(
                    qblk - dil * DSA_BLK + ph, DSA_BLK, stride=dil)
                k_b.append(kp_ref[i, :].astype(BF16))
                v_b.append(vp_ref[i, :].astype(BF16))
                ok_b.append(band_ok_first)
            else:
                i = rows_of(dil, ph, j - 1)
                k_b.append(kc_ref[i, :].astype(BF16))
                v_b.append(vc_ref[i, :].astype(BF16))
                ok_b.append(band_ok)
        qm = [[jnp.where(lo, q[u], 0.0).astype(BF16), jnp.where(lo, 0.0, q[u]).astype(BF16)] for u in n]
        s_b = [[jnp.where(ok_b[u], _dot_nt(qm[u][h], k_b[u]), neg) for h in heads] for u in n]
        s_d = [[jnp.where(diag_ok, _dot_nt(qm[u][h], k_d[u]), neg) for h in heads] for u in n]
        mx = [[jnp.max(jnp.maximum(s_b[u][h], s_d[u][h]), axis=-1, keepdims=True) for h in heads] for u in n]
        p_b = [[jnp.exp(s_b[u][h] - mx[u][h]) for h in heads] for u in n]
        p_d = [[jnp.exp(s_d[u][h] - mx[u][h]) for h in heads] for u in n]
        den = [[jnp.sum(p_b[u][h] + p_d[u][h], axis=-1, keepdims=True) for h in heads] for u in n]
        pv = [[_dot(p_b[u][h].astype(BF16), v_b[u]) + _dot(p_d[u][h].astype(BF16), v_d[u]) for h in heads]
              for u in n]
        return idx, [(both(pv[u]), both([full(mx[u][0]), full(mx[u][1])]),
                      both([full(den[u][0]), full(den[u][1])])) for u in n]

    dils = sorted((d for _, d in DSA_PATTERNS), reverse=True)
    for pi, dil in enumerate(dils):
        subs = [(ph, j) for j in range(qblk // (dil * DSA_BLK)) for ph in range(dil)]
        for g0 in range(0, len(subs), DSA_GROUP):
            idx, tiles = group_softmax(dil, subs[g0:g0 + DSA_GROUP])
            for i, (pv, mx, den) in zip(idx, tiles):
                if pi > 0:
                    m_old = m_scr[i, :]
                    m_new = jnp.maximum(m_old, mx)
                    w_old = jnp.exp(m_old - m_new)
                    w_cur = jnp.exp(mx - m_new)
                    pv = acc_scr[i, :] * w_old + pv * w_cur
                    den = l_scr[i, :] * w_old + den * w_cur
                    mx = m_new
                if pi < len(dils) - 1:
                    acc_scr[i, :] = pv
                    m_scr[i, :] = mx
                    l_scr[i, :] = den
                else:
                    o_ref[i, :] = pv / den


def _dsa_prompt(q, k, v, nb, t):
    assert all(w // d == DSA_BLK for w, d in DSA_PATTERNS) and t % DSA_QBLK == 0
    ngrp = DSA_W // LANES
    r3 = lambda a: a.reshape(nb, t, DSA_W)
    cur = pl.BlockSpec((None, DSA_QBLK, LANES), lambda b, g, i: (b, i, g))
    prv = pl.BlockSpec((None, DSA_QBLK, LANES), lambda b, g, i: (b, jnp.maximum(i - 1, 0), g))
    return pl.pallas_call(
        _dsa_kernel,
        grid=(nb, ngrp, t // DSA_QBLK),
        in_specs=[cur, prv, cur, prv, cur],
        out_specs=cur,
        out_shape=jax.ShapeDtypeStruct((nb, t, DSA_W), F32),
        scratch_shapes=[pltpu.VMEM((DSA_QBLK, LANES), F32)] * 3,
        compiler_params=_cparams("parallel", "parallel", "parallel"),
        name="dsa_prompt",
    )(r3(q), r3(k), r3(k), r3(v), r3(v)).reshape(nb * t, DSA_W)


def _multiplicity(dist):
    total = jnp.zeros(dist.shape, F32)
    for window, dil in DSA_PATTERNS:
        hit = (dist >= 0) & (dist <= window) & ((dist & (dil - 1)) == 0)
        total = total + hit.astype(F32)
    return total


def _dsa_step_kernel(q_ref, kc_ref, vc_ref, kn_ref, vn_ref, o_ref):
    tq = q_ref.shape[0]
    wb = kc_ref.shape[1]
    qi = lax.broadcasted_iota(jnp.int32, (2 * tq, wb), 0) % tq
    w_c = _multiplicity(wb + qi - lax.broadcasted_iota(jnp.int32, (2 * tq, wb), 1))
    qn = lax.broadcasted_iota(jnp.int32, (2 * tq, LANES), 0) % tq
    nn = lax.broadcasted_iota(jnp.int32, (2 * tq, LANES), 1)
    w_n = jnp.where(nn < tq, _multiplicity(qn - nn), 0.0)
    lo = lax.broadcasted_iota(jnp.int32, (tq, LANES), 1) < DSA_HEAD_DIM
    pad = jnp.zeros((LANES - tq, LANES), F32)
    neg = -jnp.inf
    for g in range(DSA_W // LANES):
        cols = slice(g * LANES, (g + 1) * LANES)
        q = q_ref[:, cols]
        q2 = jnp.concatenate([jnp.where(lo, q, 0.0), jnp.where(lo, 0.0, q)], axis=0).astype(BF16)
        kt_c, vt_c = kc_ref[cols, :].astype(BF16), vc_ref[cols, :].astype(BF16)
        k_n = jnp.concatenate([kn_ref[:, cols], pad], axis=0).astype(BF16)
        v_n = jnp.concatenate([vn_ref[:, cols], pad], axis=0).astype(BF16)
        s_c = jnp.where(w_c > 0, _dot(q2, kt_c), neg)
        s_n = jnp.where(w_n > 0, _dot_nt(q2, k_n), neg)
        mx = jnp.maximum(jnp.max(s_c, axis=-1, keepdims=True), jnp.max(s_n, axis=-1, keepdims=True))
        p_c = w_c * jnp.exp(s_c - mx)
        p_n = w_n * jnp.exp(s_n - mx)
        den = jnp.sum(p_c, axis=-1, keepdims=True) + jnp.sum(p_n, axis=-1, keepdims=True)
        o2 = (_dot_nt(p_c.astype(BF16), vt_c) + _dot(p_n.astype(BF16), v_n)) / den
        o_ref[:, cols] = jnp.where(lo, o2[:tq], o2[tq:])


def _dsa_step(q, k_new, v_new, k_cache, v_cache, layer, nb, t):
    wb = k_cache.shape[3]
    new = pl.BlockSpec((None, t, DSA_W), lambda b: (b, 0, 0))
    cache = pl.BlockSpec((None, None, DSA_W, wb), lambda b: (layer, b, 0, 0))
    r3 = lambda a: a.reshape(nb, t, DSA_W)
    return pl.pallas_call(
        _dsa_step_kernel,
        grid=(nb,),
        in_specs=[new, cache, cache, new, new],
        out_specs=new,
        out_shape=jax.ShapeDtypeStruct((nb, t, DSA_W), F32),
        compiler_params=_cparams("parallel"),
        name="dsa_step",
    )(r3(q), k_cache, v_cache, r3(k_new), r3(v_new)).reshape(nb * t, DSA_W)


RET_CHUNK = 256
RET_SEQS_PER_STEP = 2


def _ret_kernel(q_ref, k_ref, v_ref, g_ref, s0_ref, dec_ref, qd_ref, kd_ref, cd_ref, bd_ref, nw_ref,
                o_ref, so_ref, s_scr, *, chunk):
    tb = pl.program_id(1)
    ngrp = RET_W // LANES

    @pl.when(tb == 0)
    def _():
        s_scr[...] = s0_ref[...]

    lo = lax.broadcasted_iota(jnp.int32, (chunk, LANES), 1) < RET_HEAD_DIM
    nw = nw_ref[...]
    bd = bd_ref[...]

    def body(c, carry):
        rows = pl.ds(pl.multiple_of(c * chunk, chunk), chunk)
        for b in range(q_ref.shape[0]):
            for g in range(ngrp):
                cols = slice(g * LANES, (g + 1) * LANES)
                q, k, v = q_ref[b, rows, cols], k_ref[b, rows, cols], v_ref[b, rows, cols]
                kb, vb = k.astype(BF16), v.astype(BF16)
                parts = []
                for hh in range(2):
                    qm = jnp.where(lo if hh == 0 else jnp.logical_not(lo), q, 0.0).astype(BF16)
                    inner = _dot_nt(qm, kb) * dec_ref[2 * g + hh]
                    parts.append(_dot(inner.astype(BF16), vb))
                s_prev = s_scr[b, g]
                o = jnp.where(lo, parts[0], parts[1]) + _dot(q.astype(BF16), s_prev.astype(BF16)) * qd_ref[g]
                s_scr[b, g] = s_prev * cd_ref[g] + bd * _dot_tn((k * kd_ref[g]).astype(BF16), vb)
                o2 = o * o
                ms = jnp.where(lo, jnp.sum(jnp.where(lo, o2, 0.0), axis=-1, keepdims=True),
                               jnp.sum(jnp.where(lo, 0.0, o2), axis=-1, keepdims=True)) * (1.0 / RET_HEAD_DIM)
                o_ref[b, rows, cols] = o * lax.rsqrt(ms + NORM_EPS) * nw * _silu(g_ref[b, rows, cols])
        return carry

    lax.fori_loop(0, q_ref.shape[1] // chunk, body, 0)

    @pl.when(tb == pl.num_programs(1) - 1)
    def _():
        so_ref[...] = s_scr[...]


def _ret_tables(chunk):
    log_gamma = jnp.log(1.0 - 2.0 ** (-5.0 - jnp.arange(RET_HEADS, dtype=F32)))
    i = jnp.arange(chunk, dtype=F32)
    diff = i[:, None] - i[None, :]
    causal = diff >= 0
    decay = jnp.where(causal[None], jnp.exp(log_gamma[:, None, None] * jnp.where(causal, diff, 0.0)[None]), 0.0)
    per_lane = lambda a: jnp.repeat(a, RET_HEAD_DIM, axis=0).reshape(RET_W // LANES, LANES, -1)
    q_dec = per_lane(jnp.exp(log_gamma[:, None] * (i[None, :] + 1.0))).transpose(0, 2, 1)
    k_dec = per_lane(jnp.exp(log_gamma[:, None] * (chunk - 1.0 - i)[None, :])).transpose(0, 2, 1)
    c_dec = jnp.broadcast_to(per_lane(jnp.exp(log_gamma * chunk)[:, None]), (RET_W // LANES, LANES, LANES))
    head_of = jnp.arange(LANES) // RET_HEAD_DIM
    block_diag = (head_of[:, None] == head_of[None, :]).astype(F32)
    return decay, q_dec, k_dec, c_dec, block_diag


def _retention(q, k, v, gate, state_bd, ret_norm, nb, t):
    chunk = RET_CHUNK if t % RET_CHUNK == 0 else t
    tblk = min(t, 1024)
    ngrp = RET_W // LANES
    bblk = _seqs_per_step(nb, t, RET_SEQS_PER_STEP)
    decay, q_dec, k_dec, c_dec, block_diag = _ret_tables(chunk)
    r3 = lambda a: a.reshape(nb, t, RET_W)
    tok = pl.BlockSpec((bblk, tblk, RET_W), lambda b, i: (b, i, 0))
    st = pl.BlockSpec((bblk, ngrp, LANES, LANES), lambda b, i: (b, 0, 0, 0))
    full = lambda a: pl.BlockSpec(a.shape, lambda b, i: (0,) * a.ndim)
    nw = jnp.tile(ret_norm, LANES // RET_HEAD_DIM).reshape(1, LANES)
    o, s_new = pl.pallas_call(
        functools.partial(_ret_kernel, chunk=chunk),
        grid=(nb // bblk, t // tblk),
        in_specs=[tok, tok, tok, tok, st, full(decay), full(q_dec), full(k_dec), full(c_dec),
                  full(block_diag), full(nw)],
        out_specs=[tok, st],
        out_shape=[jax.ShapeDtypeStruct((nb, t, RET_W), F32),
                   jax.ShapeDtypeStruct((nb, ngrp, LANES, LANES), F32)],
        scratch_shapes=[pltpu.VMEM((bblk, ngrp, LANES, LANES), F32)],
        compiler_params=_cparams("parallel", "arbitrary"),
        name="retention",
    )(r3(q), r3(k), r3(v), r3(gate), state_bd, decay, q_dec, k_dec, c_dec, block_diag, nw)
    return o.reshape(nb * t, RET_W), s_new


def _to_block_diag(s):
    nb = s.shape[0]
    s = s.reshape(nb, 2, 2, RET_HEAD_DIM, RET_HEAD_DIM)
    z = jnp.zeros_like(s[:, :, 0])
    top = jnp.concatenate([s[:, :, 0], z], axis=-1)
    bot = jnp.concatenate([z, s[:, :, 1]], axis=-1)
    return jnp.concatenate([top, bot], axis=-2)


def _from_block_diag(s):
    h = RET_HEAD_DIM
    return jnp.stack([s[:, :, :h, :h], s[:, :, h:, h:]], axis=2).reshape(s.shape[0], RET_HEADS, h, h)


GDN_PREP_CHUNKS_PER_ITER = 2
GDN_SCAN_SEQS_PER_STEP = 2


def _softplus(x):
    return jnp.maximum(x, 0.0) + jnp.log1p(jnp.exp(-jnp.abs(x)))


def _gdn_prep_kernel(cv_ref, halo_ref, hist_ref, ba_ref, cw_ref, ab_ref,
                     qg_ref, kd_ref, u_ref, w_ref, at_ref, el_ref,
                     xp_scr, cs_scr, beta_scr, g_scr, *, chunk, cpi):
    tb = pl.program_id(1)
    tblk = cv_ref.shape[0]
    nhist = CONV_WIDTH - 1
    xp_scr[SUBLANES:, :] = cv_ref[...]

    @pl.when(tb == 0)
    def _():
        xp_scr[:SUBLANES, :] = hist_ref[...]

    @pl.when(tb > 0)
    def _():
        xp_scr[:SUBLANES, :] = halo_ref[...]

    for cg in range(CONV_DIM // LANES):
        cols = slice(cg * LANES, (cg + 1) * LANES)
        acc = None
        for i in range(CONV_WIDTH):
            start = SUBLANES - nhist + i
            term = xp_scr[start:start + tblk, cols] * cw_ref[i:i + 1, cols]
            acc = term if acc is None else acc + term
        cs_scr[:, cols] = _silu(acc)

    ba = ba_ref[...]
    ab = ab_ref[...]
    beta_scr[...] = jax.nn.sigmoid(ba)
    g_scr[...] = -jnp.exp(ab[0:1, :]) * _softplus(ba + ab[1:2, :])

    ri = lax.broadcasted_iota(jnp.int32, (chunk, chunk), 0)
    ci = lax.broadcasted_iota(jnp.int32, (chunk, chunk), 1)
    incl = ri >= ci
    strict = ri > ci
    tri = incl.astype(F32)
    lane_pick = (lax.broadcasted_iota(jnp.int32, (SUBLANES, LANES), 0)
                 == lax.broadcasted_iota(jnp.int32, (SUBLANES, LANES), 1)).astype(F32)
    base = min(SUBLANES, chunk)
    assert chunk % base == 0 and (chunk // base) & (chunk // base - 1) == 0
    same_blk = [(ri >> sh) == (ci >> sh) for sh in range(int(math.log2(base)), int(math.log2(chunk)) + 1)]

    def l2n(x):
        return x * lax.rsqrt(jnp.sum(x * x, axis=-1, keepdims=True) + NORM_EPS)

    def step(it, carry):
        probs = []
        for cc in range(cpi):
            c = it * cpi + cc
            rows = pl.ds(pl.multiple_of(c * chunk, chunk), chunk)
            beta_c = beta_scr[rows, :]
            gcol = _dot(tri, g_scr[rows, :], precision=HIGHEST)
            grow = _dot_nt(lane_pick, gcol, precision=HIGHEST)
            for h in range(GDN_HEADS):
                q = l2n(cs_scr[rows, h * LANES:(h + 1) * LANES])
                k = l2n(cs_scr[rows, GDN_W + h * LANES:GDN_W + (h + 1) * LANES])
                v = cs_scr[rows, 2 * GDN_W + h * LANES:2 * GDN_W + (h + 1) * LANES]
                beta = jnp.broadcast_to(beta_c[:, h:h + 1], (chunk, LANES))
                gc = jnp.broadcast_to(gcol[:, GDN_HEADS + h:GDN_HEADS + h + 1], (chunk, LANES))
                gdiff = gc[:, :chunk] - grow[GDN_HEADS + h:GDN_HEADS + h + 1, :]
                dmask = jnp.where(incl, jnp.exp(jnp.where(incl, gdiff, 0.0)), 0.0)
                kb = k * beta
                kbf = k.astype(BF16)
                lower = jnp.where(strict, _dot_nt(kb.astype(BF16), kbf) * dmask, 0.0)
                npow = jnp.where(same_blk[0], -lower, 0.0)
                probs.append(dict(c=c, rows=rows, h=h, q=q, k=k, v=v, beta=beta, gc=gc, dmask=dmask, kb=kb,
                                  kbf=kbf, lower=lower, npow=npow, qmat=npow))
        for _ in range(int(math.log2(base)) - 1):
            for p in probs:
                nb = p["npow"].astype(BF16)
                p["npow"] = _dot(nb, nb)
            for p in probs:
                p["qmat"] = p["qmat"] + p["npow"] + _dot(p["qmat"].astype(BF16), p["npow"].astype(BF16))
        for lvl in range(1, len(same_blk)):
            for p in probs:
                cb = jnp.where(jnp.logical_and(same_blk[lvl], jnp.logical_not(same_blk[lvl - 1])),
                               p["lower"], 0.0)
                p["x"] = cb + _dot(p["qmat"].astype(BF16), cb.astype(BF16))
            for p in probs:
                qb16 = p["qmat"].astype(BF16)
                p["qmat"] = p["qmat"] - p["x"] - _dot(p["x"].astype(BF16), qb16)
        for p in probs:
            rows, h, gc = p["rows"], p["h"], p["gc"]
            cols = slice(h * LANES, (h + 1) * LANES)
            eg = jnp.exp(gc)
            rhs_u = p["v"] * p["beta"]
            rhs_w = p["kb"] * eg
            qb = p["qmat"].astype(BF16)
            qs = p["q"] * (GDN_HEAD_DIM ** -0.5)
            g_last = gc[chunk - 1:chunk, :]
            qg_ref[rows, cols] = qs * eg
            kd_ref[rows, cols] = p["k"] * jnp.exp(g_last - gc)
            u_ref[rows, cols] = rhs_u + _dot(qb, rhs_u.astype(BF16))
            w_ref[rows, cols] = rhs_w + _dot(qb, rhs_w.astype(BF16))
            at_ref[rows, h * chunk:(h + 1) * chunk] = _dot_nt(qs.astype(BF16), p["kbf"]) * p["dmask"]
            el_ref[pl.ds(pl.multiple_of(p["c"] * SUBLANES, SUBLANES), SUBLANES), cols] = jnp.broadcast_to(
                jnp.exp(g_last), (SUBLANES, LANES))
        return carry

    lax.fori_loop(0, tblk // (chunk * cpi), step, 0)


def _gdn_scan_kernel(qg_ref, kd_ref, u_ref, w_ref, at_ref, el_ref, z_ref, s0_ref, nw_ref,
                     o_ref, so_ref, s_scr, *, chunk):
    tb = pl.program_id(1)

    @pl.when(tb == 0)
    def _():
        s_scr[...] = s0_ref[...]

    nw = nw_ref[...]
    probs = [(b, h) for b in range(qg_ref.shape[0]) for h in range(GDN_HEADS)]

    def step(c, carry):
        rows = pl.ds(pl.multiple_of(c * chunk, chunk), chunk)
        erow = pl.ds(pl.multiple_of(c * SUBLANES, SUBLANES), SUBLANES)
        cols = lambda h: slice(h * LANES, (h + 1) * LANES)
        s_prev = [s_scr[b, h] for b, h in probs]
        sb = [s.astype(BF16) for s in s_prev]
        v_new = [u_ref[b, rows, cols(h)] - _dot(w_ref[b, rows, cols(h)].astype(BF16), sb[i])
                 for i, (b, h) in enumerate(probs)]
        vb = [v.astype(BF16) for v in v_new]
        for i, (b, h) in enumerate(probs):
            el = el_ref[b, erow, cols(h)][0:1, :]
            s_scr[b, h] = s_prev[i] * el + _dot_tn(kd_ref[b, rows, cols(h)].astype(BF16), vb[i])
        for i, (b, h) in enumerate(probs):
            attn = at_ref[b, rows, h * chunk:(h + 1) * chunk].astype(BF16)
            o = _dot(qg_ref[b, rows, cols(h)].astype(BF16), sb[i]) + _dot(attn, vb[i])
            ms = jnp.mean(o * o, axis=-1, keepdims=True)
            o_ref[b, rows, cols(h)] = o * lax.rsqrt(ms + NORM_EPS) * nw * _silu(z_ref[b, rows, cols(h)])
        return carry

    lax.fori_loop(0, qg_ref.shape[1] // chunk, step, 0)

    @pl.when(tb == pl.num_programs(1) - 1)
    def _():
        so_ref[...] = s_scr[...]


def _gated_delta(cv, z, ba, conv_hist, state, conv_w, a_log, dt_bias, gdn_norm, nb, t):
    chunk = CHUNK if t % CHUNK == 0 else t
    tblk = min(t, 512)
    nhist = CONV_WIDTH - 1
    nchunk_blk = tblk // chunk
    hist_pad = jnp.concatenate([jnp.zeros((nb, SUBLANES - nhist, CONV_DIM), F32), conv_hist], axis=1)
    cw_pad = jnp.concatenate([conv_w, jnp.zeros((SUBLANES - CONV_WIDTH, CONV_DIM), F32)], axis=0)
    ab = jnp.zeros((SUBLANES, LANES), F32)
    ab = ab.at[0, GDN_HEADS:2 * GDN_HEADS].set(a_log).at[1, GDN_HEADS:2 * GDN_HEADS].set(dt_bias)
    cv3 = cv.reshape(nb, t, CONV_DIM)
    tok = lambda w: pl.BlockSpec((None, tblk, w), lambda b, i: (b, i, 0))
    full = lambda a: pl.BlockSpec(a.shape, lambda b, i: (0,) * a.ndim)
    halo = pl.BlockSpec((None, SUBLANES, CONV_DIM),
                        lambda b, i: (b, jnp.maximum(i * (tblk // SUBLANES) - 1, 0), 0))
    el_spec = pl.BlockSpec((None, nchunk_blk * SUBLANES, GDN_W), lambda b, i: (b, i, 0))
    tok_shape = lambda w: jax.ShapeDtypeStruct((nb, t, w), F32)
    el_shape = jax.ShapeDtypeStruct((nb, (t // chunk) * SUBLANES, GDN_W), F32)
    qg, kd, u, w, attn, el = pl.pallas_call(
        functools.partial(_gdn_prep_kernel, chunk=chunk, cpi=min(GDN_PREP_CHUNKS_PER_ITER, nchunk_blk)),
        grid=(nb, t // tblk),
        in_specs=[tok(CONV_DIM), halo, pl.BlockSpec((None, SUBLANES, CONV_DIM), lambda b, i: (b, 0, 0)),
                  tok(LANES), full(cw_pad), full(ab)],
        out_specs=[tok(GDN_W), tok(GDN_W), tok(GDN_W), tok(GDN_W), tok(GDN_HEADS * chunk), el_spec],
        out_shape=[tok_shape(GDN_W)] * 4 + [tok_shape(GDN_HEADS * chunk), el_shape],
        scratch_shapes=[pltpu.VMEM((tblk + SUBLANES, CONV_DIM), F32), pltpu.VMEM((tblk, CONV_DIM), F32),
                        pltpu.VMEM((tblk, LANES), F32), pltpu.VMEM((tblk, LANES), F32)],
        compiler_params=_cparams("parallel", "parallel"),
        name="gdn_prep",
    )(cv3, cv3, hist_pad, ba.reshape(nb, t, LANES), cw_pad, ab)

    bblk = _seqs_per_step(nb, t, GDN_SCAN_SEQS_PER_STEP)
    stok = lambda w: pl.BlockSpec((bblk, tblk, w), lambda b, i: (b, i, 0))
    sel_spec = pl.BlockSpec((bblk, nchunk_blk * SUBLANES, GDN_W), lambda b, i: (b, i, 0))
    st = pl.BlockSpec((bblk, GDN_HEADS, LANES, LANES), lambda b, i: (b, 0, 0, 0))
    nw = gdn_norm.reshape(1, LANES)
    o, s_new = pl.pallas_call(
        functools.partial(_gdn_scan_kernel, chunk=chunk),
        grid=(nb // bblk, t // tblk),
        in_specs=[stok(GDN_W), stok(GDN_W), stok(GDN_W), stok(GDN_W), stok(GDN_HEADS * chunk), sel_spec,
                  stok(GDN_W), st, full(nw)],
        out_specs=[stok(GDN_W), st],
        out_shape=[tok_shape(GDN_W), jax.ShapeDtypeStruct((nb, GDN_HEADS, LANES, LANES), F32)],
        scratch_shapes=[pltpu.VMEM((bblk, GDN_HEADS, LANES, LANES), F32)],
        compiler_params=_cparams("parallel", "arbitrary"),
        name="gdn_scan",
    )(qg, kd, u, w, attn, el, z.reshape(nb, t, GDN_W), state, nw)
    conv_new = jnp.concatenate([conv_hist, cv3], axis=1)[:, -nhist:]
    return o.reshape(nb * t, GDN_W), s_new, conv_new


FFN_TILE = 256


def _out_ffn_kernel(x_ref, oa_ref, orr_ref, oc_ref, g1_ref, sh2_ref, sc2_ref, g2_ref, nw_ref, wo_ref, wg_ref,
                    wu_ref, wd_ref, fn_ref, out_ref, *, final):
    mix = (_dot(oa_ref[...].astype(BF16), wo_ref[0:DSA_W, :])
           + _dot(orr_ref[...].astype(BF16), wo_ref[DSA_W:DSA_W + RET_W, :])
           + _dot(oc_ref[...].astype(BF16), wo_ref[DSA_W + RET_W:, :]))
    x1 = x_ref[...] + g1_ref[...] * mix
    hb = _rms_mod(x1, nw_ref[...], sc2_ref[...], sh2_ref[...]).astype(BF16)
    acc = None
    for j in range(wg_ref.shape[1] // FFN_TILE):
        cols = slice(j * FFN_TILE, (j + 1) * FFN_TILE)
        act = (_silu(_dot(hb, wg_ref[:, cols])) * _dot(hb, wu_ref[:, cols])).astype(BF16)
        down = _dot(act, wd_ref[cols, :])
        acc = down if acc is None else acc + down
    x2 = x1 + g2_ref[...] * acc
    if final:
        ms = jnp.mean(x2 * x2, axis=-1, keepdims=True)
        x2 = x2 * lax.rsqrt(ms + NORM_EPS) * fn_ref[...]
    out_ref[...] = x2


def _out_ffn(x2, t, oa, orr, oc, mod3, norm_w, w_out_b, wg_b, wu_b, wd_b, final_norm, final, tm):
    m, d = x2.shape
    row = lambda w: pl.BlockSpec((tm, w), lambda i: (i, 0))
    const = lambda a: pl.BlockSpec(a.shape, lambda i: (0,) * a.ndim)
    nw = norm_w.reshape(1, d)
    fn = final_norm.reshape(1, d)
    return pl.pallas_call(
        functools.partial(_out_ffn_kernel, final=final),
        grid=(m // tm,),
        in_specs=[row(d), row(DSA_W), row(RET_W), row(GDN_W),
                  _mod_spec(tm, t, d, 2), _mod_spec(tm, t, d, 3), _mod_spec(tm, t, d, 4), _mod_spec(tm, t, d, 5),
                  const(nw), const(w_out_b), const(wg_b), const(wu_b), const(wd_b), const(fn)],
        out_specs=row(d),
        out_shape=jax.ShapeDtypeStruct((m, d), F32),
        compiler_params=_cparams("parallel"),
        name="out_ffn",
    )(x2, oa, orr, oc, mod3, mod3, mod3, mod3, nw, w_out_b, wg_b, wu_b, wd_b, fn)


def _rope_tables(pos, inv_freq, reps):
    ang = pos.astype(F32)[:, None] * inv_freq[None, :]
    cos, sin = jnp.cos(ang), jnp.sin(ang)
    return (jnp.tile(jnp.concatenate([cos, cos], axis=-1), (1, reps)),
            jnp.tile(jnp.concatenate([-sin, sin], axis=-1), (1, reps)))


def _trunk(x, modp, pos, k_hist, v_hist, s_ret, s_gdn, conv_hist, wts):
    (norm_mix, norm_ffn, w_in_b, ret_norm, conv_w, a_log, dt_bias, gdn_norm, w_out_b, wg_b, wu_b, wd_b,
     final_norm) = wts
    nb, t, d = x.shape
    m = nb * t
    depth = w_in_b.shape[0]
    tm = min(256, m)
    inv_a = 1.0 / (ROPE_THETA ** (jnp.arange(0, DSA_HEAD_DIM, 2, dtype=F32) / DSA_HEAD_DIM))
    inv_r = 1.0 / (10000.0 ** jnp.linspace(0.0, 1.0, RET_HEAD_DIM // 2, dtype=F32))
    tabs = _rope_tables(pos, inv_a, DSA_HEADS) + _rope_tables(pos, inv_r, RET_HEADS)
    if t < tm:
        tabs = tuple(jnp.tile(a, (tm // t, 1)) for a in tabs)
    x2 = x.reshape(m, d)
    ks, vs, rs, gs, cs = [], [], [], [], []
    for l in range(depth):
        if t >= tm:
            mod3 = modp[l].reshape(nb, 1, 6 * d)
        else:
            mod3 = jnp.repeat(modp[l], t, axis=0).reshape(m // tm, tm, 6 * d)
        qa, ka, va, qr, kr, vr, gr, cv, z, ba = _inproj(x2, t, mod3, norm_mix[l], w_in_b[l], tabs, tm)
        if k_hist is None:
            oa = _dsa_prompt(qa, ka, va, nb, t)
        else:
            oa = _dsa_step(qa, ka, va, k_hist, v_hist, l, nb, t)
        orr, sr = _retention(qr, kr, vr, gr, _to_block_diag(s_ret[l]), ret_norm[l], nb, t)
        oc, sg, cvn = _gated_delta(cv, z, ba, conv_hist[l], s_gdn[l], conv_w[l], a_log[l], dt_bias[l],
                                   gdn_norm[l], nb, t)
        x2 = _out_ffn(x2, t, oa, orr, oc, mod3, norm_ffn[l], w_out_b[l], wg_b[l], wu_b[l], wd_b[l],
                      final_norm, l == depth - 1, tm)
        ks.append(ka.reshape(nb, t, DSA_HEADS, DSA_HEAD_DIM))
        vs.append(va.reshape(nb, t, DSA_HEADS, DSA_HEAD_DIM))
        rs.append(_from_block_diag(sr))
        gs.append(sg)
        cs.append(cvn)
    return (x2.reshape(nb, t, d), jnp.stack(ks), jnp.stack(vs), jnp.stack(rs), jnp.stack(gs), jnp.stack(cs))


def kernel(x_prompt, x_sample, cache_win_k, cache_win_v, state_ret, state_gdn, state_conv, c_prompt, c_sample, ada_w, ada_b, norm_mix, norm_ffn, w_in, ret_norm, conv_w, a_log, dt_bias, gdn_norm, w_out, w_gate, w_up, w_down, final_norm):
    nb, t_p, d = x_prompt.shape
    db, t_s, _ = x_sample.shape
    depth = ada_w.shape[0]
    rows = nb + db
    rows_pad = -(-rows // SUBLANES) * SUBLANES
    c_all = jnp.concatenate([c_prompt, c_sample, jnp.zeros((rows_pad - rows, d), F32)], axis=0)
    mod = _modulation(c_all, ada_w, ada_b)
    w_in_b = jnp.pad(w_in, ((0, 0), (0, 0), (0, IN_COLS_PAD - IN_COLS))).astype(BF16)
    wts = (norm_mix, norm_ffn, w_in_b, ret_norm, conv_w, a_log, dt_bias, gdn_norm, w_out.astype(BF16),
           w_gate.astype(BF16), w_up.astype(BF16), w_down.astype(BF16), final_norm)

    zr = jnp.zeros((depth, nb, RET_HEADS, RET_HEAD_DIM, RET_HEAD_DIM), F32)
    zg = jnp.zeros((depth, nb, GDN_HEADS, GDN_HEAD_DIM, GDN_HEAD_DIM), F32)
    zc = jnp.zeros((depth, nb, CONV_WIDTH - 1, CONV_DIM), F32)
    y_p, kp, vp, rp, gp, cp = _trunk(x_prompt, mod[:, :nb], jnp.arange(t_p, dtype=jnp.int32),
                                     None, None, zr, zg, zc, wts)
    wb = cache_win_k.shape[2]
    y_s, ks, vs, rs, gs, cs = _trunk(x_sample, mod[:, nb:rows], PAST_LEN + jnp.arange(t_s, dtype=jnp.int32),
                                     cache_win_k.reshape(depth, db, wb, DSA_W).transpose(0, 1, 3, 2),
                                     cache_win_v.reshape(depth, db, wb, DSA_W).transpose(0, 1, 3, 2),
                                     state_ret, state_gdn, state_conv, wts)
    return (y_p, y_s, kp[:, :, -DSA_MAX_WINDOW:], vp[:, :, -DSA_MAX_WINDOW:], rp, gp, cp, ks, vs, rs, gs, cs)
```

```python
import functools
import math

import jax
import jax.numpy as jnp
from jax import lax
from jax.experimental import pallas as pl
from jax.experimental.pallas import tpu as pltpu

F32 = jnp.float32
BF16 = jnp.bfloat16
HIGHEST = lax.Precision.HIGHEST

DSA_HEAD_DIM = 64
DSA_HEADS = 4
DSA_PATTERNS = ((128, 1), (512, 4), (2048, 16))
DSA_MAX_WINDOW = 2048
ROPE_THETA = 10000.0
RET_HEAD_DIM = 64
RET_HEADS = 4
GDN_HEAD_DIM = 128
GDN_HEADS = 4
CONV_WIDTH = 4
CHUNK = 64
NORM_EPS = 1e-6
PAST_LEN = 16384

DSA_W = DSA_HEADS * DSA_HEAD_DIM
RET_W = RET_HEADS * RET_HEAD_DIM
GDN_W = GDN_HEADS * GDN_HEAD_DIM
CONV_DIM = 3 * GDN_W
LANES = 128
SUBLANES = 8
VMEM_LIMIT = 56 * 1024 * 1024
TOKEN_TILE = 512

_C_QA, _C_KA, _C_VA = 0, DSA_W, 2 * DSA_W
_C_QR = 3 * DSA_W
_C_KR, _C_VR, _C_GR = _C_QR + RET_W, _C_QR + 2 * RET_W, _C_QR + 3 * RET_W
_C_CV = _C_QR + 4 * RET_W
_C_Z = _C_CV + CONV_DIM
_C_BA = _C_Z + GDN_W
IN_COLS = _C_BA + 2 * GDN_HEADS
IN_COLS_PAD = _C_BA + LANES


def _cparams(*sem):
    return pltpu.CompilerParams(dimension_semantics=sem, vmem_limit_bytes=VMEM_LIMIT)


def _dot(a, b, **kw):
    return jnp.dot(a, b, preferred_element_type=F32, **kw)


def _dot_nt(a, b, **kw):
    return lax.dot_general(a, b, (((1,), (1,)), ((), ())), preferred_element_type=F32, **kw)


def _dot_tn(a, b, **kw):
    return lax.dot_general(a, b, (((0,), (0,)), ((), ())), preferred_element_type=F32, **kw)


def _silu(x):
    return x * jax.nn.sigmoid(x)


def _seqs_per_step(nb, t, base):
    return math.gcd(nb, base * (4 if t < CHUNK else 1))


def _mod_kernel(c_ref, w_ref, b_ref, o_ref):
    a = _silu(c_ref[...]).astype(BF16)
    o_ref[...] = _dot(a, w_ref[...].astype(BF16)) + b_ref[...]


def _modulation(c_all, ada_w, ada_b, tn=1536):
    depth, d, n = ada_w.shape
    bp = c_all.shape[0]
    return pl.pallas_call(
        _mod_kernel,
        grid=(depth, n // tn),
        in_specs=[
            pl.BlockSpec((bp, d), lambda l, j: (0, 0)),
            pl.BlockSpec((None, d, tn), lambda l, j: (l, 0, j)),
            pl.BlockSpec((None, 1, tn), lambda l, j: (l, 0, j)),
        ],
        out_specs=pl.BlockSpec((None, bp, tn), lambda l, j: (l, 0, j)),
        out_shape=jax.ShapeDtypeStruct((depth, bp, n), F32),
        compiler_params=_cparams("parallel", "parallel"),
        name="modulation",
    )(c_all, ada_w, ada_b.reshape(depth, 1, n))


def _rms_mod(x, nw, sc, sh):
    ms = jnp.mean(x * x, axis=-1, keepdims=True)
    return (x * lax.rsqrt(ms + NORM_EPS) * nw) * (1.0 + sc) + sh


def _inproj_kernel(x_ref, nw_ref, sh_ref, sc_ref, w_ref, ca_ref, sa_ref, cr_ref, sr_ref,
                   qa_ref, ka_ref, va_ref, qr_ref, kr_ref, vr_ref, gr_ref, cv_ref, z_ref, ba_ref):
    tm = x_ref.shape[0]
    hb = _rms_mod(x_ref[...], nw_ref[...], sc_ref[...], sh_ref[...]).astype(BF16)

    def proj(c0, width):
        return _dot(hb, w_ref[:, c0:c0 + width])

    lane = lax.broadcasted_iota(jnp.int32, (tm, DSA_W), 1)
    first_half = (lane % DSA_HEAD_DIM) < (DSA_HEAD_DIM // 2)

    def rope(y, cos, sin_signed):
        partner = jnp.where(first_half, pltpu.roll(y, DSA_W - DSA_HEAD_DIM // 2, 1),
                            pltpu.roll(y, DSA_HEAD_DIM // 2, 1))
        return y * cos + partner * sin_signed

    ca, sa, cr, sr = ca_ref[...], sa_ref[...], cr_ref[...], sr_ref[...]
    qa_ref[...] = rope(proj(_C_QA, DSA_W), ca, sa) * (DSA_HEAD_DIM ** -0.5)
    ka_ref[...] = rope(proj(_C_KA, DSA_W), ca, sa)
    va_ref[...] = proj(_C_VA, DSA_W)
    qr_ref[...] = rope(proj(_C_QR, RET_W), cr, sr)
    kr_ref[...] = rope(proj(_C_KR, RET_W), cr, sr) * (RET_HEAD_DIM ** -0.5)
    vr_ref[...] = proj(_C_VR, RET_W)
    gr_ref[...] = proj(_C_GR, RET_W)
    for s in range(CONV_DIM // GDN_W):
        cv_ref[:, s * GDN_W:(s + 1) * GDN_W] = proj(_C_CV + s * GDN_W, GDN_W)
    z_ref[...] = proj(_C_Z, GDN_W)
    ba_ref[...] = proj(_C_BA, LANES)


def _mod_spec(tm, t, d, col):
    if t >= tm:
        return pl.BlockSpec((None, 1, d), lambda i: ((i * tm) // t, 0, col))
    return pl.BlockSpec((None, tm, d), lambda i: (i, 0, col))


def _inproj(x2, t, mod3, norm_w, w_in_b, tabs, tm):
    m, d = x2.shape
    nt = tabs[0].shape[0] // tm
    widths = (DSA_W,) * 3 + (RET_W,) * 4 + (CONV_DIM, GDN_W, LANES)
    tab_spec = pl.BlockSpec((tm, DSA_W), lambda i: (i % nt, 0))
    return pl.pallas_call(
        _inproj_kernel,
        grid=(m // tm,),
        in_specs=[
            pl.BlockSpec((tm, d), lambda i: (i, 0)),
            pl.BlockSpec((1, d), lambda i: (0, 0)),
            _mod_spec(tm, t, d, 0),
            _mod_spec(tm, t, d, 1),
            pl.BlockSpec((d, IN_COLS_PAD), lambda i: (0, 0), pipeline_mode=pl.Buffered(1)),
            tab_spec, tab_spec, tab_spec, tab_spec,
        ],
        out_specs=[pl.BlockSpec((tm, w), lambda i: (i, 0)) for w in widths],
        out_shape=[jax.ShapeDtypeStruct((m, w), F32) for w in widths],
        compiler_params=_cparams("parallel"),
        name="inproj",
    )(x2, norm_w.reshape(1, d), mod3, mod3, w_in_b, *tabs)


DSA_BLK = 128


DSA_QBLK = DSA_MAX_WINDOW
DSA_GROUP = 4


def _dsa_kernel(q_ref, kp_ref, kc_ref, vp_ref, vc_ref, o_ref, acc_scr, m_scr, l_scr):
    blk = pl.program_id(2)
    qblk = q_ref.shape[0]
    row = lax.broadcasted_iota(jnp.int32, (DSA_BLK, DSA_BLK), 0)
    col = lax.broadcasted_iota(jnp.int32, (DSA_BLK, DSA_BLK), 1)
    diag_ok = col <= row
    band_ok = col >= row
    band_ok_first = jnp.logical_and(band_ok, blk > 0)
    lo = lax.broadcasted_iota(jnp.int32, (DSA_BLK, LANES), 1) < DSA_HEAD_DIM
    neg = -jnp.inf
    full = lambda x: jnp.broadcast_to(x, (DSA_BLK, LANES))
    both = lambda pair: jnp.where(lo, pair[0], pair[1])

    def rows_of(dil, ph, j):
        start = ph + dil * DSA_BLK * j
        return pl.ds(start, DSA_BLK) if dil == 1 else pl.ds(start, DSA_BLK, stride=dil)

    def group_softmax(dil, subs):
        n = range(len(subs))
        heads = range(2)
        idx = [rows_of(dil, ph, j) for ph, j in subs]
        q = [q_ref[i, :] for i in idx]
        k_d = [kc_ref[i, :].astype(BF16) for i in idx]
        v_d = [vc_ref[i, :].astype(BF16) for i in idx]
        k_b, v_b, ok_b = [], [], []
        for ph, j in subs:
            if j == 0:
                i = pl.ds(qblk - dil * DSA_BLK + ph, DSA_BLK) if dil == 1 else pl.ds(
                    qblk - dil * DSA_BLK + ph, DSA_BLK, stride=dil)
                k_b.append(kp_ref[i, :].astype(BF16))
                v_b.append(vp_ref[i, :].astype(BF16))
                ok_b.append(band_ok_first)
            else:
                i = rows_of(dil, ph, j - 1)
                k_b.append(kc_ref[i, :].astype(BF16))
                v_b.append(vc_ref[i, :].astype(BF16))
                ok_b.append(band_ok)
        qm = [[jnp.where(lo, q[u], 0.0).astype(BF16), jnp.where(lo, 0.0, q[u]).astype(BF16)] for u in n]
        s_b = [[jnp.where(ok_b[u], _dot_nt(qm[u][h], k_b[u]), neg) for h in heads] for u in n]
        s_d = [[jnp.where(diag_ok, _dot_nt(qm[u][h], k_d[u]), neg) for h in heads] for u in n]
        mx = [[jnp.max(jnp.maximum(s_b[u][h], s_d[u][h]), axis=-1, keepdims=True) for h in heads] for u in n]
        p_b = [[jnp.exp(s_b[u][h] - mx[u][h]) for h in heads] for u in n]
        p_d = [[jnp.exp(s_d[u][h] - mx[u][h]) for h in heads] for u in n]
        den = [[jnp.sum(p_b[u][h] + p_d[u][h], axis=-1, keepdims=True) for h in heads] for u in n]
        pv = [[_dot(p_b[u][h].astype(BF16), v_b[u]) + _dot(p_d[u][h].astype(BF16), v_d[u]) for h in heads]
              for u in n]
        return idx, [(both(pv[u]), both([full(mx[u][0]), full(mx[u][1])]),
                      both([full(den[u][0]), full(den[u][1])])) for u in n]

    dils = sorted((d for _, d in DSA_PATTERNS), reverse=True)
    for pi, dil in enumerate(dils):
        subs = [(ph, j) for j in range(qblk // (dil * DSA_BLK)) for ph in range(dil)]
        for g0 in range(0, len(subs), DSA_GROUP):
            idx, tiles = group_softmax(dil, subs[g0:g0 + DSA_GROUP])
            for i, (pv, mx, den) in zip(idx, tiles):
                if pi > 0:
                    m_old = m_scr[i, :]
                    m_new = jnp.maximum(m_old, mx)
                    w_old = jnp.exp(m_old - m_new)
                    w_cur = jnp.exp(mx - m_new)
                    pv = acc_scr[i, :] * w_old + pv * w_cur
                    den = l_scr[i, :] * w_old + den * w_cur
                    mx = m_new
                if pi < len(dils) - 1:
                    acc_scr[i, :] = pv
                    m_scr[i, :] = mx
                    l_scr[i, :] = den
                else:
                    o_ref[i, :] = pv / den


def _dsa_prompt(q, k, v, nb, t):
    assert all(w // d == DSA_BLK for w, d in DSA_PATTERNS) and t % DSA_QBLK == 0
    ngrp = DSA_W // LANES
    r3 = lambda a: a.reshape(nb, t, DSA_W)
    cur = pl.BlockSpec((None, DSA_QBLK, LANES), lambda b, g, i: (b, i, g))
    prv = pl.BlockSpec((None, DSA_QBLK, LANES), lambda b, g, i: (b, jnp.maximum(i - 1, 0), g))
    return pl.pallas_call(
        _dsa_kernel,
        grid=(nb, ngrp, t // DSA_QBLK),
        in_specs=[cur, prv, cur, prv, cur],
        out_specs=cur,
        out_shape=jax.ShapeDtypeStruct((nb, t, DSA_W), F32),
        scratch_shapes=[pltpu.VMEM((DSA_QBLK, LANES), F32)] * 3,
        compiler_params=_cparams("parallel", "parallel", "parallel"),
        name="dsa_prompt",
    )(r3(q), r3(k), r3(k), r3(v), r3(v)).reshape(nb * t, DSA_W)


def _multiplicity(dist):
    total = jnp.zeros(dist.shape, F32)
    for window, dil in DSA_PATTERNS:
        hit = (dist >= 0) & (dist <= window) & ((dist & (dil - 1)) == 0)
        total = total + hit.astype(F32)
    return total


def _dsa_step_kernel(q_ref, kc_ref, vc_ref, kn_ref, vn_ref, o_ref):
    tq = q_ref.shape[0]
    wb = kc_ref.shape[1]
    qi = lax.broadcasted_iota(jnp.int32, (2 * tq, wb), 0) % tq
    w_c = _multiplicity(wb + qi - lax.broadcasted_iota(jnp.int32, (2 * tq, wb), 1))
    qn = lax.broadcasted_iota(jnp.int32, (2 * tq, LANES), 0) % tq
    nn = lax.broadcasted_iota(jnp.int32, (2 * tq, LANES), 1)
    w_n = jnp.where(nn < tq, _multiplicity(qn - nn), 0.0)
    lo = lax.broadcasted_iota(jnp.int32, (tq, LANES), 1) < DSA_HEAD_DIM
    pad = jnp.zeros((LANES - tq, LANES), F32)
    neg = -jnp.inf
    for g in range(DSA_W // LANES):
        cols = slice(g * LANES, (g + 1) * LANES)
        q = q_ref[:, cols]
        q2 = jnp.concatenate([jnp.where(lo, q, 0.0), jnp.where(lo, 0.0, q)], axis=0).astype(BF16)
        kt_c, vt_c = kc_ref[cols, :].astype(BF16), vc_ref[cols, :].astype(BF16)
        k_n = jnp.concatenate([kn_ref[:, cols], pad], axis=0).astype(BF16)
        v_n = jnp.concatenate([vn_ref[:, cols], pad], axis=0).astype(BF16)
        s_c = jnp.where(w_c > 0, _dot(q2, kt_c), neg)
        s_n = jnp.where(w_n > 0, _dot_nt(q2, k_n), neg)
        mx = jnp.maximum(jnp.max(s_c, axis=-1, keepdims=True), jnp.max(s_n, axis=-1, keepdims=True))
        p_c = w_c * jnp.exp(s_c - mx)
        p_n = w_n * jnp.exp(s_n - mx)
        den = jnp.sum(p_c, axis=-1, keepdims=True) + jnp.sum(p_n, axis=-1, keepdims=True)
        o2 = (_dot_nt(p_c.astype(BF16), vt_c) + _dot(p_n.astype(BF16), v_n)) / den
        o_ref[:, cols] = jnp.where(lo, o2[:tq], o2[tq:])


def _dsa_step(q, k_new, v_new, k_cache, v_cache, layer, nb, t):
    wb = k_cache.shape[3]
    new = pl.BlockSpec((None, t, DSA_W), lambda b: (b, 0, 0))
    cache = pl.BlockSpec((None, None, DSA_W, wb), lambda b: (layer, b, 0, 0))
    r3 = lambda a: a.reshape(nb, t, DSA_W)
    return pl.pallas_call(
        _dsa_step_kernel,
        grid=(nb,),
        in_specs=[new, cache, cache, new, new],
        out_specs=new,
        out_shape=jax.ShapeDtypeStruct((nb, t, DSA_W), F32),
        compiler_params=_cparams("parallel"),
        name="dsa_step",
    )(r3(q), k_cache, v_cache, r3(k_new), r3(v_new)).reshape(nb * t, DSA_W)


RET_CHUNK = 256
RET_SEQS_PER_STEP = 2


def _ret_kernel(q_ref, k_ref, v_ref, g_ref, s0_ref, dec_ref, qd_ref, kd_ref, cd_ref, bd_ref, nw_ref,
                o_ref, so_ref, s_scr, *, chunk):
    tb = pl.program_id(1)
    ngrp = RET_W // LANES

    @pl.when(tb == 0)
    def _():
        s_scr[...] = s0_ref[...]

    lo = lax.broadcasted_iota(jnp.int32, (chunk, LANES), 1) < RET_HEAD_DIM
    nw = nw_ref[...]
    bd = bd_ref[...]

    def body(c, carry):
        rows = pl.ds(pl.multiple_of(c * chunk, chunk), chunk)
        for b in range(q_ref.shape[0]):
            for g in range(ngrp):
                cols = slice(g * LANES, (g + 1) * LANES)
                q, k, v = q_ref[b, rows, cols], k_ref[b, rows, cols], v_ref[b, rows, cols]
                kb, vb = k.astype(BF16), v.astype(BF16)
                parts = []
                for hh in range(2):
                    qm = jnp.where(lo if hh == 0 else jnp.logical_not(lo), q, 0.0).astype(BF16)
                    inner = _dot_nt(qm, kb) * dec_ref[2 * g + hh]
                    parts.append(_dot(inner.astype(BF16), vb))
                s_prev = s_scr[b, g]
                o = jnp.where(lo, parts[0], parts[1]) + _dot(q.astype(BF16), s_prev.astype(BF16)) * qd_ref[g]
                s_scr[b, g] = s_prev * cd_ref[g] + bd * _dot_tn((k * kd_ref[g]).astype(BF16), vb)
                o2 = o * o
                ms = jnp.where(lo, jnp.sum(jnp.where(lo, o2, 0.0), axis=-1, keepdims=True),
                               jnp.sum(jnp.where(lo, 0.0, o2), axis=-1, keepdims=True)) * (1.0 / RET_HEAD_DIM)
                o_ref[b, rows, cols] = o * lax.rsqrt(ms + NORM_EPS) * nw * _silu(g_ref[b, rows, cols])
        return carry

    lax.fori_loop(0, q_ref.shape[1] // chunk, body, 0)

    @pl.when(tb == pl.num_programs(1) - 1)
    def _():
        so_ref[...] = s_scr[...]


def _ret_tables(chunk):
    log_gamma = jnp.log(1.0 - 2.0 ** (-5.0 - jnp.arange(RET_HEADS, dtype=F32)))
    i = jnp.arange(chunk, dtype=F32)
    diff = i[:, None] - i[None, :]
    causal = diff >= 0
    decay = jnp.where(causal[None], jnp.exp(log_gamma[:, None, None] * jnp.where(causal, diff, 0.0)[None]), 0.0)
    per_lane = lambda a: jnp.repeat(a, RET_HEAD_DIM, axis=0).reshape(RET_W // LANES, LANES, -1)
    q_dec = per_lane(jnp.exp(log_gamma[:, None] * (i[None, :] + 1.0))).transpose(0, 2, 1)
    k_dec = per_lane(jnp.exp(log_gamma[:, None] * (chunk - 1.0 - i)[None, :])).transpose(0, 2, 1)
    c_dec = jnp.broadcast_to(per_lane(jnp.exp(log_gamma * chunk)[:, None]), (RET_W // LANES, LANES, LANES))
    head_of = jnp.arange(LANES) // RET_HEAD_DIM
    block_diag = (head_of[:, None] == head_of[None, :]).astype(F32)
    return decay, q_dec, k_dec, c_dec, block_diag


def _retention(q, k, v, gate, state_bd, ret_norm, nb, t):
    chunk = RET_CHUNK if t % RET_CHUNK == 0 else t
    tblk = min(t, 1024)
    ngrp = RET_W // LANES
    bblk = _seqs_per_step(nb, t, RET_SEQS_PER_STEP)
    decay, q_dec, k_dec, c_dec, block_diag = _ret_tables(chunk)
    r3 = lambda a: a.reshape(nb, t, RET_W)
    tok = pl.BlockSpec((bblk, tblk, RET_W), lambda b, i: (b, i, 0))
    st = pl.BlockSpec((bblk, ngrp, LANES, LANES), lambda b, i: (b, 0, 0, 0))
    full = lambda a: pl.BlockSpec(a.shape, lambda b, i: (0,) * a.ndim)
    nw = jnp.tile(ret_norm, LANES // RET_HEAD_DIM).reshape(1, LANES)
    o, s_new = pl.pallas_call(
        functools.partial(_ret_kernel, chunk=chunk),
        grid=(nb // bblk, t // tblk),
        in_specs=[tok, tok, tok, tok, st, full(decay), full(q_dec), full(k_dec), full(c_dec),
                  full(block_diag), full(nw)],
        out_specs=[tok, st],
        out_shape=[jax.ShapeDtypeStruct((nb, t, RET_W), F32),
                   jax.ShapeDtypeStruct((nb, ngrp, LANES, LANES), F32)],
        scratch_shapes=[pltpu.VMEM((bblk, ngrp, LANES, LANES), F32)],
        compiler_params=_cparams("parallel", "arbitrary"),
        name="retention",
    )(r3(q), r3(k), r3(v), r3(gate), state_bd, decay, q_dec, k_dec, c_dec, block_diag, nw)
    return o.reshape(nb * t, RET_W), s_new


def _to_block_diag(s):
    nb = s.shape[0]
    s = s.reshape(nb, 2, 2, RET_HEAD_DIM, RET_HEAD_DIM)
    z = jnp.zeros_like(s[:, :, 0])
    top = jnp.concatenate([s[:, :, 0], z], axis=-1)
    bot = jnp.concatenate([z, s[:, :, 1]], axis=-1)
    return jnp.concatenate([top, bot], axis=-2)


def _from_block_diag(s):
    h = RET_HEAD_DIM
    return jnp.stack([s[:, :, :h, :h], s[:, :, h:, h:]], axis=2).reshape(s.shape[0], RET_HEADS, h, h)


GDN_PREP_CHUNKS_PER_ITER = 8
GDN_SCAN_SEQS_PER_STEP = 2


def _softplus(x):
    return jnp.maximum(x, 0.0) + jnp.log1p(jnp.exp(-jnp.abs(x)))


def _gdn_prep_kernel(cv_ref, halo_ref, hist_ref, ba_ref, cw_ref, ab_ref,
                     qg_ref, kd_ref, u_ref, w_ref, at_ref, el_ref,
                     xp_scr, cs_scr, beta_scr, g_scr, *, chunk, cpi):
    tb = pl.program_id(1)
    tblk = cv_ref.shape[0]
    nhist = CONV_WIDTH - 1
    xp_scr[SUBLANES:, :] = cv_ref[...]

    @pl.when(tb == 0)
    def _():
        xp_scr[:SUBLANES, :] = hist_ref[...]

    @pl.when(tb > 0)
    def _():
        xp_scr[:SUBLANES, :] = halo_ref[...]

    for cg in range(CONV_DIM // LANES):
        cols = slice(cg * LANES, (cg + 1) * LANES)
        acc = None
        for i in range(CONV_WIDTH):
            start = SUBLANES - nhist + i
            term = xp_scr[start:start + tblk, cols] * cw_ref[i:i + 1, cols]
            acc = term if acc is None else acc + term
        cs_scr[:, cols] = _silu(acc)

    ba = ba_ref[...]
    ab = ab_ref[...]
    beta_scr[...] = jax.nn.sigmoid(ba)
    g_scr[...] = -jnp.exp(ab[0:1, :]) * _softplus(ba + ab[1:2, :])

    ri = lax.broadcasted_iota(jnp.int32, (chunk, chunk), 0)
    ci = lax.broadcasted_iota(jnp.int32, (chunk, chunk), 1)
    incl = ri >= ci
    strict = ri > ci
    tri = incl.astype(F32)
    lane_pick = (lax.broadcasted_iota(jnp.int32, (SUBLANES, LANES), 0)
                 == lax.broadcasted_iota(jnp.int32, (SUBLANES, LANES), 1)).astype(F32)
    base = min(SUBLANES, chunk)
    assert chunk % base == 0 and (chunk // base) & (chunk // base - 1) == 0
    same_blk = [(ri >> sh) == (ci >> sh) for sh in range(int(math.log2(base)), int(math.log2(chunk)) + 1)]

    def l2n(x):
        return x * lax.rsqrt(jnp.sum(x * x, axis=-1, keepdims=True) + NORM_EPS)

    def step(it, carry):
        probs = []
        for cc in range(cpi):
            c = it * cpi + cc
            rows = pl.ds(pl.multiple_of(c * chunk, chunk), chunk)
            beta_c = beta_scr[rows, :]
            gcol = _dot(tri, g_scr[rows, :], precision=HIGHEST)
            grow = _dot_nt(lane_pick, gcol, precision=HIGHEST)
            for h in range(GDN_HEADS):
                q = l2n(cs_scr[rows, h * LANES:(h + 1) * LANES])
                k = l2n(cs_scr[rows, GDN_W + h * LANES:GDN_W + (h + 1) * LANES])
                v = cs_scr[rows, 2 * GDN_W + h * LANES:2 * GDN_W + (h + 1) * LANES]
                beta = jnp.broadcast_to(beta_c[:, h:h + 1], (chunk, LANES))
                gc = jnp.broadcast_to(gcol[:, GDN_HEADS + h:GDN_HEADS + h + 1], (chunk, LANES))
                gdiff = gc[:, :chunk] - grow[GDN_HEADS + h:GDN_HEADS + h + 1, :]
                dmask = jnp.where(incl, jnp.exp(jnp.where(incl, gdiff, 0.0)), 0.0)
                kb = k * beta
                kbf = k.astype(BF16)
                lower = jnp.where(strict, _dot_nt(kb.astype(BF16), kbf) * dmask, 0.0)
                npow = jnp.where(same_blk[0], -lower, 0.0)
                probs.append(dict(c=c, rows=rows, h=h, q=q, k=k, v=v, beta=beta, gc=gc, dmask=dmask, kb=kb,
                                  kbf=kbf, lower=lower, npow=npow, qmat=npow))
        for _ in range(int(math.log2(base)) - 1):
            for p in probs:
                nb = p["npow"].astype(BF16)
                p["npow"] = _dot(nb, nb)
            for p in probs:
                p["qmat"] = p["qmat"] + p["npow"] + _dot(p["qmat"].astype(BF16), p["npow"].astype(BF16))
        for lvl in range(1, len(same_blk)):
            for p in probs:
                cb = jnp.where(jnp.logical_and(same_blk[lvl], jnp.logical_not(same_blk[lvl - 1])),
                               p["lower"], 0.0)
                p["x"] = cb + _dot(p["qmat"].astype(BF16), cb.astype(BF16))
            for p in probs:
                qb16 = p["qmat"].astype(BF16)
                p["qmat"] = p["qmat"] - p["x"] - _dot(p["x"].astype(BF16), qb16)
        for p in probs:
            rows, h, gc = p["rows"], p["h"], p["gc"]
            cols = slice(h * LANES, (h + 1) * LANES)
            eg = jnp.exp(gc)
            rhs_u = p["v"] * p["beta"]
            rhs_w = p["kb"] * eg
            qb = p["qmat"].astype(BF16)
            qs = p["q"] * (GDN_HEAD_DIM ** -0.5)
            g_last = gc[chunk - 1:chunk, :]
            qg_ref[rows, cols] = (qs * eg).astype(qg_ref.dtype)
            kd_ref[rows, cols] = (p["k"] * jnp.exp(g_last - gc)).astype(kd_ref.dtype)
            u_ref[rows, cols] = rhs_u + _dot(qb, rhs_u.astype(BF16))
            w_ref[rows, cols] = (rhs_w + _dot(qb, rhs_w.astype(BF16))).astype(w_ref.dtype)
            at_ref[rows, h * chunk:(h + 1) * chunk] = (_dot_nt(qs.astype(BF16), p["kbf"]) * p["dmask"]).astype(
                at_ref.dtype)
            el_ref[pl.ds(pl.multiple_of(p["c"] * SUBLANES, SUBLANES), SUBLANES), cols] = jnp.broadcast_to(
                jnp.exp(g_last), (SUBLANES, LANES))
        return carry

    lax.fori_loop(0, tblk // (chunk * cpi), step, 0)


def _gdn_scan_kernel(qg_ref, kd_ref, u_ref, w_ref, at_ref, el_ref, z_ref, s0_ref, nw_ref,
                     o_ref, so_ref, s_scr, *, chunk):
    tb = pl.program_id(1)

    @pl.when(tb == 0)
    def _():
        s_scr[...] = s0_ref[...]

    nw = nw_ref[...]
    probs = [(b, h) for b in range(qg_ref.shape[0]) for h in range(GDN_HEADS)]

    def step(c, carry):
        rows = pl.ds(pl.multiple_of(c * chunk, chunk), chunk)
        erow = pl.ds(pl.multiple_of(c * SUBLANES, SUBLANES), SUBLANES)
        cols = lambda h: slice(h * LANES, (h + 1) * LANES)
        s_prev = [s_scr[b, h] for b, h in probs]
        sb = [s.astype(BF16) for s in s_prev]
        v_new = [u_ref[b, rows, cols(h)] - _dot(w_ref[b, rows, cols(h)].astype(BF16), sb[i])
                 for i, (b, h) in enumerate(probs)]
        vb = [v.astype(BF16) for v in v_new]
        for i, (b, h) in enumerate(probs):
            el = el_ref[b, erow, cols(h)][0:1, :]
            s_scr[b, h] = s_prev[i] * el + _dot_tn(kd_ref[b, rows, cols(h)].astype(BF16), vb[i])
        for i, (b, h) in enumerate(probs):
            attn = at_ref[b, rows, h * chunk:(h + 1) * chunk].astype(BF16)
            o = _dot(qg_ref[b, rows, cols(h)].astype(BF16), sb[i]) + _dot(attn, vb[i])
            ms = jnp.mean(o * o, axis=-1, keepdims=True)
            o_ref[b, rows, cols(h)] = o * lax.rsqrt(ms + NORM_EPS) * nw * _silu(z_ref[b, rows, cols(h)])
        return carry

    lax.fori_loop(0, qg_ref.shape[1] // chunk, step, 0)

    @pl.when(tb == pl.num_programs(1) - 1)
    def _():
        so_ref[...] = s_scr[...]


def _gated_delta(cv, z, ba, conv_hist, state, conv_w, a_log, dt_bias, gdn_norm, nb, t):
    chunk = CHUNK if t % CHUNK == 0 else t
    tblk = min(t, 512)
    nhist = CONV_WIDTH - 1
    nchunk_blk = tblk // chunk
    hist_pad = jnp.concatenate([jnp.zeros((nb, SUBLANES - nhist, CONV_DIM), F32), conv_hist], axis=1)
    cw_pad = jnp.concatenate([conv_w, jnp.zeros((SUBLANES - CONV_WIDTH, CONV_DIM), F32)], axis=0)
    ab = jnp.zeros((SUBLANES, LANES), F32)
    ab = ab.at[0, GDN_HEADS:2 * GDN_HEADS].set(a_log).at[1, GDN_HEADS:2 * GDN_HEADS].set(dt_bias)
    cv3 = cv.reshape(nb, t, CONV_DIM)
    tok = lambda w: pl.BlockSpec((None, tblk, w), lambda b, i: (b, i, 0))
    full = lambda a: pl.BlockSpec(a.shape, lambda b, i: (0,) * a.ndim)
    halo = pl.BlockSpec((None, SUBLANES, CONV_DIM),
                        lambda b, i: (b, jnp.maximum(i * (tblk // SUBLANES) - 1, 0), 0))
    el_spec = pl.BlockSpec((None, nchunk_blk * SUBLANES, GDN_W), lambda b, i: (b, i, 0))
    tok_shape = lambda w, dt=F32: jax.ShapeDtypeStruct((nb, t, w), dt)
    el_shape = jax.ShapeDtypeStruct((nb, (t // chunk) * SUBLANES, GDN_W), F32)
    opd = BF16 if chunk % (2 * SUBLANES) == 0 else F32
    qg, kd, u, w, attn, el = pl.pallas_call(
        functools.partial(_gdn_prep_kernel, chunk=chunk, cpi=min(GDN_PREP_CHUNKS_PER_ITER, nchunk_blk)),
        grid=(nb, t // tblk),
        in_specs=[tok(CONV_DIM), halo, pl.BlockSpec((None, SUBLANES, CONV_DIM), lambda b, i: (b, 0, 0)),
                  tok(LANES), full(cw_pad), full(ab)],
        out_specs=[tok(GDN_W), tok(GDN_W), tok(GDN_W), tok(GDN_W), tok(GDN_HEADS * chunk), el_spec],
        out_shape=[tok_shape(GDN_W, opd), tok_shape(GDN_W, opd), tok_shape(GDN_W), tok_shape(GDN_W, opd),
                   tok_shape(GDN_HEADS * chunk, opd), el_shape],
        scratch_shapes=[pltpu.VMEM((tblk + SUBLANES, CONV_DIM), F32), pltpu.VMEM((tblk, CONV_DIM), F32),
                        pltpu.VMEM((tblk, LANES), F32), pltpu.VMEM((tblk, LANES), F32)],
        compiler_params=_cparams("parallel", "parallel"),
        name="gdn_prep",
    )(cv3, cv3, hist_pad, ba.reshape(nb, t, LANES), cw_pad, ab)

    bblk = _seqs_per_step(nb, t, GDN_SCAN_SEQS_PER_STEP)
    stok = lambda w: pl.BlockSpec((bblk, tblk, w), lambda b, i: (b, i, 0))
    sel_spec = pl.BlockSpec((bblk, nchunk_blk * SUBLANES, GDN_W), lambda b, i: (b, i, 0))
    st = pl.BlockSpec((bblk, GDN_HEADS, LANES, LANES), lambda b, i: (b, 0, 0, 0))
    nw = gdn_norm.reshape(1, LANES)
    o, s_new = pl.pallas_call(
        functools.partial(_gdn_scan_kernel, chunk=chunk),
        grid=(nb // bblk, t // tblk),
        in_specs=[stok(GDN_W), stok(GDN_W), stok(GDN_W), stok(GDN_W), stok(GDN_HEADS * chunk), sel_spec,
                  stok(GDN_W), st, full(nw)],
        out_specs=[stok(GDN_W), st],
        out_shape=[tok_shape(GDN_W), jax.ShapeDtypeStruct((nb, GDN_HEADS, LANES, LANES), F32)],
        scratch_shapes=[pltpu.VMEM((bblk, GDN_HEADS, LANES, LANES), F32)],
        compiler_params=_cparams("parallel", "arbitrary"),
        name="gdn_scan",
    )(qg, kd, u, w, attn, el, z.reshape(nb, t, GDN_W), state, nw)
    conv_new = jnp.concatenate([conv_hist, cv3], axis=1)[:, -nhist:]
    return o.reshape(nb * t, GDN_W), s_new, conv_new


FFN_TILE = 256


def _out_ffn_kernel(x_ref, oa_ref, orr_ref, oc_ref, g1_ref, sh2_ref, sc2_ref, g2_ref, nw_ref, wo_ref, wg_ref,
                    wu_ref, wd_ref, fn_ref, out_ref, *, final):
    mix = (_dot(oa_ref[...].astype(BF16), wo_ref[0:DSA_W, :])
           + _dot(orr_ref[...].astype(BF16), wo_ref[DSA_W:DSA_W + RET_W, :])
           + _dot(oc_ref[...].astype(BF16), wo_ref[DSA_W + RET_W:, :]))
    x1 = x_ref[...] + g1_ref[...] * mix
    hb = _rms_mod(x1, nw_ref[...], sc2_ref[...], sh2_ref[...]).astype(BF16)
    acc = None
    for j in range(wg_ref.shape[1] // FFN_TILE):
        cols = slice(j * FFN_TILE, (j + 1) * FFN_TILE)
        act = (_silu(_dot(hb, wg_ref[:, cols])) * _dot(hb, wu_ref[:, cols])).astype(BF16)
        down = _dot(act, wd_ref[cols, :])
        acc = down if acc is None else acc + down
    x2 = x1 + g2_ref[...] * acc
    if final:
        ms = jnp.mean(x2 * x2, axis=-1, keepdims=True)
        x2 = x2 * lax.rsqrt(ms + NORM_EPS) * fn_ref[...]
    out_ref[...] = x2


def _out_ffn(x2, t, oa, orr, oc, mod3, norm_w, w_out_b, wg_b, wu_b, wd_b, final_norm, final, tm):
    m, d = x2.shape
    row = lambda w: pl.BlockSpec((tm, w), lambda i: (i, 0))
    const = lambda a: pl.BlockSpec(a.shape, lambda i: (0,) * a.ndim, pipeline_mode=pl.Buffered(1))
    nw = norm_w.reshape(1, d)
    fn = final_norm.reshape(1, d)
    return pl.pallas_call(
        functools.partial(_out_ffn_kernel, final=final),
        grid=(m // tm,),
        in_specs=[row(d), row(DSA_W), row(RET_W), row(GDN_W),
                  _mod_spec(tm, t, d, 2), _mod_spec(tm, t, d, 3), _mod_spec(tm, t, d, 4), _mod_spec(tm, t, d, 5),
                  const(nw), const(w_out_b), const(wg_b), const(wu_b), const(wd_b), const(fn)],
        out_specs=row(d),
        out_shape=jax.ShapeDtypeStruct((m, d), F32),
        compiler_params=_cparams("parallel"),
        name="out_ffn",
    )(x2, oa, orr, oc, mod3, mod3, mod3, mod3, nw, w_out_b, wg_b, wu_b, wd_b, fn)


def _rope_tables(pos, inv_freq, reps):
    ang = pos.astype(F32)[:, None] * inv_freq[None, :]
    cos, sin = jnp.cos(ang), jnp.sin(ang)
    return (jnp.tile(jnp.concatenate([cos, cos], axis=-1), (1, reps)),
            jnp.tile(jnp.concatenate([-sin, sin], axis=-1), (1, reps)))


def _trunk(x, modp, pos, k_hist, v_hist, s_ret, s_gdn, conv_hist, wts):
    (norm_mix, norm_ffn, w_in_b, ret_norm, conv_w, a_log, dt_bias, gdn_norm, w_out_b, wg_b, wu_b, wd_b,
     final_norm) = wts
    nb, t, d = x.shape
    m = nb * t
    depth = w_in_b.shape[0]
    tm = min(TOKEN_TILE, m)
    inv_a = 1.0 / (ROPE_THETA ** (jnp.arange(0, DSA_HEAD_DIM, 2, dtype=F32) / DSA_HEAD_DIM))
    inv_r = 1.0 / (10000.0 ** jnp.linspace(0.0, 1.0, RET_HEAD_DIM // 2, dtype=F32))
    tabs = _rope_tables(pos, inv_a, DSA_HEADS) + _rope_tables(pos, inv_r, RET_HEADS)
    if t < tm:
        tabs = tuple(jnp.tile(a, (tm // t, 1)) for a in tabs)
    x2 = x.reshape(m, d)
    ks, vs, rs, gs, cs = [], [], [], [], []
    for l in range(depth):
        if t >= tm:
            mod3 = modp[l].reshape(nb, 1, 6 * d)
        else:
            mod3 = jnp.repeat(modp[l], t, axis=0).reshape(m // tm, tm, 6 * d)
        qa, ka, va, qr, kr, vr, gr, cv, z, ba = _inproj(x2, t, mod3, norm_mix[l], w_in_b[l], tabs, tm)
        if k_hist is None:
            oa = _dsa_prompt(qa, ka, va, nb, t)
        else:
            oa = _dsa_step(qa, ka, va, k_hist, v_hist, l, nb, t)
        orr, sr = _retention(qr, kr, vr, gr, _to_block_diag(s_ret[l]), ret_norm[l], nb, t)
        oc, sg, cvn = _gated_delta(cv, z, ba, conv_hist[l], s_gdn[l], conv_w[l], a_log[l], dt_bias[l],
                                   gdn_norm[l], nb, t)
        x2 = _out_ffn(x2, t, oa, orr, oc, mod3, norm_ffn[l], w_out_b[l], wg_b[l], wu_b[l], wd_b[l],
                      final_norm, l == depth - 1, tm)
        ks.append(ka.reshape(nb, t, DSA_HEADS, DSA_HEAD_DIM))
        vs.append(va.reshape(nb, t, DSA_HEADS, DSA_HEAD_DIM))
        rs.append(_from_block_diag(sr))
        gs.append(sg)
        cs.append(cvn)
    return (x2.reshape(nb, t, d), jnp.stack(ks), jnp.stack(vs), jnp.stack(rs), jnp.stack(gs), jnp.stack(cs))


def kernel(x_prompt, x_sample, cache_win_k, cache_win_v, state_ret, state_gdn, state_conv, c_prompt, c_sample, ada_w, ada_b, norm_mix, norm_ffn, w_in, ret_norm, conv_w, a_log, dt_bias, gdn_norm, w_out, w_gate, w_up, w_down, final_norm):
    nb, t_p, d = x_prompt.shape
    db, t_s, _ = x_sample.shape
    depth = ada_w.shape[0]
    rows = nb + db
    rows_pad = -(-rows // SUBLANES) * SUBLANES
    c_all = jnp.concatenate([c_prompt, c_sample, jnp.zeros((rows_pad - rows, d), F32)], axis=0)
    mod = _modulation(c_all, ada_w, ada_b)
    w_in_b = jnp.pad(w_in, ((0, 0), (0, 0), (0, IN_COLS_PAD - IN_COLS))).astype(BF16)
    wts = (norm_mix, norm_ffn, w_in_b, ret_norm, conv_w, a_log, dt_bias, gdn_norm, w_out.astype(BF16),
           w_gate.astype(BF16), w_up.astype(BF16), w_down.astype(BF16), final_norm)

    zr = jnp.zeros((depth, nb, RET_HEADS, RET_HEAD_DIM, RET_HEAD_DIM), F32)
    zg = jnp.zeros((depth, nb, GDN_HEADS, GDN_HEAD_DIM, GDN_HEAD_DIM), F32)
    zc = jnp.zeros((depth, nb, CONV_WIDTH - 1, CONV_DIM), F32)
    y_p, kp, vp, rp, gp, cp = _trunk(x_prompt, mod[:, :nb], jnp.arange(t_p, dtype=jnp.int32),
                                     None, None, zr, zg, zc, wts)
    wb = cache_win_k.shape[2]
    y_s, ks, vs, rs, gs, cs = _trunk(x_sample, mod[:, nb:rows], PAST_LEN + jnp.arange(t_s, dtype=jnp.int32),
                                     cache_win_k.reshape(depth, db, wb, DSA_W).transpose(0, 1, 3, 2),
                                     cache_win_v.reshape(depth, db, wb, DSA_W).transpose(0, 1, 3, 2),
                                     state_ret, state_gdn, state_conv, wts)
    return (y_p, y_s, kp[:, :, -DSA_MAX_WINDOW:], vp[:, :, -DSA_MAX_WINDOW:], rp, gp, cp, ks, vs, rs, gs, cs)
```

```python
import functools
import math

import jax
import jax.numpy as jnp
from jax import lax
from jax.experimental import pallas as pl
from jax.experimental.pallas import tpu as pltpu

F32 = jnp.float32
BF16 = jnp.bfloat16
HIGHEST = lax.Precision.HIGHEST

DSA_HEAD_DIM = 64
DSA_HEADS = 4
DSA_PATTERNS = ((128, 1), (512, 4), (2048, 16))
DSA_MAX_WINDOW = 2048
ROPE_THETA = 10000.0
RET_HEAD_DIM = 64
RET_HEADS = 4
GDN_HEAD_DIM = 128
GDN_HEADS = 4
CONV_WIDTH = 4
CHUNK = 64
NORM_EPS = 1e-6
PAST_LEN = 16384

DSA_W = DSA_HEADS * DSA_HEAD_DIM
RET_W = RET_HEADS * RET_HEAD_DIM
GDN_W = GDN_HEADS * GDN_HEAD_DIM
CONV_DIM = 3 * GDN_W
LANES = 128
SUBLANES = 8
VMEM_LIMIT = 56 * 1024 * 1024
TOKEN_TILE = 512

_C_QA, _C_KA, _C_VA = 0, DSA_W, 2 * DSA_W
_C_QR = 3 * DSA_W
_C_KR, _C_VR, _C_GR = _C_QR + RET_W, _C_QR + 2 * RET_W, _C_QR + 3 * RET_W
_C_CV = _C_QR + 4 * RET_W
_C_Z = _C_CV + CONV_DIM
_C_BA = _C_Z + GDN_W
IN_COLS = _C_BA + 2 * GDN_HEADS
IN_COLS_PAD = _C_BA + LANES


def _cparams(*sem):
    return pltpu.CompilerParams(dimension_semantics=sem, vmem_limit_bytes=VMEM_LIMIT)


def _dot(a, b, **kw):
    return jnp.dot(a, b, preferred_element_type=F32, **kw)


def _dot_nt(a, b, **kw):
    return lax.dot_general(a, b, (((1,), (1,)), ((), ())), preferred_element_type=F32, **kw)


def _dot_tn(a, b, **kw):
    return lax.dot_general(a, b, (((0,), (0,)), ((), ())), preferred_element_type=F32, **kw)


def _silu(x):
    return x * jax.nn.sigmoid(x)


def _seqs_per_step(nb, t, base):
    return math.gcd(nb, base * (4 if t < CHUNK else 1))


def _mod_kernel(c_ref, w_ref, b_ref, o_ref):
    a = _silu(c_ref[...]).astype(BF16)
    o_ref[...] = _dot(a, w_ref[...].astype(BF16)) + b_ref[...]


def _modulation(c_all, ada_w, ada_b, tn=1536):
    depth, d, n = ada_w.shape
    bp = c_all.shape[0]
    return pl.pallas_call(
        _mod_kernel,
        grid=(depth, n // tn),
        in_specs=[
            pl.BlockSpec((bp, d), lambda l, j: (0, 0)),
            pl.BlockSpec((None, d, tn), lambda l, j: (l, 0, j)),
            pl.BlockSpec((None, 1, tn), lambda l, j: (l, 0, j)),
        ],
        out_specs=pl.BlockSpec((None, bp, tn), lambda l, j: (l, 0, j)),
        out_shape=jax.ShapeDtypeStruct((depth, bp, n), F32),
        compiler_params=_cparams("parallel", "parallel"),
        name="modulation",
    )(c_all, ada_w, ada_b.reshape(depth, 1, n))


def _rms_mod(x, nw, sc, sh):
    ms = jnp.mean(x * x, axis=-1, keepdims=True)
    return (x * lax.rsqrt(ms + NORM_EPS) * nw) * (1.0 + sc) + sh


def _inproj_kernel(x_ref, nw_ref, sh_ref, sc_ref, w_ref, ca_ref, sa_ref, cr_ref, sr_ref,
                   qa_ref, ka_ref, va_ref, qr_ref, kr_ref, vr_ref, gr_ref, cv_ref, z_ref, ba_ref):
    tm = x_ref.shape[0]
    hb = _rms_mod(x_ref[...], nw_ref[...], sc_ref[...], sh_ref[...]).astype(BF16)

    def proj(c0, width):
        return _dot(hb, w_ref[:, c0:c0 + width])

    lane = lax.broadcasted_iota(jnp.int32, (tm, DSA_W), 1)
    first_half = (lane % DSA_HEAD_DIM) < (DSA_HEAD_DIM // 2)

    def rope(y, cos, sin_signed):
        partner = jnp.where(first_half, pltpu.roll(y, DSA_W - DSA_HEAD_DIM // 2, 1),
                            pltpu.roll(y, DSA_HEAD_DIM // 2, 1))
        return y * cos + partner * sin_signed

    wide = lambda ref: jnp.concatenate([ref[...]] * (DSA_W // LANES), axis=1)
    ca, sa, cr, sr = wide(ca_ref), wide(sa_ref), wide(cr_ref), wide(sr_ref)
    qa_ref[...] = rope(proj(_C_QA, DSA_W), ca, sa) * (DSA_HEAD_DIM ** -0.5)
    ka_ref[...] = rope(proj(_C_KA, DSA_W), ca, sa)
    va_ref[...] = proj(_C_VA, DSA_W)
    qr_ref[...] = rope(proj(_C_QR, RET_W), cr, sr)
    kr_ref[...] = rope(proj(_C_KR, RET_W), cr, sr) * (RET_HEAD_DIM ** -0.5)
    vr_ref[...] = proj(_C_VR, RET_W)
    gr_ref[...] = proj(_C_GR, RET_W)
    for s in range(CONV_DIM // GDN_W):
        cv_ref[:, s * GDN_W:(s + 1) * GDN_W] = proj(_C_CV + s * GDN_W, GDN_W)
    z_ref[...] = proj(_C_Z, GDN_W)
    ba_ref[...] = proj(_C_BA, LANES)


def _mod_spec(tm, t, d, col):
    if t >= tm:
        return pl.BlockSpec((None, 1, d), lambda i: ((i * tm) // t, 0, col))
    return pl.BlockSpec((None, tm, d), lambda i: (i, 0, col))


def _inproj(x2, t, mod3, norm_w, w_in_b, tabs, tm):
    m, d = x2.shape
    nt = tabs[0].shape[0] // tm
    widths = (DSA_W,) * 3 + (RET_W,) * 4 + (CONV_DIM, GDN_W, LANES)
    tab_spec = pl.BlockSpec((tm, LANES), lambda i: (i % nt, 0))
    return pl.pallas_call(
        _inproj_kernel,
        grid=(m // tm,),
        in_specs=[
            pl.BlockSpec((tm, d), lambda i: (i, 0)),
            pl.BlockSpec((1, d), lambda i: (0, 0)),
            _mod_spec(tm, t, d, 0),
            _mod_spec(tm, t, d, 1),
            pl.BlockSpec((d, IN_COLS_PAD), lambda i: (0, 0), pipeline_mode=pl.Buffered(1)),
            tab_spec, tab_spec, tab_spec, tab_spec,
        ],
        out_specs=[pl.BlockSpec((tm, w), lambda i: (i, 0)) for w in widths],
        out_shape=[jax.ShapeDtypeStruct((m, w), F32) for w in widths],
        compiler_params=_cparams("parallel"),
        name="inproj",
    )(x2, norm_w.reshape(1, d), mod3, mod3, w_in_b, *tabs)


DSA_BLK = 128


DSA_QBLK = DSA_MAX_WINDOW
DSA_GROUP = 4


def _dsa_kernel(q_ref, kp_ref, kc_ref, vp_ref, vc_ref, o_ref, acc_scr, m_scr, l_scr):
    blk = pl.program_id(2)
    qblk = q_ref.shape[0]
    row = lax.broadcasted_iota(jnp.int32, (2 * DSA_BLK, 2 * DSA_BLK), 0) % DSA_BLK
    col = lax.broadcasted_iota(jnp.int32, (2 * DSA_BLK, 2 * DSA_BLK), 1)
    ok = jnp.logical_and(col >= row, col <= row + DSA_BLK)
    ok_first = jnp.logical_and(ok, jnp.logical_or(col >= DSA_BLK, blk > 0))
    lo = lax.broadcasted_iota(jnp.int32, (DSA_BLK, LANES), 1) < DSA_HEAD_DIM
    neg = -jnp.inf
    halves = lambda x: jnp.where(lo, x[:DSA_BLK], x[DSA_BLK:])

    def rows_of(dil, start, n):
        return pl.ds(start, n) if dil == 1 else pl.ds(start, n, stride=dil)

    def group_softmax(dil, subs):
        n = range(len(subs))
        idx = [rows_of(dil, ph + dil * DSA_BLK * j, DSA_BLK) for ph, j in subs]
        q = [q_ref[i, :] for i in idx]
        k, v = [], []
        for u, (ph, j) in enumerate(subs):
            if j == 0:
                band = rows_of(dil, qblk - dil * DSA_BLK + ph, DSA_BLK)
                k.append(jnp.concatenate([kp_ref[band, :], kc_ref[idx[u], :]], axis=0).astype(BF16))
                v.append(jnp.concatenate([vp_ref[band, :], vc_ref[idx[u], :]], axis=0).astype(BF16))
            else:
                both = rows_of(dil, ph + dil * DSA_BLK * (j - 1), 2 * DSA_BLK)
                k.append(kc_ref[both, :].astype(BF16))
                v.append(vc_ref[both, :].astype(BF16))
        q2 = [jnp.concatenate([jnp.where(lo, q[u], 0.0), jnp.where(lo, 0.0, q[u])], axis=0).astype(BF16)
              for u in n]
        s = [jnp.where(ok_first if subs[u][1] == 0 else ok, _dot_nt(q2[u], k[u]), neg) for u in n]
        mx = [jnp.max(s[u], axis=-1, keepdims=True) for u in n]
        p = [jnp.exp(s[u] - mx[u]) for u in n]
        den = [jnp.sum(p[u], axis=-1, keepdims=True) for u in n]
        pv = [_dot(p[u].astype(BF16), v[u]) for u in n]
        return idx, [(halves(pv[u]), halves(jnp.broadcast_to(mx[u], (2 * DSA_BLK, LANES))),
                      halves(jnp.broadcast_to(den[u], (2 * DSA_BLK, LANES)))) for u in n]

    dils = sorted((d for _, d in DSA_PATTERNS), reverse=True)
    for pi, dil in enumerate(dils):
        subs = [(ph, j) for j in range(qblk // (dil * DSA_BLK)) for ph in range(dil)]
        for g0 in range(0, len(subs), DSA_GROUP):
            idx, tiles = group_softmax(dil, subs[g0:g0 + DSA_GROUP])
            for i, (pv, mx, den) in zip(idx, tiles):
                if pi > 0:
                    m_old = m_scr[i, :]
                    m_new = jnp.maximum(m_old, mx)
                    w_old = jnp.exp(m_old - m_new)
                    w_cur = jnp.exp(mx - m_new)
                    pv = acc_scr[i, :] * w_old + pv * w_cur
                    den = l_scr[i, :] * w_old + den * w_cur
                    mx = m_new
                if pi < len(dils) - 1:
                    acc_scr[i, :] = pv
                    m_scr[i, :] = mx
                    l_scr[i, :] = den
                else:
                    o_ref[i, :] = pv / den


def _dsa_prompt(q, k, v, nb, t):
    assert all(w // d == DSA_BLK for w, d in DSA_PATTERNS) and t % DSA_QBLK == 0
    ngrp = DSA_W // LANES
    r3 = lambda a: a.reshape(nb, t, DSA_W)
    cur = pl.BlockSpec((None, DSA_QBLK, LANES), lambda b, g, i: (b, i, g))
    prv = pl.BlockSpec((None, DSA_QBLK, LANES), lambda b, g, i: (b, jnp.maximum(i - 1, 0), g))
    return pl.pallas_call(
        _dsa_kernel,
        grid=(nb, ngrp, t // DSA_QBLK),
        in_specs=[cur, prv, cur, prv, cur],
        out_specs=cur,
        out_shape=jax.ShapeDtypeStruct((nb, t, DSA_W), F32),
        scratch_shapes=[pltpu.VMEM((DSA_QBLK, LANES), F32)] * 3,
        compiler_params=_cparams("parallel", "parallel", "parallel"),
        name="dsa_prompt",
    )(r3(q), r3(k), r3(k), r3(v), r3(v)).reshape(nb * t, DSA_W)


def _multiplicity(dist):
    total = jnp.zeros(dist.shape, F32)
    for window, dil in DSA_PATTERNS:
        hit = (dist >= 0) & (dist <= window) & ((dist & (dil - 1)) == 0)
        total = total + hit.astype(F32)
    return total


def _dsa_step_kernel(q_ref, kc_ref, vc_ref, kn_ref, vn_ref, o_ref):
    tq = q_ref.shape[0]
    wb = kc_ref.shape[1]
    qi = lax.broadcasted_iota(jnp.int32, (2 * tq, wb), 0) % tq
    w_c = _multiplicity(wb + qi - lax.broadcasted_iota(jnp.int32, (2 * tq, wb), 1))
    qn = lax.broadcasted_iota(jnp.int32, (2 * tq, LANES), 0) % tq
    nn = lax.broadcasted_iota(jnp.int32, (2 * tq, LANES), 1)
    w_n = jnp.where(nn < tq, _multiplicity(qn - nn), 0.0)
    lo = lax.broadcasted_iota(jnp.int32, (tq, LANES), 1) < DSA_HEAD_DIM
    pad = jnp.zeros((LANES - tq, LANES), F32)
    neg = -jnp.inf
    for g in range(DSA_W // LANES):
        cols = slice(g * LANES, (g + 1) * LANES)
        q = q_ref[:, cols]
        q2 = jnp.concatenate([jnp.where(lo, q, 0.0), jnp.where(lo, 0.0, q)], axis=0).astype(BF16)
        kt_c, vt_c = kc_ref[cols, :].astype(BF16), vc_ref[cols, :].astype(BF16)
        k_n = jnp.concatenate([kn_ref[:, cols], pad], axis=0).astype(BF16)
        v_n = jnp.concatenate([vn_ref[:, cols], pad], axis=0).astype(BF16)
        s_c = jnp.where(w_c > 0, _dot(q2, kt_c), neg)
        s_n = jnp.where(w_n > 0, _dot_nt(q2, k_n), neg)
        mx = jnp.maximum(jnp.max(s_c, axis=-1, keepdims=True), jnp.max(s_n, axis=-1, keepdims=True))
        p_c = w_c * jnp.exp(s_c - mx)
        p_n = w_n * jnp.exp(s_n - mx)
        den = jnp.sum(p_c, axis=-1, keepdims=True) + jnp.sum(p_n, axis=-1, keepdims=True)
        o2 = (_dot_nt(p_c.astype(BF16), vt_c) + _dot(p_n.astype(BF16), v_n)) / den
        o_ref[:, cols] = jnp.where(lo, o2[:tq], o2[tq:])


def _dsa_step(q, k_new, v_new, k_cache, v_cache, layer, nb, t):
    wb = k_cache.shape[3]
    new = pl.BlockSpec((None, t, DSA_W), lambda b: (b, 0, 0))
    cache = pl.BlockSpec((None, None, DSA_W, wb), lambda b: (layer, b, 0, 0))
    r3 = lambda a: a.reshape(nb, t, DSA_W)
    return pl.pallas_call(
        _dsa_step_kernel,
        grid=(nb,),
        in_specs=[new, cache, cache, new, new],
        out_specs=new,
        out_shape=jax.ShapeDtypeStruct((nb, t, DSA_W), F32),
        compiler_params=_cparams("parallel"),
        name="dsa_step",
    )(r3(q), k_cache, v_cache, r3(k_new), r3(v_new)).reshape(nb * t, DSA_W)


RET_CHUNK = 256
RET_SEQS_PER_STEP = 2
RET_UNROLL = 4


def _ret_kernel(q_ref, k_ref, v_ref, g_ref, s0_ref, dec_ref, qd_ref, kd_ref, cd_ref, bd_ref, nw_ref,
                o_ref, so_ref, s_scr, *, chunk):
    tb = pl.program_id(1)
    ngrp = RET_W // LANES

    @pl.when(tb == 0)
    def _():
        s_scr[...] = s0_ref[...]

    lo = lax.broadcasted_iota(jnp.int32, (chunk, LANES), 1) < RET_HEAD_DIM
    nw = nw_ref[...]
    bd = bd_ref[...]

    def body(c, carry):
        rows = pl.ds(pl.multiple_of(c * chunk, chunk), chunk)
        for b in range(q_ref.shape[0]):
            for g in range(ngrp):
                cols = slice(g * LANES, (g + 1) * LANES)
                q, k, v = q_ref[b, rows, cols], k_ref[b, rows, cols], v_ref[b, rows, cols]
                kb, vb = k.astype(BF16), v.astype(BF16)
                parts = []
                for hh in range(2):
                    qm = jnp.where(lo if hh == 0 else jnp.logical_not(lo), q, 0.0).astype(BF16)
                    inner = _dot_nt(qm, kb) * dec_ref[2 * g + hh]
                    parts.append(_dot(inner.astype(BF16), vb))
                s_prev = s_scr[b, g]
                o = jnp.where(lo, parts[0], parts[1]) + _dot(q.astype(BF16), s_prev.astype(BF16)) * qd_ref[g]
                s_scr[b, g] = s_prev * cd_ref[g] + bd * _dot_tn((k * kd_ref[g]).astype(BF16), vb)
                o2 = o * o
                ms = jnp.where(lo, jnp.sum(jnp.where(lo, o2, 0.0), axis=-1, keepdims=True),
                               jnp.sum(jnp.where(lo, 0.0, o2), axis=-1, keepdims=True)) * (1.0 / RET_HEAD_DIM)
                o_ref[b, rows, cols] = o * lax.rsqrt(ms + NORM_EPS) * nw * _silu(g_ref[b, rows, cols])
        return carry

    nchunk = q_ref.shape[1] // chunk
    lax.fori_loop(0, nchunk, body, 0, unroll=min(RET_UNROLL, nchunk))

    @pl.when(tb == pl.num_programs(1) - 1)
    def _():
        so_ref[...] = s_scr[...]


def _ret_tables(chunk):
    log_gamma = jnp.log(1.0 - 2.0 ** (-5.0 - jnp.arange(RET_HEADS, dtype=F32)))
    i = jnp.arange(chunk, dtype=F32)
    diff = i[:, None] - i[None, :]
    causal = diff >= 0
    decay = jnp.where(causal[None], jnp.exp(log_gamma[:, None, None] * jnp.where(causal, diff, 0.0)[None]), 0.0)
    per_lane = lambda a: jnp.repeat(a, RET_HEAD_DIM, axis=0).reshape(RET_W // LANES, LANES, -1)
    q_dec = per_lane(jnp.exp(log_gamma[:, None] * (i[None, :] + 1.0))).transpose(0, 2, 1)
    k_dec = per_lane(jnp.exp(log_gamma[:, None] * (chunk - 1.0 - i)[None, :])).transpose(0, 2, 1)
    c_dec = jnp.broadcast_to(per_lane(jnp.exp(log_gamma * chunk)[:, None]), (RET_W // LANES, LANES, LANES))
    head_of = jnp.arange(LANES) // RET_HEAD_DIM
    block_diag = (head_of[:, None] == head_of[None, :]).astype(F32)
    return decay, q_dec, k_dec, c_dec, block_diag


def _retention(q, k, v, gate, state_bd, ret_norm, nb, t):
    chunk = RET_CHUNK if t % RET_CHUNK == 0 else t
    tblk = min(t, 1024)
    ngrp = RET_W // LANES
    bblk = _seqs_per_step(nb, t, RET_SEQS_PER_STEP)
    decay, q_dec, k_dec, c_dec, block_diag = _ret_tables(chunk)
    r3 = lambda a: a.reshape(nb, t, RET_W)
    tok = pl.BlockSpec((bblk, tblk, RET_W), lambda b, i: (b, i, 0))
    st = pl.BlockSpec((bblk, ngrp, LANES, LANES), lambda b, i: (b, 0, 0, 0))
    full = lambda a: pl.BlockSpec(a.shape, lambda b, i: (0,) * a.ndim)
    nw = jnp.tile(ret_norm, LANES // RET_HEAD_DIM).reshape(1, LANES)
    o, s_new = pl.pallas_call(
        functools.partial(_ret_kernel, chunk=chunk),
        grid=(nb // bblk, t // tblk),
        in_specs=[tok, tok, tok, tok, st, full(decay), full(q_dec), full(k_dec), full(c_dec),
                  full(block_diag), full(nw)],
        out_specs=[tok, st],
        out_shape=[jax.ShapeDtypeStruct((nb, t, RET_W), F32),
                   jax.ShapeDtypeStruct((nb, ngrp, LANES, LANES), F32)],
        scratch_shapes=[pltpu.VMEM((bblk, ngrp, LANES, LANES), F32)],
        compiler_params=_cparams("parallel", "arbitrary"),
        name="retention",
    )(r3(q), r3(k), r3(v), r3(gate), state_bd, decay, q_dec, k_dec, c_dec, block_diag, nw)
    return o.reshape(nb * t, RET_W), s_new


def _to_block_diag(s):
    nb = s.shape[0]
    s = s.reshape(nb, 2, 2, RET_HEAD_DIM, RET_HEAD_DIM)
    z = jnp.zeros_like(s[:, :, 0])
    top = jnp.concatenate([s[:, :, 0], z], axis=-1)
    bot = jnp.concatenate([z, s[:, :, 1]], axis=-1)
    return jnp.concatenate([top, bot], axis=-2)


def _from_block_diag(s):
    h = RET_HEAD_DIM
    return jnp.stack([s[:, :, :h, :h], s[:, :, h:, h:]], axis=2).reshape(s.shape[0], RET_HEADS, h, h)


GDN_PREP_CHUNKS_PER_ITER = 8
GDN_SCAN_SEQS_PER_STEP = 2
GDN_SCAN_UNROLL = 4


def _softplus(x):
    return jnp.maximum(x, 0.0) + jnp.log1p(jnp.exp(-jnp.abs(x)))


def _gdn_prep_kernel(cv_ref, halo_ref, hist_ref, ba_ref, cw_ref, ab_ref,
                     qg_ref, kd_ref, u_ref, w_ref, at_ref, el_ref,
                     xp_scr, cs_scr, beta_scr, g_scr, *, chunk, cpi):
    tb = pl.program_id(1)
    nseq, tblk = cv_ref.shape[0], cv_ref.shape[1]
    nhist = CONV_WIDTH - 1
    ncg = CONV_DIM // LANES
    ab = ab_ref[...]
    for b in range(nseq):
        for cg in range(ncg):
            xp_scr[b, cg, SUBLANES:, :] = cv_ref[b, :, cg * LANES:(cg + 1) * LANES]

        @pl.when(tb == 0)
        def _():
            for cg in range(ncg):
                xp_scr[b, cg, :SUBLANES, :] = hist_ref[b, :, cg * LANES:(cg + 1) * LANES]

        @pl.when(tb > 0)
        def _():
            for cg in range(ncg):
                xp_scr[b, cg, :SUBLANES, :] = halo_ref[b, :, cg * LANES:(cg + 1) * LANES]

        for cg in range(ncg):
            cols = slice(cg * LANES, (cg + 1) * LANES)
            acc = None
            for i in range(CONV_WIDTH):
                start = SUBLANES - nhist + i
                term = xp_scr[b, cg, pl.ds(start, tblk, stride=1), :] * cw_ref[i:i + 1, cols]
                acc = term if acc is None else acc + term
            cs_scr[b, :, cols] = _silu(acc)

        ba = ba_ref[b]
        beta_scr[b] = jax.nn.sigmoid(ba)
        g_scr[b] = -jnp.exp(ab[0:1, :]) * _softplus(ba + ab[1:2, :])

    ri = lax.broadcasted_iota(jnp.int32, (chunk, chunk), 0)
    ci = lax.broadcasted_iota(jnp.int32, (chunk, chunk), 1)
    incl = ri >= ci
    strict = ri > ci
    tri = incl.astype(F32)
    lane_pick = (lax.broadcasted_iota(jnp.int32, (SUBLANES, LANES), 0)
                 == lax.broadcasted_iota(jnp.int32, (SUBLANES, LANES), 1)).astype(F32)
    base = min(SUBLANES, chunk)
    assert chunk % base == 0 and (chunk // base) & (chunk // base - 1) == 0
    same_blk = [(ri >> sh) == (ci >> sh) for sh in range(int(math.log2(base)), int(math.log2(chunk)) + 1)]

    def l2n(x):
        return x * lax.rsqrt(jnp.sum(x * x, axis=-1, keepdims=True) + NORM_EPS)

    def step(it, carry):
        probs = []
        for b, cc in [(b, cc) for b in range(nseq) for cc in range(cpi)]:
            c = it * cpi + cc
            rows = pl.ds(pl.multiple_of(c * chunk, chunk), chunk)
            beta_c = beta_scr[b, rows, :]
            gcol = _dot(tri, g_scr[b, rows, :], precision=HIGHEST)
            grow = _dot_nt(lane_pick, gcol, precision=HIGHEST)
            for h in range(GDN_HEADS):
                q = l2n(cs_scr[b, rows, h * LANES:(h + 1) * LANES])
                k = l2n(cs_scr[b, rows, GDN_W + h * LANES:GDN_W + (h + 1) * LANES])
                v = cs_scr[b, rows, 2 * GDN_W + h * LANES:2 * GDN_W + (h + 1) * LANES]
                beta = jnp.broadcast_to(beta_c[:, h:h + 1], (chunk, LANES))
                gc = jnp.broadcast_to(gcol[:, GDN_HEADS + h:GDN_HEADS + h + 1], (chunk, LANES))
                gdiff = gc[:, :chunk] - grow[GDN_HEADS + h:GDN_HEADS + h + 1, :]
                dmask = jnp.where(incl, jnp.exp(jnp.where(incl, gdiff, 0.0)), 0.0)
                kb = k * beta
                kbf = k.astype(BF16)
                lower = jnp.where(strict, _dot_nt(kb.astype(BF16), kbf) * dmask, 0.0)
                npow = jnp.where(same_blk[0], -lower, 0.0)
                probs.append(dict(b=b, c=c, rows=rows, h=h, q=q, k=k, v=v, beta=beta, gc=gc, dmask=dmask, kb=kb,
                                  kbf=kbf, lower=lower, npow=npow, qmat=npow))
        for _ in range(int(math.log2(base)) - 1):
            for p in probs:
                nb = p["npow"].astype(BF16)
                p["npow"] = _dot(nb, nb)
            for p in probs:
                p["qmat"] = p["qmat"] + p["npow"] + _dot(p["qmat"].astype(BF16), p["npow"].astype(BF16))
        for lvl in range(1, len(same_blk)):
            for p in probs:
                cb = jnp.where(jnp.logical_and(same_blk[lvl], jnp.logical_not(same_blk[lvl - 1])),
                               p["lower"], 0.0)
                p["x"] = cb + _dot(p["qmat"].astype(BF16), cb.astype(BF16))
            for p in probs:
                qb16 = p["qmat"].astype(BF16)
                p["qmat"] = p["qmat"] - p["x"] - _dot(p["x"].astype(BF16), qb16)
        for p in probs:
            b, rows, h, gc = p["b"], p["rows"], p["h"], p["gc"]
            cols = slice(h * LANES, (h + 1) * LANES)
            eg = jnp.exp(gc)
            rhs_u = p["v"] * p["beta"]
            rhs_w = p["kb"] * eg
            qb = p["qmat"].astype(BF16)
            qs = p["q"] * (GDN_HEAD_DIM ** -0.5)
            g_last = gc[chunk - 1:chunk, :]
            qg_ref[b, rows, cols] = (qs * eg).astype(qg_ref.dtype)
            kd_ref[b, rows, cols] = (p["k"] * jnp.exp(g_last - gc)).astype(kd_ref.dtype)
            u_ref[b, rows, cols] = rhs_u + _dot(qb, rhs_u.astype(BF16))
            w_ref[b, rows, cols] = (rhs_w + _dot(qb, rhs_w.astype(BF16))).astype(w_ref.dtype)
            at_ref[b, rows, h * chunk:(h + 1) * chunk] = (
                _dot_nt(qs.astype(BF16), p["kbf"]) * p["dmask"]).astype(at_ref.dtype)
            el_ref[b, pl.ds(pl.multiple_of(p["c"] * SUBLANES, SUBLANES), SUBLANES), cols] = jnp.broadcast_to(
                jnp.exp(g_last), (SUBLANES, LANES))
        return carry

    lax.fori_loop(0, tblk // (chunk * cpi), step, 0)


def _gdn_scan_kernel(qg_ref, kd_ref, u_ref, w_ref, at_ref, el_ref, z_ref, s0_ref, nw_ref,
                     o_ref, so_ref, s_scr, *, chunk):
    tb = pl.program_id(1)

    @pl.when(tb == 0)
    def _():
        s_scr[...] = s0_ref[...]

    nw = nw_ref[...]
    probs = [(b, h) for b in range(qg_ref.shape[0]) for h in range(GDN_HEADS)]

    def step(c, carry):
        rows = pl.ds(pl.multiple_of(c * chunk, chunk), chunk)
        erow = pl.ds(pl.multiple_of(c * SUBLANES, SUBLANES), SUBLANES)
        cols = lambda h: slice(h * LANES, (h + 1) * LANES)
        s_prev = [s_scr[b, h] for b, h in probs]
        sb = [s.astype(BF16) for s in s_prev]
        v_new = [u_ref[b, rows, cols(h)] - _dot(w_ref[b, rows, cols(h)].astype(BF16), sb[i])
                 for i, (b, h) in enumerate(probs)]
        vb = [v.astype(BF16) for v in v_new]
        for i, (b, h) in enumerate(probs):
            el = el_ref[b, erow, cols(h)][0:1, :]
            s_scr[b, h] = s_prev[i] * el + _dot_tn(kd_ref[b, rows, cols(h)].astype(BF16), vb[i])
        for i, (b, h) in enumerate(probs):
            attn = at_ref[b, rows, h * chunk:(h + 1) * chunk].astype(BF16)
            o = _dot(qg_ref[b, rows, cols(h)].astype(BF16), sb[i]) + _dot(attn, vb[i])
            ms = jnp.mean(o * o, axis=-1, keepdims=True)
            o_ref[b, rows, cols(h)] = o * lax.rsqrt(ms + NORM_EPS) * nw * _silu(z_ref[b, rows, cols(h)])
        return carry

    nchunk = qg_ref.shape[1] // chunk
    lax.fori_loop(0, nchunk, step, 0, unroll=min(GDN_SCAN_UNROLL, nchunk))

    @pl.when(tb == pl.num_programs(1) - 1)
    def _():
        so_ref[...] = s_scr[...]


def _gated_delta(cv, z, ba, conv_hist, states, layer, conv_w, a_log, dt_bias, gdn_norm, nb, t):
    chunk = CHUNK if t % CHUNK == 0 else t
    tblk = min(t, 512)
    nhist = CONV_WIDTH - 1
    nchunk_blk = tblk // chunk
    hist_pad = jnp.concatenate([jnp.zeros((nb, SUBLANES - nhist, CONV_DIM), F32), conv_hist], axis=1)
    cw_pad = jnp.concatenate([conv_w, jnp.zeros((SUBLANES - CONV_WIDTH, CONV_DIM), F32)], axis=0)
    ab = jnp.zeros((SUBLANES, LANES), F32)
    ab = ab.at[0, GDN_HEADS:2 * GDN_HEADS].set(a_log).at[1, GDN_HEADS:2 * GDN_HEADS].set(dt_bias)
    cv3 = cv.reshape(nb, t, CONV_DIM)
    pblk = _seqs_per_step(nb, t, 2) if t < CHUNK else 1
    tok = lambda w: pl.BlockSpec((pblk, tblk, w), lambda b, i: (b, i, 0))
    full = lambda a: pl.BlockSpec(a.shape, lambda b, i: (0,) * a.ndim)
    halo = pl.BlockSpec((pblk, SUBLANES, CONV_DIM),
                        lambda b, i: (b, jnp.maximum(i * (tblk // SUBLANES) - 1, 0), 0))
    el_spec = pl.BlockSpec((pblk, nchunk_blk * SUBLANES, GDN_W), lambda b, i: (b, i, 0))
    tok_shape = lambda w, dt=F32: jax.ShapeDtypeStruct((nb, t, w), dt)
    el_shape = jax.ShapeDtypeStruct((nb, (t // chunk) * SUBLANES, GDN_W), F32)
    opd = BF16 if chunk % (2 * SUBLANES) == 0 else F32
    qg, kd, u, w, attn, el = pl.pallas_call(
        functools.partial(_gdn_prep_kernel, chunk=chunk, cpi=min(GDN_PREP_CHUNKS_PER_ITER, nchunk_blk)),
        grid=(nb // pblk, t // tblk),
        in_specs=[tok(CONV_DIM), halo, pl.BlockSpec((pblk, SUBLANES, CONV_DIM), lambda b, i: (b, 0, 0)),
                  tok(LANES), full(cw_pad), full(ab)],
        out_specs=[tok(GDN_W), tok(GDN_W), tok(GDN_W), tok(GDN_W), tok(GDN_HEADS * chunk), el_spec],
        out_shape=[tok_shape(GDN_W, opd), tok_shape(GDN_W, opd), tok_shape(GDN_W), tok_shape(GDN_W, opd),
                   tok_shape(GDN_HEADS * chunk, opd), el_shape],
        scratch_shapes=[pltpu.VMEM((pblk, CONV_DIM // LANES, tblk + SUBLANES, LANES), F32),
                        pltpu.VMEM((pblk, tblk, CONV_DIM), F32),
                        pltpu.VMEM((pblk, tblk, LANES), F32), pltpu.VMEM((pblk, tblk, LANES), F32)],
        compiler_params=_cparams("parallel", "parallel"),
        name="gdn_prep",
    )(cv3, cv3, hist_pad, ba.reshape(nb, t, LANES), cw_pad, ab)

    bblk = _seqs_per_step(nb, t, GDN_SCAN_SEQS_PER_STEP)
    stok = lambda w: pl.BlockSpec((bblk, tblk, w), lambda b, i: (b, i, 0))
    sel_spec = pl.BlockSpec((bblk, nchunk_blk * SUBLANES, GDN_W), lambda b, i: (b, i, 0))
    st_in = pl.BlockSpec((None, bblk, GDN_HEADS, LANES, LANES), lambda b, i: (layer, b, 0, 0, 0))
    st = pl.BlockSpec((bblk, GDN_HEADS, LANES, LANES), lambda b, i: (b, 0, 0, 0))
    nw = gdn_norm.reshape(1, LANES)
    o, s_new = pl.pallas_call(
        functools.partial(_gdn_scan_kernel, chunk=chunk),
        grid=(nb // bblk, t // tblk),
        in_specs=[stok(GDN_W), stok(GDN_W), stok(GDN_W), stok(GDN_W), stok(GDN_HEADS * chunk), sel_spec,
                  stok(GDN_W), st_in, full(nw)],
        out_specs=[stok(GDN_W), st],
        out_shape=[tok_shape(GDN_W), jax.ShapeDtypeStruct((nb, GDN_HEADS, LANES, LANES), F32)],
        scratch_shapes=[pltpu.VMEM((bblk, GDN_HEADS, LANES, LANES), F32)],
        compiler_params=_cparams("parallel", "arbitrary"),
        name="gdn_scan",
    )(qg, kd, u, w, attn, el, z.reshape(nb, t, GDN_W), states, nw)
    conv_new = jnp.concatenate([conv_hist, cv3], axis=1)[:, -nhist:]
    return o.reshape(nb * t, GDN_W), s_new, conv_new


FFN_TILE = 256


def _out_ffn_kernel(x_ref, oa_ref, orr_ref, oc_ref, g1_ref, sh2_ref, sc2_ref, g2_ref, nw_ref, wo_ref, wg_ref,
                    wu_ref, wd_ref, fn_ref, out_ref, *, final):
    mix = (_dot(oa_ref[...].astype(BF16), wo_ref[0:DSA_W, :])
           + _dot(orr_ref[...].astype(BF16), wo_ref[DSA_W:DSA_W + RET_W, :])
           + _dot(oc_ref[...].astype(BF16), wo_ref[DSA_W + RET_W:, :]))
    x1 = x_ref[...] + g1_ref[...] * mix
    hb = _rms_mod(x1, nw_ref[...], sc2_ref[...], sh2_ref[...]).astype(BF16)
    acc = None
    for j in range(wg_ref.shape[1] // FFN_TILE):
        cols = slice(j * FFN_TILE, (j + 1) * FFN_TILE)
        act = (_silu(_dot(hb, wg_ref[:, cols])) * _dot(hb, wu_ref[:, cols])).astype(BF16)
        down = _dot(act, wd_ref[cols, :])
        acc = down if acc is None else acc + down
    x2 = x1 + g2_ref[...] * acc
    if final:
        ms = jnp.mean(x2 * x2, axis=-1, keepdims=True)
        x2 = x2 * lax.rsqrt(ms + NORM_EPS) * fn_ref[...]
    out_ref[...] = x2


def _out_ffn(x2, t, oa, orr, oc, mod3, norm_w, w_out_b, wg_b, wu_b, wd_b, final_norm, final, tm):
    m, d = x2.shape
    row = lambda w: pl.BlockSpec((tm, w), lambda i: (i, 0))
    const = lambda a: pl.BlockSpec(a.shape, lambda i: (0,) * a.ndim, pipeline_mode=pl.Buffered(1))
    nw = norm_w.reshape(1, d)
    fn = final_norm.reshape(1, d)
    return pl.pallas_call(
        functools.partial(_out_ffn_kernel, final=final),
        grid=(m // tm,),
        in_specs=[row(d), row(DSA_W), row(RET_W), row(GDN_W),
                  _mod_spec(tm, t, d, 2), _mod_spec(tm, t, d, 3), _mod_spec(tm, t, d, 4), _mod_spec(tm, t, d, 5),
                  const(nw), const(w_out_b), const(wg_b), const(wu_b), const(wd_b), const(fn)],
        out_specs=row(d),
        out_shape=jax.ShapeDtypeStruct((m, d), F32),
        compiler_params=_cparams("parallel"),
        name="out_ffn",
    )(x2, oa, orr, oc, mod3, mod3, mod3, mod3, nw, w_out_b, wg_b, wu_b, wd_b, fn)


def _rope_tables(pos, inv_freq):
    reps = LANES // (2 * inv_freq.shape[0])
    inv_lane = jnp.tile(jnp.concatenate([inv_freq, inv_freq]), reps)
    sign_lane = jnp.tile(jnp.concatenate([-jnp.ones_like(inv_freq), jnp.ones_like(inv_freq)]), reps)
    ang = pos.astype(F32)[:, None] * inv_lane[None, :]
    return jnp.cos(ang), jnp.sin(ang) * sign_lane[None, :]


def _trunk(x, modp, pos, k_hist, v_hist, s_ret, s_gdn, conv_hist, wts):
    (norm_mix, norm_ffn, w_in_b, ret_norm, conv_w, a_log, dt_bias, gdn_norm, w_out_b, wg_b, wu_b, wd_b,
     final_norm) = wts
    nb, t, d = x.shape
    m = nb * t
    depth = w_in_b.shape[0]
    tm = min(TOKEN_TILE, m)
    inv_a = 1.0 / (ROPE_THETA ** (jnp.arange(0, DSA_HEAD_DIM, 2, dtype=F32) / DSA_HEAD_DIM))
    inv_r = 1.0 / (10000.0 ** jnp.linspace(0.0, 1.0, RET_HEAD_DIM // 2, dtype=F32))
    tabs = _rope_tables(pos, inv_a) + _rope_tables(pos, inv_r)
    if t < tm:
        tabs = tuple(jnp.tile(a, (tm // t, 1)) for a in tabs)
    x2 = x.reshape(m, d)
    ks, vs, rs, gs, cs = [], [], [], [], []
    for l in range(depth):
        if t >= tm:
            mod3 = modp[l].reshape(nb, 1, 6 * d)
        else:
            mod3 = jnp.repeat(modp[l], t, axis=0).reshape(m // tm, tm, 6 * d)
        qa, ka, va, qr, kr, vr, gr, cv, z, ba = _inproj(x2, t, mod3, norm_mix[l], w_in_b[l], tabs, tm)
        if k_hist is None:
            oa = _dsa_prompt(qa, ka, va, nb, t)
        else:
            oa = _dsa_step(qa, ka, va, k_hist, v_hist, l, nb, t)
        orr, sr = _retention(qr, kr, vr, gr, _to_block_diag(s_ret[l]), ret_norm[l], nb, t)
        oc, sg, cvn = _gated_delta(cv, z, ba, conv_hist[l], s_gdn, l, conv_w[l], a_log[l], dt_bias[l],
                                   gdn_norm[l], nb, t)
        x2 = _out_ffn(x2, t, oa, orr, oc, mod3, norm_ffn[l], w_out_b[l], wg_b[l], wu_b[l], wd_b[l],
                      final_norm, l == depth - 1, tm)
        keep = min(t, DSA_MAX_WINDOW)
        ks.append(ka.reshape(nb, t, DSA_HEADS, DSA_HEAD_DIM)[:, t - keep:])
        vs.append(va.reshape(nb, t, DSA_HEADS, DSA_HEAD_DIM)[:, t - keep:])
        rs.append(_from_block_diag(sr))
        gs.append(sg)
        cs.append(cvn)
    return (x2.reshape(nb, t, d), jnp.stack(ks), jnp.stack(vs), jnp.stack(rs), jnp.stack(gs), jnp.stack(cs))


def kernel(x_prompt, x_sample, cache_win_k, cache_win_v, state_ret, state_gdn, state_conv, c_prompt, c_sample, ada_w, ada_b, norm_mix, norm_ffn, w_in, ret_norm, conv_w, a_log, dt_bias, gdn_norm, w_out, w_gate, w_up, w_down, final_norm):
    nb, t_p, d = x_prompt.shape
    db, t_s, _ = x_sample.shape
    depth = ada_w.shape[0]
    rows = nb + db
    rows_pad = -(-rows // SUBLANES) * SUBLANES
    c_all = jnp.concatenate([c_prompt, c_sample, jnp.zeros((rows_pad - rows, d), F32)], axis=0)
    mod = _modulation(c_all, ada_w, ada_b)
    w_in_b = jnp.pad(w_in, ((0, 0), (0, 0), (0, IN_COLS_PAD - IN_COLS))).astype(BF16)
    wts = (norm_mix, norm_ffn, w_in_b, ret_norm, conv_w, a_log, dt_bias, gdn_norm, w_out.astype(BF16),
           w_gate.astype(BF16), w_up.astype(BF16), w_down.astype(BF16), final_norm)

    zr = jnp.zeros((depth, nb, RET_HEADS, RET_HEAD_DIM, RET_HEAD_DIM), F32)
    zg = jnp.zeros((depth, nb, GDN_HEADS, GDN_HEAD_DIM, GDN_HEAD_DIM), F32)
    zc = jnp.zeros((depth, nb, CONV_WIDTH - 1, CONV_DIM), F32)
    y_p, kp, vp, rp, gp, cp = _trunk(x_prompt, mod[:, :nb], jnp.arange(t_p, dtype=jnp.int32),
                                     None, None, zr, zg, zc, wts)
    wb = cache_win_k.shape[2]
    y_s, ks, vs, rs, gs, cs = _trunk(x_sample, mod[:, nb:rows], PAST_LEN + jnp.arange(t_s, dtype=jnp.int32),
                                     cache_win_k.reshape(depth, db, wb, DSA_W).transpose(0, 1, 3, 2),
                                     cache_win_v.reshape(depth, db, wb, DSA_W).transpose(0, 1, 3, 2),
                                     state_ret, state_gdn, state_conv, wts)
    return (y_p, y_s, kp, vp, rp, gp, cp, ks, vs, rs, gs, cs)
```

```python
import functools
import math

import jax
import jax.numpy as jnp
from jax import lax
from jax.experimental import pallas as pl
from jax.experimental.pallas import tpu as pltpu

F32 = jnp.float32
BF16 = jnp.bfloat16
HIGHEST = lax.Precision.HIGHEST

DSA_HEAD_DIM = 64
DSA_HEADS = 4
DSA_PATTERNS = ((128, 1), (512, 4), (2048, 16))
DSA_MAX_WINDOW = 2048
ROPE_THETA = 10000.0
RET_HEAD_DIM = 64
RET_HEADS = 4
GDN_HEAD_DIM = 128
GDN_HEADS = 4
CONV_WIDTH = 4
CHUNK = 64
NORM_EPS = 1e-6
PAST_LEN = 16384

DSA_W = DSA_HEADS * DSA_HEAD_DIM
RET_W = RET_HEADS * RET_HEAD_DIM
GDN_W = GDN_HEADS * GDN_HEAD_DIM
CONV_DIM = 3 * GDN_W
LANES = 128
SUBLANES = 8
VMEM_LIMIT = 56 * 1024 * 1024
TOKEN_TILE = 512

_C_QA, _C_KA, _C_VA = 0, DSA_W, 2 * DSA_W
_C_QR = 3 * DSA_W
_C_KR, _C_VR, _C_GR = _C_QR + RET_W, _C_QR + 2 * RET_W, _C_QR + 3 * RET_W
_C_CV = _C_QR + 4 * RET_W
_C_Z = _C_CV + CONV_DIM
_C_BA = _C_Z + GDN_W
IN_COLS = _C_BA + 2 * GDN_HEADS
IN_COLS_PAD = _C_BA + LANES


def _cparams(*sem):
    return pltpu.CompilerParams(dimension_semantics=sem, vmem_limit_bytes=VMEM_LIMIT)


def _dot(a, b, **kw):
    return jnp.dot(a, b, preferred_element_type=F32, **kw)


def _dot_nt(a, b, **kw):
    return lax.dot_general(a, b, (((1,), (1,)), ((), ())), preferred_element_type=F32, **kw)


def _dot_tn(a, b, **kw):
    return lax.dot_general(a, b, (((0,), (0,)), ((), ())), preferred_element_type=F32, **kw)


def _silu(x):
    return x * jax.nn.sigmoid(x)


def _seqs_per_step(nb, t, base):
    return math.gcd(nb, base * (4 if t < CHUNK else 1))


def _mod_kernel(c_ref, w_ref, b_ref, o_ref):
    a = _silu(c_ref[...]).astype(BF16)
    o_ref[...] = _dot(a, w_ref[...].astype(BF16)) + b_ref[...]


def _modulation(c_all, ada_w, ada_b, tn=1536):
    depth, d, n = ada_w.shape
    bp = c_all.shape[0]
    return pl.pallas_call(
        _mod_kernel,
        grid=(depth, n // tn),
        in_specs=[
            pl.BlockSpec((bp, d), lambda l, j: (0, 0)),
            pl.BlockSpec((None, d, tn), lambda l, j: (l, 0, j)),
            pl.BlockSpec((None, 1, tn), lambda l, j: (l, 0, j)),
        ],
        out_specs=pl.BlockSpec((None, bp, tn), lambda l, j: (l, 0, j)),
        out_shape=jax.ShapeDtypeStruct((depth, bp, n), F32),
        compiler_params=_cparams("parallel", "parallel"),
        name="modulation",
    )(c_all, ada_w, ada_b.reshape(depth, 1, n))


def _rms_mod(x, nw, sc, sh):
    ms = jnp.mean(x * x, axis=-1, keepdims=True)
    return (x * lax.rsqrt(ms + NORM_EPS) * nw) * (1.0 + sc) + sh


def _inproj_kernel(x_ref, nw_ref, sh_ref, sc_ref, w_ref, ca_ref, sa_ref, cr_ref, sr_ref,
                   qa_ref, ka_ref, va_ref, qr_ref, kr_ref, vr_ref, gr_ref, cv_ref, z_ref, ba_ref):
    tm = x_ref.shape[0]
    hb = _rms_mod(x_ref[...], nw_ref[...], sc_ref[...], sh_ref[...]).astype(BF16)

    def proj(c0, width):
        return _dot(hb, w_ref[:, c0:c0 + width])

    lane = lax.broadcasted_iota(jnp.int32, (tm, DSA_W), 1)
    first_half = (lane % DSA_HEAD_DIM) < (DSA_HEAD_DIM // 2)

    def rope(y, cos, sin_signed):
        partner = jnp.where(first_half, pltpu.roll(y, DSA_W - DSA_HEAD_DIM // 2, 1),
                            pltpu.roll(y, DSA_HEAD_DIM // 2, 1))
        return y * cos + partner * sin_signed

    wide = lambda ref: jnp.concatenate([ref[...]] * (DSA_W // LANES), axis=1)
    ca, sa, cr, sr = wide(ca_ref), wide(sa_ref), wide(cr_ref), wide(sr_ref)
    qa_ref[...] = rope(proj(_C_QA, DSA_W), ca, sa) * (DSA_HEAD_DIM ** -0.5)
    ka_ref[...] = rope(proj(_C_KA, DSA_W), ca, sa)
    va_ref[...] = proj(_C_VA, DSA_W)
    qr_ref[...] = rope(proj(_C_QR, RET_W), cr, sr)
    kr_ref[...] = rope(proj(_C_KR, RET_W), cr, sr) * (RET_HEAD_DIM ** -0.5)
    vr_ref[...] = proj(_C_VR, RET_W)
    gr_ref[...] = proj(_C_GR, RET_W)
    for s in range(CONV_DIM // GDN_W):
        cv_ref[:, s * GDN_W:(s + 1) * GDN_W] = proj(_C_CV + s * GDN_W, GDN_W)
    z_ref[...] = proj(_C_Z, GDN_W)
    ba_ref[...] = proj(_C_BA, LANES)


def _mod_spec(tm, t, d, col):
    if t >= tm:
        return pl.BlockSpec((None, 1, d), lambda i: ((i * tm) // t, 0, col))
    return pl.BlockSpec((None, tm, d), lambda i: (i, 0, col))


def _inproj(x2, t, mod3, norm_w, w_in_b, layer, tabs, tm):
    m, d = x2.shape
    nt = tabs[0].shape[0] // tm
    widths = (DSA_W,) * 3 + (RET_W,) * 4 + (CONV_DIM, GDN_W, LANES)
    tab_spec = pl.BlockSpec((tm, LANES), lambda i: (i % nt, 0))
    return pl.pallas_call(
        _inproj_kernel,
        grid=(m // tm,),
        in_specs=[
            pl.BlockSpec((tm, d), lambda i: (i, 0)),
            pl.BlockSpec((1, d), lambda i: (0, 0)),
            _mod_spec(tm, t, d, 0),
            _mod_spec(tm, t, d, 1),
            pl.BlockSpec((None, d, IN_COLS_PAD), lambda i: (layer, 0, 0), pipeline_mode=pl.Buffered(1)),
            tab_spec, tab_spec, tab_spec, tab_spec,
        ],
        out_specs=[pl.BlockSpec((tm, w), lambda i: (i, 0)) for w in widths],
        out_shape=[jax.ShapeDtypeStruct((m, w), F32) for w in widths],
        compiler_params=_cparams("parallel"),
        name="inproj",
    )(x2, norm_w.reshape(1, d), mod3, mod3, w_in_b, *tabs)


DSA_BLK = 128


DSA_QBLK = DSA_MAX_WINDOW
DSA_GROUP = 4


def _dsa_kernel(q_ref, kp_ref, kc_ref, vp_ref, vc_ref, o_ref, acc_scr, m_scr, l_scr):
    blk = pl.program_id(2)
    qblk = q_ref.shape[0]
    row = lax.broadcasted_iota(jnp.int32, (2 * DSA_BLK, 2 * DSA_BLK), 0) % DSA_BLK
    col = lax.broadcasted_iota(jnp.int32, (2 * DSA_BLK, 2 * DSA_BLK), 1)
    ok = jnp.logical_and(col >= row, col <= row + DSA_BLK)
    ok_first = jnp.logical_and(ok, jnp.logical_or(col >= DSA_BLK, blk > 0))
    lo = lax.broadcasted_iota(jnp.int32, (DSA_BLK, LANES), 1) < DSA_HEAD_DIM
    neg = -jnp.inf
    halves = lambda x: jnp.where(lo, x[:DSA_BLK], x[DSA_BLK:])

    def rows_of(dil, start, n):
        return pl.ds(start, n) if dil == 1 else pl.ds(start, n, stride=dil)

    def group_softmax(dil, subs):
        n = range(len(subs))
        idx = [rows_of(dil, ph + dil * DSA_BLK * j, DSA_BLK) for ph, j in subs]
        q = [q_ref[i, :] for i in idx]
        k, v = [], []
        for u, (ph, j) in enumerate(subs):
            if j == 0:
                band = rows_of(dil, qblk - dil * DSA_BLK + ph, DSA_BLK)
                k.append(jnp.concatenate([kp_ref[band, :], kc_ref[idx[u], :]], axis=0).astype(BF16))
                v.append(jnp.concatenate([vp_ref[band, :], vc_ref[idx[u], :]], axis=0).astype(BF16))
            else:
                both = rows_of(dil, ph + dil * DSA_BLK * (j - 1), 2 * DSA_BLK)
                k.append(kc_ref[both, :].astype(BF16))
                v.append(vc_ref[both, :].astype(BF16))
        q2 = [jnp.concatenate([jnp.where(lo, q[u], 0.0), jnp.where(lo, 0.0, q[u])], axis=0).astype(BF16)
              for u in n]
        s = [jnp.where(ok_first if subs[u][1] == 0 else ok, _dot_nt(q2[u], k[u]), neg) for u in n]
        mx = [jnp.max(s[u], axis=-1, keepdims=True) for u in n]
        p = [jnp.exp(s[u] - mx[u]) for u in n]
        den = [jnp.sum(p[u], axis=-1, keepdims=True) for u in n]
        pv = [_dot(p[u].astype(BF16), v[u]) for u in n]
        return idx, [(halves(pv[u]), halves(jnp.broadcast_to(mx[u], (2 * DSA_BLK, LANES))),
                      halves(jnp.broadcast_to(den[u], (2 * DSA_BLK, LANES)))) for u in n]

    dils = sorted((d for _, d in DSA_PATTERNS), reverse=True)
    for pi, dil in enumerate(dils):
        subs = [(ph, j) for j in range(qblk // (dil * DSA_BLK)) for ph in range(dil)]
        for g0 in range(0, len(subs), DSA_GROUP):
            idx, tiles = group_softmax(dil, subs[g0:g0 + DSA_GROUP])
            for i, (pv, mx, den) in zip(idx, tiles):
                if pi > 0:
                    m_old = m_scr[i, :]
                    m_new = jnp.maximum(m_old, mx)
                    w_old = jnp.exp(m_old - m_new)
                    w_cur = jnp.exp(mx - m_new)
                    pv = acc_scr[i, :] * w_old + pv * w_cur
                    den = l_scr[i, :] * w_old + den * w_cur
                    mx = m_new
                if pi < len(dils) - 1:
                    acc_scr[i, :] = pv
                    m_scr[i, :] = mx
                    l_scr[i, :] = den
                else:
                    o_ref[i, :] = pv / den


def _dsa_prompt(q, k, v, nb, t):
    assert all(w // d == DSA_BLK for w, d in DSA_PATTERNS) and t % DSA_QBLK == 0
    ngrp = DSA_W // LANES
    r3 = lambda a: a.reshape(nb, t, DSA_W)
    cur = pl.BlockSpec((None, DSA_QBLK, LANES), lambda b, g, i: (b, i, g))
    prv = pl.BlockSpec((None, DSA_QBLK, LANES), lambda b, g, i: (b, jnp.maximum(i - 1, 0), g))
    return pl.pallas_call(
        _dsa_kernel,
        grid=(nb, ngrp, t // DSA_QBLK),
        in_specs=[cur, prv, cur, prv, cur],
        out_specs=cur,
        out_shape=jax.ShapeDtypeStruct((nb, t, DSA_W), F32),
        scratch_shapes=[pltpu.VMEM((DSA_QBLK, LANES), F32)] * 3,
        compiler_params=_cparams("parallel", "parallel", "parallel"),
        name="dsa_prompt",
    )(r3(q), r3(k), r3(k), r3(v), r3(v)).reshape(nb * t, DSA_W)


def _multiplicity(dist):
    total = jnp.zeros(dist.shape, F32)
    for window, dil in DSA_PATTERNS:
        hit = (dist >= 0) & (dist <= window) & ((dist & (dil - 1)) == 0)
        total = total + hit.astype(F32)
    return total


def _dsa_step_kernel(q_ref, kc_ref, vc_ref, kn_ref, vn_ref, o_ref):
    tq = q_ref.shape[0]
    wb = kc_ref.shape[1]
    qi = lax.broadcasted_iota(jnp.int32, (2 * tq, wb), 0) % tq
    w_c = _multiplicity(wb + qi - lax.broadcasted_iota(jnp.int32, (2 * tq, wb), 1))
    qn = lax.broadcasted_iota(jnp.int32, (2 * tq, LANES), 0) % tq
    nn = lax.broadcasted_iota(jnp.int32, (2 * tq, LANES), 1)
    w_n = jnp.where(nn < tq, _multiplicity(qn - nn), 0.0)
    lo = lax.broadcasted_iota(jnp.int32, (tq, LANES), 1) < DSA_HEAD_DIM
    pad = jnp.zeros((LANES - tq, LANES), F32)
    neg = -jnp.inf
    for g in range(DSA_W // LANES):
        cols = slice(g * LANES, (g + 1) * LANES)
        q = q_ref[:, cols]
        q2 = jnp.concatenate([jnp.where(lo, q, 0.0), jnp.where(lo, 0.0, q)], axis=0).astype(BF16)
        kt_c, vt_c = kc_ref[cols, :].astype(BF16), vc_ref[cols, :].astype(BF16)
        k_n = jnp.concatenate([kn_ref[:, cols], pad], axis=0).astype(BF16)
        v_n = jnp.concatenate([vn_ref[:, cols], pad], axis=0).astype(BF16)
        s_c = jnp.where(w_c > 0, _dot(q2, kt_c), neg)
        s_n = jnp.where(w_n > 0, _dot_nt(q2, k_n), neg)
        mx = jnp.maximum(jnp.max(s_c, axis=-1, keepdims=True), jnp.max(s_n, axis=-1, keepdims=True))
        p_c = w_c * jnp.exp(s_c - mx)
        p_n = w_n * jnp.exp(s_n - mx)
        den = jnp.sum(p_c, axis=-1, keepdims=True) + jnp.sum(p_n, axis=-1, keepdims=True)
        o2 = (_dot_nt(p_c.astype(BF16), vt_c) + _dot(p_n.astype(BF16), v_n)) / den
        o_ref[:, cols] = jnp.where(lo, o2[:tq], o2[tq:])


def _dsa_step(q, k_new, v_new, k_cache, v_cache, layer, nb, t):
    wb = k_cache.shape[3]
    new = pl.BlockSpec((None, t, DSA_W), lambda b: (b, 0, 0))
    cache = pl.BlockSpec((None, None, DSA_W, wb), lambda b: (layer, b, 0, 0))
    r3 = lambda a: a.reshape(nb, t, DSA_W)
    return pl.pallas_call(
        _dsa_step_kernel,
        grid=(nb,),
        in_specs=[new, cache, cache, new, new],
        out_specs=new,
        out_shape=jax.ShapeDtypeStruct((nb, t, DSA_W), F32),
        compiler_params=_cparams("parallel"),
        name="dsa_step",
    )(r3(q), k_cache, v_cache, r3(k_new), r3(v_new)).reshape(nb * t, DSA_W)


RET_CHUNK = 256
RET_SEQS_PER_STEP = 2
RET_UNROLL = 4


def _ret_kernel(q_ref, k_ref, v_ref, g_ref, s0_ref, dec_ref, qd_ref, kd_ref, cd_ref, bd_ref, nw_ref,
                o_ref, so_ref, s_scr, *, chunk):
    tb = pl.program_id(1)
    ngrp = RET_W // LANES

    @pl.when(tb == 0)
    def _():
        s_scr[...] = s0_ref[...]

    lo = lax.broadcasted_iota(jnp.int32, (chunk, LANES), 1) < RET_HEAD_DIM
    nw = nw_ref[...]
    bd = bd_ref[...]

    def body(c, carry):
        rows = pl.ds(pl.multiple_of(c * chunk, chunk), chunk)
        for b in range(q_ref.shape[0]):
            for g in range(ngrp):
                cols = slice(g * LANES, (g + 1) * LANES)
                q, k, v = q_ref[b, rows, cols], k_ref[b, rows, cols], v_ref[b, rows, cols]
                kb, vb = k.astype(BF16), v.astype(BF16)
                parts = []
                for hh in range(2):
                    qm = jnp.where(lo if hh == 0 else jnp.logical_not(lo), q, 0.0).astype(BF16)
                    inner = _dot_nt(qm, kb) * dec_ref[2 * g + hh]
                    parts.append(_dot(inner.astype(BF16), vb))
                s_prev = s_scr[b, g]
                o = jnp.where(lo, parts[0], parts[1]) + _dot(q.astype(BF16), s_prev.astype(BF16)) * qd_ref[g]
                s_scr[b, g] = s_prev * cd_ref[g] + bd * _dot_tn((k * kd_ref[g]).astype(BF16), vb)
                o2 = o * o
                ms = jnp.where(lo, jnp.sum(jnp.where(lo, o2, 0.0), axis=-1, keepdims=True),
                               jnp.sum(jnp.where(lo, 0.0, o2), axis=-1, keepdims=True)) * (1.0 / RET_HEAD_DIM)
                o_ref[b, rows, cols] = o * lax.rsqrt(ms + NORM_EPS) * nw * _silu(g_ref[b, rows, cols])
        return carry

    nchunk = q_ref.shape[1] // chunk
    lax.fori_loop(0, nchunk, body, 0, unroll=min(RET_UNROLL, nchunk))

    @pl.when(tb == pl.num_programs(1) - 1)
    def _():
        so_ref[...] = s_scr[...]


def _ret_tables(chunk):
    log_gamma = jnp.log(1.0 - 2.0 ** (-5.0 - jnp.arange(RET_HEADS, dtype=F32)))
    i = jnp.arange(chunk, dtype=F32)
    diff = i[:, None] - i[None, :]
    causal = diff >= 0
    decay = jnp.where(causal[None], jnp.exp(log_gamma[:, None, None] * jnp.where(causal, diff, 0.0)[None]), 0.0)
    per_lane = lambda a: jnp.repeat(a, RET_HEAD_DIM, axis=0).reshape(RET_W // LANES, LANES, -1)
    q_dec = per_lane(jnp.exp(log_gamma[:, None] * (i[None, :] + 1.0))).transpose(0, 2, 1)
    k_dec = per_lane(jnp.exp(log_gamma[:, None] * (chunk - 1.0 - i)[None, :])).transpose(0, 2, 1)
    c_dec = jnp.broadcast_to(per_lane(jnp.exp(log_gamma * chunk)[:, None]), (RET_W // LANES, LANES, LANES))
    head_of = jnp.arange(LANES) // RET_HEAD_DIM
    block_diag = (head_of[:, None] == head_of[None, :]).astype(F32)
    return decay, q_dec, k_dec, c_dec, block_diag


def _retention(q, k, v, gate, state_bd, ret_norm, nb, t):
    chunk = RET_CHUNK if t % RET_CHUNK == 0 else t
    tblk = min(t, 1024)
    ngrp = RET_W // LANES
    bblk = _seqs_per_step(nb, t, RET_SEQS_PER_STEP)
    decay, q_dec, k_dec, c_dec, block_diag = _ret_tables(chunk)
    r3 = lambda a: a.reshape(nb, t, RET_W)
    tok = pl.BlockSpec((bblk, tblk, RET_W), lambda b, i: (b, i, 0))
    st = pl.BlockSpec((bblk, ngrp, LANES, LANES), lambda b, i: (b, 0, 0, 0))
    full = lambda a: pl.BlockSpec(a.shape, lambda b, i: (0,) * a.ndim)
    nw = jnp.tile(ret_norm, LANES // RET_HEAD_DIM).reshape(1, LANES)
    o, s_new = pl.pallas_call(
        functools.partial(_ret_kernel, chunk=chunk),
        grid=(nb // bblk, t // tblk),
        in_specs=[tok, tok, tok, tok, st, full(decay), full(q_dec), full(k_dec), full(c_dec),
                  full(block_diag), full(nw)],
        out_specs=[tok, st],
        out_shape=[jax.ShapeDtypeStruct((nb, t, RET_W), F32),
                   jax.ShapeDtypeStruct((nb, ngrp, LANES, LANES), F32)],
        scratch_shapes=[pltpu.VMEM((bblk, ngrp, LANES, LANES), F32)],
        compiler_params=_cparams("parallel", "arbitrary"),
        name="retention",
    )(r3(q), r3(k), r3(v), r3(gate), state_bd, decay, q_dec, k_dec, c_dec, block_diag, nw)
    return o.reshape(nb * t, RET_W), s_new


def _to_block_diag(s):
    nb = s.shape[0]
    s = s.reshape(nb, 2, 2, RET_HEAD_DIM, RET_HEAD_DIM)
    z = jnp.zeros_like(s[:, :, 0])
    top = jnp.concatenate([s[:, :, 0], z], axis=-1)
    bot = jnp.concatenate([z, s[:, :, 1]], axis=-1)
    return jnp.concatenate([top, bot], axis=-2)


def _from_block_diag(s):
    h = RET_HEAD_DIM
    return jnp.stack([s[:, :, :h, :h], s[:, :, h:, h:]], axis=2).reshape(s.shape[0], RET_HEADS, h, h)


GDN_PREP_CHUNKS_PER_ITER = 8
GDN_SCAN_SEQS_PER_STEP = 2
GDN_SCAN_UNROLL = 4


def _softplus(x):
    return jnp.maximum(x, 0.0) + jnp.log1p(jnp.exp(-jnp.abs(x)))


def _gdn_prep_kernel(cv_ref, halo_ref, hist_ref, ba_ref, cw_ref, ab_ref,
                     qg_ref, kd_ref, u_ref, w_ref, at_ref, el_ref,
                     xp_scr, cs_scr, beta_scr, g_scr, *, chunk, cpi):
    tb = pl.program_id(1)
    nseq, tblk = cv_ref.shape[0], cv_ref.shape[1]
    nhist = CONV_WIDTH - 1
    ncg = CONV_DIM // LANES
    ab = ab_ref[...]
    for b in range(nseq):
        for cg in range(ncg):
            xp_scr[b, cg, SUBLANES:, :] = cv_ref[b, :, cg * LANES:(cg + 1) * LANES]

        @pl.when(tb == 0)
        def _():
            for cg in range(ncg):
                xp_scr[b, cg, :SUBLANES, :] = hist_ref[b, :, cg * LANES:(cg + 1) * LANES]

        @pl.when(tb > 0)
        def _():
            for cg in range(ncg):
                xp_scr[b, cg, :SUBLANES, :] = halo_ref[b, :, cg * LANES:(cg + 1) * LANES]

        for cg in range(ncg):
            cols = slice(cg * LANES, (cg + 1) * LANES)
            acc = None
            for i in range(CONV_WIDTH):
                start = SUBLANES - nhist + i
                term = xp_scr[b, cg, pl.ds(start, tblk, stride=1), :] * cw_ref[i:i + 1, cols]
                acc = term if acc is None else acc + term
            cs_scr[b, :, cols] = _silu(acc)

        ba = ba_ref[b]
        beta_scr[b] = jax.nn.sigmoid(ba)
        g_scr[b] = -jnp.exp(ab[0:1, :]) * _softplus(ba + ab[1:2, :])

    ri = lax.broadcasted_iota(jnp.int32, (chunk, chunk), 0)
    ci = lax.broadcasted_iota(jnp.int32, (chunk, chunk), 1)
    incl = ri >= ci
    strict = ri > ci
    tri = incl.astype(F32)
    lane_pick = (lax.broadcasted_iota(jnp.int32, (SUBLANES, LANES), 0)
                 == lax.broadcasted_iota(jnp.int32, (SUBLANES, LANES), 1)).astype(F32)
    base = min(SUBLANES, chunk)
    assert chunk % base == 0 and (chunk // base) & (chunk // base - 1) == 0
    same_blk = [(ri >> sh) == (ci >> sh) for sh in range(int(math.log2(base)), int(math.log2(chunk)) + 1)]

    def l2n(x):
        return x * lax.rsqrt(jnp.sum(x * x, axis=-1, keepdims=True) + NORM_EPS)

    def step(it, carry):
        probs = []
        for b, cc in [(b, cc) for b in range(nseq) for cc in range(cpi)]:
            c = it * cpi + cc
            rows = pl.ds(pl.multiple_of(c * chunk, chunk), chunk)
            beta_c = beta_scr[b, rows, :]
            gcol = _dot(tri, g_scr[b, rows, :], precision=HIGHEST)
            grow = _dot_nt(lane_pick, gcol, precision=HIGHEST)
            for h in range(GDN_HEADS):
                q = l2n(cs_scr[b, rows, h * LANES:(h + 1) * LANES])
                k = l2n(cs_scr[b, rows, GDN_W + h * LANES:GDN_W + (h + 1) * LANES])
                v = cs_scr[b, rows, 2 * GDN_W + h * LANES:2 * GDN_W + (h + 1) * LANES]
                beta = jnp.broadcast_to(beta_c[:, h:h + 1], (chunk, LANES))
                gc = jnp.broadcast_to(gcol[:, GDN_HEADS + h:GDN_HEADS + h + 1], (chunk, LANES))
                gdiff = gc[:, :chunk] - grow[GDN_HEADS + h:GDN_HEADS + h + 1, :]
                dmask = jnp.where(incl, jnp.exp(jnp.where(incl, gdiff, 0.0)), 0.0)
                kb = k * beta
                kbf = k.astype(BF16)
                lower = jnp.where(strict, _dot_nt(kb.astype(BF16), kbf) * dmask, 0.0)
                npow = jnp.where(same_blk[0], -lower, 0.0)
                probs.append(dict(b=b, c=c, rows=rows, h=h, q=q, k=k, v=v, beta=beta, gc=gc, dmask=dmask, kb=kb,
                                  kbf=kbf, lower=lower, npow=npow, qmat=npow))
        for _ in range(int(math.log2(base)) - 1):
            for p in probs:
                nb = p["npow"].astype(BF16)
                p["npow"] = _dot(nb, nb)
            for p in probs:
                p["qmat"] = p["qmat"] + p["npow"] + _dot(p["qmat"].astype(BF16), p["npow"].astype(BF16))
        for lvl in range(1, len(same_blk)):
            for p in probs:
                cb = jnp.where(jnp.logical_and(same_blk[lvl], jnp.logical_not(same_blk[lvl - 1])),
                               p["lower"], 0.0)
                p["x"] = cb + _dot(p["qmat"].astype(BF16), cb.astype(BF16))
            for p in probs:
                qb16 = p["qmat"].astype(BF16)
                p["qmat"] = p["qmat"] - p["x"] - _dot(p["x"].astype(BF16), qb16)
        for p in probs:
            b, rows, h, gc = p["b"], p["rows"], p["h"], p["gc"]
            cols = slice(h * LANES, (h + 1) * LANES)
            eg = jnp.exp(gc)
            rhs_u = p["v"] * p["beta"]
            rhs_w = p["kb"] * eg
            qb = p["qmat"].astype(BF16)
            qs = p["q"] * (GDN_HEAD_DIM ** -0.5)
            g_last = gc[chunk - 1:chunk, :]
            qg_ref[b, rows, cols] = (qs * eg).astype(qg_ref.dtype)
            kd_ref[b, rows, cols] = (p["k"] * jnp.exp(g_last - gc)).astype(kd_ref.dtype)
            u_ref[b, rows, cols] = rhs_u + _dot(qb, rhs_u.astype(BF16))
            w_ref[b, rows, cols] = (rhs_w + _dot(qb, rhs_w.astype(BF16))).astype(w_ref.dtype)
            at_ref[b, rows, h * chunk:(h + 1) * chunk] = (
                _dot_nt(qs.astype(BF16), p["kbf"]) * p["dmask"]).astype(at_ref.dtype)
            el_ref[b, pl.ds(pl.multiple_of(p["c"] * SUBLANES, SUBLANES), SUBLANES), cols] = jnp.broadcast_to(
                jnp.exp(g_last), (SUBLANES, LANES))
        return carry

    lax.fori_loop(0, tblk // (chunk * cpi), step, 0)


def _gdn_scan_kernel(qg_ref, kd_ref, u_ref, w_ref, at_ref, el_ref, z_ref, s0_ref, nw_ref,
                     o_ref, so_ref, s_scr, *, chunk):
    tb = pl.program_id(1)

    @pl.when(tb == 0)
    def _():
        s_scr[...] = s0_ref[...]

    nw = nw_ref[...]
    probs = [(b, h) for b in range(qg_ref.shape[0]) for h in range(GDN_HEADS)]

    def step(c, carry):
        rows = pl.ds(pl.multiple_of(c * chunk, chunk), chunk)
        erow = pl.ds(pl.multiple_of(c * SUBLANES, SUBLANES), SUBLANES)
        cols = lambda h: slice(h * LANES, (h + 1) * LANES)
        s_prev = [s_scr[b, h] for b, h in probs]
        sb = [s.astype(BF16) for s in s_prev]
        v_new = [u_ref[b, rows, cols(h)] - _dot(w_ref[b, rows, cols(h)].astype(BF16), sb[i])
                 for i, (b, h) in enumerate(probs)]
        vb = [v.astype(BF16) for v in v_new]
        for i, (b, h) in enumerate(probs):
            el = el_ref[b, erow, cols(h)][0:1, :]
            s_scr[b, h] = s_prev[i] * el + _dot_tn(kd_ref[b, rows, cols(h)].astype(BF16), vb[i])
        for i, (b, h) in enumerate(probs):
            attn = at_ref[b, rows, h * chunk:(h + 1) * chunk].astype(BF16)
            o = _dot(qg_ref[b, rows, cols(h)].astype(BF16), sb[i]) + _dot(attn, vb[i])
            ms = jnp.mean(o * o, axis=-1, keepdims=True)
            o_ref[b, rows, cols(h)] = o * lax.rsqrt(ms + NORM_EPS) * nw * _silu(z_ref[b, rows, cols(h)])
        return carry

    nchunk = qg_ref.shape[1] // chunk
    lax.fori_loop(0, nchunk, step, 0, unroll=min(GDN_SCAN_UNROLL, nchunk))

    @pl.when(tb == pl.num_programs(1) - 1)
    def _():
        so_ref[...] = s_scr[...]


def _gated_delta(cv, z, ba, conv_hist, states, layer, conv_w, a_log, dt_bias, gdn_norm, nb, t):
    chunk = CHUNK if t % CHUNK == 0 else t
    tblk = min(t, 512)
    nhist = CONV_WIDTH - 1
    nchunk_blk = tblk // chunk
    hist_pad = jnp.concatenate([jnp.zeros((nb, SUBLANES - nhist, CONV_DIM), F32), conv_hist], axis=1)
    cw_pad = jnp.concatenate([conv_w, jnp.zeros((SUBLANES - CONV_WIDTH, CONV_DIM), F32)], axis=0)
    ab = jnp.zeros((SUBLANES, LANES), F32)
    ab = ab.at[0, GDN_HEADS:2 * GDN_HEADS].set(a_log).at[1, GDN_HEADS:2 * GDN_HEADS].set(dt_bias)
    cv3 = cv.reshape(nb, t, CONV_DIM)
    pblk = _seqs_per_step(nb, t, 2) if t < CHUNK else 1
    tok = lambda w: pl.BlockSpec((pblk, tblk, w), lambda b, i: (b, i, 0))
    full = lambda a: pl.BlockSpec(a.shape, lambda b, i: (0,) * a.ndim)
    halo = pl.BlockSpec((pblk, SUBLANES, CONV_DIM),
                        lambda b, i: (b, jnp.maximum(i * (tblk // SUBLANES) - 1, 0), 0))
    el_spec = pl.BlockSpec((pblk, nchunk_blk * SUBLANES, GDN_W), lambda b, i: (b, i, 0))
    tok_shape = lambda w, dt=F32: jax.ShapeDtypeStruct((nb, t, w), dt)
    el_shape = jax.ShapeDtypeStruct((nb, (t // chunk) * SUBLANES, GDN_W), F32)
    opd = BF16 if chunk % (2 * SUBLANES) == 0 else F32
    qg, kd, u, w, attn, el = pl.pallas_call(
        functools.partial(_gdn_prep_kernel, chunk=chunk, cpi=min(GDN_PREP_CHUNKS_PER_ITER, nchunk_blk)),
        grid=(nb // pblk, t // tblk),
        in_specs=[tok(CONV_DIM), halo, pl.BlockSpec((pblk, SUBLANES, CONV_DIM), lambda b, i: (b, 0, 0)),
                  tok(LANES), full(cw_pad), full(ab)],
        out_specs=[tok(GDN_W), tok(GDN_W), tok(GDN_W), tok(GDN_W), tok(GDN_HEADS * chunk), el_spec],
        out_shape=[tok_shape(GDN_W, opd), tok_shape(GDN_W, opd), tok_shape(GDN_W), tok_shape(GDN_W, opd),
                   tok_shape(GDN_HEADS * chunk, opd), el_shape],
        scratch_shapes=[pltpu.VMEM((pblk, CONV_DIM // LANES, tblk + SUBLANES, LANES), F32),
                        pltpu.VMEM((pblk, tblk, CONV_DIM), F32),
                        pltpu.VMEM((pblk, tblk, LANES), F32), pltpu.VMEM((pblk, tblk, LANES), F32)],
        compiler_params=_cparams("parallel", "parallel"),
        name="gdn_prep",
    )(cv3, cv3, hist_pad, ba.reshape(nb, t, LANES), cw_pad, ab)

    bblk = _seqs_per_step(nb, t, GDN_SCAN_SEQS_PER_STEP)
    stok = lambda w: pl.BlockSpec((bblk, tblk, w), lambda b, i: (b, i, 0))
    sel_spec = pl.BlockSpec((bblk, nchunk_blk * SUBLANES, GDN_W), lambda b, i: (b, i, 0))
    st_in = pl.BlockSpec((None, bblk, GDN_HEADS, LANES, LANES), lambda b, i: (layer, b, 0, 0, 0))
    st = pl.BlockSpec((bblk, GDN_HEADS, LANES, LANES), lambda b, i: (b, 0, 0, 0))
    nw = gdn_norm.reshape(1, LANES)
    o, s_new = pl.pallas_call(
        functools.partial(_gdn_scan_kernel, chunk=chunk),
        grid=(nb // bblk, t // tblk),
        in_specs=[stok(GDN_W), stok(GDN_W), stok(GDN_W), stok(GDN_W), stok(GDN_HEADS * chunk), sel_spec,
                  stok(GDN_W), st_in, full(nw)],
        out_specs=[stok(GDN_W), st],
        out_shape=[tok_shape(GDN_W), jax.ShapeDtypeStruct((nb, GDN_HEADS, LANES, LANES), F32)],
        scratch_shapes=[pltpu.VMEM((bblk, GDN_HEADS, LANES, LANES), F32)],
        compiler_params=_cparams("parallel", "arbitrary"),
        name="gdn_scan",
    )(qg, kd, u, w, attn, el, z.reshape(nb, t, GDN_W), states, nw)
    conv_new = jnp.concatenate([conv_hist, cv3], axis=1)[:, -nhist:]
    return o.reshape(nb * t, GDN_W), s_new, conv_new


FFN_TILE = 256


def _out_ffn_kernel(x_ref, oa_ref, orr_ref, oc_ref, g1_ref, sh2_ref, sc2_ref, g2_ref, nw_ref, wo_ref, wg_ref,
                    wu_ref, wd_ref, fn_ref, out_ref, *, final):
    mix = (_dot(oa_ref[...].astype(BF16), wo_ref[0:DSA_W, :])
           + _dot(orr_ref[...].astype(BF16), wo_ref[DSA_W:DSA_W + RET_W, :])
           + _dot(oc_ref[...].astype(BF16), wo_ref[DSA_W + RET_W:, :]))
    x1 = x_ref[...] + g1_ref[...] * mix
    hb = _rms_mod(x1, nw_ref[...], sc2_ref[...], sh2_ref[...]).astype(BF16)
    acc = None
    for j in range(wg_ref.shape[1] // FFN_TILE):
        cols = slice(j * FFN_TILE, (j + 1) * FFN_TILE)
        act = (_silu(_dot(hb, wg_ref[:, cols])) * _dot(hb, wu_ref[:, cols])).astype(BF16)
        down = _dot(act, wd_ref[cols, :])
        acc = down if acc is None else acc + down
    x2 = x1 + g2_ref[...] * acc
    if final:
        ms = jnp.mean(x2 * x2, axis=-1, keepdims=True)
        x2 = x2 * lax.rsqrt(ms + NORM_EPS) * fn_ref[...]
    out_ref[...] = x2


def _out_ffn(x2, t, oa, orr, oc, mod3, norm_w, w_out_b, wg_b, wu_b, wd_b, layer, final_norm, final, tm):
    m, d = x2.shape
    row = lambda w: pl.BlockSpec((tm, w), lambda i: (i, 0))
    const = lambda a: pl.BlockSpec(a.shape, lambda i: (0,) * a.ndim, pipeline_mode=pl.Buffered(1))
    wspec = lambda a: pl.BlockSpec((None,) + a.shape[1:], lambda i: (layer,) + (0,) * (a.ndim - 1),
                                   pipeline_mode=pl.Buffered(1))
    nw = norm_w.reshape(1, d)
    fn = final_norm.reshape(1, d)
    return pl.pallas_call(
        functools.partial(_out_ffn_kernel, final=final),
        grid=(m // tm,),
        in_specs=[row(d), row(DSA_W), row(RET_W), row(GDN_W),
                  _mod_spec(tm, t, d, 2), _mod_spec(tm, t, d, 3), _mod_spec(tm, t, d, 4), _mod_spec(tm, t, d, 5),
                  const(nw), wspec(w_out_b), wspec(wg_b), wspec(wu_b), wspec(wd_b), const(fn)],
        out_specs=row(d),
        out_shape=jax.ShapeDtypeStruct((m, d), F32),
        compiler_params=_cparams("parallel"),
        name="out_ffn",
    )(x2, oa, orr, oc, mod3, mod3, mod3, mod3, nw, w_out_b, wg_b, wu_b, wd_b, fn)


def _rope_tables(pos, inv_freq):
    reps = LANES // (2 * inv_freq.shape[0])
    inv_lane = jnp.tile(jnp.concatenate([inv_freq, inv_freq]), reps)
    sign_lane = jnp.tile(jnp.concatenate([-jnp.ones_like(inv_freq), jnp.ones_like(inv_freq)]), reps)
    ang = pos.astype(F32)[:, None] * inv_lane[None, :]
    return jnp.cos(ang), jnp.sin(ang) * sign_lane[None, :]


def _trunk(x, modp, pos, k_hist, v_hist, s_ret, s_gdn, conv_hist, wts):
    (norm_mix, norm_ffn, w_in_b, ret_norm, conv_w, a_log, dt_bias, gdn_norm, w_out_b, wg_b, wu_b, wd_b,
     final_norm) = wts
    nb, t, d = x.shape
    m = nb * t
    depth = w_in_b.shape[0]
    tm = min(TOKEN_TILE, m)
    inv_a = 1.0 / (ROPE_THETA ** (jnp.arange(0, DSA_HEAD_DIM, 2, dtype=F32) / DSA_HEAD_DIM))
    inv_r = 1.0 / (10000.0 ** jnp.linspace(0.0, 1.0, RET_HEAD_DIM // 2, dtype=F32))
    tabs = _rope_tables(pos, inv_a) + _rope_tables(pos, inv_r)
    if t < tm:
        tabs = tuple(jnp.tile(a, (tm // t, 1)) for a in tabs)
    x2 = x.reshape(m, d)
    ks, vs, rs, gs, cs = [], [], [], [], []
    for l in range(depth):
        if t >= tm:
            mod3 = modp[l].reshape(nb, 1, 6 * d)
        else:
            mod3 = jnp.repeat(modp[l], t, axis=0).reshape(m // tm, tm, 6 * d)
        qa, ka, va, qr, kr, vr, gr, cv, z, ba = _inproj(x2, t, mod3, norm_mix[l], w_in_b, l, tabs, tm)
        if k_hist is None:
            oa = _dsa_prompt(qa, ka, va, nb, t)
        else:
            oa = _dsa_step(qa, ka, va, k_hist, v_hist, l, nb, t)
        orr, sr = _retention(qr, kr, vr, gr, _to_block_diag(s_ret[l]), ret_norm[l], nb, t)
        oc, sg, cvn = _gated_delta(cv, z, ba, conv_hist[l], s_gdn, l, conv_w[l], a_log[l], dt_bias[l],
                                   gdn_norm[l], nb, t)
        x2 = _out_ffn(x2, t, oa, orr, oc, mod3, norm_ffn[l], w_out_b, wg_b, wu_b, wd_b, l,
                      final_norm, l == depth - 1, tm)
        keep = min(t, DSA_MAX_WINDOW)
        window = lambda a: a.reshape(nb, t, DSA_W)[:, t - keep:].reshape(nb, keep, DSA_HEADS, DSA_HEAD_DIM)
        ks.append(window(ka))
        vs.append(window(va))
        rs.append(_from_block_diag(sr))
        gs.append(sg)
        cs.append(cvn)
    return (x2.reshape(nb, t, d), jnp.stack(ks), jnp.stack(vs), jnp.stack(rs), jnp.stack(gs), jnp.stack(cs))


def kernel(x_prompt, x_sample, cache_win_k, cache_win_v, state_ret, state_gdn, state_conv, c_prompt, c_sample, ada_w, ada_b, norm_mix, norm_ffn, w_in, ret_norm, conv_w, a_log, dt_bias, gdn_norm, w_out, w_gate, w_up, w_down, final_norm):
    nb, t_p, d = x_prompt.shape
    db, t_s, _ = x_sample.shape
    depth = ada_w.shape[0]
    rows = nb + db
    rows_pad = -(-rows // SUBLANES) * SUBLANES
    c_all = jnp.concatenate([c_prompt, c_sample, jnp.zeros((rows_pad - rows, d), F32)], axis=0)
    mod = _modulation(c_all, ada_w, ada_b)
    w_in_b = jnp.pad(w_in, ((0, 0), (0, 0), (0, IN_COLS_PAD - IN_COLS))).astype(BF16)
    wts = (norm_mix, norm_ffn, w_in_b, ret_norm, conv_w, a_log, dt_bias, gdn_norm, w_out.astype(BF16),
           w_gate.astype(BF16), w_up.astype(BF16), w_down.astype(BF16), final_norm)

    zr = jnp.zeros((depth, nb, RET_HEADS, RET_HEAD_DIM, RET_HEAD_DIM), F32)
    zg = jnp.zeros((depth, nb, GDN_HEADS, GDN_HEAD_DIM, GDN_HEAD_DIM), F32)
    zc = jnp.zeros((depth, nb, CONV_WIDTH - 1, CONV_DIM), F32)
    y_p, kp, vp, rp, gp, cp = _trunk(x_prompt, mod[:, :nb], jnp.arange(t_p, dtype=jnp.int32),
                                     None, None, zr, zg, zc, wts)
    wb = cache_win_k.shape[2]
    y_s, ks, vs, rs, gs, cs = _trunk(x_sample, mod[:, nb:rows], PAST_LEN + jnp.arange(t_s, dtype=jnp.int32),
                                     cache_win_k.reshape(depth, db, wb, DSA_W).transpose(0, 1, 3, 2),
                                     cache_win_v.reshape(depth, db, wb, DSA_W).transpose(0, 1, 3, 2),
                                     state_ret, state_gdn, state_conv, wts)
    return (y_p, y_s, kp, vp, rp, gp, cp, ks, vs, rs, gs, cs)
```

```python
import functools
import math

import jax
import jax.numpy as jnp
from jax import lax
from jax.experimental import pallas as pl
from jax.experimental.pallas import tpu as pltpu

F32 = jnp.float32
BF16 = jnp.bfloat16
HIGHEST = lax.Precision.HIGHEST

DSA_HEAD_DIM = 64
DSA_HEADS = 4
DSA_PATTERNS = ((128, 1), (512, 4), (2048, 16))
DSA_MAX_WINDOW = 2048
ROPE_THETA = 10000.0
RET_HEAD_DIM = 64
RET_HEADS = 4
GDN_HEAD_DIM = 128
GDN_HEADS = 4
CONV_WIDTH = 4
CHUNK = 64
NORM_EPS = 1e-6
PAST_LEN = 16384

DSA_W = DSA_HEADS * DSA_HEAD_DIM
RET_W = RET_HEADS * RET_HEAD_DIM
GDN_W = GDN_HEADS * GDN_HEAD_DIM
CONV_DIM = 3 * GDN_W
LANES = 128
SUBLANES = 8
VMEM_LIMIT = 56 * 1024 * 1024
TOKEN_TILE = 512

_C_QA, _C_KA, _C_VA = 0, DSA_W, 2 * DSA_W
_C_QR = 3 * DSA_W
_C_KR, _C_VR, _C_GR = _C_QR + RET_W, _C_QR + 2 * RET_W, _C_QR + 3 * RET_W
_C_CV = _C_QR + 4 * RET_W
_C_Z = _C_CV + CONV_DIM
_C_BA = _C_Z + GDN_W
IN_COLS = _C_BA + 2 * GDN_HEADS
IN_COLS_PAD = _C_BA + LANES


def _cparams(*sem):
    return pltpu.CompilerParams(dimension_semantics=sem, vmem_limit_bytes=VMEM_LIMIT)


def _dot(a, b, **kw):
    return jnp.dot(a, b, preferred_element_type=F32, **kw)


def _dot_nt(a, b, **kw):
    return lax.dot_general(a, b, (((1,), (1,)), ((), ())), preferred_element_type=F32, **kw)


def _dot_tn(a, b, **kw):
    return lax.dot_general(a, b, (((0,), (0,)), ((), ())), preferred_element_type=F32, **kw)


def _silu(x):
    return x * jax.nn.sigmoid(x)


def _seqs_per_step(nb, t, base):
    return math.gcd(nb, base * (4 if t < CHUNK else 1))


def _mod_kernel(c_ref, w_ref, b_ref, o_ref):
    a = _silu(c_ref[...]).astype(BF16)
    o_ref[...] = _dot(a, w_ref[...].astype(BF16)) + b_ref[...]


def _modulation(c_all, ada_w, ada_b, tn=1536):
    depth, d, n = ada_w.shape
    bp = c_all.shape[0]
    return pl.pallas_call(
        _mod_kernel,
        grid=(depth, n // tn),
        in_specs=[
            pl.BlockSpec((bp, d), lambda l, j: (0, 0)),
            pl.BlockSpec((None, d, tn), lambda l, j: (l, 0, j)),
            pl.BlockSpec((None, 1, tn), lambda l, j: (l, 0, j)),
        ],
        out_specs=pl.BlockSpec((None, bp, tn), lambda l, j: (l, 0, j)),
        out_shape=jax.ShapeDtypeStruct((depth, bp, n), F32),
        compiler_params=_cparams("parallel", "parallel"),
        name="modulation",
    )(c_all, ada_w, ada_b.reshape(depth, 1, n))


def _rms_mod(x, nw, sc, sh):
    ms = jnp.mean(x * x, axis=-1, keepdims=True)
    return (x * lax.rsqrt(ms + NORM_EPS) * nw) * (1.0 + sc) + sh


def _causal_conv_silu(xp_scr, seq, cw_ref, cg, nrows):
    cols = slice(cg * LANES, (cg + 1) * LANES)
    acc = None
    for i in range(CONV_WIDTH):
        start = SUBLANES - (CONV_WIDTH - 1) + i
        term = xp_scr[seq, cg, pl.ds(start, nrows, stride=1), :] * cw_ref[i:i + 1, cols]
        acc = term if acc is None else acc + term
    return _silu(acc)


def _inproj_kernel(x_ref, nw_ref, sh_ref, sc_ref, w_ref, ca_ref, sa_ref, cr_ref, sr_ref, *rest, tiles_per_seq):
    if tiles_per_seq:
        hist_ref, cw_ref = rest[:2]
        rest = rest[2:]
    qa_ref, ka_ref, va_ref, qr_ref, kr_ref, vr_ref, gr_ref, cv_ref, z_ref, ba_ref = rest[:10]
    tm = x_ref.shape[0]
    if tiles_per_seq:
        tail_ref, xp_scr = rest[10:]

        @pl.when(pl.program_id(0) % tiles_per_seq == 0)
        def _():
            tail_ref[...] = hist_ref[...]

    hb = _rms_mod(x_ref[...], nw_ref[...], sc_ref[...], sh_ref[...]).astype(BF16)

    def proj(c0, width):
        return _dot(hb, w_ref[:, c0:c0 + width])

    lane = lax.broadcasted_iota(jnp.int32, (tm, DSA_W), 1)
    first_half = (lane % DSA_HEAD_DIM) < (DSA_HEAD_DIM // 2)

    def rope(y, cos, sin_signed):
        partner = jnp.where(first_half, pltpu.roll(y, DSA_W - DSA_HEAD_DIM // 2, 1),
                            pltpu.roll(y, DSA_HEAD_DIM // 2, 1))
        return y * cos + partner * sin_signed

    wide = lambda ref: jnp.concatenate([ref[...]] * (DSA_W // LANES), axis=1)
    ca, sa, cr, sr = wide(ca_ref), wide(sa_ref), wide(cr_ref), wide(sr_ref)
    qa_ref[...] = rope(proj(_C_QA, DSA_W), ca, sa) * (DSA_HEAD_DIM ** -0.5)
    ka_ref[...] = rope(proj(_C_KA, DSA_W), ca, sa)
    va_ref[...] = proj(_C_VA, DSA_W)
    qr_ref[...] = rope(proj(_C_QR, RET_W), cr, sr)
    kr_ref[...] = rope(proj(_C_KR, RET_W), cr, sr) * (RET_HEAD_DIM ** -0.5)
    vr_ref[...] = proj(_C_VR, RET_W)
    gr_ref[...] = proj(_C_GR, RET_W)
    for s in range(CONV_DIM // GDN_W):
        y = proj(_C_CV + s * GDN_W, GDN_W)
        if not tiles_per_seq:
            cv_ref[:, s * GDN_W:(s + 1) * GDN_W] = y
            continue
        for j in range(GDN_W // LANES):
            cg = s * (GDN_W // LANES) + j
            cols = slice(cg * LANES, (cg + 1) * LANES)
            yj = y[:, j * LANES:(j + 1) * LANES]
            xp_scr[0, cg, :SUBLANES, :] = tail_ref[:, cols]
            xp_scr[0, cg, SUBLANES:, :] = yj
            cv_ref[:, cols] = _causal_conv_silu(xp_scr, 0, cw_ref, cg, tm)
            tail_ref[:, cols] = yj[tm - SUBLANES:, :]
    z_ref[...] = proj(_C_Z, GDN_W)
    ba_ref[...] = proj(_C_BA, LANES)


def _mod_spec(tm, t, d, col):
    if t >= tm:
        return pl.BlockSpec((None, 1, d), lambda i: ((i * tm) // t, 0, col))
    return pl.BlockSpec((None, tm, d), lambda i: (i, 0, col))


def _conv_operands(conv_hist, conv_w):
    nb = conv_hist.shape[0]
    hist_pad = jnp.concatenate([jnp.zeros((nb, SUBLANES - (CONV_WIDTH - 1), CONV_DIM), F32), conv_hist], axis=1)
    cw_pad = jnp.concatenate([conv_w, jnp.zeros((SUBLANES - CONV_WIDTH, CONV_DIM), F32)], axis=0)
    return hist_pad, cw_pad


def _inproj(x2, t, mod3, norm_w, w_in_b, layer, tabs, tm, conv_hist, conv_w):
    m, d = x2.shape
    nt = tabs[0].shape[0] // tm
    widths = (DSA_W,) * 3 + (RET_W,) * 4 + (CONV_DIM, GDN_W, LANES)
    tab_spec = pl.BlockSpec((tm, LANES), lambda i: (i % nt, 0))
    tiles_per_seq = t // tm if t % tm == 0 else 0
    in_specs = [
        pl.BlockSpec((tm, d), lambda i: (i, 0)),
        pl.BlockSpec((1, d), lambda i: (0, 0)),
        _mod_spec(tm, t, d, 0),
        _mod_spec(tm, t, d, 1),
        pl.BlockSpec((None, d, IN_COLS_PAD), lambda i: (layer, 0, 0), pipeline_mode=pl.Buffered(1)),
        tab_spec, tab_spec, tab_spec, tab_spec,
    ]
    operands = [x2, norm_w.reshape(1, d), mod3, mod3, w_in_b, *tabs]
    out_specs = [pl.BlockSpec((tm, w), lambda i: (i, 0)) for w in widths]
    out_shape = [jax.ShapeDtypeStruct((m, w), F32) for w in widths]
    scratch = []
    if tiles_per_seq:
        hist_pad, cw_pad = _conv_operands(conv_hist, conv_w)
        seq_tail = pl.BlockSpec((None, SUBLANES, CONV_DIM), lambda i: (i // tiles_per_seq, 0, 0))
        in_specs += [seq_tail, pl.BlockSpec(cw_pad.shape, lambda i: (0, 0))]
        operands += [hist_pad, cw_pad]
        out_specs.append(seq_tail)
        out_shape.append(jax.ShapeDtypeStruct((m // t, SUBLANES, CONV_DIM), F32))
        scratch.append(pltpu.VMEM((1, CONV_DIM // LANES, tm + SUBLANES, LANES), F32))
    outs = pl.pallas_call(
        functools.partial(_inproj_kernel, tiles_per_seq=tiles_per_seq),
        grid=(m // tm,),
        in_specs=in_specs,
        out_specs=out_specs,
        out_shape=out_shape,
        scratch_shapes=scratch,
        compiler_params=_cparams("arbitrary"),
        name="inproj",
    )(*operands)
    return (outs[:10], outs[10]) if tiles_per_seq else (outs, None)


DSA_BLK = 128


DSA_QBLK = DSA_MAX_WINDOW
DSA_GROUP = 4


def _dsa_kernel(q_ref, kp_ref, kc_ref, vp_ref, vc_ref, o_ref, acc_scr, m_scr, l_scr):
    blk = pl.program_id(2)
    qblk = q_ref.shape[0]
    row = lax.broadcasted_iota(jnp.int32, (2 * DSA_BLK, 2 * DSA_BLK), 0) % DSA_BLK
    col = lax.broadcasted_iota(jnp.int32, (2 * DSA_BLK, 2 * DSA_BLK), 1)
    ok = jnp.logical_and(col >= row, col <= row + DSA_BLK)
    ok_first = jnp.logical_and(ok, jnp.logical_or(col >= DSA_BLK, blk > 0))
    lo = lax.broadcasted_iota(jnp.int32, (DSA_BLK, LANES), 1) < DSA_HEAD_DIM
    neg = -jnp.inf
    halves = lambda x: jnp.where(lo, x[:DSA_BLK], x[DSA_BLK:])

    def rows_of(dil, start, n):
        return pl.ds(start, n) if dil == 1 else pl.ds(start, n, stride=dil)

    def group_softmax(dil, subs):
        n = range(len(subs))
        idx = [rows_of(dil, ph + dil * DSA_BLK * j, DSA_BLK) for ph, j in subs]
        q = [q_ref[i, :] for i in idx]
        k, v = [], []
        for u, (ph, j) in enumerate(subs):
            if j == 0:
                band = rows_of(dil, qblk - dil * DSA_BLK + ph, DSA_BLK)
                k.append(jnp.concatenate([kp_ref[band, :], kc_ref[idx[u], :]], axis=0).astype(BF16))
                v.append(jnp.concatenate([vp_ref[band, :], vc_ref[idx[u], :]], axis=0).astype(BF16))
            else:
                both = rows_of(dil, ph + dil * DSA_BLK * (j - 1), 2 * DSA_BLK)
                k.append(kc_ref[both, :].astype(BF16))
                v.append(vc_ref[both, :].astype(BF16))
        q2 = [jnp.concatenate([jnp.where(lo, q[u], 0.0), jnp.where(lo, 0.0, q[u])], axis=0).astype(BF16)
              for u in n]
        s = [jnp.where(ok_first if subs[u][1] == 0 else ok, _dot_nt(q2[u], k[u]), neg) for u in n]
        mx = [jnp.max(s[u], axis=-1, keepdims=True) for u in n]
        p = [jnp.exp(s[u] - mx[u]) for u in n]
        den = [jnp.sum(p[u], axis=-1, keepdims=True) for u in n]
        pv = [_dot(p[u].astype(BF16), v[u]) for u in n]
        return idx, [(halves(pv[u]), halves(jnp.broadcast_to(mx[u], (2 * DSA_BLK, LANES))),
                      halves(jnp.broadcast_to(den[u], (2 * DSA_BLK, LANES)))) for u in n]

    dils = sorted((d for _, d in DSA_PATTERNS), reverse=True)
    for pi, dil in enumerate(dils):
        subs = [(ph, j) for j in range(qblk // (dil * DSA_BLK)) for ph in range(dil)]
        for g0 in range(0, len(subs), DSA_GROUP):
            idx, tiles = group_softmax(dil, subs[g0:g0 + DSA_GROUP])
            for i, (pv, mx, den) in zip(idx, tiles):
                if pi > 0:
                    m_old = m_scr[i, :]
                    m_new = jnp.maximum(m_old, mx)
                    w_old = jnp.exp(m_old - m_new)
                    w_cur = jnp.exp(mx - m_new)
                    pv = acc_scr[i, :] * w_old + pv * w_cur
                    den = l_scr[i, :] * w_old + den * w_cur
                    mx = m_new
                if pi < len(dils) - 1:
                    acc_scr[i, :] = pv
                    m_scr[i, :] = mx
                    l_scr[i, :] = den
                else:
                    o_ref[i, :] = pv / den


def _dsa_prompt(q, k, v, nb, t):
    assert all(w // d == DSA_BLK for w, d in DSA_PATTERNS) and t % DSA_QBLK == 0
    ngrp = DSA_W // LANES
    r3 = lambda a: a.reshape(nb, t, DSA_W)
    cur = pl.BlockSpec((None, DSA_QBLK, LANES), lambda b, g, i: (b, i, g))
    prv = pl.BlockSpec((None, DSA_QBLK, LANES), lambda b, g, i: (b, jnp.maximum(i - 1, 0), g))
    return pl.pallas_call(
        _dsa_kernel,
        grid=(nb, ngrp, t // DSA_QBLK),
        in_specs=[cur, prv, cur, prv, cur],
        out_specs=cur,
        out_shape=jax.ShapeDtypeStruct((nb, t, DSA_W), F32),
        scratch_shapes=[pltpu.VMEM((DSA_QBLK, LANES), F32)] * 3,
        compiler_params=_cparams("parallel", "parallel", "parallel"),
        name="dsa_prompt",
    )(r3(q), r3(k), r3(k), r3(v), r3(v)).reshape(nb * t, DSA_W)


DSA_STEP_SEQS = 2


def _multiplicity(dist):
    total = jnp.zeros(dist.shape, F32)
    for window, dil in DSA_PATTERNS:
        hit = (dist >= 0) & (dist <= window) & ((dist & (dil - 1)) == 0)
        total = total + hit.astype(F32)
    return total


def _dsa_step_kernel(q_ref, kc_ref, vc_ref, kn_ref, vn_ref, o_ref):
    nseq, tq = q_ref.shape[0], q_ref.shape[1]
    wb = kc_ref.shape[2]
    qi = lax.broadcasted_iota(jnp.int32, (2 * tq, wb), 0) % tq
    w_c = _multiplicity(wb + qi - lax.broadcasted_iota(jnp.int32, (2 * tq, wb), 1))
    qn = lax.broadcasted_iota(jnp.int32, (2 * tq, LANES), 0) % tq
    nn = lax.broadcasted_iota(jnp.int32, (2 * tq, LANES), 1)
    w_n = jnp.where(nn < tq, _multiplicity(qn - nn), 0.0)
    lo = lax.broadcasted_iota(jnp.int32, (tq, LANES), 1) < DSA_HEAD_DIM
    pad = jnp.zeros((LANES - tq, LANES), F32)
    neg = -jnp.inf
    for b, g in [(b, g) for b in range(nseq) for g in range(DSA_W // LANES)]:
        cols = slice(g * LANES, (g + 1) * LANES)
        q = q_ref[b, :, cols]
        q2 = jnp.concatenate([jnp.where(lo, q, 0.0), jnp.where(lo, 0.0, q)], axis=0).astype(BF16)
        kt_c, vt_c = kc_ref[b, cols, :].astype(BF16), vc_ref[b, cols, :].astype(BF16)
        k_n = jnp.concatenate([kn_ref[b, :, cols], pad], axis=0).astype(BF16)
        v_n = jnp.concatenate([vn_ref[b, :, cols], pad], axis=0).astype(BF16)
        s_c = jnp.where(w_c > 0, _dot(q2, kt_c), neg)
        s_n = jnp.where(w_n > 0, _dot_nt(q2, k_n), neg)
        mx = jnp.maximum(jnp.max(s_c, axis=-1, keepdims=True), jnp.max(s_n, axis=-1, keepdims=True))
        p_c = w_c * jnp.exp(s_c - mx)
        p_n = w_n * jnp.exp(s_n - mx)
        den = jnp.sum(p_c, axis=-1, keepdims=True) + jnp.sum(p_n, axis=-1, keepdims=True)
        o2 = (_dot_nt(p_c.astype(BF16), vt_c) + _dot(p_n.astype(BF16), v_n)) / den
        o_ref[b, :, cols] = jnp.where(lo, o2[:tq], o2[tq:])


def _dsa_step(q, k_new, v_new, k_cache, v_cache, layer, nb, t):
    wb = k_cache.shape[3]
    bblk = math.gcd(nb, DSA_STEP_SEQS)
    new = pl.BlockSpec((bblk, t, DSA_W), lambda b: (b, 0, 0))
    cache = pl.BlockSpec((None, bblk, DSA_W, wb), lambda b: (layer, b, 0, 0))
    r3 = lambda a: a.reshape(nb, t, DSA_W)
    return pl.pallas_call(
        _dsa_step_kernel,
        grid=(nb // bblk,),
        in_specs=[new, cache, cache, new, new],
        out_specs=new,
        out_shape=jax.ShapeDtypeStruct((nb, t, DSA_W), F32),
        compiler_params=_cparams("parallel"),
        name="dsa_step",
    )(r3(q), k_cache, v_cache, r3(k_new), r3(v_new)).reshape(nb * t, DSA_W)


RET_CHUNK = 256
RET_SEQS_PER_STEP = 2
RET_UNROLL = 4


def _ret_kernel(q_ref, k_ref, v_ref, g_ref, s0_ref, dec_ref, qd_ref, kd_ref, cd_ref, bd_ref, nw_ref,
                o_ref, so_ref, s_scr, *, chunk):
    tb = pl.program_id(1)
    ngrp = RET_W // LANES

    @pl.when(tb == 0)
    def _():
        s_scr[...] = s0_ref[...]

    lo = lax.broadcasted_iota(jnp.int32, (chunk, LANES), 1) < RET_HEAD_DIM
    nw = nw_ref[...]
    bd = bd_ref[...]

    def body(c, carry):
        rows = pl.ds(pl.multiple_of(c * chunk, chunk), chunk)
        for b in range(q_ref.shape[0]):
            for g in range(ngrp):
                cols = slice(g * LANES, (g + 1) * LANES)
                q, k, v = q_ref[b, rows, cols], k_ref[b, rows, cols], v_ref[b, rows, cols]
                kb, vb = k.astype(BF16), v.astype(BF16)
                parts = []
                for hh in range(2):
                    qm = jnp.where(lo if hh == 0 else jnp.logical_not(lo), q, 0.0).astype(BF16)
                    inner = _dot_nt(qm, kb) * dec_ref[2 * g + hh]
                    parts.append(_dot(inner.astype(BF16), vb))
                s_prev = s_scr[b, g]
                o = jnp.where(lo, parts[0], parts[1]) + _dot(q.astype(BF16), s_prev.astype(BF16)) * qd_ref[g]
                s_scr[b, g] = s_prev * cd_ref[g] + bd * _dot_tn((k * kd_ref[g]).astype(BF16), vb)
                o2 = o * o
                ms = jnp.where(lo, jnp.sum(jnp.where(lo, o2, 0.0), axis=-1, keepdims=True),
                               jnp.sum(jnp.where(lo, 0.0, o2), axis=-1, keepdims=True)) * (1.0 / RET_HEAD_DIM)
                o_ref[b, rows, cols] = o * lax.rsqrt(ms + NORM_EPS) * nw * _silu(g_ref[b, rows, cols])
        return carry

    nchunk = q_ref.shape[1] // chunk
    lax.fori_loop(0, nchunk, body, 0, unroll=min(RET_UNROLL, nchunk))

    @pl.when(tb == pl.num_programs(1) - 1)
    def _():
        so_ref[...] = s_scr[...]


def _ret_tables(chunk):
    log_gamma = jnp.log(1.0 - 2.0 ** (-5.0 - jnp.arange(RET_HEADS, dtype=F32)))
    i = jnp.arange(chunk, dtype=F32)
    diff = i[:, None] - i[None, :]
    causal = diff >= 0
    decay = jnp.where(causal[None], jnp.exp(log_gamma[:, None, None] * jnp.where(causal, diff, 0.0)[None]), 0.0)
    per_lane = lambda a: jnp.repeat(a, RET_HEAD_DIM, axis=0).reshape(RET_W // LANES, LANES, -1)
    q_dec = per_lane(jnp.exp(log_gamma[:, None] * (i[None, :] + 1.0))).transpose(0, 2, 1)
    k_dec = per_lane(jnp.exp(log_gamma[:, None] * (chunk - 1.0 - i)[None, :])).transpose(0, 2, 1)
    c_dec = jnp.broadcast_to(per_lane(jnp.exp(log_gamma * chunk)[:, None]), (RET_W // LANES, LANES, LANES))
    head_of = jnp.arange(LANES) // RET_HEAD_DIM
    block_diag = (head_of[:, None] == head_of[None, :]).astype(F32)
    return decay, q_dec, k_dec, c_dec, block_diag


def _retention(q, k, v, gate, state_bd, ret_norm, nb, t):
    chunk = RET_CHUNK if t % RET_CHUNK == 0 else t
    tblk = min(t, 1024)
    ngrp = RET_W // LANES
    bblk = _seqs_per_step(nb, t, RET_SEQS_PER_STEP)
    decay, q_dec, k_dec, c_dec, block_diag = _ret_tables(chunk)
    r3 = lambda a: a.reshape(nb, t, RET_W)
    tok = pl.BlockSpec((bblk, tblk, RET_W), lambda b, i: (b, i, 0))
    st = pl.BlockSpec((bblk, ngrp, LANES, LANES), lambda b, i: (b, 0, 0, 0))
    full = lambda a: pl.BlockSpec(a.shape, lambda b, i: (0,) * a.ndim)
    nw = jnp.tile(ret_norm, LANES // RET_HEAD_DIM).reshape(1, LANES)
    o, s_new = pl.pallas_call(
        functools.partial(_ret_kernel, chunk=chunk),
        grid=(nb // bblk, t // tblk),
        in_specs=[tok, tok, tok, tok, st, full(decay), full(q_dec), full(k_dec), full(c_dec),
                  full(block_diag), full(nw)],
        out_specs=[tok, st],
        out_shape=[jax.ShapeDtypeStruct((nb, t, RET_W), F32),
                   jax.ShapeDtypeStruct((nb, ngrp, LANES, LANES), F32)],
        scratch_shapes=[pltpu.VMEM((bblk, ngrp, LANES, LANES), F32)],
        compiler_params=_cparams("parallel", "arbitrary"),
        name="retention",
    )(r3(q), r3(k), r3(v), r3(gate), state_bd, decay, q_dec, k_dec, c_dec, block_diag, nw)
    return o.reshape(nb * t, RET_W), s_new


def _to_block_diag(s):
    nb = s.shape[0]
    s = s.reshape(nb, 2, 2, RET_HEAD_DIM, RET_HEAD_DIM)
    z = jnp.zeros_like(s[:, :, 0])
    top = jnp.concatenate([s[:, :, 0], z], axis=-1)
    bot = jnp.concatenate([z, s[:, :, 1]], axis=-1)
    return jnp.concatenate([top, bot], axis=-2)


def _from_block_diag(s):
    h = RET_HEAD_DIM
    return jnp.stack([s[:, :, :h, :h], s[:, :, h:, h:]], axis=2).reshape(s.shape[0], RET_HEADS, h, h)


GDN_PREP_CHUNKS_PER_ITER = 8
GDN_SCAN_SEQS_PER_STEP = 2
GDN_SCAN_UNROLL = 4


def _softplus(x):
    return jnp.maximum(x, 0.0) + jnp.log1p(jnp.exp(-jnp.abs(x)))


def _gdn_prep_kernel(cv_ref, hist_ref, ba_ref, cw_ref, ab_ref,
                     qg_ref, kd_ref, u_ref, w_ref, at_ref, el_ref, beta_scr, g_scr, *conv_scr, chunk, cpi):
    nseq, tblk = cv_ref.shape[0], cv_ref.shape[1]
    ncg = CONV_DIM // LANES
    ab = ab_ref[...]
    cs_scr = cv_ref
    for b in range(nseq):
        if conv_scr:
            xp_scr, cs_scr = conv_scr
            for cg in range(ncg):
                cols = slice(cg * LANES, (cg + 1) * LANES)
                xp_scr[b, cg, :SUBLANES, :] = hist_ref[b, :, cols]
                xp_scr[b, cg, SUBLANES:, :] = cv_ref[b, :, cols]
                cs_scr[b, :, cols] = _causal_conv_silu(xp_scr, b, cw_ref, cg, tblk)

        ba = ba_ref[b]
        beta_scr[b] = jax.nn.sigmoid(ba)
        g_scr[b] = -jnp.exp(ab[0:1, :]) * _softplus(ba + ab[1:2, :])

    ri = lax.broadcasted_iota(jnp.int32, (chunk, chunk), 0)
    ci = lax.broadcasted_iota(jnp.int32, (chunk, chunk), 1)
    incl = ri >= ci
    strict = ri > ci
    tri = incl.astype(F32)
    lane_pick = (lax.broadcasted_iota(jnp.int32, (SUBLANES, LANES), 0)
                 == lax.broadcasted_iota(jnp.int32, (SUBLANES, LANES), 1)).astype(F32)
    base = min(SUBLANES, chunk)
    assert chunk % base == 0 and (chunk // base) & (chunk // base - 1) == 0
    same_blk = [(ri >> sh) == (ci >> sh) for sh in range(int(math.log2(base)), int(math.log2(chunk)) + 1)]

    def l2n(x):
        return x * lax.rsqrt(jnp.sum(x * x, axis=-1, keepdims=True) + NORM_EPS)

    def step(it, carry):
        probs = []
        for b, cc in [(b, cc) for b in range(nseq) for cc in range(cpi)]:
            c = it * cpi + cc
            rows = pl.ds(pl.multiple_of(c * chunk, chunk), chunk)
            beta_c = beta_scr[b, rows, :]
            gcol = _dot(tri, g_scr[b, rows, :], precision=HIGHEST)
            grow = _dot_nt(lane_pick, gcol, precision=HIGHEST)
            for h in range(GDN_HEADS):
                q = l2n(cs_scr[b, rows, h * LANES:(h + 1) * LANES])
                k = l2n(cs_scr[b, rows, GDN_W + h * LANES:GDN_W + (h + 1) * LANES])
                v = cs_scr[b, rows, 2 * GDN_W + h * LANES:2 * GDN_W + (h + 1) * LANES]
                beta = jnp.broadcast_to(beta_c[:, h:h + 1], (chunk, LANES))
                gc = jnp.broadcast_to(gcol[:, GDN_HEADS + h:GDN_HEADS + h + 1], (chunk, LANES))
                gdiff = gc[:, :chunk] - grow[GDN_HEADS + h:GDN_HEADS + h + 1, :]
                dmask = jnp.where(incl, jnp.exp(jnp.where(incl, gdiff, 0.0)), 0.0)
                kb = k * beta
                kbf = k.astype(BF16)
                lower = jnp.where(strict, _dot_nt(kb.astype(BF16), kbf) * dmask, 0.0)
                npow = jnp.where(same_blk[0], -lower, 0.0)
                probs.append(dict(b=b, c=c, rows=rows, h=h, q=q, k=k, v=v, beta=beta, gc=gc, dmask=dmask, kb=kb,
                                  kbf=kbf, lower=lower, npow=npow, qmat=npow))
        for _ in range(int(math.log2(base)) - 1):
            for p in probs:
                nb = p["npow"].astype(BF16)
                p["npow"] = _dot(nb, nb)
            for p in probs:
                p["qmat"] = p["qmat"] + p["npow"] + _dot(p["qmat"].astype(BF16), p["npow"].astype(BF16))
        for lvl in range(1, len(same_blk)):
            for p in probs:
                cb = jnp.where(jnp.logical_and(same_blk[lvl], jnp.logical_not(same_blk[lvl - 1])),
                               p["lower"], 0.0)
                p["x"] = cb + _dot(p["qmat"].astype(BF16), cb.astype(BF16))
            for p in probs:
                qb16 = p["qmat"].astype(BF16)
                p["qmat"] = p["qmat"] - p["x"] - _dot(p["x"].astype(BF16), qb16)
        for p in probs:
            b, rows, h, gc = p["b"], p["rows"], p["h"], p["gc"]
            cols = slice(h * LANES, (h + 1) * LANES)
            eg = jnp.exp(gc)
            rhs_u = p["v"] * p["beta"]
            rhs_w = p["kb"] * eg
            qb = p["qmat"].astype(BF16)
            qs = p["q"] * (GDN_HEAD_DIM ** -0.5)
            g_last = gc[chunk - 1:chunk, :]
            qg_ref[b, rows, cols] = (qs * eg).astype(qg_ref.dtype)
            kd_ref[b, rows, cols] = (p["k"] * jnp.exp(g_last - gc)).astype(kd_ref.dtype)
            u_ref[b, rows, cols] = rhs_u + _dot(qb, rhs_u.astype(BF16))
            w_ref[b, rows, cols] = (rhs_w + _dot(qb, rhs_w.astype(BF16))).astype(w_ref.dtype)
            at_ref[b, rows, h * chunk:(h + 1) * chunk] = (
                _dot_nt(qs.astype(BF16), p["kbf"]) * p["dmask"]).astype(at_ref.dtype)
            el_ref[b, pl.ds(pl.multiple_of(p["c"] * SUBLANES, SUBLANES), SUBLANES), cols] = jnp.broadcast_to(
                jnp.exp(g_last), (SUBLANES, LANES))
        return carry

    lax.fori_loop(0, tblk // (chunk * cpi), step, 0)


def _gdn_scan_kernel(qg_ref, kd_ref, u_ref, w_ref, at_ref, el_ref, z_ref, s0_ref, nw_ref,
                     o_ref, so_ref, s_scr, *, chunk):
    tb = pl.program_id(1)

    @pl.when(tb == 0)
    def _():
        s_scr[...] = s0_ref[...]

    nw = nw_ref[...]
    probs = [(b, h) for b in range(qg_ref.shape[0]) for h in range(GDN_HEADS)]

    def step(c, carry):
        rows = pl.ds(pl.multiple_of(c * chunk, chunk), chunk)
        erow = pl.ds(pl.multiple_of(c * SUBLANES, SUBLANES), SUBLANES)
        cols = lambda h: slice(h * LANES, (h + 1) * LANES)
        s_prev = [s_scr[b, h] for b, h in probs]
        sb = [s.astype(BF16) for s in s_prev]
        v_new = [u_ref[b, rows, cols(h)] - _dot(w_ref[b, rows, cols(h)].astype(BF16), sb[i])
                 for i, (b, h) in enumerate(probs)]
        vb = [v.astype(BF16) for v in v_new]
        for i, (b, h) in enumerate(probs):
            el = el_ref[b, erow, cols(h)][0:1, :]
            s_scr[b, h] = s_prev[i] * el + _dot_tn(kd_ref[b, rows, cols(h)].astype(BF16), vb[i])
        for i, (b, h) in enumerate(probs):
            attn = at_ref[b, rows, h * chunk:(h + 1) * chunk].astype(BF16)
            o = _dot(qg_ref[b, rows, cols(h)].astype(BF16), sb[i]) + _dot(attn, vb[i])
            ms = jnp.mean(o * o, axis=-1, keepdims=True)
            o_ref[b, rows, cols(h)] = o * lax.rsqrt(ms + NORM_EPS) * nw * _silu(z_ref[b, rows, cols(h)])
        return carry

    nchunk = qg_ref.shape[1] // chunk
    lax.fori_loop(0, nchunk, step, 0, unroll=min(GDN_SCAN_UNROLL, nchunk))

    @pl.when(tb == pl.num_programs(1) - 1)
    def _():
        so_ref[...] = s_scr[...]


def _gated_delta(cv, conv_done, z, ba, conv_hist, states, layer, conv_w, a_log, dt_bias, gdn_norm, nb, t):
    chunk = CHUNK if t % CHUNK == 0 else t
    tblk = min(t, 512)
    assert conv_done or tblk == t
    nchunk_blk = tblk // chunk
    hist_pad, cw_pad = _conv_operands(conv_hist, conv_w)
    ab = jnp.zeros((SUBLANES, LANES), F32)
    ab = ab.at[0, GDN_HEADS:2 * GDN_HEADS].set(a_log).at[1, GDN_HEADS:2 * GDN_HEADS].set(dt_bias)
    cv3 = cv.reshape(nb, t, CONV_DIM)
    pblk = _seqs_per_step(nb, t, 2) if t < CHUNK else 1
    tok = lambda w: pl.BlockSpec((pblk, tblk, w), lambda b, i: (b, i, 0))
    full = lambda a: pl.BlockSpec(a.shape, lambda b, i: (0,) * a.ndim)
    el_spec =pl.BlockSpec((pblk, nchunk_blk * SUBLANES, GDN_W), lambda b, i: (b, i, 0))
    tok_shape = lambda w, dt=F32: jax.ShapeDtypeStruct((nb, t, w), dt)
    el_shape = jax.ShapeDtypeStruct((nb, (t // chunk) * SUBLANES, GDN_W), F32)
    opd = BF16 if chunk % (2 * SUBLANES) == 0 else F32
    qg, kd, u, w, attn, el = pl.pallas_call(
        functools.partial(_gdn_prep_kernel, chunk=chunk, cpi=min(GDN_PREP_CHUNKS_PER_ITER, nchunk_blk)),
        grid=(nb // pblk, t // tblk),
        in_specs=[tok(CONV_DIM), pl.BlockSpec((pblk, SUBLANES, CONV_DIM), lambda b, i: (b, 0, 0)),
                  tok(LANES), full(cw_pad), full(ab)],
        out_specs=[tok(GDN_W), tok(GDN_W), tok(GDN_W), tok(GDN_W), tok(GDN_HEADS * chunk), el_spec],
        out_shape=[tok_shape(GDN_W, opd), tok_shape(GDN_W, opd), tok_shape(GDN_W), tok_shape(GDN_W, opd),
                   tok_shape(GDN_HEADS * chunk, opd), el_shape],
        scratch_shapes=[pltpu.VMEM((pblk, tblk, LANES), F32), pltpu.VMEM((pblk, tblk, LANES), F32)] + (
            [] if conv_done else [pltpu.VMEM((pblk, CONV_DIM // LANES, tblk + SUBLANES, LANES), F32),
                                  pltpu.VMEM((pblk, tblk, CONV_DIM), F32)]),
        compiler_params=_cparams("parallel", "parallel"),
        name="gdn_prep",
    )(cv3, hist_pad, ba.reshape(nb, t, LANES), cw_pad, ab)

    bblk = _seqs_per_step(nb, t, GDN_SCAN_SEQS_PER_STEP)
    stok = lambda w: pl.BlockSpec((bblk, tblk, w), lambda b, i: (b, i, 0))
    sel_spec = pl.BlockSpec((bblk, nchunk_blk * SUBLANES, GDN_W), lambda b, i: (b, i, 0))
    st_in = pl.BlockSpec((None, bblk, GDN_HEADS, LANES, LANES), lambda b, i: (layer, b, 0, 0, 0))
    st = pl.BlockSpec((bblk, GDN_HEADS, LANES, LANES), lambda b, i: (b, 0, 0, 0))
    nw = gdn_norm.reshape(1, LANES)
    o, s_new = pl.pallas_call(
        functools.partial(_gdn_scan_kernel, chunk=chunk),
        grid=(nb // bblk, t // tblk),
        in_specs=[stok(GDN_W), stok(GDN_W), stok(GDN_W), stok(GDN_W), stok(GDN_HEADS * chunk), sel_spec,
                  stok(GDN_W), st_in, full(nw)],
        out_specs=[stok(GDN_W), st],
        out_shape=[tok_shape(GDN_W), jax.ShapeDtypeStruct((nb, GDN_HEADS, LANES, LANES), F32)],
        scratch_shapes=[pltpu.VMEM((bblk, GDN_HEADS, LANES, LANES), F32)],
        compiler_params=_cparams("parallel", "arbitrary"),
        name="gdn_scan",
    )(qg, kd, u, w, attn, el, z.reshape(nb, t, GDN_W), states, nw)
    return o.reshape(nb * t, GDN_W), s_new


FFN_TILE = 256


def _out_ffn_kernel(x_ref, oa_ref, orr_ref, oc_ref, g1_ref, sh2_ref, sc2_ref, g2_ref, nw_ref, wo_ref, wg_ref,
                    wu_ref, wd_ref, fn_ref, out_ref, *, final):
    mix = (_dot(oa_ref[...].astype(BF16), wo_ref[0:DSA_W, :])
           + _dot(orr_ref[...].astype(BF16), wo_ref[DSA_W:DSA_W + RET_W, :])
           + _dot(oc_ref[...].astype(BF16), wo_ref[DSA_W + RET_W:, :]))
    x1 = x_ref[...] + g1_ref[...] * mix
    hb = _rms_mod(x1, nw_ref[...], sc2_ref[...], sh2_ref[...]).astype(BF16)
    acc = None
    for j in range(wg_ref.shape[1] // FFN_TILE):
        cols = slice(j * FFN_TILE, (j + 1) * FFN_TILE)
        act = (_silu(_dot(hb, wg_ref[:, cols])) * _dot(hb, wu_ref[:, cols])).astype(BF16)
        down = _dot(act, wd_ref[cols, :])
        acc = down if acc is None else acc + down
    x2 = x1 + g2_ref[...] * acc
    if final:
        ms = jnp.mean(x2 * x2, axis=-1, keepdims=True)
        x2 = x2 * lax.rsqrt(ms + NORM_EPS) * fn_ref[...]
    out_ref[...] = x2


def _out_ffn(x2, t, oa, orr, oc, mod3, norm_w, w_out_b, wg_b, wu_b, wd_b, layer, final_norm, final, tm):
    m, d = x2.shape
    row = lambda w: pl.BlockSpec((tm, w), lambda i: (i, 0))
    const = lambda a: pl.BlockSpec(a.shape, lambda i: (0,) * a.ndim, pipeline_mode=pl.Buffered(1))
    wspec = lambda a: pl.BlockSpec((None,) + a.shape[1:], lambda i: (layer,) + (0,) * (a.ndim - 1),
                                   pipeline_mode=pl.Buffered(1))
    nw = norm_w.reshape(1, d)
    fn = final_norm.reshape(1, d)
    return pl.pallas_call(
        functools.partial(_out_ffn_kernel, final=final),
        grid=(m // tm,),
        in_specs=[row(d), row(DSA_W), row(RET_W), row(GDN_W),
                  _mod_spec(tm, t, d, 2), _mod_spec(tm, t, d, 3), _mod_spec(tm, t, d, 4), _mod_spec(tm, t, d, 5),
                  const(nw), wspec(w_out_b), wspec(wg_b), wspec(wu_b), wspec(wd_b), const(fn)],
        out_specs=row(d),
        out_shape=jax.ShapeDtypeStruct((m, d), F32),
        compiler_params=_cparams("parallel"),
        name="out_ffn",
    )(x2, oa, orr, oc, mod3, mod3, mod3, mod3, nw, w_out_b, wg_b, wu_b, wd_b, fn)


def _rope_tables(pos, inv_freq):
    reps = LANES // (2 * inv_freq.shape[0])
    inv_lane = jnp.tile(jnp.concatenate([inv_freq, inv_freq]), reps)
    sign_lane = jnp.tile(jnp.concatenate([-jnp.ones_like(inv_freq), jnp.ones_like(inv_freq)]), reps)
    ang = pos.astype(F32)[:, None] * inv_lane[None, :]
    return jnp.cos(ang), jnp.sin(ang) * sign_lane[None, :]


def _trunk(x, modp, pos, k_hist, v_hist, s_ret, s_gdn, conv_hist, wts):
    (norm_mix, norm_ffn, w_in_b, ret_norm, conv_w, a_log, dt_bias, gdn_norm, w_out_b, wg_b, wu_b, wd_b,
     final_norm) = wts
    nb, t, d = x.shape
    m = nb * t
    depth = w_in_b.shape[0]
    tm = min(TOKEN_TILE, m)
    inv_a = 1.0 / (ROPE_THETA ** (jnp.arange(0, DSA_HEAD_DIM, 2, dtype=F32) / DSA_HEAD_DIM))
    inv_r = 1.0 / (10000.0 ** jnp.linspace(0.0, 1.0, RET_HEAD_DIM // 2, dtype=F32))
    tabs = _rope_tables(pos, inv_a) + _rope_tables(pos, inv_r)
    if t < tm:
        tabs = tuple(jnp.tile(a, (tm // t, 1)) for a in tabs)
    x2 = x.reshape(m, d)
    ks, vs, rs, gs, cs = [], [], [], [], []
    for l in range(depth):
        if t >= tm:
            mod3 = modp[l].reshape(nb, 1, 6 * d)
        else:
            mod3 = jnp.repeat(modp[l], t, axis=0).reshape(m // tm, tm, 6 * d)
        (qa, ka, va, qr, kr, vr, gr, cv, z, ba), conv_tail = _inproj(
            x2, t, mod3, norm_mix[l], w_in_b, l, tabs, tm, conv_hist[l], conv_w[l])
        nhist = CONV_WIDTH - 1
        if conv_tail is None:
            cvn = jnp.concatenate([conv_hist[l], cv.reshape(nb, t, CONV_DIM)], axis=1)[:, -nhist:]
        else:
            cvn = conv_tail[:, SUBLANES - nhist:]
        if k_hist is None:
            oa = _dsa_prompt(qa, ka, va, nb, t)
        else:
            oa = _dsa_step(qa, ka, va, k_hist, v_hist, l, nb, t)
        orr, sr = _retention(qr, kr, vr, gr, _to_block_diag(s_ret[l]), ret_norm[l], nb, t)
        oc, sg = _gated_delta(cv, conv_tail is not None, z, ba, conv_hist[l], s_gdn, l, conv_w[l], a_log[l],
                              dt_bias[l], gdn_norm[l], nb, t)
        x2 = _out_ffn(x2, t, oa, orr, oc, mod3, norm_ffn[l], w_out_b, wg_b, wu_b, wd_b, l,
                      final_norm, l == depth - 1, tm)
        keep = min(t, DSA_MAX_WINDOW)
        window = lambda a: a.reshape(nb, t, DSA_W)[:, t - keep:].reshape(nb, keep, DSA_HEADS, DSA_HEAD_DIM)
        ks.append(window(ka))
        vs.append(window(va))
        rs.append(_from_block_diag(sr))
        gs.append(sg)
        cs.append(cvn)
    return (x2.reshape(nb, t, d), jnp.stack(ks), jnp.stack(vs), jnp.stack(rs), jnp.stack(gs), jnp.stack(cs))


def kernel(x_prompt, x_sample, cache_win_k, cache_win_v, state_ret, state_gdn, state_conv, c_prompt, c_sample, ada_w, ada_b, norm_mix, norm_ffn, w_in, ret_norm, conv_w, a_log, dt_bias, gdn_norm, w_out, w_gate, w_up, w_down, final_norm):
    nb, t_p, d = x_prompt.shape
    db, t_s, _ = x_sample.shape
    depth = ada_w.shape[0]
    rows = nb + db
    rows_pad = -(-rows // SUBLANES) * SUBLANES
    c_all = jnp.concatenate([c_prompt, c_sample, jnp.zeros((rows_pad - rows, d), F32)], axis=0)
    mod = _modulation(c_all, ada_w, ada_b)
    w_in_b = jnp.pad(w_in, ((0, 0), (0, 0), (0, IN_COLS_PAD - IN_COLS))).astype(BF16)
    wts = (norm_mix, norm_ffn, w_in_b, ret_norm, conv_w, a_log, dt_bias, gdn_norm, w_out.astype(BF16),
           w_gate.astype(BF16), w_up.astype(BF16), w_down.astype(BF16), final_norm)

    zr = jnp.zeros((depth, nb, RET_HEADS, RET_HEAD_DIM, RET_HEAD_DIM), F32)
    zg = jnp.zeros((depth, nb, GDN_HEADS, GDN_HEAD_DIM, GDN_HEAD_DIM), F32)
    zc = jnp.zeros((depth, nb, CONV_WIDTH - 1, CONV_DIM), F32)
    y_p, kp, vp, rp, gp, cp = _trunk(x_prompt, mod[:, :nb], jnp.arange(t_p, dtype=jnp.int32),
                                     None, None, zr, zg, zc, wts)
    wb = cache_win_k.shape[2]
    y_s, ks, vs, rs, gs, cs = _trunk(x_sample, mod[:, nb:rows], PAST_LEN + jnp.arange(t_s, dtype=jnp.int32),
                                     cache_win_k.reshape(depth, db, wb, DSA_W).transpose(0, 1, 3, 2),
                                     cache_win_v.reshape(depth, db, wb, DSA_W).transpose(0, 1, 3, 2),
                                     state_ret, state_gdn, state_conv, wts)
    return (y_p, y_s, kp, vp, rp, gp, cp, ks, vs, rs, gs, cs)
```

```python
import functools
import math

import jax
import jax.numpy as jnp
from jax import lax
from jax.experimental import pallas as pl
from jax.experimental.pallas import tpu as pltpu

F32 = jnp.float32
BF16 = jnp.bfloat16
HIGHEST = lax.Precision.HIGHEST

DSA_HEAD_DIM = 64
DSA_HEADS = 4
DSA_PATTERNS = ((128, 1), (512, 4), (2048, 16))
DSA_MAX_WINDOW = 2048
ROPE_THETA = 10000.0
RET_HEAD_DIM = 64
RET_HEADS = 4
GDN_HEAD_DIM = 128
GDN_HEADS = 4
CONV_WIDTH = 4
CHUNK = 64
NORM_EPS = 1e-6
PAST_LEN = 16384

DSA_W = DSA_HEADS * DSA_HEAD_DIM
RET_W = RET_HEADS * RET_HEAD_DIM
GDN_W = GDN_HEADS * GDN_HEAD_DIM
CONV_DIM = 3 * GDN_W
LANES = 128
SUBLANES = 8
VMEM_LIMIT = 56 * 1024 * 1024
TOKEN_TILE = 512

_C_QA, _C_KA, _C_VA = 0, DSA_W, 2 * DSA_W
_C_QR = 3 * DSA_W
_C_KR, _C_VR, _C_GR = _C_QR + RET_W, _C_QR + 2 * RET_W, _C_QR + 3 * RET_W
_C_CV = _C_QR + 4 * RET_W
_C_Z = _C_CV + CONV_DIM
_C_BA = _C_Z + GDN_W
IN_COLS = _C_BA + 2 * GDN_HEADS
IN_COLS_PAD = _C_BA + LANES


def _cparams(*sem):
    return pltpu.CompilerParams(dimension_semantics=sem, vmem_limit_bytes=VMEM_LIMIT)


def _dot(a, b, **kw):
    return jnp.dot(a, b, preferred_element_type=F32, **kw)


def _dot_nt(a, b, **kw):
    return lax.dot_general(a, b, (((1,), (1,)), ((), ())), preferred_element_type=F32, **kw)


def _dot_tn(a, b, **kw):
    return lax.dot_general(a, b, (((0,), (0,)), ((), ())), preferred_element_type=F32, **kw)


def _silu(x):
    return x * jax.nn.sigmoid(x)


def _seqs_per_step(nb, t, base):
    return math.gcd(nb, base * (4 if t < CHUNK else 1))


def _mod_kernel(c_ref, w_ref, b_ref, o_ref):
    a = _silu(c_ref[...]).astype(BF16)
    o_ref[...] = _dot(a, w_ref[...].astype(BF16)) + b_ref[...]


def _modulation(c_all, ada_w, ada_b, tn=1536):
    depth, d, n = ada_w.shape
    bp = c_all.shape[0]
    return pl.pallas_call(
        _mod_kernel,
        grid=(depth, n // tn),
        in_specs=[
            pl.BlockSpec((bp, d), lambda l, j: (0, 0)),
            pl.BlockSpec((None, d, tn), lambda l, j: (l, 0, j)),
            pl.BlockSpec((None, 1, tn), lambda l, j: (l, 0, j)),
        ],
        out_specs=pl.BlockSpec((None, bp, tn), lambda l, j: (l, 0, j)),
        out_shape=jax.ShapeDtypeStruct((depth, bp, n), F32),
        compiler_params=_cparams("parallel", "parallel"),
        name="modulation",
    )(c_all, ada_w, ada_b.reshape(depth, 1, n))


def _rms_mod(x, nw, sc, sh):
    ms = jnp.mean(x * x, axis=-1, keepdims=True)
    return (x * lax.rsqrt(ms + NORM_EPS) * nw) * (1.0 + sc) + sh


def _causal_conv_silu(xp_scr, seq, cw_ref, cg, nrows):
    cols = slice(cg * LANES, (cg + 1) * LANES)
    acc = None
    for i in range(CONV_WIDTH):
        start = SUBLANES - (CONV_WIDTH - 1) + i
        term = xp_scr[seq, cg, pl.ds(start, nrows, stride=1), :] * cw_ref[i:i + 1, cols]
        acc = term if acc is None else acc + term
    return _silu(acc)


def _inproj_kernel(x_ref, nw_ref, sh_ref, sc_ref, w_ref, ca_ref, sa_ref, cr_ref, sr_ref, *rest, tiles_per_seq):
    if tiles_per_seq:
        hist_ref, cw_ref = rest[:2]
        rest = rest[2:]
    qa_ref, ka_ref, va_ref, qr_ref, kr_ref, vr_ref, gr_ref, cv_ref, z_ref, ba_ref = rest[:10]
    tm = x_ref.shape[0]
    if tiles_per_seq:
        tail_ref, xp_scr = rest[10:]

        @pl.when(pl.program_id(0) % tiles_per_seq == 0)
        def _():
            tail_ref[...] = hist_ref[...]

    hb = _rms_mod(x_ref[...], nw_ref[...], sc_ref[...], sh_ref[...]).astype(BF16)

    def proj(c0, width):
        return _dot(hb, w_ref[:, c0:c0 + width])

    lane = lax.broadcasted_iota(jnp.int32, (tm, DSA_W), 1)
    first_half = (lane % DSA_HEAD_DIM) < (DSA_HEAD_DIM // 2)

    def rope(y, cos, sin_signed):
        partner = jnp.where(first_half, pltpu.roll(y, DSA_W - DSA_HEAD_DIM // 2, 1),
                            pltpu.roll(y, DSA_HEAD_DIM // 2, 1))
        return y * cos + partner * sin_signed

    wide = lambda ref: jnp.concatenate([ref[...]] * (DSA_W // LANES), axis=1)
    ca, sa, cr, sr = wide(ca_ref), wide(sa_ref), wide(cr_ref), wide(sr_ref)
    qa_ref[...] = rope(proj(_C_QA, DSA_W), ca, sa) * (DSA_HEAD_DIM ** -0.5)
    ka_ref[...] = rope(proj(_C_KA, DSA_W), ca, sa)
    va_ref[...] = proj(_C_VA, DSA_W)
    qr_ref[...] = rope(proj(_C_QR, RET_W), cr, sr)
    kr_ref[...] = rope(proj(_C_KR, RET_W), cr, sr) * (RET_HEAD_DIM ** -0.5)
    vr_ref[...] = proj(_C_VR, RET_W)
    gr_ref[...] = proj(_C_GR, RET_W)
    for s in range(CONV_DIM // GDN_W):
        y = proj(_C_CV + s * GDN_W, GDN_W)
        if not tiles_per_seq:
            cv_ref[:, s * GDN_W:(s + 1) * GDN_W] = y
            continue
        for j in range(GDN_W // LANES):
            cg = s * (GDN_W // LANES) + j
            cols = slice(cg * LANES, (cg + 1) * LANES)
            yj = y[:, j * LANES:(j + 1) * LANES]
            xp_scr[0, cg, :SUBLANES, :] = tail_ref[:, cols]
            xp_scr[0, cg, SUBLANES:, :] = yj
            cv_ref[:, cols] = _causal_conv_silu(xp_scr, 0, cw_ref, cg, tm)
            tail_ref[:, cols] = yj[tm - SUBLANES:, :]
    z_ref[...] = proj(_C_Z, GDN_W)
    ba_ref[...] = proj(_C_BA, LANES)


def _mod_spec(tm, t, d, col):
    if t >= tm:
        return pl.BlockSpec((None, 1, d), lambda i: ((i * tm) // t, 0, col))
    return pl.BlockSpec((None, tm, d), lambda i: (i, 0, col))


def _conv_operands(conv_hist, conv_w):
    nb = conv_hist.shape[0]
    hist_pad = jnp.concatenate([jnp.zeros((nb, SUBLANES - (CONV_WIDTH - 1), CONV_DIM), F32), conv_hist], axis=1)
    cw_pad = jnp.concatenate([conv_w, jnp.zeros((SUBLANES - CONV_WIDTH, CONV_DIM), F32)], axis=0)
    return hist_pad, cw_pad


def _inproj(x2, t, mod3, norm_w, w_in_b, layer, tabs, tm, conv_hist, conv_w):
    m, d = x2.shape
    nt = tabs[0].shape[0] // tm
    widths = (DSA_W,) * 3 + (RET_W,) * 4 + (CONV_DIM, GDN_W, LANES)
    tab_spec = pl.BlockSpec((tm, LANES), lambda i: (i % nt, 0))
    tiles_per_seq = t // tm if t % tm == 0 else 0
    in_specs = [
        pl.BlockSpec((tm, d), lambda i: (i, 0)),
        pl.BlockSpec((1, d), lambda i: (0, 0)),
        _mod_spec(tm, t, d, 0),
        _mod_spec(tm, t, d, 1),
        pl.BlockSpec((None, d, IN_COLS_PAD), lambda i: (layer, 0, 0), pipeline_mode=pl.Buffered(1)),
        tab_spec, tab_spec, tab_spec, tab_spec,
    ]
    operands = [x2, norm_w.reshape(1, d), mod3, mod3, w_in_b, *tabs]
    out_specs = [pl.BlockSpec((tm, w), lambda i: (i, 0)) for w in widths]
    out_shape = [jax.ShapeDtypeStruct((m, w), F32) for w in widths]
    scratch = []
    if tiles_per_seq:
        hist_pad, cw_pad = _conv_operands(conv_hist, conv_w)
        seq_tail = pl.BlockSpec((None, SUBLANES, CONV_DIM), lambda i: (i // tiles_per_seq, 0, 0))
        in_specs += [seq_tail, pl.BlockSpec(cw_pad.shape, lambda i: (0, 0))]
        operands += [hist_pad, cw_pad]
        out_specs.append(seq_tail)
        out_shape.append(jax.ShapeDtypeStruct((m // t, SUBLANES, CONV_DIM), F32))
        scratch.append(pltpu.VMEM((1, CONV_DIM // LANES, tm + SUBLANES, LANES), F32))
    outs = pl.pallas_call(
        functools.partial(_inproj_kernel, tiles_per_seq=tiles_per_seq),
        grid=(m // tm,),
        in_specs=in_specs,
        out_specs=out_specs,
        out_shape=out_shape,
        scratch_shapes=scratch,
        compiler_params=_cparams("arbitrary"),
        name="inproj",
    )(*operands)
    return (outs[:10], outs[10]) if tiles_per_seq else (outs, None)


DSA_BLK = 128


DSA_QBLK = DSA_MAX_WINDOW
DSA_GROUP = 4


def _dsa_kernel(q_ref, kp_ref, kc_ref, vp_ref, vc_ref, o_ref, acc_scr, m_scr, l_scr):
    blk = pl.program_id(2)
    qblk = q_ref.shape[0]
    row = lax.broadcasted_iota(jnp.int32, (2 * DSA_BLK, 2 * DSA_BLK), 0) % DSA_BLK
    col = lax.broadcasted_iota(jnp.int32, (2 * DSA_BLK, 2 * DSA_BLK), 1)
    ok = jnp.logical_and(col >= row, col <= row + DSA_BLK)
    ok_first = jnp.logical_and(ok, jnp.logical_or(col >= DSA_BLK, blk > 0))
    lo = lax.broadcasted_iota(jnp.int32, (DSA_BLK, LANES), 1) < DSA_HEAD_DIM
    neg = -jnp.inf
    halves = lambda x: jnp.where(lo, x[:DSA_BLK], x[DSA_BLK:])

    def rows_of(dil, start, n):
        return pl.ds(start, n) if dil == 1 else pl.ds(start, n, stride=dil)

    def group_softmax(dil, subs):
        n = range(len(subs))
        idx = [rows_of(dil, ph + dil * DSA_BLK * j, DSA_BLK) for ph, j in subs]
        q = [q_ref[i, :] for i in idx]
        k, v = [], []
        for u, (ph, j) in enumerate(subs):
            if j == 0:
                band = rows_of(dil, qblk - dil * DSA_BLK + ph, DSA_BLK)
                k.append(jnp.concatenate([kp_ref[band, :], kc_ref[idx[u], :]], axis=0).astype(BF16))
                v.append(jnp.concatenate([vp_ref[band, :], vc_ref[idx[u], :]], axis=0).astype(BF16))
            else:
                both = rows_of(dil, ph + dil * DSA_BLK * (j - 1), 2 * DSA_BLK)
                k.append(kc_ref[both, :].astype(BF16))
                v.append(vc_ref[both, :].astype(BF16))
        q2 = [jnp.concatenate([jnp.where(lo, q[u], 0.0), jnp.where(lo, 0.0, q[u])], axis=0).astype(BF16)
              for u in n]
        s = [jnp.where(ok_first if subs[u][1] == 0 else ok, _dot_nt(q2[u], k[u]), neg) for u in n]
        mx = [jnp.max(s[u], axis=-1, keepdims=True) for u in n]
        p = [jnp.exp(s[u] - mx[u]) for u in n]
        den = [jnp.sum(p[u], axis=-1, keepdims=True) for u in n]
        pv = [_dot(p[u].astype(BF16), v[u]) for u in n]
        return idx, [(halves(pv[u]), halves(jnp.broadcast_to(mx[u], (2 * DSA_BLK, LANES))),
                      halves(jnp.broadcast_to(den[u], (2 * DSA_BLK, LANES)))) for u in n]

    dils = sorted((d for _, d in DSA_PATTERNS), reverse=True)
    for pi, dil in enumerate(dils):
        subs = [(ph, j) for j in range(qblk // (dil * DSA_BLK)) for ph in range(dil)]
        for g0 in range(0, len(subs), DSA_GROUP):
            idx, tiles = group_softmax(dil, subs[g0:g0 + DSA_GROUP])
            for i, (pv, mx, den) in zip(idx, tiles):
                if pi > 0:
                    m_old = m_scr[i, :]
                    m_new = jnp.maximum(m_old, mx)
                    w_old = jnp.exp(m_old - m_new)
                    w_cur = jnp.exp(mx - m_new)
                    pv = acc_scr[i, :] * w_old + pv * w_cur
                    den = l_scr[i, :] * w_old + den * w_cur
                    mx = m_new
                if pi < len(dils) - 1:
                    acc_scr[i, :] = pv
                    m_scr[i, :] = mx
                    l_scr[i, :] = den
                else:
                    o_ref[i, :] = pv / den


def _dsa_prompt(q, k, v, nb, t):
    assert all(w // d == DSA_BLK for w, d in DSA_PATTERNS) and t % DSA_QBLK == 0
    ngrp = DSA_W // LANES
    r3 = lambda a: a.reshape(nb, t, DSA_W)
    cur = pl.BlockSpec((None, DSA_QBLK, LANES), lambda b, g, i: (b, i, g))
    prv = pl.BlockSpec((None, DSA_QBLK, LANES), lambda b, g, i: (b, jnp.maximum(i - 1, 0), g))
    return pl.pallas_call(
        _dsa_kernel,
        grid=(nb, ngrp, t // DSA_QBLK),
        in_specs=[cur, prv, cur, prv, cur],
        out_specs=cur,
        out_shape=jax.ShapeDtypeStruct((nb, t, DSA_W), F32),
        scratch_shapes=[pltpu.VMEM((DSA_QBLK, LANES), F32)] * 3,
        compiler_params=_cparams("parallel", "parallel", "parallel"),
        name="dsa_prompt",
    )(r3(q), r3(k), r3(k), r3(v), r3(v)).reshape(nb * t, DSA_W)


DSA_STEP_SEQS = 2


def _multiplicity(dist):
    total = jnp.zeros(dist.shape, F32)
    for window, dil in DSA_PATTERNS:
        hit = (dist >= 0) & (dist <= window) & ((dist & (dil - 1)) == 0)
        total = total + hit.astype(F32)
    return total


def _dsa_step_kernel(q_ref, kc_ref, vc_ref, kn_ref, vn_ref, o_ref):
    nseq, tq = q_ref.shape[0], q_ref.shape[1]
    wb = kc_ref.shape[2]
    qi = lax.broadcasted_iota(jnp.int32, (2 * tq, wb), 0) % tq
    w_c = _multiplicity(wb + qi - lax.broadcasted_iota(jnp.int32, (2 * tq, wb), 1))
    qn = lax.broadcasted_iota(jnp.int32, (2 * tq, LANES), 0) % tq
    nn = lax.broadcasted_iota(jnp.int32, (2 * tq, LANES), 1)
    w_n = jnp.where(nn < tq, _multiplicity(qn - nn), 0.0)
    lo = lax.broadcasted_iota(jnp.int32, (tq, LANES), 1) < DSA_HEAD_DIM
    pad = jnp.zeros((LANES - tq, LANES), F32)
    neg = -jnp.inf
    for b, g in [(b, g) for b in range(nseq) for g in range(DSA_W // LANES)]:
        cols = slice(g * LANES, (g + 1) * LANES)
        q = q_ref[b, :, cols]
        q2 = jnp.concatenate([jnp.where(lo, q, 0.0), jnp.where(lo, 0.0, q)], axis=0).astype(BF16)
        kt_c, vt_c = kc_ref[b, cols, :].astype(BF16), vc_ref[b, cols, :].astype(BF16)
        k_n = jnp.concatenate([kn_ref[b, :, cols], pad], axis=0).astype(BF16)
        v_n = jnp.concatenate([vn_ref[b, :, cols], pad], axis=0).astype(BF16)
        s_c = jnp.where(w_c > 0, _dot(q2, kt_c), neg)
        s_n = jnp.where(w_n > 0, _dot_nt(q2, k_n), neg)
        mx = jnp.maximum(jnp.max(s_c, axis=-1, keepdims=True), jnp.max(s_n, axis=-1, keepdims=True))
        p_c = w_c * jnp.exp(s_c - mx)
        p_n = w_n * jnp.exp(s_n - mx)
        den = jnp.sum(p_c, axis=-1, keepdims=True) + jnp.sum(p_n, axis=-1, keepdims=True)
        o2 = (_dot_nt(p_c.astype(BF16), vt_c) + _dot(p_n.astype(BF16), v_n)) / den
        o_ref[b, :, cols] = jnp.where(lo, o2[:tq], o2[tq:])


def _dsa_step(q, k_new, v_new, k_cache, v_cache, layer, nb, t):
    wb = k_cache.shape[3]
    bblk = math.gcd(nb, DSA_STEP_SEQS)
    new = pl.BlockSpec((bblk, t, DSA_W), lambda b: (b, 0, 0))
    cache = pl.BlockSpec((None, bblk, DSA_W, wb), lambda b: (layer, b, 0, 0))
    r3 = lambda a: a.reshape(nb, t, DSA_W)
    return pl.pallas_call(
        _dsa_step_kernel,
        grid=(nb // bblk,),
        in_specs=[new, cache, cache, new, new],
        out_specs=new,
        out_shape=jax.ShapeDtypeStruct((nb, t, DSA_W), F32),
        compiler_params=_cparams("parallel"),
        name="dsa_step",
    )(r3(q), k_cache, v_cache, r3(k_new), r3(v_new)).reshape(nb * t, DSA_W)


RET_CHUNK = 256
RET_SEQS_PER_STEP = 2
RET_UNROLL = 4


def _ret_kernel(q_ref, k_ref, v_ref, g_ref, s0_ref, dec_ref, qd_ref, kd_ref, cd_ref, bd_ref, nw_ref,
                o_ref, so_ref, s_scr, *, chunk):
    tb = pl.program_id(1)
    ngrp = RET_W // LANES

    @pl.when(tb == 0)
    def _():
        s_scr[...] = s0_ref[...]

    lo = lax.broadcasted_iota(jnp.int32, (chunk, LANES), 1) < RET_HEAD_DIM
    nw = nw_ref[...]
    bd = bd_ref[...]

    def body(c, carry):
        rows = pl.ds(pl.multiple_of(c * chunk, chunk), chunk)
        for b in range(q_ref.shape[0]):
            for g in range(ngrp):
                cols = slice(g * LANES, (g + 1) * LANES)
                q, k, v = q_ref[b, rows, cols], k_ref[b, rows, cols], v_ref[b, rows, cols]
                kb, vb = k.astype(BF16), v.astype(BF16)
                parts = []
                for hh in range(2):
                    qm = jnp.where(lo if hh == 0 else jnp.logical_not(lo), q, 0.0).astype(BF16)
                    inner = _dot_nt(qm, kb) * dec_ref[2 * g + hh]
                    parts.append(_dot(inner.astype(BF16), vb))
                s_prev = s_scr[b, g]
                o = jnp.where(lo, parts[0], parts[1]) + _dot(q.astype(BF16), s_prev.astype(BF16)) * qd_ref[g]
                s_scr[b, g] = s_prev * cd_ref[g] + bd * _dot_tn((k * kd_ref[g]).astype(BF16), vb)
                o2 = o * o
                ms = jnp.where(lo, jnp.sum(jnp.where(lo, o2, 0.0), axis=-1, keepdims=True),
                               jnp.sum(jnp.where(lo, 0.0, o2), axis=-1, keepdims=True)) * (1.0 / RET_HEAD_DIM)
                o_ref[b, rows, cols] = o * lax.rsqrt(ms + NORM_EPS) * nw * _silu(g_ref[b, rows, cols])
        return carry

    nchunk = q_ref.shape[1] // chunk
    lax.fori_loop(0, nchunk, body, 0, unroll=min(RET_UNROLL, nchunk))

    @pl.when(tb == pl.num_programs(1) - 1)
    def _():
        so_ref[...] = s_scr[...]


def _ret_tables(chunk):
    log_gamma = jnp.log(1.0 - 2.0 ** (-5.0 - jnp.arange(RET_HEADS, dtype=F32)))
    i = jnp.arange(chunk, dtype=F32)
    diff = i[:, None] - i[None, :]
    causal = diff >= 0
    decay = jnp.where(causal[None], jnp.exp(log_gamma[:, None, None] * jnp.where(causal, diff, 0.0)[None]), 0.0)
    per_lane = lambda a: jnp.repeat(a, RET_HEAD_DIM, axis=0).reshape(RET_W // LANES, LANES, -1)
    q_dec = per_lane(jnp.exp(log_gamma[:, None] * (i[None, :] + 1.0))).transpose(0, 2, 1)
    k_dec = per_lane(jnp.exp(log_gamma[:, None] * (chunk - 1.0 - i)[None, :])).transpose(0, 2, 1)
    c_dec = jnp.broadcast_to(per_lane(jnp.exp(log_gamma * chunk)[:, None]), (RET_W // LANES, LANES, LANES))
    head_of = jnp.arange(LANES) // RET_HEAD_DIM
    block_diag = (head_of[:, None] == head_of[None, :]).astype(F32)
    return decay, q_dec, k_dec, c_dec, block_diag


def _retention(q, k, v, gate, state_bd, ret_norm, nb, t):
    chunk = RET_CHUNK if t % RET_CHUNK == 0 else t
    tblk = min(t, 1024)
    ngrp = RET_W // LANES
    bblk = _seqs_per_step(nb, t, RET_SEQS_PER_STEP)
    decay, q_dec, k_dec, c_dec, block_diag = _ret_tables(chunk)
    r3 = lambda a: a.reshape(nb, t, RET_W)
    tok = pl.BlockSpec((bblk, tblk, RET_W), lambda b, i: (b, i, 0))
    st = pl.BlockSpec((bblk, ngrp, LANES, LANES), lambda b, i: (b, 0, 0, 0))
    full = lambda a: pl.BlockSpec(a.shape, lambda b, i: (0,) * a.ndim)
    nw = jnp.tile(ret_norm, LANES // RET_HEAD_DIM).reshape(1, LANES)
    o, s_new = pl.pallas_call(
        functools.partial(_ret_kernel, chunk=chunk),
        grid=(nb // bblk, t // tblk),
        in_specs=[tok, tok, tok, tok, st, full(decay), full(q_dec), full(k_dec), full(c_dec),
                  full(block_diag), full(nw)],
        out_specs=[tok, st],
        out_shape=[jax.ShapeDtypeStruct((nb, t, RET_W), F32),
                   jax.ShapeDtypeStruct((nb, ngrp, LANES, LANES), F32)],
        scratch_shapes=[pltpu.VMEM((bblk, ngrp, LANES, LANES), F32)],
        compiler_params=_cparams("parallel", "arbitrary"),
        name="retention",
    )(r3(q), r3(k), r3(v), r3(gate), state_bd, decay, q_dec, k_dec, c_dec, block_diag, nw)
    return o.reshape(nb * t, RET_W), s_new


def _to_block_diag(s):
    nb = s.shape[0]
    s = s.reshape(nb, 2, 2, RET_HEAD_DIM, RET_HEAD_DIM)
    z = jnp.zeros_like(s[:, :, 0])
    top = jnp.concatenate([s[:, :, 0], z], axis=-1)
    bot = jnp.concatenate([z, s[:, :, 1]], axis=-1)
    return jnp.concatenate([top, bot], axis=-2)


def _from_block_diag(s):
    h = RET_HEAD_DIM
    return jnp.stack([s[:, :, :h, :h], s[:, :, h:, h:]], axis=2).reshape(s.shape[0], RET_HEADS, h, h)


GDN_PREP_CHUNKS_PER_ITER = 8
GDN_SCAN_SEQS_PER_STEP = 2
GDN_SCAN_UNROLL = 4


def _softplus(x):
    return jnp.maximum(x, 0.0) + jnp.log1p(jnp.exp(-jnp.abs(x)))


def _gdn_prep_kernel(cv_ref, hist_ref, ba_ref, cw_ref, ab_ref,
                     qg_ref, kd_ref, u_ref, w_ref, at_ref, el_ref, beta_scr, g_scr, *conv_scr, chunk, cpi):
    nseq, tblk = cv_ref.shape[0], cv_ref.shape[1]
    ncg = CONV_DIM // LANES
    ab = ab_ref[...]
    cs_scr = cv_ref
    for b in range(nseq):
        if conv_scr:
            xp_scr, cs_scr = conv_scr
            for cg in range(ncg):
                cols = slice(cg * LANES, (cg + 1) * LANES)
                xp_scr[b, cg, :SUBLANES, :] = hist_ref[b, :, cols]
                xp_scr[b, cg, SUBLANES:, :] = cv_ref[b, :, cols]
                cs_scr[b, :, cols] = _causal_conv_silu(xp_scr, b, cw_ref, cg, tblk)

        ba = ba_ref[b]
        beta_scr[b] = jax.nn.sigmoid(ba)
        g_scr[b] = -jnp.exp(ab[0:1, :]) * _softplus(ba + ab[1:2, :])

    grp = 2 if 2 * chunk == LANES else 1
    width = grp * chunk
    ri = lax.broadcasted_iota(jnp.int32, (chunk, width), 0)
    lane = lax.broadcasted_iota(jnp.int32, (chunk, width), 1)
    ci = lane % chunk
    first = lane < chunk
    incl = ri >= ci
    strict = ri > ci
    tri = (lax.broadcasted_iota(jnp.int32, (chunk, chunk), 0)
           >= lax.broadcasted_iota(jnp.int32, (chunk, chunk), 1)).astype(F32)
    lane_pick = (lax.broadcasted_iota(jnp.int32, (SUBLANES, LANES), 0)
                 == lax.broadcasted_iota(jnp.int32, (SUBLANES, LANES), 1)).astype(F32)
    base = min(SUBLANES, chunk)
    assert chunk % base == 0 and (chunk // base) & (chunk // base - 1) == 0
    same_blk = [(ri >> sh) == (ci >> sh) for sh in range(int(math.log2(base)), int(math.log2(chunk)) + 1)]

    def l2n(x):
        return x * lax.rsqrt(jnp.sum(x * x, axis=-1, keepdims=True) + NORM_EPS)

    def side_by_side(per_head):
        if grp == 1:
            return per_head[0][:, :width]
        return jnp.where(first, per_head[0][:, :width], per_head[1][:, :width])

    def block_diag(y):
        if grp == 1:
            return y.astype(BF16)
        return jnp.concatenate([jnp.where(first, y, 0.0), jnp.where(first, 0.0, y)], axis=0).astype(BF16)

    def block_rows(per_head):
        if grp == 1:
            return per_head[0].astype(BF16)
        a, b = per_head
        return jnp.concatenate([jnp.concatenate([a, jnp.zeros_like(b)], axis=1),
                                jnp.concatenate([jnp.zeros_like(a), b], axis=1)], axis=0).astype(BF16)

    def step(it, carry):
        probs = []
        for b, cc in [(b, cc) for b in range(nseq) for cc in range(cpi)]:
            c = it * cpi + cc
            rows = pl.ds(pl.multiple_of(c * chunk, chunk), chunk)
            beta_c = beta_scr[b, rows, :]
            gcol = _dot(tri, g_scr[b, rows, :], precision=HIGHEST)
            grow = _dot_nt(lane_pick, jnp.concatenate([gcol] * grp, axis=0), precision=HIGHEST)
            for h0 in range(0, GDN_HEADS, grp):
                hs = range(h0, h0 + grp)
                q = [l2n(cs_scr[b, rows, h * LANES:(h + 1) * LANES]) for h in hs]
                k = [l2n(cs_scr[b, rows, GDN_W + h * LANES:GDN_W + (h + 1) * LANES]) for h in hs]
                v = [cs_scr[b, rows, 2 * GDN_W + h * LANES:2 * GDN_W + (h + 1) * LANES] for h in hs]
                beta = [jnp.broadcast_to(beta_c[:, h:h + 1], (chunk, LANES)) for h in hs]
                gc = [jnp.broadcast_to(gcol[:, GDN_HEADS + h:GDN_HEADS + h + 1], (chunk, LANES)) for h in hs]
                g_row = side_by_side([jnp.broadcast_to(grow[GDN_HEADS + h:GDN_HEADS + h + 1, :], (chunk, width))
                                      for h in hs])
                gdiff = side_by_side(gc) - g_row
                dmask = jnp.where(incl, jnp.exp(jnp.where(incl, gdiff, 0.0)), 0.0)
                kb = [k[i] * beta[i] for i in range(grp)]
                k_rows = block_rows(k)
                lower = jnp.where(strict, _dot_nt(jnp.concatenate(kb, axis=1).astype(BF16), k_rows) * dmask, 0.0)
                npow = jnp.where(same_blk[0], -lower, 0.0)
                probs.append(dict(b=b, c=c, rows=rows, h0=h0, q=q, k=k, v=v, beta=beta, gc=gc, dmask=dmask,
                                  kb=kb, k_rows=k_rows, lower=lower, npow=npow, qmat=npow))
        for _ in range(int(math.log2(base)) - 1):
            for p in probs:
                p["npow"] = _dot(p["npow"].astype(BF16), block_diag(p["npow"]))
            for p in probs:
                p["qmat"] = p["qmat"] + p["npow"] + _dot(p["qmat"].astype(BF16), block_diag(p["npow"]))
        for lvl in range(1, len(same_blk)):
            for p in probs:
                cb = jnp.where(jnp.logical_and(same_blk[lvl], jnp.logical_not(same_blk[lvl - 1])),
                               p["lower"], 0.0)
                p["x"] = cb + _dot(p["qmat"].astype(BF16), block_diag(cb))
            for p in probs:
                p["qmat"] = p["qmat"] - p["x"] - _dot(p["x"].astype(BF16), block_diag(p["qmat"]))
        for p in probs:
            b, rows, h0, gc = p["b"], p["rows"], p["h0"], p["gc"]
            cols = slice(h0 * LANES, (h0 + grp) * LANES)
            idx = range(grp)
            eg = [jnp.exp(gc[i]) for i in idx]
            rhs_u = [p["v"][i] * p["beta"][i] for i in idx]
            rhs_w = [p["kb"][i] * eg[i] for i in idx]
            qb = p["qmat"].astype(BF16)
            qs = [p["q"][i] * (GDN_HEAD_DIM ** -0.5) for i in idx]
            g_last = [gc[i][chunk - 1:chunk, :] for i in idx]
            cat = lambda xs: jnp.concatenate(xs, axis=1)
            qg_ref[b, rows, cols] = cat([qs[i] * eg[i] for i in idx]).astype(qg_ref.dtype)
            kd_ref[b, rows, cols] = cat([p["k"][i] * jnp.exp(g_last[i] - gc[i]) for i in idx]).astype(kd_ref.dtype)
            u_ref[b, rows, cols] = cat(rhs_u) + _dot(qb, block_rows(rhs_u))
            w_ref[b, rows, cols] = (cat(rhs_w) + _dot(qb, block_rows(rhs_w))).astype(w_ref.dtype)
            at_ref[b, rows, h0 * chunk:(h0 + grp) * chunk] = (
                _dot_nt(cat(qs).astype(BF16), p["k_rows"]) * p["dmask"]).astype(at_ref.dtype)
            el_ref[b, pl.ds(pl.multiple_of(p["c"] * SUBLANES, SUBLANES), SUBLANES), cols] = cat(
                [jnp.broadcast_to(jnp.exp(g_last[i]), (SUBLANES, LANES)) for i in idx])
        return carry

    lax.fori_loop(0, tblk // (chunk * cpi), step, 0)


def _gdn_scan_kernel(qg_ref, kd_ref, u_ref, w_ref, at_ref, el_ref, z_ref, s0_ref, nw_ref,
                     o_ref, so_ref, s_scr, *, chunk):
    tb = pl.program_id(1)

    @pl.when(tb == 0)
    def _():
        s_scr[...] = s0_ref[...]

    nw = nw_ref[...]
    probs = [(b, h) for b in range(qg_ref.shape[0]) for h in range(GDN_HEADS)]

    def step(c, carry):
        rows = pl.ds(pl.multiple_of(c * chunk, chunk), chunk)
        erow = pl.ds(pl.multiple_of(c * SUBLANES, SUBLANES), SUBLANES)
        cols = lambda h: slice(h * LANES, (h + 1) * LANES)
        s_prev = [s_scr[b, h] for b, h in probs]
        sb = [s.astype(BF16) for s in s_prev]
        v_new = [u_ref[b, rows, cols(h)] - _dot(w_ref[b, rows, cols(h)].astype(BF16), sb[i])
                 for i, (b, h) in enumerate(probs)]
        vb = [v.astype(BF16) for v in v_new]
        for i, (b, h) in enumerate(probs):
            el = el_ref[b, erow, cols(h)][0:1, :]
            s_scr[b, h] = s_prev[i] * el + _dot_tn(kd_ref[b, rows, cols(h)].astype(BF16), vb[i])
        for i, (b, h) in enumerate(probs):
            attn = at_ref[b, rows, h * chunk:(h + 1) * chunk].astype(BF16)
            o = _dot(qg_ref[b, rows, cols(h)].astype(BF16), sb[i]) + _dot(attn, vb[i])
            ms = jnp.mean(o * o, axis=-1, keepdims=True)
            o_ref[b, rows, cols(h)] = o * lax.rsqrt(ms + NORM_EPS) * nw * _silu(z_ref[b, rows, cols(h)])
        return carry

    nchunk = qg_ref.shape[1] // chunk
    lax.fori_loop(0, nchunk, step, 0, unroll=min(GDN_SCAN_UNROLL, nchunk))

    @pl.when(tb == pl.num_programs(1) - 1)
    def _():
        so_ref[...] = s_scr[...]


def _gated_delta(cv, conv_done, z, ba, conv_hist, states, layer, conv_w, a_log, dt_bias, gdn_norm, nb, t):
    chunk = CHUNK if t % CHUNK == 0 else t
    tblk = min(t, 512)
    assert conv_done or tblk == t
    nchunk_blk = tblk // chunk
    hist_pad, cw_pad = _conv_operands(conv_hist, conv_w)
    ab = jnp.zeros((SUBLANES, LANES), F32)
    ab = ab.at[0, GDN_HEADS:2 * GDN_HEADS].set(a_log).at[1, GDN_HEADS:2 * GDN_HEADS].set(dt_bias)
    cv3 = cv.reshape(nb, t, CONV_DIM)
    pblk = _seqs_per_step(nb, t, 2)
    tok = lambda w: pl.BlockSpec((pblk, tblk, w), lambda b, i: (b, i, 0))
    full = lambda a: pl.BlockSpec(a.shape, lambda b, i: (0,) * a.ndim)
    el_spec =pl.BlockSpec((pblk, nchunk_blk * SUBLANES, GDN_W), lambda b, i: (b, i, 0))
    tok_shape = lambda w, dt=F32: jax.ShapeDtypeStruct((nb, t, w), dt)
    el_shape = jax.ShapeDtypeStruct((nb, (t // chunk) * SUBLANES, GDN_W), F32)
    opd = BF16 if chunk % (2 * SUBLANES) == 0 else F32
    qg, kd, u, w, attn, el = pl.pallas_call(
        functools.partial(_gdn_prep_kernel, chunk=chunk, cpi=min(GDN_PREP_CHUNKS_PER_ITER, nchunk_blk)),
        grid=(nb // pblk, t // tblk),
        in_specs=[tok(CONV_DIM), pl.BlockSpec((pblk, SUBLANES, CONV_DIM), lambda b, i: (b, 0, 0)),
                  tok(LANES), full(cw_pad), full(ab)],
        out_specs=[tok(GDN_W), tok(GDN_W), tok(GDN_W), tok(GDN_W), tok(GDN_HEADS * chunk), el_spec],
        out_shape=[tok_shape(GDN_W, opd), tok_shape(GDN_W, opd), tok_shape(GDN_W), tok_shape(GDN_W, opd),
                   tok_shape(GDN_HEADS * chunk, opd), el_shape],
        scratch_shapes=[pltpu.VMEM((pblk, tblk, LANES), F32), pltpu.VMEM((pblk, tblk, LANES), F32)] + (
            [] if conv_done else [pltpu.VMEM((pblk, CONV_DIM // LANES, tblk + SUBLANES, LANES), F32),
                                  pltpu.VMEM((pblk, tblk, CONV_DIM), F32)]),
        compiler_params=_cparams("parallel", "parallel"),
        name="gdn_prep",
    )(cv3, hist_pad, ba.reshape(nb, t, LANES), cw_pad, ab)

    bblk = _seqs_per_step(nb, t, GDN_SCAN_SEQS_PER_STEP)
    stok = lambda w: pl.BlockSpec((bblk, tblk, w), lambda b, i: (b, i, 0))
    sel_spec = pl.BlockSpec((bblk, nchunk_blk * SUBLANES, GDN_W), lambda b, i: (b, i, 0))
    st_in = pl.BlockSpec((None, bblk, GDN_HEADS, LANES, LANES), lambda b, i: (layer, b, 0, 0, 0))
    st = pl.BlockSpec((bblk, GDN_HEADS, LANES, LANES), lambda b, i: (b, 0, 0, 0))
    nw = gdn_norm.reshape(1, LANES)
    o, s_new = pl.pallas_call(
        functools.partial(_gdn_scan_kernel, chunk=chunk),
        grid=(nb // bblk, t // tblk),
        in_specs=[stok(GDN_W), stok(GDN_W), stok(GDN_W), stok(GDN_W), stok(GDN_HEADS * chunk), sel_spec,
                  stok(GDN_W), st_in, full(nw)],
        out_specs=[stok(GDN_W), st],
        out_shape=[tok_shape(GDN_W), jax.ShapeDtypeStruct((nb, GDN_HEADS, LANES, LANES), F32)],
        scratch_shapes=[pltpu.VMEM((bblk, GDN_HEADS, LANES, LANES), F32)],
        compiler_params=_cparams("parallel", "arbitrary"),
        name="gdn_scan",
    )(qg, kd, u, w, attn, el, z.reshape(nb, t, GDN_W), states, nw)
    return o.reshape(nb * t, GDN_W), s_new


FFN_TILE = 256


def _out_ffn_kernel(x_ref, oa_ref, orr_ref, oc_ref, g1_ref, sh2_ref, sc2_ref, g2_ref, nw_ref, wo_ref, wg_ref,
                    wu_ref, wd_ref, fn_ref, out_ref, *, final):
    mix = (_dot(oa_ref[...].astype(BF16), wo_ref[0:DSA_W, :])
           + _dot(orr_ref[...].astype(BF16), wo_ref[DSA_W:DSA_W + RET_W, :])
           + _dot(oc_ref[...].astype(BF16), wo_ref[DSA_W + RET_W:, :]))
    x1 = x_ref[...] + g1_ref[...] * mix
    hb = _rms_mod(x1, nw_ref[...], sc2_ref[...], sh2_ref[...]).astype(BF16)
    acc = None
    for j in range(wg_ref.shape[1] // FFN_TILE):
        cols = slice(j * FFN_TILE, (j + 1) * FFN_TILE)
        act = (_silu(_dot(hb, wg_ref[:, cols])) * _dot(hb, wu_ref[:, cols])).astype(BF16)
        down = _dot(act, wd_ref[cols, :])
        acc = down if acc is None else acc + down
    x2 = x1 + g2_ref[...] * acc
    if final:
        ms = jnp.mean(x2 * x2, axis=-1, keepdims=True)
        x2 = x2 * lax.rsqrt(ms + NORM_EPS) * fn_ref[...]
    out_ref[...] = x2


def _out_ffn(x2, t, oa, orr, oc, mod3, norm_w, w_out_b, wg_b, wu_b, wd_b, layer, final_norm, final, tm):
    m, d = x2.shape
    row = lambda w: pl.BlockSpec((tm, w), lambda i: (i, 0))
    const = lambda a: pl.BlockSpec(a.shape, lambda i: (0,) * a.ndim, pipeline_mode=pl.Buffered(1))
    wspec = lambda a: pl.BlockSpec((None,) + a.shape[1:], lambda i: (layer,) + (0,) * (a.ndim - 1),
                                   pipeline_mode=pl.Buffered(1))
    nw = norm_w.reshape(1, d)
    fn = final_norm.reshape(1, d)
    return pl.pallas_call(
        functools.partial(_out_ffn_kernel, final=final),
        grid=(m // tm,),
        in_specs=[row(d), row(DSA_W), row(RET_W), row(GDN_W),
                  _mod_spec(tm, t, d, 2), _mod_spec(tm, t, d, 3), _mod_spec(tm, t, d, 4), _mod_spec(tm, t, d, 5),
                  const(nw), wspec(w_out_b), wspec(wg_b), wspec(wu_b), wspec(wd_b), const(fn)],
        out_specs=row(d),
        out_shape=jax.ShapeDtypeStruct((m, d), F32),
        compiler_params=_cparams("parallel"),
        name="out_ffn",
    )(x2, oa, orr, oc, mod3, mod3, mod3, mod3, nw, w_out_b, wg_b, wu_b, wd_b, fn)


ROPE_SPLIT = 64


def _rope_tables(pos0, t, inv_freq):
    reps = LANES // (2 * inv_freq.shape[0])
    inv_lane = jnp.tile(jnp.concatenate([inv_freq, inv_freq]), reps)[None, :]
    sign_lane = jnp.tile(jnp.concatenate([-jnp.ones_like(inv_freq), jnp.ones_like(inv_freq)]), reps)[None, :]
    if t % ROPE_SPLIT:
        ang = (pos0 + jnp.arange(t, dtype=jnp.int32)).astype(F32)[:, None] * inv_lane
        return jnp.cos(ang), jnp.sin(ang) * sign_lane
    coarse = (pos0 + ROPE_SPLIT * jnp.arange(t // ROPE_SPLIT, dtype=jnp.int32)).astype(F32)[:, None] * inv_lane
    fine = jnp.arange(ROPE_SPLIT, dtype=jnp.int32).astype(F32)[:, None] * inv_lane
    cc, sc = jnp.cos(coarse)[:, None, :], jnp.sin(coarse)[:, None, :]
    cf, sf = jnp.cos(fine)[None], jnp.sin(fine)[None]
    cos = (cc * cf - sc * sf).reshape(t, LANES)
    sin = (sc * cf + cc * sf).reshape(t, LANES)
    return cos, sin * sign_lane


def _trunk(x, modp, pos0, k_hist, v_hist, s_ret, s_gdn, conv_hist, wts):
    (norm_mix, norm_ffn, w_in_b, ret_norm, conv_w, a_log, dt_bias, gdn_norm, w_out_b, wg_b, wu_b, wd_b,
     final_norm) = wts
    nb, t, d = x.shape
    m = nb * t
    depth = w_in_b.shape[0]
    tm = min(TOKEN_TILE, m)
    inv_a = 1.0 / (ROPE_THETA ** (jnp.arange(0, DSA_HEAD_DIM, 2, dtype=F32) / DSA_HEAD_DIM))
    inv_r = 1.0 / (10000.0 ** jnp.linspace(0.0, 1.0, RET_HEAD_DIM // 2, dtype=F32))
    tabs = _rope_tables(pos0, t, inv_a) + _rope_tables(pos0, t, inv_r)
    if t < tm:
        tabs = tuple(jnp.tile(a, (tm // t, 1)) for a in tabs)
    x2 = x.reshape(m, d)
    ks, vs, rs, gs, cs = [], [], [], [], []
    for l in range(depth):
        if t >= tm:
            mod3 = modp[l].reshape(nb, 1, 6 * d)
        else:
            mod3 = jnp.repeat(modp[l], t, axis=0).reshape(m // tm, tm, 6 * d)
        (qa, ka, va, qr, kr, vr, gr, cv, z, ba), conv_tail = _inproj(
            x2, t, mod3, norm_mix[l], w_in_b, l, tabs, tm, conv_hist[l], conv_w[l])
        nhist = CONV_WIDTH - 1
        if conv_tail is None:
            cvn = jnp.concatenate([conv_hist[l], cv.reshape(nb, t, CONV_DIM)], axis=1)[:, -nhist:]
        else:
            cvn = conv_tail[:, SUBLANES - nhist:]
        if k_hist is None:
            oa = _dsa_prompt(qa, ka, va, nb, t)
        else:
            oa = _dsa_step(qa, ka, va, k_hist, v_hist, l, nb, t)
        orr, sr = _retention(qr, kr, vr, gr, _to_block_diag(s_ret[l]), ret_norm[l], nb, t)
        oc, sg = _gated_delta(cv, conv_tail is not None, z, ba, conv_hist[l], s_gdn, l, conv_w[l], a_log[l],
                              dt_bias[l], gdn_norm[l], nb, t)
        x2 = _out_ffn(x2, t, oa, orr, oc, mod3, norm_ffn[l], w_out_b, wg_b, wu_b, wd_b, l,
                      final_norm, l == depth - 1, tm)
        keep = min(t, DSA_MAX_WINDOW)
        window = lambda a: a.reshape(nb, t, DSA_W)[:, t - keep:].reshape(nb, keep, DSA_HEADS, DSA_HEAD_DIM)
        ks.append(window(ka))
        vs.append(window(va))
        rs.append(_from_block_diag(sr))
        gs.append(sg)
        cs.append(cvn)
    return (x2.reshape(nb, t, d), jnp.stack(ks), jnp.stack(vs), jnp.stack(rs), jnp.stack(gs), jnp.stack(cs))


def kernel(x_prompt, x_sample, cache_win_k, cache_win_v, state_ret, state_gdn, state_conv, c_prompt, c_sample, ada_w, ada_b, norm_mix, norm_ffn, w_in, ret_norm, conv_w, a_log, dt_bias, gdn_norm, w_out, w_gate, w_up, w_down, final_norm):
    nb, t_p, d = x_prompt.shape
    db, t_s, _ = x_sample.shape
    depth = ada_w.shape[0]
    rows = nb + db
    rows_pad = -(-rows // SUBLANES) * SUBLANES
    c_all = jnp.concatenate([c_prompt, c_sample, jnp.zeros((rows_pad - rows, d), F32)], axis=0)
    mod = _modulation(c_all, ada_w, ada_b)
    w_in_b = jnp.pad(w_in, ((0, 0), (0, 0), (0, IN_COLS_PAD - IN_COLS))).astype(BF16)
    wts = (norm_mix, norm_ffn, w_in_b, ret_norm, conv_w, a_log, dt_bias, gdn_norm, w_out.astype(BF16),
           w_gate.astype(BF16), w_up.astype(BF16), w_down.astype(BF16), final_norm)

    zr = jnp.zeros((depth, nb, RET_HEADS, RET_HEAD_DIM, RET_HEAD_DIM), F32)
    zg = jnp.zeros((depth, nb, GDN_HEADS, GDN_HEAD_DIM, GDN_HEAD_DIM), F32)
    zc = jnp.zeros((depth, nb, CONV_WIDTH - 1, CONV_DIM), F32)
    y_p, kp, vp, rp, gp, cp = _trunk(x_prompt, mod[:, :nb], 0,
                                     None, None, zr, zg, zc, wts)
    wb = cache_win_k.shape[2]
    y_s, ks, vs, rs, gs, cs = _trunk(x_sample, mod[:, nb:rows], PAST_LEN,
                                     cache_win_k.reshape(depth, db, wb, DSA_W).transpose(0, 1, 3, 2),
                                     cache_win_v.reshape(depth, db, wb, DSA_W).transpose(0, 1, 3, 2),
                                     state_ret, state_gdn, state_conv, wts)
    return (y_p, y_s, kp, vp, rp, gp, cp, ks, vs, rs, gs, cs)
```

```python
import functools
import math

import jax
import jax.numpy as jnp
from jax import lax
from jax.experimental import pallas as pl
from jax.experimental.pallas import tpu as pltpu

F32 = jnp.float32
BF16 = jnp.bfloat16
HIGHEST = lax.Precision.HIGHEST

DSA_HEAD_DIM = 64
DSA_HEADS = 4
DSA_PATTERNS = ((128, 1), (512, 4), (2048, 16))
DSA_MAX_WINDOW = 2048
ROPE_THETA = 10000.0
RET_HEAD_DIM = 64
RET_HEADS = 4
GDN_HEAD_DIM = 128
GDN_HEADS = 4
CONV_WIDTH = 4
CHUNK = 64
NORM_EPS = 1e-6
PAST_LEN = 16384

DSA_W = DSA_HEADS * DSA_HEAD_DIM
RET_W = RET_HEADS * RET_HEAD_DIM
GDN_W = GDN_HEADS * GDN_HEAD_DIM
CONV_DIM = 3 * GDN_W
LANES = 128
SUBLANES = 8
VMEM_LIMIT = 56 * 1024 * 1024
TOKEN_TILE = 512

_C_QA, _C_KA, _C_VA = 0, DSA_W, 2 * DSA_W
_C_QR = 3 * DSA_W
_C_KR, _C_VR, _C_GR = _C_QR + RET_W, _C_QR + 2 * RET_W, _C_QR + 3 * RET_W
_C_CV = _C_QR + 4 * RET_W
_C_Z = _C_CV + CONV_DIM
_C_BA = _C_Z + GDN_W
IN_COLS = _C_BA + 2 * GDN_HEADS
IN_COLS_PAD = _C_BA + LANES


def _cparams(*sem):
    return pltpu.CompilerParams(dimension_semantics=sem, vmem_limit_bytes=VMEM_LIMIT)


def _dot(a, b, **kw):
    return jnp.dot(a, b, preferred_element_type=F32, **kw)


def _dot_nt(a, b, **kw):
    return lax.dot_general(a, b, (((1,), (1,)), ((), ())), preferred_element_type=F32, **kw)


def _dot_tn(a, b, **kw):
    return lax.dot_general(a, b, (((0,), (0,)), ((), ())), preferred_element_type=F32, **kw)


def _silu(x):
    return x * jax.nn.sigmoid(x)


def _seqs_per_step(nb, t, base):
    return math.gcd(nb, base * (4 if t < CHUNK else 1))


def _mod_kernel(c_ref, w_ref, b_ref, o_ref):
    a = _silu(c_ref[...]).astype(BF16)
    o_ref[...] = _dot(a, w_ref[...].astype(BF16)) + b_ref[...]


def _modulation(c_all, ada_w, ada_b, tn=1536):
    depth, d, n = ada_w.shape
    bp = c_all.shape[0]
    return pl.pallas_call(
        _mod_kernel,
        grid=(depth, n // tn),
        in_specs=[
            pl.BlockSpec((bp, d), lambda l, j: (0, 0)),
            pl.BlockSpec((None, d, tn), lambda l, j: (l, 0, j)),
            pl.BlockSpec((None, 1, tn), lambda l, j: (l, 0, j)),
        ],
        out_specs=pl.BlockSpec((None, bp, tn), lambda l, j: (l, 0, j)),
        out_shape=jax.ShapeDtypeStruct((depth, bp, n), F32),
        compiler_params=_cparams("parallel", "parallel"),
        name="modulation",
    )(c_all, ada_w, ada_b.reshape(depth, 1, n))


def _rms_mod(x, nw, sc, sh):
    ms = jnp.mean(x * x, axis=-1, keepdims=True)
    return (x * lax.rsqrt(ms + NORM_EPS) * nw) * (1.0 + sc) + sh


def _causal_conv_silu(xp_scr, seq, cw_ref, cg, nrows):
    cols = slice(cg * LANES, (cg + 1) * LANES)
    acc = None
    for i in range(CONV_WIDTH):
        start = SUBLANES - (CONV_WIDTH - 1) + i
        term = xp_scr[seq, cg, pl.ds(start, nrows, stride=1), :] * cw_ref[i:i + 1, cols]
        acc = term if acc is None else acc + term
    return _silu(acc)


def _inproj_kernel(x_ref, nw_ref, sh_ref, sc_ref, w_ref, ca_ref, sa_ref, cr_ref, sr_ref, *rest, tiles_per_seq):
    if tiles_per_seq:
        hist_ref, cw_ref = rest[:2]
        rest = rest[2:]
    qa_ref, ka_ref, va_ref, qr_ref, kr_ref, vr_ref, gr_ref, cv_ref, z_ref, ba_ref = rest[:10]
    tm = x_ref.shape[0]
    if tiles_per_seq:
        tail_ref, kt_ref, vt_ref, xp_scr = rest[10:]

        @pl.when(pl.program_id(0) % tiles_per_seq == 0)
        def _():
            tail_ref[...] = hist_ref[...]

    hb = _rms_mod(x_ref[...], nw_ref[...], sc_ref[...], sh_ref[...]).astype(BF16)

    def proj(c0, width):
        return _dot(hb, w_ref[:, c0:c0 + width])

    lane = lax.broadcasted_iota(jnp.int32, (tm, DSA_W), 1)
    first_half = (lane % DSA_HEAD_DIM) < (DSA_HEAD_DIM // 2)

    def rope(y, cos, sin_signed):
        partner = jnp.where(first_half, pltpu.roll(y, DSA_W - DSA_HEAD_DIM // 2, 1),
                            pltpu.roll(y, DSA_HEAD_DIM // 2, 1))
        return y * cos + partner * sin_signed

    wide = lambda ref: jnp.concatenate([ref[...]] * (DSA_W // LANES), axis=1)
    ca, sa, cr, sr = wide(ca_ref), wide(sa_ref), wide(cr_ref), wide(sr_ref)
    qa_ref[...] = rope(proj(_C_QA, DSA_W), ca, sa) * (DSA_HEAD_DIM ** -0.5)
    ka = rope(proj(_C_KA, DSA_W), ca, sa)
    va = proj(_C_VA, DSA_W)
    ka_ref[...] = ka
    va_ref[...] = va
    if tiles_per_seq:
        kt_ref[...] = ka.T
        vt_ref[...] = va.T

    qr_ref[...] = rope(proj(_C_QR, RET_W), cr, sr)
    kr_ref[...] = rope(proj(_C_KR, RET_W), cr, sr) * (RET_HEAD_DIM ** -0.5)
    vr_ref[...] = proj(_C_VR, RET_W)
    gr_ref[...] = proj(_C_GR, RET_W)
    for s in range(CONV_DIM // GDN_W):
        y = proj(_C_CV + s * GDN_W, GDN_W)
        if not tiles_per_seq:
            cv_ref[:, s * GDN_W:(s + 1) * GDN_W] = y
            continue
        for j in range(GDN_W // LANES):
            cg = s * (GDN_W // LANES) + j
            cols = slice(cg * LANES, (cg + 1) * LANES)
            yj = y[:, j * LANES:(j + 1) * LANES]
            xp_scr[0, cg, :SUBLANES, :] = tail_ref[:, cols]
            xp_scr[0, cg, SUBLANES:, :] = yj
            cv_ref[:, cols] = _causal_conv_silu(xp_scr, 0, cw_ref, cg, tm)
            tail_ref[:, cols] = yj[tm - SUBLANES:, :]
    z_ref[...] = proj(_C_Z, GDN_W)
    ba_ref[...] = proj(_C_BA, LANES)


def _mod_spec(tm, t, d, col):
    if t >= tm:
        return pl.BlockSpec((None, 1, d), lambda i: ((i * tm) // t, 0, col))
    return pl.BlockSpec((None, tm, d), lambda i: (i, 0, col))


def _conv_operands(conv_hist, conv_w):
    nb = conv_hist.shape[0]
    hist_pad = jnp.concatenate([jnp.zeros((nb, SUBLANES - (CONV_WIDTH - 1), CONV_DIM), F32), conv_hist], axis=1)
    cw_pad = jnp.concatenate([conv_w, jnp.zeros((SUBLANES - CONV_WIDTH, CONV_DIM), F32)], axis=0)
    return hist_pad, cw_pad


def _inproj(x2, t, mod3, norm_w, w_in_b, layer, tabs, tm, conv_hist, conv_w):
    m, d = x2.shape
    nt = tabs[0].shape[0] // tm
    widths = (DSA_W,) * 3 + (RET_W,) * 4 + (CONV_DIM, GDN_W, LANES)
    tab_spec = pl.BlockSpec((tm, LANES), lambda i: (i % nt, 0))
    tiles_per_seq = t // tm if t % tm == 0 else 0
    window_tiles = min(t, DSA_MAX_WINDOW) // tm
    in_specs = [
        pl.BlockSpec((tm, d), lambda i: (i, 0)),
        pl.BlockSpec((1, d), lambda i: (0, 0)),
        _mod_spec(tm, t, d, 0),
        _mod_spec(tm, t, d, 1),
        pl.BlockSpec((None, d, IN_COLS_PAD), lambda i: (layer, 0, 0), pipeline_mode=pl.Buffered(1)),
        tab_spec, tab_spec, tab_spec, tab_spec,
    ]
    operands = [x2, norm_w.reshape(1, d), mod3, mod3, w_in_b, *tabs]
    out_specs = [pl.BlockSpec((tm, w), lambda i: (i, 0)) for w in widths]
    out_shape = [jax.ShapeDtypeStruct((m, w), F32) for w in widths]
    scratch = []
    if tiles_per_seq:
        hist_pad, cw_pad = _conv_operands(conv_hist, conv_w)
        seq_tail = pl.BlockSpec((None, SUBLANES, CONV_DIM), lambda i: (i // tiles_per_seq, 0, 0))
        in_specs += [seq_tail, pl.BlockSpec(cw_pad.shape, lambda i: (0, 0))]
        operands += [hist_pad, cw_pad]
        out_specs.append(seq_tail)
        out_shape.append(jax.ShapeDtypeStruct((m // t, SUBLANES, CONV_DIM), F32))
        first_win = tiles_per_seq - window_tiles
        win_t = pl.BlockSpec((None, DSA_W, tm), lambda i: (i // tiles_per_seq, 0,
                                                            jnp.maximum(i % tiles_per_seq - first_win, 0)))
        out_specs += [win_t, win_t]
        out_shape += [jax.ShapeDtypeStruct((m // t, DSA_W, window_tiles * tm), F32)] * 2
        scratch.append(pltpu.VMEM((1, CONV_DIM // LANES, tm + SUBLANES, LANES), F32))
    outs = pl.pallas_call(
        functools.partial(_inproj_kernel, tiles_per_seq=tiles_per_seq),
        grid=(m // tm,),
        in_specs=in_specs,
        out_specs=out_specs,
        out_shape=out_shape,
        scratch_shapes=scratch,
        compiler_params=_cparams("arbitrary"),
        name="inproj",
    )(*operands)
    return (outs[:10], outs[10:]) if tiles_per_seq else (outs, None)


DSA_BLK = 128


DSA_QBLK = DSA_MAX_WINDOW
DSA_GROUP = 4


def _dsa_kernel(q_ref, kp_ref, kc_ref, vp_ref, vc_ref, o_ref, acc_scr, m_scr, l_scr):
    blk = pl.program_id(2)
    qblk = q_ref.shape[0]
    row = lax.broadcasted_iota(jnp.int32, (2 * DSA_BLK, 2 * DSA_BLK), 0) % DSA_BLK
    col = lax.broadcasted_iota(jnp.int32, (2 * DSA_BLK, 2 * DSA_BLK), 1)
    ok = jnp.logical_and(col >= row, col <= row + DSA_BLK)
    ok_first = jnp.logical_and(ok, jnp.logical_or(col >= DSA_BLK, blk > 0))
    lo = lax.broadcasted_iota(jnp.int32, (DSA_BLK, LANES), 1) < DSA_HEAD_DIM
    neg = -jnp.inf
    halves = lambda x: jnp.where(lo, x[:DSA_BLK], x[DSA_BLK:])

    def rows_of(dil, start, n):
        return pl.ds(start, n) if dil == 1 else pl.ds(start, n, stride=dil)

    def group_softmax(dil, subs):
        n = range(len(subs))
        idx = [rows_of(dil, ph + dil * DSA_BLK * j, DSA_BLK) for ph, j in subs]
        q = [q_ref[i, :] for i in idx]
        k, v = [], []
        for u, (ph, j) in enumerate(subs):
            if j == 0:
                band = rows_of(dil, qblk - dil * DSA_BLK + ph, DSA_BLK)
                k.append(jnp.concatenate([kp_ref[band, :], kc_ref[idx[u], :]], axis=0).astype(BF16))
                v.append(jnp.concatenate([vp_ref[band, :], vc_ref[idx[u], :]], axis=0).astype(BF16))
            else:
                both = rows_of(dil, ph + dil * DSA_BLK * (j - 1), 2 * DSA_BLK)
                k.append(kc_ref[both, :].astype(BF16))
                v.append(vc_ref[both, :].astype(BF16))
        q2 = [jnp.concatenate([jnp.where(lo, q[u], 0.0), jnp.where(lo, 0.0, q[u])], axis=0).astype(BF16)
              for u in n]
        s = [jnp.where(ok_first if subs[u][1] == 0 else ok, _dot_nt(q2[u], k[u]), neg) for u in n]
        mx = [jnp.max(s[u], axis=-1, keepdims=True) for u in n]
        p = [jnp.exp(s[u] - mx[u]) for u in n]
        den = [jnp.sum(p[u], axis=-1, keepdims=True) for u in n]
        pv = [_dot(p[u].astype(BF16), v[u]) for u in n]
        return idx, [(halves(pv[u]), halves(jnp.broadcast_to(mx[u], (2 * DSA_BLK, LANES))),
                      halves(jnp.broadcast_to(den[u], (2 * DSA_BLK, LANES)))) for u in n]

    dils = sorted((d for _, d in DSA_PATTERNS), reverse=True)
    for pi, dil in enumerate(dils):
        subs = [(ph, j) for j in range(qblk // (dil * DSA_BLK)) for ph in range(dil)]
        for g0 in range(0, len(subs), DSA_GROUP):
            idx, tiles = group_softmax(dil, subs[g0:g0 + DSA_GROUP])
            for i, (pv, mx, den) in zip(idx, tiles):
                if pi > 0:
                    m_old = m_scr[i, :]
                    m_new = jnp.maximum(m_old, mx)
                    w_old = jnp.exp(m_old - m_new)
                    w_cur = jnp.exp(mx - m_new)
                    pv = acc_scr[i, :] * w_old + pv * w_cur
                    den = l_scr[i, :] * w_old + den * w_cur
                    mx = m_new
                if pi < len(dils) - 1:
                    acc_scr[i, :] = pv
                    m_scr[i, :] = mx
                    l_scr[i, :] = den
                else:
                    o_ref[i, :] = pv / den


def _dsa_prompt(q, k, v, nb, t):
    assert all(w // d == DSA_BLK for w, d in DSA_PATTERNS) and t % DSA_QBLK == 0
    ngrp = DSA_W // LANES
    r3 = lambda a: a.reshape(nb, t, DSA_W)
    cur = pl.BlockSpec((None, DSA_QBLK, LANES), lambda b, g, i: (b, i, g))
    prv = pl.BlockSpec((None, DSA_QBLK, LANES), lambda b, g, i: (b, jnp.maximum(i - 1, 0), g))
    return pl.pallas_call(
        _dsa_kernel,
        grid=(nb, ngrp, t // DSA_QBLK),
        in_specs=[cur, prv, cur, prv, cur],
        out_specs=cur,
        out_shape=jax.ShapeDtypeStruct((nb, t, DSA_W), F32),
        scratch_shapes=[pltpu.VMEM((DSA_QBLK, LANES), F32)] * 3,
        compiler_params=_cparams("parallel", "parallel", "parallel"),
        name="dsa_prompt",
    )(r3(q), r3(k), r3(k), r3(v), r3(v)).reshape(nb * t, DSA_W)


DSA_STEP_SEQS = 2


def _multiplicity(dist):
    total = jnp.zeros(dist.shape, F32)
    for window, dil in DSA_PATTERNS:
        hit = (dist >= 0) & (dist <= window) & ((dist & (dil - 1)) == 0)
        total = total + hit.astype(F32)
    return total


def _dsa_step_kernel(q_ref, kc_ref, vc_ref, kn_ref, vn_ref, o_ref):
    nseq, tq = q_ref.shape[0], q_ref.shape[1]
    wb = kc_ref.shape[2]
    qi = lax.broadcasted_iota(jnp.int32, (2 * tq, wb), 0) % tq
    w_c = _multiplicity(wb + qi - lax.broadcasted_iota(jnp.int32, (2 * tq, wb), 1))
    qn = lax.broadcasted_iota(jnp.int32, (2 * tq, LANES), 0) % tq
    nn = lax.broadcasted_iota(jnp.int32, (2 * tq, LANES), 1)
    w_n = jnp.where(nn < tq, _multiplicity(qn - nn), 0.0)
    lo = lax.broadcasted_iota(jnp.int32, (tq, LANES), 1) < DSA_HEAD_DIM
    pad = jnp.zeros((LANES - tq, LANES), F32)
    neg = -jnp.inf
    for b, g in [(b, g) for b in range(nseq) for g in range(DSA_W // LANES)]:
        cols = slice(g * LANES, (g + 1) * LANES)
        q = q_ref[b, :, cols]
        q2 = jnp.concatenate([jnp.where(lo, q, 0.0), jnp.where(lo, 0.0, q)], axis=0).astype(BF16)
        kt_c, vt_c = kc_ref[b, cols, :].astype(BF16), vc_ref[b, cols, :].astype(BF16)
        k_n = jnp.concatenate([kn_ref[b, :, cols], pad], axis=0).astype(BF16)
        v_n = jnp.concatenate([vn_ref[b, :, cols], pad], axis=0).astype(BF16)
        s_c = jnp.where(w_c > 0, _dot(q2, kt_c), neg)
        s_n = jnp.where(w_n > 0, _dot_nt(q2, k_n), neg)
        mx = jnp.maximum(jnp.max(s_c, axis=-1, keepdims=True), jnp.max(s_n, axis=-1, keepdims=True))
        p_c = w_c * jnp.exp(s_c - mx)
        p_n = w_n * jnp.exp(s_n - mx)
        den = jnp.sum(p_c, axis=-1, keepdims=True) + jnp.sum(p_n, axis=-1, keepdims=True)
        o2 = (_dot_nt(p_c.astype(BF16), vt_c) + _dot(p_n.astype(BF16), v_n)) / den
        o_ref[b, :, cols] = jnp.where(lo, o2[:tq], o2[tq:])


def _dsa_step(q, k_new, v_new, k_cache, v_cache, layer, nb, t):
    wb = k_cache.shape[3]
    bblk = math.gcd(nb, DSA_STEP_SEQS)
    new = pl.BlockSpec((bblk, t, DSA_W), lambda b: (b, 0, 0))
    cache = pl.BlockSpec((None, bblk, DSA_W, wb), lambda b: (layer, b, 0, 0))
    r3 = lambda a: a.reshape(nb, t, DSA_W)
    return pl.pallas_call(
        _dsa_step_kernel,
        grid=(nb // bblk,),
        in_specs=[new, cache, cache, new, new],
        out_specs=new,
        out_shape=jax.ShapeDtypeStruct((nb, t, DSA_W), F32),
        compiler_params=_cparams("parallel"),
        name="dsa_step",
    )(r3(q), k_cache, v_cache, r3(k_new), r3(v_new)).reshape(nb * t, DSA_W)


RET_CHUNK = 256
RET_SEQS_PER_STEP = 2
RET_UNROLL = 4


def _ret_kernel(q_ref, k_ref, v_ref, g_ref, s0_ref, dec_ref, qd_ref, kd_ref, cd_ref, bd_ref, nw_ref,
                o_ref, so_ref, s_scr, *, chunk):
    tb = pl.program_id(1)
    ngrp = RET_W // LANES

    @pl.when(tb == 0)
    def _():
        s_scr[...] = s0_ref[...]

    lo = lax.broadcasted_iota(jnp.int32, (chunk, LANES), 1) < RET_HEAD_DIM
    nw = nw_ref[...]
    bd = bd_ref[...]

    def body(c, carry):
        rows = pl.ds(pl.multiple_of(c * chunk, chunk), chunk)
        for b in range(q_ref.shape[0]):
            for g in range(ngrp):
                cols = slice(g * LANES, (g + 1) * LANES)
                q, k, v = q_ref[b, rows, cols], k_ref[b, rows, cols], v_ref[b, rows, cols]
                kb, vb = k.astype(BF16), v.astype(BF16)
                parts = []
                for hh in range(2):
                    qm = jnp.where(lo if hh == 0 else jnp.logical_not(lo), q, 0.0).astype(BF16)
                    inner = _dot_nt(qm, kb) * dec_ref[2 * g + hh]
                    parts.append(_dot(inner.astype(BF16), vb))
                s_prev = s_scr[b, g]
                o = jnp.where(lo, parts[0], parts[1]) + _dot(q.astype(BF16), s_prev.astype(BF16)) * qd_ref[g]
                s_scr[b, g] = s_prev * cd_ref[g] + bd * _dot_tn((k * kd_ref[g]).astype(BF16), vb)
                o2 = o * o
                ms = jnp.where(lo, jnp.sum(jnp.where(lo, o2, 0.0), axis=-1, keepdims=True),
                               jnp.sum(jnp.where(lo, 0.0, o2), axis=-1, keepdims=True)) * (1.0 / RET_HEAD_DIM)
                o_ref[b, rows, cols] = o * lax.rsqrt(ms + NORM_EPS) * nw * _silu(g_ref[b, rows, cols])
        return carry

    nchunk = q_ref.shape[1] // chunk
    lax.fori_loop(0, nchunk, body, 0, unroll=min(RET_UNROLL, nchunk))

    @pl.when(tb == pl.num_programs(1) - 1)
    def _():
        so_ref[...] = s_scr[...]


def _ret_tables(chunk):
    log_gamma = jnp.log(1.0 - 2.0 ** (-5.0 - jnp.arange(RET_HEADS, dtype=F32)))
    i = jnp.arange(chunk, dtype=F32)
    diff = i[:, None] - i[None, :]
    causal = diff >= 0
    decay = jnp.where(causal[None], jnp.exp(log_gamma[:, None, None] * jnp.where(causal, diff, 0.0)[None]), 0.0)
    per_lane = lambda a: jnp.repeat(a, RET_HEAD_DIM, axis=0).reshape(RET_W // LANES, LANES, -1)
    q_dec = per_lane(jnp.exp(log_gamma[:, None] * (i[None, :] + 1.0))).transpose(0, 2, 1)
    k_dec = per_lane(jnp.exp(log_gamma[:, None] * (chunk - 1.0 - i)[None, :])).transpose(0, 2, 1)
    c_dec = jnp.broadcast_to(per_lane(jnp.exp(log_gamma * chunk)[:, None]), (RET_W // LANES, LANES, LANES))
    head_of = jnp.arange(LANES) // RET_HEAD_DIM
    block_diag = (head_of[:, None] == head_of[None, :]).astype(F32)
    return decay, q_dec, k_dec, c_dec, block_diag


def _retention(q, k, v, gate, state_bd, ret_norm, nb, t):
    chunk = RET_CHUNK if t % RET_CHUNK == 0 else t
    tblk = min(t, 1024)
    ngrp = RET_W // LANES
    bblk = _seqs_per_step(nb, t, RET_SEQS_PER_STEP)
    decay, q_dec, k_dec, c_dec, block_diag = _ret_tables(chunk)
    r3 = lambda a: a.reshape(nb, t, RET_W)
    tok = pl.BlockSpec((bblk, tblk, RET_W), lambda b, i: (b, i, 0))
    st = pl.BlockSpec((bblk, ngrp, LANES, LANES), lambda b, i: (b, 0, 0, 0))
    full = lambda a: pl.BlockSpec(a.shape, lambda b, i: (0,) * a.ndim)
    nw = jnp.tile(ret_norm, LANES // RET_HEAD_DIM).reshape(1, LANES)
    o, s_new = pl.pallas_call(
        functools.partial(_ret_kernel, chunk=chunk),
        grid=(nb // bblk, t // tblk),
        in_specs=[tok, tok, tok, tok, st, full(decay), full(q_dec), full(k_dec), full(c_dec),
                  full(block_diag), full(nw)],
        out_specs=[tok, st],
        out_shape=[jax.ShapeDtypeStruct((nb, t, RET_W), F32),
                   jax.ShapeDtypeStruct((nb, ngrp, LANES, LANES), F32)],
        scratch_shapes=[pltpu.VMEM((bblk, ngrp, LANES, LANES), F32)],
        compiler_params=_cparams("parallel", "arbitrary"),
        name="retention",
    )(r3(q), r3(k), r3(v), r3(gate), state_bd, decay, q_dec, k_dec, c_dec, block_diag, nw)
    return o.reshape(nb * t, RET_W), s_new


def _to_block_diag(s):
    nb = s.shape[0]
    s = s.reshape(nb, 2, 2, RET_HEAD_DIM, RET_HEAD_DIM)
    z = jnp.zeros_like(s[:, :, 0])
    top = jnp.concatenate([s[:, :, 0], z], axis=-1)
    bot = jnp.concatenate([z, s[:, :, 1]], axis=-1)
    return jnp.concatenate([top, bot], axis=-2)


def _from_block_diag(s):
    h = RET_HEAD_DIM
    return jnp.stack([s[:, :, :h, :h], s[:, :, h:, h:]], axis=2).reshape(s.shape[0], RET_HEADS, h, h)


GDN_PREP_CHUNKS_PER_ITER = 8
GDN_SCAN_SEQS_PER_STEP = 2
GDN_SCAN_UNROLL = 4


def _softplus(x):
    return jnp.maximum(x, 0.0) + jnp.log1p(jnp.exp(-jnp.abs(x)))


def _gdn_prep_kernel(cv_ref, hist_ref, ba_ref, cw_ref, ab_ref,
                     qg_ref, kd_ref, u_ref, w_ref, at_ref, el_ref, beta_scr, g_scr, *conv_scr, chunk, cpi):
    nseq, tblk = cv_ref.shape[0], cv_ref.shape[1]
    ncg = CONV_DIM // LANES
    ab = ab_ref[...]
    cs_scr = cv_ref
    for b in range(nseq):
        if conv_scr:
            xp_scr, cs_scr = conv_scr
            for cg in range(ncg):
                cols = slice(cg * LANES, (cg + 1) * LANES)
                xp_scr[b, cg, :SUBLANES, :] = hist_ref[b, :, cols]
                xp_scr[b, cg, SUBLANES:, :] = cv_ref[b, :, cols]
                cs_scr[b, :, cols] = _causal_conv_silu(xp_scr, b, cw_ref, cg, tblk)

        ba = ba_ref[b]
        beta_scr[b] = jax.nn.sigmoid(ba)
        g_scr[b] = -jnp.exp(ab[0:1, :]) * _softplus(ba + ab[1:2, :])

    grp = 2 if 2 * chunk == LANES else 1
    width = grp * chunk
    ri = lax.broadcasted_iota(jnp.int32, (chunk, width), 0)
    lane = lax.broadcasted_iota(jnp.int32, (chunk, width), 1)
    ci = lane % chunk
    first = lane < chunk
    incl = ri >= ci
    strict = ri > ci
    tri = (lax.broadcasted_iota(jnp.int32, (chunk, chunk), 0)
           >= lax.broadcasted_iota(jnp.int32, (chunk, chunk), 1)).astype(F32)
    lane_pick = (lax.broadcasted_iota(jnp.int32, (SUBLANES, LANES), 0)
                 == lax.broadcasted_iota(jnp.int32, (SUBLANES, LANES), 1)).astype(F32)
    base = min(SUBLANES, chunk)
    assert chunk % base == 0 and (chunk // base) & (chunk // base - 1) == 0
    same_blk = [(ri >> sh) == (ci >> sh) for sh in range(int(math.log2(base)), int(math.log2(chunk)) + 1)]

    def l2n(x):
        return x * lax.rsqrt(jnp.sum(x * x, axis=-1, keepdims=True) + NORM_EPS)

    def side_by_side(per_head):
        if grp == 1:
            return per_head[0][:, :width]
        return jnp.where(first, per_head[0][:, :width], per_head[1][:, :width])

    def block_diag(y):
        if grp == 1:
            return y.astype(BF16)
        return jnp.concatenate([jnp.where(first, y, 0.0), jnp.where(first, 0.0, y)], axis=0).astype(BF16)

    def block_rows(per_head):
        if grp == 1:
            return per_head[0].astype(BF16)
        a, b = per_head
        return jnp.concatenate([jnp.concatenate([a, jnp.zeros_like(b)], axis=1),
                                jnp.concatenate([jnp.zeros_like(a), b], axis=1)], axis=0).astype(BF16)

    def step(it, carry):
        probs = []
        for b, cc in [(b, cc) for b in range(nseq) for cc in range(cpi)]:
            c = it * cpi + cc
            rows = pl.ds(pl.multiple_of(c * chunk, chunk), chunk)
            beta_c = beta_scr[b, rows, :]
            gcol = _dot(tri, g_scr[b, rows, :], precision=HIGHEST)
            grow = _dot_nt(lane_pick, jnp.concatenate([gcol] * grp, axis=0), precision=HIGHEST)
            for h0 in range(0, GDN_HEADS, grp):
                hs = range(h0, h0 + grp)
                q = [l2n(cs_scr[b, rows, h * LANES:(h + 1) * LANES]) for h in hs]
                k = [l2n(cs_scr[b, rows, GDN_W + h * LANES:GDN_W + (h + 1) * LANES]) for h in hs]
                v = [cs_scr[b, rows, 2 * GDN_W + h * LANES:2 * GDN_W + (h + 1) * LANES] for h in hs]
                beta = [jnp.broadcast_to(beta_c[:, h:h + 1], (chunk, LANES)) for h in hs]
                gc = [jnp.broadcast_to(gcol[:, GDN_HEADS + h:GDN_HEADS + h + 1], (chunk, LANES)) for h in hs]
                g_row = side_by_side([jnp.broadcast_to(grow[GDN_HEADS + h:GDN_HEADS + h + 1, :], (chunk, width))
                                      for h in hs])
                gdiff = side_by_side(gc) - g_row
                dmask = jnp.where(incl, jnp.exp(jnp.where(incl, gdiff, 0.0)), 0.0)
                kb = [k[i] * beta[i] for i in range(grp)]
                k_rows = block_rows(k)
                lower = jnp.where(strict, _dot_nt(jnp.concatenate(kb, axis=1).astype(BF16), k_rows) * dmask, 0.0)
                npow = jnp.where(same_blk[0], -lower, 0.0)
                probs.append(dict(b=b, c=c, rows=rows, h0=h0, q=q, k=k, v=v, beta=beta, gc=gc, dmask=dmask,
                                  kb=kb, k_rows=k_rows, lower=lower, npow=npow, qmat=npow))
        for _ in range(int(math.log2(base)) - 1):
            for p in probs:
                p["npow"] = _dot(p["npow"].astype(BF16), block_diag(p["npow"]))
            for p in probs:
                p["qmat"] = p["qmat"] + p["npow"] + _dot(p["qmat"].astype(BF16), block_diag(p["npow"]))
        for lvl in range(1, len(same_blk)):
            for p in probs:
                cb = jnp.where(jnp.logical_and(same_blk[lvl], jnp.logical_not(same_blk[lvl - 1])),
                               p["lower"], 0.0)
                p["x"] = cb + _dot(p["qmat"].astype(BF16), block_diag(cb))
            for p in probs:
                p["qmat"] = p["qmat"] - p["x"] - _dot(p["x"].astype(BF16), block_diag(p["qmat"]))
        for p in probs:
            b, rows, h0, gc = p["b"], p["rows"], p["h0"], p["gc"]
            cols = slice(h0 * LANES, (h0 + grp) * LANES)
            idx = range(grp)
            eg = [jnp.exp(gc[i]) for i in idx]
            rhs_u = [p["v"][i] * p["beta"][i] for i in idx]
            rhs_w = [p["kb"][i] * eg[i] for i in idx]
            qb = p["qmat"].astype(BF16)
            qs = [p["q"][i] * (GDN_HEAD_DIM ** -0.5) for i in idx]
            g_last = [gc[i][chunk - 1:chunk, :] for i in idx]
            cat = lambda xs: jnp.concatenate(xs, axis=1)
            qg_ref[b, rows, cols] = cat([qs[i] * eg[i] for i in idx]).astype(qg_ref.dtype)
            kd_ref[b, rows, cols] = cat([p["k"][i] * jnp.exp(g_last[i] - gc[i]) for i in idx]).astype(kd_ref.dtype)
            u_ref[b, rows, cols] = cat(rhs_u) + _dot(qb, block_rows(rhs_u))
            w_ref[b, rows, cols] = (cat(rhs_w) + _dot(qb, block_rows(rhs_w))).astype(w_ref.dtype)
            at_ref[b, rows, h0 * chunk:(h0 + grp) * chunk] = (
                _dot_nt(cat(qs).astype(BF16), p["k_rows"]) * p["dmask"]).astype(at_ref.dtype)
            el_ref[b, pl.ds(pl.multiple_of(p["c"] * SUBLANES, SUBLANES), SUBLANES), cols] = cat(
                [jnp.broadcast_to(jnp.exp(g_last[i]), (SUBLANES, LANES)) for i in idx])
        return carry

    lax.fori_loop(0, tblk // (chunk * cpi), step, 0)


def _gdn_scan_kernel(qg_ref, kd_ref, u_ref, w_ref, at_ref, el_ref, z_ref, s0_ref, nw_ref,
                     o_ref, so_ref, s_scr, *, chunk):
    tb = pl.program_id(1)

    @pl.when(tb == 0)
    def _():
        s_scr[...] = s0_ref[...]

    nw = nw_ref[...]
    probs = [(b, h) for b in range(qg_ref.shape[0]) for h in range(GDN_HEADS)]

    def step(c, carry):
        rows = pl.ds(pl.multiple_of(c * chunk, chunk), chunk)
        erow = pl.ds(pl.multiple_of(c * SUBLANES, SUBLANES), SUBLANES)
        cols = lambda h: slice(h * LANES, (h + 1) * LANES)
        s_prev = [s_scr[b, h] for b, h in probs]
        sb = [s.astype(BF16) for s in s_prev]
        v_new = [u_ref[b, rows, cols(h)] - _dot(w_ref[b, rows, cols(h)].astype(BF16), sb[i])
                 for i, (b, h) in enumerate(probs)]
        vb = [v.astype(BF16) for v in v_new]
        for i, (b, h) in enumerate(probs):
            el = el_ref[b, erow, cols(h)][0:1, :]
            s_scr[b, h] = s_prev[i] * el + _dot_tn(kd_ref[b, rows, cols(h)].astype(BF16), vb[i])
        for i, (b, h) in enumerate(probs):
            attn = at_ref[b, rows, h * chunk:(h + 1) * chunk].astype(BF16)
            o = _dot(qg_ref[b, rows, cols(h)].astype(BF16), sb[i]) + _dot(attn, vb[i])
            ms = jnp.mean(o * o, axis=-1, keepdims=True)
            o_ref[b, rows, cols(h)] = o * lax.rsqrt(ms + NORM_EPS) * nw * _silu(z_ref[b, rows, cols(h)])
        return carry

    nchunk = qg_ref.shape[1] // chunk
    lax.fori_loop(0, nchunk, step, 0, unroll=min(GDN_SCAN_UNROLL, nchunk))

    @pl.when(tb == pl.num_programs(1) - 1)
    def _():
        so_ref[...] = s_scr[...]


def _gated_delta(cv, conv_done, z, ba, conv_hist, states, layer, conv_w, a_log, dt_bias, gdn_norm, nb, t):
    chunk = CHUNK if t % CHUNK == 0 else t
    tblk = min(t, 512)
    assert conv_done or tblk == t
    nchunk_blk = tblk // chunk
    hist_pad, cw_pad = _conv_operands(conv_hist, conv_w)
    ab = jnp.zeros((SUBLANES, LANES), F32)
    ab = ab.at[0, GDN_HEADS:2 * GDN_HEADS].set(a_log).at[1, GDN_HEADS:2 * GDN_HEADS].set(dt_bias)
    cv3 = cv.reshape(nb, t, CONV_DIM)
    pblk = _seqs_per_step(nb, t, 2)
    tok = lambda w: pl.BlockSpec((pblk, tblk, w), lambda b, i: (b, i, 0))
    full = lambda a: pl.BlockSpec(a.shape, lambda b, i: (0,) * a.ndim)
    el_spec =pl.BlockSpec((pblk, nchunk_blk * SUBLANES, GDN_W), lambda b, i: (b, i, 0))
    tok_shape = lambda w, dt=F32: jax.ShapeDtypeStruct((nb, t, w), dt)
    el_shape = jax.ShapeDtypeStruct((nb, (t // chunk) * SUBLANES, GDN_W), F32)
    opd = BF16 if chunk % (2 * SUBLANES) == 0 else F32
    qg, kd, u, w, attn, el = pl.pallas_call(
        functools.partial(_gdn_prep_kernel, chunk=chunk, cpi=min(GDN_PREP_CHUNKS_PER_ITER, nchunk_blk)),
        grid=(nb // pblk, t // tblk),
        in_specs=[tok(CONV_DIM), pl.BlockSpec((pblk, SUBLANES, CONV_DIM), lambda b, i: (b, 0, 0)),
                  tok(LANES), full(cw_pad), full(ab)],
        out_specs=[tok(GDN_W), tok(GDN_W), tok(GDN_W), tok(GDN_W), tok(GDN_HEADS * chunk), el_spec],
        out_shape=[tok_shape(GDN_W, opd), tok_shape(GDN_W, opd), tok_shape(GDN_W), tok_shape(GDN_W, opd),
                   tok_shape(GDN_HEADS * chunk, opd), el_shape],
        scratch_shapes=[pltpu.VMEM((pblk, tblk, LANES), F32), pltpu.VMEM((pblk, tblk, LANES), F32)] + (
            [] if conv_done else [pltpu.VMEM((pblk, CONV_DIM // LANES, tblk + SUBLANES, LANES), F32),
                                  pltpu.VMEM((pblk, tblk, CONV_DIM), F32)]),
        compiler_params=_cparams("parallel", "parallel"),
        name="gdn_prep",
    )(cv3, hist_pad, ba.reshape(nb, t, LANES), cw_pad, ab)

    bblk = _seqs_per_step(nb, t, GDN_SCAN_SEQS_PER_STEP)
    stok = lambda w: pl.BlockSpec((bblk, tblk, w), lambda b, i: (b, i, 0))
    sel_spec = pl.BlockSpec((bblk, nchunk_blk * SUBLANES, GDN_W), lambda b, i: (b, i, 0))
    st_in = pl.BlockSpec((None, bblk, GDN_HEADS, LANES, LANES), lambda b, i: (layer, b, 0, 0, 0))
    st = pl.BlockSpec((bblk, GDN_HEADS, LANES, LANES), lambda b, i: (b, 0, 0, 0))
    nw = gdn_norm.reshape(1, LANES)
    o, s_new = pl.pallas_call(
        functools.partial(_gdn_scan_kernel, chunk=chunk),
        grid=(nb // bblk, t // tblk),
        in_specs=[stok(GDN_W), stok(GDN_W), stok(GDN_W), stok(GDN_W), stok(GDN_HEADS * chunk), sel_spec,
                  stok(GDN_W), st_in, full(nw)],
        out_specs=[stok(GDN_W), st],
        out_shape=[tok_shape(GDN_W), jax.ShapeDtypeStruct((nb, GDN_HEADS, LANES, LANES), F32)],
        scratch_shapes=[pltpu.VMEM((bblk, GDN_HEADS, LANES, LANES), F32)],
        compiler_params=_cparams("parallel", "arbitrary"),
        name="gdn_scan",
    )(qg, kd, u, w, attn, el, z.reshape(nb, t, GDN_W), states, nw)
    return o.reshape(nb * t, GDN_W), s_new


FFN_TILE = 256


def _out_ffn_kernel(x_ref, oa_ref, orr_ref, oc_ref, g1_ref, sh2_ref, sc2_ref, g2_ref, nw_ref, wo_ref, wg_ref,
                    wu_ref, wd_ref, fn_ref, out_ref, *, final):
    mix = (_dot(oa_ref[...].astype(BF16), wo_ref[0:DSA_W, :])
           + _dot(orr_ref[...].astype(BF16), wo_ref[DSA_W:DSA_W + RET_W, :])
           + _dot(oc_ref[...].astype(BF16), wo_ref[DSA_W + RET_W:, :]))
    x1 = x_ref[...] + g1_ref[...] * mix
    hb = _rms_mod(x1, nw_ref[...], sc2_ref[...], sh2_ref[...]).astype(BF16)
    acc = None
    for j in range(wg_ref.shape[1] // FFN_TILE):
        cols = slice(j * FFN_TILE, (j + 1) * FFN_TILE)
        act = (_silu(_dot(hb, wg_ref[:, cols])) * _dot(hb, wu_ref[:, cols])).astype(BF16)
        down = _dot(act, wd_ref[cols, :])
        acc = down if acc is None else acc + down
    x2 = x1 + g2_ref[...] * acc
    if final:
        ms = jnp.mean(x2 * x2, axis=-1, keepdims=True)
        x2 = x2 * lax.rsqrt(ms + NORM_EPS) * fn_ref[...]
    out_ref[...] = x2


def _out_ffn(x2, t, oa, orr, oc, mod3, norm_w, w_out_b, wg_b, wu_b, wd_b, layer, final_norm, final, tm):
    m, d = x2.shape
    row = lambda w: pl.BlockSpec((tm, w), lambda i: (i, 0))
    const = lambda a: pl.BlockSpec(a.shape, lambda i: (0,) * a.ndim, pipeline_mode=pl.Buffered(1))
    wspec = lambda a: pl.BlockSpec((None,) + a.shape[1:], lambda i: (layer,) + (0,) * (a.ndim - 1),
                                   pipeline_mode=pl.Buffered(1))
    nw = norm_w.reshape(1, d)
    fn = final_norm.reshape(1, d)
    return pl.pallas_call(
        functools.partial(_out_ffn_kernel, final=final),
        grid=(m // tm,),
        in_specs=[row(d), row(DSA_W), row(RET_W), row(GDN_W),
                  _mod_spec(tm, t, d, 2), _mod_spec(tm, t, d, 3), _mod_spec(tm, t, d, 4), _mod_spec(tm, t, d, 5),
                  const(nw), wspec(w_out_b), wspec(wg_b), wspec(wu_b), wspec(wd_b), const(fn)],
        out_specs=row(d),
        out_shape=jax.ShapeDtypeStruct((m, d), F32),
        compiler_params=_cparams("parallel"),
        name="out_ffn",
    )(x2, oa, orr, oc, mod3, mod3, mod3, mod3, nw, w_out_b, wg_b, wu_b, wd_b, fn)


ROPE_SPLIT = 64


def _rope_tables(pos0, t, inv_freq):
    reps = LANES // (2 * inv_freq.shape[0])
    inv_lane = jnp.tile(jnp.concatenate([inv_freq, inv_freq]), reps)[None, :]
    sign_lane = jnp.tile(jnp.concatenate([-jnp.ones_like(inv_freq), jnp.ones_like(inv_freq)]), reps)[None, :]
    if t % ROPE_SPLIT:
        ang = (pos0 + jnp.arange(t, dtype=jnp.int32)).astype(F32)[:, None] * inv_lane
        return jnp.cos(ang), jnp.sin(ang) * sign_lane
    coarse = (pos0 + ROPE_SPLIT * jnp.arange(t // ROPE_SPLIT, dtype=jnp.int32)).astype(F32)[:, None] * inv_lane
    fine = jnp.arange(ROPE_SPLIT, dtype=jnp.int32).astype(F32)[:, None] * inv_lane
    cc, sc = jnp.cos(coarse)[:, None, :], jnp.sin(coarse)[:, None, :]
    cf, sf = jnp.cos(fine)[None], jnp.sin(fine)[None]
    cos = (cc * cf - sc * sf).reshape(t, LANES)
    sin = (sc * cf + cc * sf).reshape(t, LANES)
    return cos, sin * sign_lane


def _trunk(x, modp, pos0, k_hist, v_hist, s_ret, s_gdn, conv_hist, wts):
    (norm_mix, norm_ffn, w_in_b, ret_norm, conv_w, a_log, dt_bias, gdn_norm, w_out_b, wg_b, wu_b, wd_b,
     final_norm) = wts
    nb, t, d = x.shape
    m = nb * t
    depth = w_in_b.shape[0]
    tm = min(TOKEN_TILE, m)
    inv_a = 1.0 / (ROPE_THETA ** (jnp.arange(0, DSA_HEAD_DIM, 2, dtype=F32) / DSA_HEAD_DIM))
    inv_r = 1.0 / (10000.0 ** jnp.linspace(0.0, 1.0, RET_HEAD_DIM // 2, dtype=F32))
    tabs = _rope_tables(pos0, t, inv_a) + _rope_tables(pos0, t, inv_r)
    if t < tm:
        tabs = tuple(jnp.tile(a, (tm // t, 1)) for a in tabs)
    x2 = x.reshape(m, d)
    ks, vs, rs, gs, cs = [], [], [], [], []
    for l in range(depth):
        if t >= tm:
            mod3 = modp[l].reshape(nb, 1, 6 * d)
        else:
            mod3 = jnp.repeat(modp[l], t, axis=0).reshape(m // tm, tm, 6 * d)
        (qa, ka, va, qr, kr, vr, gr, cv, z, ba), fused = _inproj(
            x2, t, mod3, norm_mix[l], w_in_b, l, tabs, tm, conv_hist[l], conv_w[l])
        nhist = CONV_WIDTH - 1
        keep = min(t, DSA_MAX_WINDOW)
        if fused is None:
            cvn = jnp.concatenate([conv_hist[l], cv.reshape(nb, t, CONV_DIM)], axis=1)[:, -nhist:]
            window = lambda a: a.reshape(nb, t, DSA_W)[:, t - keep:].reshape(nb, keep, DSA_HEADS, DSA_HEAD_DIM)
            kwin, vwin = window(ka), window(va)
        else:
            conv_tail, k_t, v_t = fused
            cvn = conv_tail[:, SUBLANES - nhist:]
            untransposed = lambda a: a.reshape(nb, DSA_HEADS, DSA_HEAD_DIM, keep).transpose(0, 3, 1, 2)
            kwin, vwin = untransposed(k_t), untransposed(v_t)
        if k_hist is None:
            oa = _dsa_prompt(qa, ka, va, nb, t)
        else:
            oa = _dsa_step(qa, ka, va, k_hist, v_hist, l, nb, t)
        orr, sr = _retention(qr, kr, vr, gr, _to_block_diag(s_ret[l]), ret_norm[l], nb, t)
        oc, sg = _gated_delta(cv, fused is not None, z, ba, conv_hist[l], s_gdn, l, conv_w[l], a_log[l],
                              dt_bias[l], gdn_norm[l], nb, t)
        x2 = _out_ffn(x2, t, oa, orr, oc, mod3, norm_ffn[l], w_out_b, wg_b, wu_b, wd_b, l,
                      final_norm, l == depth - 1, tm)
        ks.append(kwin)
        vs.append(vwin)
        rs.append(_from_block_diag(sr))
        gs.append(sg)
        cs.append(cvn)
    return (x2.reshape(nb, t, d), jnp.stack(ks), jnp.stack(vs), jnp.stack(rs), jnp.stack(gs), jnp.stack(cs))


def kernel(x_prompt, x_sample, cache_win_k, cache_win_v, state_ret, state_gdn, state_conv, c_prompt, c_sample, ada_w, ada_b, norm_mix, norm_ffn, w_in, ret_norm, conv_w, a_log, dt_bias, gdn_norm, w_out, w_gate, w_up, w_down, final_norm):
    nb, t_p, d = x_prompt.shape
    db, t_s, _ = x_sample.shape
    depth = ada_w.shape[0]
    rows = nb + db
    rows_pad = -(-rows // SUBLANES) * SUBLANES
    c_all = jnp.concatenate([c_prompt, c_sample, jnp.zeros((rows_pad - rows, d), F32)], axis=0)
    mod = _modulation(c_all, ada_w, ada_b)
    w_in_b = jnp.pad(w_in, ((0, 0), (0, 0), (0, IN_COLS_PAD - IN_COLS))).astype(BF16)
    wts = (norm_mix, norm_ffn, w_in_b, ret_norm, conv_w, a_log, dt_bias, gdn_norm, w_out.astype(BF16),
           w_gate.astype(BF16), w_up.astype(BF16), w_down.astype(BF16), final_norm)

    zr = jnp.zeros((depth, nb, RET_HEADS, RET_HEAD_DIM, RET_HEAD_DIM), F32)
    zg = jnp.zeros((depth, nb, GDN_HEADS, GDN_HEAD_DIM, GDN_HEAD_DIM), F32)
    zc = jnp.zeros((depth, nb, CONV_WIDTH - 1, CONV_DIM), F32)
    y_p, kp, vp, rp, gp, cp = _trunk(x_prompt, mod[:, :nb], 0,
                                     None, None, zr, zg, zc, wts)
    wb = cache_win_k.shape[2]
    y_s, ks, vs, rs, gs, cs = _trunk(x_sample, mod[:, nb:rows], PAST_LEN,
                                     cache_win_k.reshape(depth, db, wb, DSA_W).transpose(0, 1, 3, 2),
                                     cache_win_v.reshape(depth, db, wb, DSA_W).transpose(0, 1, 3, 2),
                                     state_ret, state_gdn, state_conv, wts)
    return (y_p, y_s, kp, vp, rp, gp, cp, ks, vs, rs, gs, cs)
```

```python
import functools
import math

import jax
import jax.numpy as jnp
from jax import lax
from jax.experimental import pallas as pl
from jax.experimental.pallas import tpu as pltpu

F32 = jnp.float32
BF16 = jnp.bfloat16
HIGHEST = lax.Precision.HIGHEST

DSA_HEAD_DIM = 64
DSA_HEADS = 4
DSA_PATTERNS = ((128, 1), (512, 4), (2048, 16))
DSA_MAX_WINDOW = 2048
ROPE_THETA = 10000.0
RET_HEAD_DIM = 64
RET_HEADS = 4
GDN_HEAD_DIM = 128
GDN_HEADS = 4
CONV_WIDTH = 4
CHUNK = 64
NORM_EPS = 1e-6
PAST_LEN = 16384
LOG2_E = math.log2(math.e)

DSA_W = DSA_HEADS * DSA_HEAD_DIM
RET_W = RET_HEADS * RET_HEAD_DIM
GDN_W = GDN_HEADS * GDN_HEAD_DIM
CONV_DIM = 3 * GDN_W
LANES = 128
SUBLANES = 8
VMEM_LIMIT = 56 * 1024 * 1024
TOKEN_TILE = 512

_C_QA, _C_KA, _C_VA = 0, DSA_W, 2 * DSA_W
_C_QR = 3 * DSA_W
_C_KR, _C_VR, _C_GR = _C_QR + RET_W, _C_QR + 2 * RET_W, _C_QR + 3 * RET_W
_C_CV = _C_QR + 4 * RET_W
_C_Z = _C_CV + CONV_DIM
_C_BA = _C_Z + GDN_W
IN_COLS = _C_BA + 2 * GDN_HEADS
IN_COLS_PAD = _C_BA + LANES


def _cparams(*sem):
    return pltpu.CompilerParams(dimension_semantics=sem, vmem_limit_bytes=VMEM_LIMIT)


def _dot(a, b, **kw):
    return jnp.dot(a, b, preferred_element_type=F32, **kw)


def _dot_nt(a, b, **kw):
    return lax.dot_general(a, b, (((1,), (1,)), ((), ())), preferred_element_type=F32, **kw)


def _dot_tn(a, b, **kw):
    return lax.dot_general(a, b, (((0,), (0,)), ((), ())), preferred_element_type=F32, **kw)


def _silu(x):
    return x * jax.nn.sigmoid(x)


def _seqs_per_step(nb, t, base):
    return math.gcd(nb, base * (4 if t < CHUNK else 1))


def _mod_kernel(c_ref, w_ref, b_ref, o_ref):
    a = _silu(c_ref[...]).astype(BF16)
    o_ref[...] = _dot(a, w_ref[...].astype(BF16)) + b_ref[...]


def _modulation(c_all, ada_w, ada_b, tn=1536):
    depth, d, n = ada_w.shape
    bp = c_all.shape[0]
    return pl.pallas_call(
        _mod_kernel,
        grid=(depth, n // tn),
        in_specs=[
            pl.BlockSpec((bp, d), lambda l, j: (0, 0)),
            pl.BlockSpec((None, d, tn), lambda l, j: (l, 0, j)),
            pl.BlockSpec((None, 1, tn), lambda l, j: (l, 0, j)),
        ],
        out_specs=pl.BlockSpec((None, bp, tn), lambda l, j: (l, 0, j)),
        out_shape=jax.ShapeDtypeStruct((depth, bp, n), F32),
        compiler_params=_cparams("parallel", "parallel"),
        name="modulation",
    )(c_all, ada_w, ada_b.reshape(depth, 1, n))


def _rms_mod(x, nw, sc, sh):
    ms = jnp.mean(x * x, axis=-1, keepdims=True)
    return (x * lax.rsqrt(ms + NORM_EPS) * nw) * (1.0 + sc) + sh


def _causal_conv_silu(xp_scr, seq, cw_ref, cg, nrows):
    cols = slice(cg * LANES, (cg + 1) * LANES)
    acc = None
    for i in range(CONV_WIDTH):
        start = SUBLANES - (CONV_WIDTH - 1) + i
        term = xp_scr[seq, cg, pl.ds(start, nrows, stride=1), :] * cw_ref[i:i + 1, cols]
        acc = term if acc is None else acc + term
    return _silu(acc)


def _inproj_kernel(x_ref, nw_ref, sh_ref, sc_ref, w_ref, ca_ref, sa_ref, cr_ref, sr_ref, *rest, tiles_per_seq):
    if tiles_per_seq:
        hist_ref, cw_ref = rest[:2]
        rest = rest[2:]
    qa_ref, ka_ref, va_ref, qr_ref, kr_ref, vr_ref, gr_ref, cv_ref, z_ref, ba_ref = rest[:10]
    tm = x_ref.shape[0]
    if tiles_per_seq:
        tail_ref, kt_ref, vt_ref, xp_scr = rest[10:]

        @pl.when(pl.program_id(0) % tiles_per_seq == 0)
        def _():
            tail_ref[...] = hist_ref[...]

    hb = _rms_mod(x_ref[...], nw_ref[...], sc_ref[...], sh_ref[...]).astype(BF16)

    def proj(c0, width):
        return _dot(hb, w_ref[:, c0:c0 + width])

    lane = lax.broadcasted_iota(jnp.int32, (tm, DSA_W), 1)
    first_half = (lane % DSA_HEAD_DIM) < (DSA_HEAD_DIM // 2)

    def rope(y, cos, sin_signed):
        partner = jnp.where(first_half, pltpu.roll(y, DSA_W - DSA_HEAD_DIM // 2, 1),
                            pltpu.roll(y, DSA_HEAD_DIM // 2, 1))
        return y * cos + partner * sin_signed

    wide = lambda ref: jnp.concatenate([ref[...]] * (DSA_W // LANES), axis=1)
    ca, sa, cr, sr = wide(ca_ref), wide(sa_ref), wide(cr_ref), wide(sr_ref)
    qa_ref[...] = rope(proj(_C_QA, DSA_W), ca, sa) * (DSA_HEAD_DIM ** -0.5 * LOG2_E)
    ka = rope(proj(_C_KA, DSA_W), ca, sa)
    va = proj(_C_VA, DSA_W)
    ka_ref[...] = ka
    va_ref[...] = va
    if tiles_per_seq:
        kt_ref[...] = ka.T
        vt_ref[...] = va.T

    qr_ref[...] = rope(proj(_C_QR, RET_W), cr, sr)
    kr_ref[...] = rope(proj(_C_KR, RET_W), cr, sr) * (RET_HEAD_DIM ** -0.5)
    vr_ref[...] = proj(_C_VR, RET_W)
    gr_ref[...] = proj(_C_GR, RET_W)
    for s in range(CONV_DIM // GDN_W):
        y = proj(_C_CV + s * GDN_W, GDN_W)
        if not tiles_per_seq:
            cv_ref[:, s * GDN_W:(s + 1) * GDN_W] = y
            continue
        for j in range(GDN_W // LANES):
            cg = s * (GDN_W // LANES) + j
            cols = slice(cg * LANES, (cg + 1) * LANES)
            yj = y[:, j * LANES:(j + 1) * LANES]
            xp_scr[0, cg, :SUBLANES, :] = tail_ref[:, cols]
            xp_scr[0, cg, SUBLANES:, :] = yj
            cv_ref[:, cols] = _causal_conv_silu(xp_scr, 0, cw_ref, cg, tm)
            tail_ref[:, cols] = yj[tm - SUBLANES:, :]
    z_ref[...] = proj(_C_Z, GDN_W)
    ba_ref[...] = proj(_C_BA, LANES)


def _mod_spec(tm, t, d, col):
    if t >= tm:
        return pl.BlockSpec((None, 1, d), lambda i: ((i * tm) // t, 0, col))
    return pl.BlockSpec((None, tm, d), lambda i: (i, 0, col))


def _conv_operands(conv_hist, conv_w):
    nb = conv_hist.shape[0]
    hist_pad = jnp.concatenate([jnp.zeros((nb, SUBLANES - (CONV_WIDTH - 1), CONV_DIM), F32), conv_hist], axis=1)
    cw_pad = jnp.concatenate([conv_w, jnp.zeros((SUBLANES - CONV_WIDTH, CONV_DIM), F32)], axis=0)
    return hist_pad, cw_pad


def _inproj(x2, t, mod3, norm_w, w_in_b, layer, tabs, tm, conv_hist, conv_w):
    m, d = x2.shape
    nt = tabs[0].shape[0] // tm
    widths = (DSA_W,) * 3 + (RET_W,) * 4 + (CONV_DIM, GDN_W, LANES)
    tab_spec = pl.BlockSpec((tm, LANES), lambda i: (i % nt, 0))
    tiles_per_seq = t // tm if t % tm == 0 else 0
    window_tiles = min(t, DSA_MAX_WINDOW) // tm
    in_specs = [
        pl.BlockSpec((tm, d), lambda i: (i, 0)),
        pl.BlockSpec((1, d), lambda i: (0, 0)),
        _mod_spec(tm, t, d, 0),
        _mod_spec(tm, t, d, 1),
        pl.BlockSpec((None, d, IN_COLS_PAD), lambda i: (layer, 0, 0), pipeline_mode=pl.Buffered(1)),
        tab_spec, tab_spec, tab_spec, tab_spec,
    ]
    operands = [x2, norm_w.reshape(1, d), mod3, mod3, w_in_b, *tabs]
    out_specs = [pl.BlockSpec((tm, w), lambda i: (i, 0)) for w in widths]
    out_shape = [jax.ShapeDtypeStruct((m, w), F32) for w in widths]
    scratch = []
    if tiles_per_seq:
        hist_pad, cw_pad = _conv_operands(conv_hist, conv_w)
        seq_tail = pl.BlockSpec((None, SUBLANES, CONV_DIM), lambda i: (i // tiles_per_seq, 0, 0))
        in_specs += [seq_tail, pl.BlockSpec(cw_pad.shape, lambda i: (0, 0))]
        operands += [hist_pad, cw_pad]
        out_specs.append(seq_tail)
        out_shape.append(jax.ShapeDtypeStruct((m // t, SUBLANES, CONV_DIM), F32))
        first_win = tiles_per_seq - window_tiles
        win_t = pl.BlockSpec((None, DSA_W, tm), lambda i: (i // tiles_per_seq, 0,
                                                            jnp.maximum(i % tiles_per_seq - first_win, 0)))
        out_specs += [win_t, win_t]
        out_shape += [jax.ShapeDtypeStruct((m // t, DSA_W, window_tiles * tm), F32)] * 2
        scratch.append(pltpu.VMEM((1, CONV_DIM // LANES, tm + SUBLANES, LANES), F32))
    outs = pl.pallas_call(
        functools.partial(_inproj_kernel, tiles_per_seq=tiles_per_seq),
        grid=(m // tm,),
        in_specs=in_specs,
        out_specs=out_specs,
        out_shape=out_shape,
        scratch_shapes=scratch,
        compiler_params=_cparams("arbitrary"),
        name="inproj",
    )(*operands)
    return (outs[:10], outs[10:]) if tiles_per_seq else (outs, None)


DSA_BLK = 128


DSA_QBLK = DSA_MAX_WINDOW
DSA_GROUP = 4


def _dsa_kernel(q_ref, kp_ref, kc_ref, vp_ref, vc_ref, o_ref, acc_scr, m_scr, l_scr):
    blk = pl.program_id(2)
    qblk = q_ref.shape[0]
    row = lax.broadcasted_iota(jnp.int32, (2 * DSA_BLK, 2 * DSA_BLK), 0) % DSA_BLK
    col = lax.broadcasted_iota(jnp.int32, (2 * DSA_BLK, 2 * DSA_BLK), 1)
    ok = jnp.logical_and(col >= row, col <= row + DSA_BLK)
    ok_first = jnp.logical_and(ok, jnp.logical_or(col >= DSA_BLK, blk > 0))
    lo = lax.broadcasted_iota(jnp.int32, (DSA_BLK, LANES), 1) < DSA_HEAD_DIM
    neg = -jnp.inf
    halves = lambda x: jnp.where(lo, x[:DSA_BLK], x[DSA_BLK:])

    def rows_of(dil, start, n):
        return pl.ds(start, n) if dil == 1 else pl.ds(start, n, stride=dil)

    def group_softmax(dil, subs):
        n = range(len(subs))
        idx = [rows_of(dil, ph + dil * DSA_BLK * j, DSA_BLK) for ph, j in subs]
        q = [q_ref[i, :] for i in idx]
        k, v = [], []
        for u, (ph, j) in enumerate(subs):
            if j == 0:
                band = rows_of(dil, qblk - dil * DSA_BLK + ph, DSA_BLK)
                k.append(jnp.concatenate([kp_ref[band, :], kc_ref[idx[u], :]], axis=0).astype(BF16))
                v.append(jnp.concatenate([vp_ref[band, :], vc_ref[idx[u], :]], axis=0).astype(BF16))
            else:
                both = rows_of(dil, ph + dil * DSA_BLK * (j - 1), 2 * DSA_BLK)
                k.append(kc_ref[both, :].astype(BF16))
                v.append(vc_ref[both, :].astype(BF16))
        q2 = [jnp.concatenate([jnp.where(lo, q[u], 0.0), jnp.where(lo, 0.0, q[u])], axis=0).astype(BF16)
              for u in n]
        s = [jnp.where(ok_first if subs[u][1] == 0 else ok, _dot_nt(q2[u], k[u]), neg) for u in n]
        mx = [jnp.max(s[u], axis=-1, keepdims=True) for u in n]
        p = [jnp.exp2(s[u] - mx[u]) for u in n]
        den = [jnp.sum(p[u], axis=-1, keepdims=True) for u in n]
        pv = [_dot(p[u].astype(BF16), v[u]) for u in n]
        return idx, [(halves(pv[u]), halves(jnp.broadcast_to(mx[u], (2 * DSA_BLK, LANES))),
                      halves(jnp.broadcast_to(den[u], (2 * DSA_BLK, LANES)))) for u in n]

    dils = sorted((d for _, d in DSA_PATTERNS), reverse=True)
    for pi, dil in enumerate(dils):
        subs = [(ph, j) for j in range(qblk // (dil * DSA_BLK)) for ph in range(dil)]
        for g0 in range(0, len(subs), DSA_GROUP):
            idx, tiles = group_softmax(dil, subs[g0:g0 + DSA_GROUP])
            for i, (pv, mx, den) in zip(idx, tiles):
                if pi > 0:
                    m_old = m_scr[i, :]
                    m_new = jnp.maximum(m_old, mx)
                    w_old = jnp.exp2(m_old - m_new)
                    w_cur = jnp.exp2(mx - m_new)
                    pv = acc_scr[i, :] * w_old + pv * w_cur
                    den = l_scr[i, :] * w_old + den * w_cur
                    mx = m_new
                if pi < len(dils) - 1:
                    acc_scr[i, :] = pv
                    m_scr[i, :] = mx
                    l_scr[i, :] = den
                else:
                    o_ref[i, :] = pv / den


def _dsa_prompt(q, k, v, nb, t):
    assert all(w // d == DSA_BLK for w, d in DSA_PATTERNS) and t % DSA_QBLK == 0
    ngrp = DSA_W // LANES
    r3 = lambda a: a.reshape(nb, t, DSA_W)
    cur = pl.BlockSpec((None, DSA_QBLK, LANES), lambda b, g, i: (b, i, g))
    prv = pl.BlockSpec((None, DSA_QBLK, LANES), lambda b, g, i: (b, jnp.maximum(i - 1, 0), g))
    return pl.pallas_call(
        _dsa_kernel,
        grid=(nb, ngrp, t // DSA_QBLK),
        in_specs=[cur, prv, cur, prv, cur],
        out_specs=cur,
        out_shape=jax.ShapeDtypeStruct((nb, t, DSA_W), F32),
        scratch_shapes=[pltpu.VMEM((DSA_QBLK, LANES), F32)] * 3,
        compiler_params=_cparams("parallel", "parallel", "parallel"),
        name="dsa_prompt",
    )(r3(q), r3(k), r3(k), r3(v), r3(v)).reshape(nb * t, DSA_W)


DSA_STEP_SEQS = 2


def _multiplicity(dist):
    total = jnp.zeros(dist.shape, F32)
    for window, dil in DSA_PATTERNS:
        hit = (dist >= 0) & (dist <= window) & ((dist & (dil - 1)) == 0)
        total = total + hit.astype(F32)
    return total


def _dsa_step_kernel(q_ref, kc_ref, vc_ref, kn_ref, vn_ref, o_ref):
    nseq, tq = q_ref.shape[0], q_ref.shape[1]
    wb = kc_ref.shape[2]
    qi = lax.broadcasted_iota(jnp.int32, (2 * tq, wb), 0) % tq
    w_c = _multiplicity(wb + qi - lax.broadcasted_iota(jnp.int32, (2 * tq, wb), 1))
    qn = lax.broadcasted_iota(jnp.int32, (2 * tq, LANES), 0) % tq
    nn = lax.broadcasted_iota(jnp.int32, (2 * tq, LANES), 1)
    w_n = jnp.where(nn < tq, _multiplicity(qn - nn), 0.0)
    lo = lax.broadcasted_iota(jnp.int32, (tq, LANES), 1) < DSA_HEAD_DIM
    pad = jnp.zeros((LANES - tq, LANES), F32)
    neg = -jnp.inf
    for b, g in [(b, g) for b in range(nseq) for g in range(DSA_W // LANES)]:
        cols = slice(g * LANES, (g + 1) * LANES)
        q = q_ref[b, :, cols]
        q2 = jnp.concatenate([jnp.where(lo, q, 0.0), jnp.where(lo, 0.0, q)], axis=0).astype(BF16)
        kt_c, vt_c = kc_ref[b, cols, :].astype(BF16), vc_ref[b, cols, :].astype(BF16)
        k_n = jnp.concatenate([kn_ref[b, :, cols], pad], axis=0).astype(BF16)
        v_n = jnp.concatenate([vn_ref[b, :, cols], pad], axis=0).astype(BF16)
        s_c = jnp.where(w_c > 0, _dot(q2, kt_c), neg)
        s_n = jnp.where(w_n > 0, _dot_nt(q2, k_n), neg)
        mx = jnp.maximum(jnp.max(s_c, axis=-1, keepdims=True), jnp.max(s_n, axis=-1, keepdims=True))
        p_c = w_c * jnp.exp2(s_c - mx)
        p_n = w_n * jnp.exp2(s_n - mx)
        den = jnp.sum(p_c, axis=-1, keepdims=True) + jnp.sum(p_n, axis=-1, keepdims=True)
        o2 = (_dot_nt(p_c.astype(BF16), vt_c) + _dot(p_n.astype(BF16), v_n)) / den
        o_ref[b, :, cols] = jnp.where(lo, o2[:tq], o2[tq:])


def _dsa_step(q, k_new, v_new, k_cache, v_cache, layer, nb, t):
    wb = k_cache.shape[3]
    bblk = math.gcd(nb, DSA_STEP_SEQS)
    new = pl.BlockSpec((bblk, t, DSA_W), lambda b: (b, 0, 0))
    cache = pl.BlockSpec((None, bblk, DSA_W, wb), lambda b: (layer, b, 0, 0))
    r3 = lambda a: a.reshape(nb, t, DSA_W)
    return pl.pallas_call(
        _dsa_step_kernel,
        grid=(nb // bblk,),
        in_specs=[new, cache, cache, new, new],
        out_specs=new,
        out_shape=jax.ShapeDtypeStruct((nb, t, DSA_W), F32),
        compiler_params=_cparams("parallel"),
        name="dsa_step",
    )(r3(q), k_cache, v_cache, r3(k_new), r3(v_new)).reshape(nb * t, DSA_W)


RET_CHUNK = 256
RET_SEQS_PER_STEP = 2
RET_UNROLL = 4


def _ret_kernel(q_ref, k_ref, v_ref, g_ref, s0_ref, dec_ref, qd_ref, kd_ref, cd_ref, bd_ref, nw_ref,
                o_ref, so_ref, s_scr, *, chunk):
    tb = pl.program_id(1)
    ngrp = RET_W // LANES

    @pl.when(tb == 0)
    def _():
        s_scr[...] = s0_ref[...]

    lo = lax.broadcasted_iota(jnp.int32, (chunk, LANES), 1) < RET_HEAD_DIM
    nw = nw_ref[...]
    bd = bd_ref[...]

    def body(c, carry):
        rows = pl.ds(pl.multiple_of(c * chunk, chunk), chunk)
        for b in range(q_ref.shape[0]):
            for g in range(ngrp):
                cols = slice(g * LANES, (g + 1) * LANES)
                q, k, v = q_ref[b, rows, cols], k_ref[b, rows, cols], v_ref[b, rows, cols]
                kb, vb = k.astype(BF16), v.astype(BF16)
                parts = []
                for hh in range(2):
                    qm = jnp.where(lo if hh == 0 else jnp.logical_not(lo), q, 0.0).astype(BF16)
                    inner = _dot_nt(qm, kb) * dec_ref[2 * g + hh]
                    parts.append(_dot(inner.astype(BF16), vb))
                s_prev = s_scr[b, g]
                o = jnp.where(lo, parts[0], parts[1]) + _dot(q.astype(BF16), s_prev.astype(BF16)) * qd_ref[g]
                s_scr[b, g] = s_prev * cd_ref[g] + bd * _dot_tn((k * kd_ref[g]).astype(BF16), vb)
                o2 = o * o
                ms = jnp.where(lo, jnp.sum(jnp.where(lo, o2, 0.0), axis=-1, keepdims=True),
                               jnp.sum(jnp.where(lo, 0.0, o2), axis=-1, keepdims=True)) * (1.0 / RET_HEAD_DIM)
                o_ref[b, rows, cols] = o * lax.rsqrt(ms + NORM_EPS) * nw * _silu(g_ref[b, rows, cols])
        return carry

    nchunk = q_ref.shape[1] // chunk
    lax.fori_loop(0, nchunk, body, 0, unroll=min(RET_UNROLL, nchunk))

    @pl.when(tb == pl.num_programs(1) - 1)
    def _():
        so_ref[...] = s_scr[...]


def _ret_tables(chunk):
    log_gamma = jnp.log(1.0 - 2.0 ** (-5.0 - jnp.arange(RET_HEADS, dtype=F32)))
    i = jnp.arange(chunk, dtype=F32)
    diff = i[:, None] - i[None, :]
    causal = diff >= 0
    decay = jnp.where(causal[None], jnp.exp(log_gamma[:, None, None] * jnp.where(causal, diff, 0.0)[None]), 0.0)
    per_lane = lambda a: jnp.repeat(a, RET_HEAD_DIM, axis=0).reshape(RET_W // LANES, LANES, -1)
    q_dec = per_lane(jnp.exp(log_gamma[:, None] * (i[None, :] + 1.0))).transpose(0, 2, 1)
    k_dec = per_lane(jnp.exp(log_gamma[:, None] * (chunk - 1.0 - i)[None, :])).transpose(0, 2, 1)
    c_dec = jnp.broadcast_to(per_lane(jnp.exp(log_gamma * chunk)[:, None]), (RET_W // LANES, LANES, LANES))
    head_of = jnp.arange(LANES) // RET_HEAD_DIM
    block_diag = (head_of[:, None] == head_of[None, :]).astype(F32)
    return decay, q_dec, k_dec, c_dec, block_diag


def _retention(q, k, v, gate, state_bd, ret_norm, nb, t):
    chunk = RET_CHUNK if t % RET_CHUNK == 0 else t
    tblk = min(t, 1024)
    ngrp = RET_W // LANES
    bblk = _seqs_per_step(nb, t, RET_SEQS_PER_STEP)
    decay, q_dec, k_dec, c_dec, block_diag = _ret_tables(chunk)
    r3 = lambda a: a.reshape(nb, t, RET_W)
    tok = pl.BlockSpec((bblk, tblk, RET_W), lambda b, i: (b, i, 0))
    st = pl.BlockSpec((bblk, ngrp, LANES, LANES), lambda b, i: (b, 0, 0, 0))
    full = lambda a: pl.BlockSpec(a.shape, lambda b, i: (0,) * a.ndim)
    nw = jnp.tile(ret_norm, LANES // RET_HEAD_DIM).reshape(1, LANES)
    o, s_new = pl.pallas_call(
        functools.partial(_ret_kernel, chunk=chunk),
        grid=(nb // bblk, t // tblk),
        in_specs=[tok, tok, tok, tok, st, full(decay), full(q_dec), full(k_dec), full(c_dec),
                  full(block_diag), full(nw)],
        out_specs=[tok, st],
        out_shape=[jax.ShapeDtypeStruct((nb, t, RET_W), F32),
                   jax.ShapeDtypeStruct((nb, ngrp, LANES, LANES), F32)],
        scratch_shapes=[pltpu.VMEM((bblk, ngrp, LANES, LANES), F32)],
        compiler_params=_cparams("parallel", "arbitrary"),
        name="retention",
    )(r3(q), r3(k), r3(v), r3(gate), state_bd, decay, q_dec, k_dec, c_dec, block_diag, nw)
    return o.reshape(nb * t, RET_W), s_new


def _to_block_diag(s):
    nb = s.shape[0]
    s = s.reshape(nb, 2, 2, RET_HEAD_DIM, RET_HEAD_DIM)
    z = jnp.zeros_like(s[:, :, 0])
    top = jnp.concatenate([s[:, :, 0], z], axis=-1)
    bot = jnp.concatenate([z, s[:, :, 1]], axis=-1)
    return jnp.concatenate([top, bot], axis=-2)


def _from_block_diag(s):
    h = RET_HEAD_DIM
    return jnp.stack([s[:, :, :h, :h], s[:, :, h:, h:]], axis=2).reshape(s.shape[0], RET_HEADS, h, h)


GDN_PREP_CHUNKS_PER_ITER = 8
GDN_SCAN_SEQS_PER_STEP = 2
GDN_SCAN_UNROLL = 4


def _softplus(x):
    return jnp.maximum(x, 0.0) + jnp.log1p(jnp.exp(-jnp.abs(x)))


def _gdn_prep_kernel(cv_ref, hist_ref, ba_ref, cw_ref, ab_ref,
                     qg_ref, kd_ref, u_ref, w_ref, at_ref, el_ref, beta_scr, g_scr, *conv_scr, chunk, cpi):
    nseq, tblk = cv_ref.shape[0], cv_ref.shape[1]
    ncg = CONV_DIM // LANES
    ab = ab_ref[...]
    cs_scr = cv_ref
    for b in range(nseq):
        if conv_scr:
            xp_scr, cs_scr = conv_scr
            for cg in range(ncg):
                cols = slice(cg * LANES, (cg + 1) * LANES)
                xp_scr[b, cg, :SUBLANES, :] = hist_ref[b, :, cols]
                xp_scr[b, cg, SUBLANES:, :] = cv_ref[b, :, cols]
                cs_scr[b, :, cols] = _causal_conv_silu(xp_scr, b, cw_ref, cg, tblk)

        ba = ba_ref[b]
        beta_scr[b] = jax.nn.sigmoid(ba)
        g_scr[b] = -jnp.exp(ab[0:1, :]) * _softplus(ba + ab[1:2, :])

    grp = 2 if 2 * chunk == LANES else 1
    width = grp * chunk
    ri = lax.broadcasted_iota(jnp.int32, (chunk, width), 0)
    lane = lax.broadcasted_iota(jnp.int32, (chunk, width), 1)
    ci = lane % chunk
    first = lane < chunk
    incl = ri >= ci
    strict = ri > ci
    tri = (lax.broadcasted_iota(jnp.int32, (chunk, chunk), 0)
           >= lax.broadcasted_iota(jnp.int32, (chunk, chunk), 1)).astype(F32)
    lane_pick = (lax.broadcasted_iota(jnp.int32, (SUBLANES, LANES), 0)
                 == lax.broadcasted_iota(jnp.int32, (SUBLANES, LANES), 1)).astype(F32)
    base = min(SUBLANES, chunk)
    assert chunk % base == 0 and (chunk // base) & (chunk // base - 1) == 0
    same_blk = [(ri >> sh) == (ci >> sh) for sh in range(int(math.log2(base)), int(math.log2(chunk)) + 1)]

    def l2n(x):
        return x * lax.rsqrt(jnp.sum(x * x, axis=-1, keepdims=True) + NORM_EPS)

    def side_by_side(per_head):
        if grp == 1:
            return per_head[0][:, :width]
        return jnp.where(first, per_head[0][:, :width], per_head[1][:, :width])

    def block_diag(y):
        if grp == 1:
            return y.astype(BF16)
        return jnp.concatenate([jnp.where(first, y, 0.0), jnp.where(first, 0.0, y)], axis=0).astype(BF16)

    def block_rows(per_head):
        if grp == 1:
            return per_head[0].astype(BF16)
        a, b = per_head
        return jnp.concatenate([jnp.concatenate([a, jnp.zeros_like(b)], axis=1),
                                jnp.concatenate([jnp.zeros_like(a), b], axis=1)], axis=0).astype(BF16)

    def step(it, carry):
        probs = []
        for b, cc in [(b, cc) for b in range(nseq) for cc in range(cpi)]:
            c = it * cpi + cc
            rows = pl.ds(pl.multiple_of(c * chunk, chunk), chunk)
            beta_c = beta_scr[b, rows, :]
            gcol = _dot(tri, g_scr[b, rows, :], precision=HIGHEST)
            grow = _dot_nt(lane_pick, jnp.concatenate([gcol] * grp, axis=0), precision=HIGHEST)
            for h0 in range(0, GDN_HEADS, grp):
                hs = range(h0, h0 + grp)
                q = [l2n(cs_scr[b, rows, h * LANES:(h + 1) * LANES]) for h in hs]
                k = [l2n(cs_scr[b, rows, GDN_W + h * LANES:GDN_W + (h + 1) * LANES]) for h in hs]
                v = [cs_scr[b, rows, 2 * GDN_W + h * LANES:2 * GDN_W + (h + 1) * LANES] for h in hs]
                beta = [jnp.broadcast_to(beta_c[:, h:h + 1], (chunk, LANES)) for h in hs]
                gc = [jnp.broadcast_to(gcol[:, GDN_HEADS + h:GDN_HEADS + h + 1], (chunk, LANES)) for h in hs]
                g_row = side_by_side([jnp.broadcast_to(grow[GDN_HEADS + h:GDN_HEADS + h + 1, :], (chunk, width))
                                      for h in hs])
                gdiff = side_by_side(gc) - g_row
                dmask = jnp.where(incl, jnp.exp(jnp.where(incl, gdiff, 0.0)), 0.0)
                kb = [k[i] * beta[i] for i in range(grp)]
                k_rows = block_rows(k)
                lower = jnp.where(strict, _dot_nt(jnp.concatenate(kb, axis=1).astype(BF16), k_rows) * dmask, 0.0)
                npow = jnp.where(same_blk[0], -lower, 0.0)
                probs.append(dict(b=b, c=c, rows=rows, h0=h0, q=q, k=k, v=v, beta=beta, gc=gc, dmask=dmask,
                                  kb=kb, k_rows=k_rows, lower=lower, npow=npow, qmat=npow))
        for _ in range(int(math.log2(base)) - 1):
            for p in probs:
                p["npow"] = _dot(p["npow"].astype(BF16), block_diag(p["npow"]))
            for p in probs:
                p["qmat"] = p["qmat"] + p["npow"] + _dot(p["qmat"].astype(BF16), block_diag(p["npow"]))
        for lvl in range(1, len(same_blk)):
            for p in probs:
                cb = jnp.where(jnp.logical_and(same_blk[lvl], jnp.logical_not(same_blk[lvl - 1])),
                               p["lower"], 0.0)
                p["x"] = cb + _dot(p["qmat"].astype(BF16), block_diag(cb))
            for p in probs:
                p["qmat"] = p["qmat"] - p["x"] - _dot(p["x"].astype(BF16), block_diag(p["qmat"]))
        for p in probs:
            b, rows, h0, gc = p["b"], p["rows"], p["h0"], p["gc"]
            cols = slice(h0 * LANES, (h0 + grp) * LANES)
            idx = range(grp)
            eg = [jnp.exp(gc[i]) for i in idx]
            rhs_u = [p["v"][i] * p["beta"][i] for i in idx]
            rhs_w = [p["kb"][i] * eg[i] for i in idx]
            qb = p["qmat"].astype(BF16)
            qs = [p["q"][i] * (GDN_HEAD_DIM ** -0.5) for i in idx]
            g_last = [gc[i][chunk - 1:chunk, :] for i in idx]
            cat = lambda xs: jnp.concatenate(xs, axis=1)
            qg_ref[b, rows, cols] = cat([qs[i] * eg[i] for i in idx]).astype(qg_ref.dtype)
            kd_ref[b, rows, cols] = cat([p["k"][i] * jnp.exp(g_last[i] - gc[i]) for i in idx]).astype(kd_ref.dtype)
            u_ref[b, rows, cols] = cat(rhs_u) + _dot(qb, block_rows(rhs_u))
            w_ref[b, rows, cols] = (cat(rhs_w) + _dot(qb, block_rows(rhs_w))).astype(w_ref.dtype)
            at_ref[b, rows, h0 * chunk:(h0 + grp) * chunk] = (
                _dot_nt(cat(qs).astype(BF16), p["k_rows"]) * p["dmask"]).astype(at_ref.dtype)
            el_ref[b, pl.ds(pl.multiple_of(p["c"] * SUBLANES, SUBLANES), SUBLANES), cols] = cat(
                [jnp.broadcast_to(jnp.exp(g_last[i]), (SUBLANES, LANES)) for i in idx])
        return carry

    lax.fori_loop(0, tblk // (chunk * cpi), step, 0)


def _gdn_scan_kernel(qg_ref, kd_ref, u_ref, w_ref, at_ref, el_ref, z_ref, s0_ref, nw_ref,
                     o_ref, so_ref, s_scr, *, chunk):
    tb = pl.program_id(1)

    @pl.when(tb == 0)
    def _():
        s_scr[...] = s0_ref[...]

    nw = nw_ref[...]
    probs = [(b, h) for b in range(qg_ref.shape[0]) for h in range(GDN_HEADS)]

    def step(c, carry):
        rows = pl.ds(pl.multiple_of(c * chunk, chunk), chunk)
        erow = pl.ds(pl.multiple_of(c * SUBLANES, SUBLANES), SUBLANES)
        cols = lambda h: slice(h * LANES, (h + 1) * LANES)
        s_prev = [s_scr[b, h] for b, h in probs]
        sb = [s.astype(BF16) for s in s_prev]
        v_new = [u_ref[b, rows, cols(h)] - _dot(w_ref[b, rows, cols(h)].astype(BF16), sb[i])
                 for i, (b, h) in enumerate(probs)]
        vb = [v.astype(BF16) for v in v_new]
        for i, (b, h) in enumerate(probs):
            el = el_ref[b, erow, cols(h)][0:1, :]
            s_scr[b, h] = s_prev[i] * el + _dot_tn(kd_ref[b, rows, cols(h)].astype(BF16), vb[i])
        for i, (b, h) in enumerate(probs):
            attn = at_ref[b, rows, h * chunk:(h + 1) * chunk].astype(BF16)
            o = _dot(qg_ref[b, rows, cols(h)].astype(BF16), sb[i]) + _dot(attn, vb[i])
            ms = jnp.mean(o * o, axis=-1, keepdims=True)
            o_ref[b, rows, cols(h)] = o * lax.rsqrt(ms + NORM_EPS) * nw * _silu(z_ref[b, rows, cols(h)])
        return carry

    nchunk = qg_ref.shape[1] // chunk
    lax.fori_loop(0, nchunk, step, 0, unroll=min(GDN_SCAN_UNROLL, nchunk))

    @pl.when(tb == pl.num_programs(1) - 1)
    def _():
        so_ref[...] = s_scr[...]


def _gated_delta(cv, conv_done, z, ba, conv_hist, states, layer, conv_w, a_log, dt_bias, gdn_norm, nb, t):
    chunk = CHUNK if t % CHUNK == 0 else t
    tblk = min(t, 512)
    assert conv_done or tblk == t
    nchunk_blk = tblk // chunk
    hist_pad, cw_pad = _conv_operands(conv_hist, conv_w)
    ab = jnp.zeros((SUBLANES, LANES), F32)
    ab = ab.at[0, GDN_HEADS:2 * GDN_HEADS].set(a_log).at[1, GDN_HEADS:2 * GDN_HEADS].set(dt_bias)
    cv3 = cv.reshape(nb, t, CONV_DIM)
    pblk = _seqs_per_step(nb, t, 2)
    tok = lambda w: pl.BlockSpec((pblk, tblk, w), lambda b, i: (b, i, 0))
    full = lambda a: pl.BlockSpec(a.shape, lambda b, i: (0,) * a.ndim)
    el_spec =pl.BlockSpec((pblk, nchunk_blk * SUBLANES, GDN_W), lambda b, i: (b, i, 0))
    tok_shape = lambda w, dt=F32: jax.ShapeDtypeStruct((nb, t, w), dt)
    el_shape = jax.ShapeDtypeStruct((nb, (t // chunk) * SUBLANES, GDN_W), F32)
    opd = BF16 if chunk % (2 * SUBLANES) == 0 else F32
    qg, kd, u, w, attn, el = pl.pallas_call(
        functools.partial(_gdn_prep_kernel, chunk=chunk, cpi=min(GDN_PREP_CHUNKS_PER_ITER, nchunk_blk)),
        grid=(nb // pblk, t // tblk),
        in_specs=[tok(CONV_DIM), pl.BlockSpec((pblk, SUBLANES, CONV_DIM), lambda b, i: (b, 0, 0)),
                  tok(LANES), full(cw_pad), full(ab)],
        out_specs=[tok(GDN_W), tok(GDN_W), tok(GDN_W), tok(GDN_W), tok(GDN_HEADS * chunk), el_spec],
        out_shape=[tok_shape(GDN_W, opd), tok_shape(GDN_W, opd), tok_shape(GDN_W), tok_shape(GDN_W, opd),
                   tok_shape(GDN_HEADS * chunk, opd), el_shape],
        scratch_shapes=[pltpu.VMEM((pblk, tblk, LANES), F32), pltpu.VMEM((pblk, tblk, LANES), F32)] + (
            [] if conv_done else [pltpu.VMEM((pblk, CONV_DIM // LANES, tblk + SUBLANES, LANES), F32),
                                  pltpu.VMEM((pblk, tblk, CONV_DIM), F32)]),
        compiler_params=_cparams("parallel", "parallel"),
        name="gdn_prep",
    )(cv3, hist_pad, ba.reshape(nb, t, LANES), cw_pad, ab)

    bblk = _seqs_per_step(nb, t, GDN_SCAN_SEQS_PER_STEP)
    stok = lambda w: pl.BlockSpec((bblk, tblk, w), lambda b, i: (b, i, 0))
    sel_spec = pl.BlockSpec((bblk, nchunk_blk * SUBLANES, GDN_W), lambda b, i: (b, i, 0))
    st_in = pl.BlockSpec((None, bblk, GDN_HEADS, LANES, LANES), lambda b, i: (layer, b, 0, 0, 0))
    st = pl.BlockSpec((bblk, GDN_HEADS, LANES, LANES), lambda b, i: (b, 0, 0, 0))
    nw = gdn_norm.reshape(1, LANES)
    o, s_new = pl.pallas_call(
        functools.partial(_gdn_scan_kernel, chunk=chunk),
        grid=(nb // bblk, t // tblk),
        in_specs=[stok(GDN_W), stok(GDN_W), stok(GDN_W), stok(GDN_W), stok(GDN_HEADS * chunk), sel_spec,
                  stok(GDN_W), st_in, full(nw)],
        out_specs=[stok(GDN_W), st],
        out_shape=[tok_shape(GDN_W), jax.ShapeDtypeStruct((nb, GDN_HEADS, LANES, LANES), F32)],
        scratch_shapes=[pltpu.VMEM((bblk, GDN_HEADS, LANES, LANES), F32)],
        compiler_params=_cparams("parallel", "arbitrary"),
        name="gdn_scan",
    )(qg, kd, u, w, attn, el, z.reshape(nb, t, GDN_W), states, nw)
    return o.reshape(nb * t, GDN_W), s_new


FFN_TILE = 256


def _out_ffn_kernel(x_ref, oa_ref, orr_ref, oc_ref, g1_ref, sh2_ref, sc2_ref, g2_ref, nw_ref, wo_ref, wg_ref,
                    wu_ref, wd_ref, fn_ref, out_ref, *, final):
    mix = (_dot(oa_ref[...].astype(BF16), wo_ref[0:DSA_W, :])
           + _dot(orr_ref[...].astype(BF16), wo_ref[DSA_W:DSA_W + RET_W, :])
           + _dot(oc_ref[...].astype(BF16), wo_ref[DSA_W + RET_W:, :]))
    x1 = x_ref[...] + g1_ref[...] * mix
    hb = _rms_mod(x1, nw_ref[...], sc2_ref[...], sh2_ref[...]).astype(BF16)
    acc = None
    for j in range(wg_ref.shape[1] // FFN_TILE):
        cols = slice(j * FFN_TILE, (j + 1) * FFN_TILE)
        act = (_silu(_dot(hb, wg_ref[:, cols])) * _dot(hb, wu_ref[:, cols])).astype(BF16)
        down = _dot(act, wd_ref[cols, :])
        acc = down if acc is None else acc + down
    x2 = x1 + g2_ref[...] * acc
    if final:
        ms = jnp.mean(x2 * x2, axis=-1, keepdims=True)
        x2 = x2 * lax.rsqrt(ms + NORM_EPS) * fn_ref[...]
    out_ref[...] = x2


def _out_ffn(x2, t, oa, orr, oc, mod3, norm_w, w_out_b, wg_b, wu_b, wd_b, layer, final_norm, final, tm):
    m, d = x2.shape
    row = lambda w: pl.BlockSpec((tm, w), lambda i: (i, 0))
    const = lambda a: pl.BlockSpec(a.shape, lambda i: (0,) * a.ndim, pipeline_mode=pl.Buffered(1))
    wspec = lambda a: pl.BlockSpec((None,) + a.shape[1:], lambda i: (layer,) + (0,) * (a.ndim - 1),
                                   pipeline_mode=pl.Buffered(1))
    nw = norm_w.reshape(1, d)
    fn = final_norm.reshape(1, d)
    return pl.pallas_call(
        functools.partial(_out_ffn_kernel, final=final),
        grid=(m // tm,),
        in_specs=[row(d), row(DSA_W), row(RET_W), row(GDN_W),
                  _mod_spec(tm, t, d, 2), _mod_spec(tm, t, d, 3), _mod_spec(tm, t, d, 4), _mod_spec(tm, t, d, 5),
                  const(nw), wspec(w_out_b), wspec(wg_b), wspec(wu_b), wspec(wd_b), const(fn)],
        out_specs=row(d),
        out_shape=jax.ShapeDtypeStruct((m, d), F32),
        compiler_params=_cparams("parallel"),
        name="out_ffn",
    )(x2, oa, orr, oc, mod3, mod3, mod3, mod3, nw, w_out_b, wg_b, wu_b, wd_b, fn)


ROPE_SPLIT = 64


def _rope_tables(pos0, t, inv_freq):
    reps = LANES // (2 * inv_freq.shape[0])
    inv_lane = jnp.tile(jnp.concatenate([inv_freq, inv_freq]), reps)[None, :]
    sign_lane = jnp.tile(jnp.concatenate([-jnp.ones_like(inv_freq), jnp.ones_like(inv_freq)]), reps)[None, :]
    if t % ROPE_SPLIT:
        ang = (pos0 + jnp.arange(t, dtype=jnp.int32)).astype(F32)[:, None] * inv_lane
        return jnp.cos(ang), jnp.sin(ang) * sign_lane
    coarse = (pos0 + ROPE_SPLIT * jnp.arange(t // ROPE_SPLIT, dtype=jnp.int32)).astype(F32)[:, None] * inv_lane
    fine = jnp.arange(ROPE_SPLIT, dtype=jnp.int32).astype(F32)[:, None] * inv_lane
    cc, sc = jnp.cos(coarse)[:, None, :], jnp.sin(coarse)[:, None, :]
    cf, sf = jnp.cos(fine)[None], jnp.sin(fine)[None]
    cos = (cc * cf - sc * sf).reshape(t, LANES)
    sin = (sc * cf + cc * sf).reshape(t, LANES)
    return cos, sin * sign_lane


def _trunk(x, modp, pos0, k_hist, v_hist, s_ret, s_gdn, conv_hist, wts):
    (norm_mix, norm_ffn, w_in_b, ret_norm, conv_w, a_log, dt_bias, gdn_norm, w_out_b, wg_b, wu_b, wd_b,
     final_norm) = wts
    nb, t, d = x.shape
    m = nb * t
    depth = w_in_b.shape[0]
    tm = min(TOKEN_TILE, m)
    inv_a = 1.0 / (ROPE_THETA ** (jnp.arange(0, DSA_HEAD_DIM, 2, dtype=F32) / DSA_HEAD_DIM))
    inv_r = 1.0 / (10000.0 ** jnp.linspace(0.0, 1.0, RET_HEAD_DIM // 2, dtype=F32))
    tabs = _rope_tables(pos0, t, inv_a) + _rope_tables(pos0, t, inv_r)
    if t < tm:
        tabs = tuple(jnp.tile(a, (tm // t, 1)) for a in tabs)
    x2 = x.reshape(m, d)
    ks, vs, rs, gs, cs = [], [], [], [], []
    for l in range(depth):
        if t >= tm:
            mod3 = modp[l].reshape(nb, 1, 6 * d)
        else:
            mod3 = jnp.repeat(modp[l], t, axis=0).reshape(m // tm, tm, 6 * d)
        (qa, ka, va, qr, kr, vr, gr, cv, z, ba), fused = _inproj(
            x2, t, mod3, norm_mix[l], w_in_b, l, tabs, tm, conv_hist[l], conv_w[l])
        nhist = CONV_WIDTH - 1
        keep = min(t, DSA_MAX_WINDOW)
        if fused is None:
            cvn = jnp.concatenate([conv_hist[l], cv.reshape(nb, t, CONV_DIM)], axis=1)[:, -nhist:]
            window = lambda a: a.reshape(nb, t, DSA_W)[:, t - keep:].reshape(nb, keep, DSA_HEADS, DSA_HEAD_DIM)
            kwin, vwin = window(ka), window(va)
        else:
            conv_tail, k_t, v_t = fused
            cvn = conv_tail[:, SUBLANES - nhist:]
            untransposed = lambda a: a.reshape(nb, DSA_HEADS, DSA_HEAD_DIM, keep).transpose(0, 3, 1, 2)
            kwin, vwin = untransposed(k_t), untransposed(v_t)
        if k_hist is None:
            oa = _dsa_prompt(qa, ka, va, nb, t)
        else:
            oa = _dsa_step(qa, ka, va, k_hist, v_hist, l, nb, t)
        orr, sr = _retention(qr, kr, vr, gr, _to_block_diag(s_ret[l]), ret_norm[l], nb, t)
        oc, sg = _gated_delta(cv, fused is not None, z, ba, conv_hist[l], s_gdn, l, conv_w[l], a_log[l],
                              dt_bias[l], gdn_norm[l], nb, t)
        x2 = _out_ffn(x2, t, oa, orr, oc, mod3, norm_ffn[l], w_out_b, wg_b, wu_b, wd_b, l,
                      final_norm, l == depth - 1, tm)
        ks.append(kwin)
        vs.append(vwin)
        rs.append(_from_block_diag(sr))
        gs.append(sg)
        cs.append(cvn)
    return (x2.reshape(nb, t, d), jnp.stack(ks), jnp.stack(vs), jnp.stack(rs), jnp.stack(gs), jnp.stack(cs))


def kernel(x_prompt, x_sample, cache_win_k, cache_win_v, state_ret, state_gdn, state_conv, c_prompt, c_sample, ada_w, ada_b, norm_mix, norm_ffn, w_in, ret_norm, conv_w, a_log, dt_bias, gdn_norm, w_out, w_gate, w_up, w_down, final_norm):
    nb, t_p, d = x_prompt.shape
    db, t_s, _ = x_sample.shape
    depth = ada_w.shape[0]
    rows = nb + db
    rows_pad = -(-rows // SUBLANES) * SUBLANES
    c_all = jnp.concatenate([c_prompt, c_sample, jnp.zeros((rows_pad - rows, d), F32)], axis=0)
    mod = _modulation(c_all, ada_w, ada_b)
    w_in_b = jnp.pad(w_in, ((0, 0), (0, 0), (0, IN_COLS_PAD - IN_COLS))).astype(BF16)
    wts = (norm_mix, norm_ffn, w_in_b, ret_norm, conv_w, a_log, dt_bias, gdn_norm, w_out.astype(BF16),
           w_gate.astype(BF16), w_up.astype(BF16), w_down.astype(BF16), final_norm)

    zr = jnp.zeros((depth, nb, RET_HEADS, RET_HEAD_DIM, RET_HEAD_DIM), F32)
    zg = jnp.zeros((depth, nb, GDN_HEADS, GDN_HEAD_DIM, GDN_HEAD_DIM), F32)
    zc = jnp.zeros((depth, nb, CONV_WIDTH - 1, CONV_DIM), F32)
    y_p, kp, vp, rp, gp, cp = _trunk(x_prompt, mod[:, :nb], 0,
                                     None, None, zr, zg, zc, wts)
    wb = cache_win_k.shape[2]
    y_s, ks, vs, rs, gs, cs = _trunk(x_sample, mod[:, nb:rows], PAST_LEN,
                                     cache_win_k.reshape(depth, db, wb, DSA_W).transpose(0, 1, 3, 2),
                                     cache_win_v.reshape(depth, db, wb, DSA_W).transpose(0, 1, 3, 2),
                                     state_ret, state_gdn, state_conv, wts)
    return (y_p, y_s, kp, vp, rp, gp, cp, ks, vs, rs, gs, cs)
```

```python
import functools
import math

import jax
import jax.numpy as jnp
from jax import lax
from jax.experimental import pallas as pl
from jax.experimental.pallas import tpu as pltpu

F32 = jnp.float32
BF16 = jnp.bfloat16
HIGHEST = lax.Precision.HIGHEST

DSA_HEAD_DIM = 64
DSA_HEADS = 4
DSA_PATTERNS = ((128, 1), (512, 4), (2048, 16))
DSA_MAX_WINDOW = 2048
ROPE_THETA = 10000.0
RET_HEAD_DIM = 64
RET_HEADS = 4
GDN_HEAD_DIM = 128
GDN_HEADS = 4
CONV_WIDTH = 4
CHUNK = 64
NORM_EPS = 1e-6
PAST_LEN = 16384
LOG2_E = math.log2(math.e)

DSA_W = DSA_HEADS * DSA_HEAD_DIM
RET_W = RET_HEADS * RET_HEAD_DIM
GDN_W = GDN_HEADS * GDN_HEAD_DIM
CONV_DIM = 3 * GDN_W
LANES = 128
SUBLANES = 8
VMEM_LIMIT = 56 * 1024 * 1024
TOKEN_TILE = 512

_C_QA, _C_KA, _C_VA = 0, DSA_W, 2 * DSA_W
_C_QR = 3 * DSA_W
_C_KR, _C_VR, _C_GR = _C_QR + RET_W, _C_QR + 2 * RET_W, _C_QR + 3 * RET_W
_C_CV = _C_QR + 4 * RET_W
_C_Z = _C_CV + CONV_DIM
_C_BA = _C_Z + GDN_W
IN_COLS = _C_BA + 2 * GDN_HEADS
IN_COLS_PAD = _C_BA + LANES


def _cparams(*sem):
    return pltpu.CompilerParams(dimension_semantics=sem, vmem_limit_bytes=VMEM_LIMIT)


def _dot(a, b, **kw):
    return jnp.dot(a, b, preferred_element_type=F32, **kw)


def _dot_nt(a, b, **kw):
    return lax.dot_general(a, b, (((1,), (1,)), ((), ())), preferred_element_type=F32, **kw)


def _dot_tn(a, b, **kw):
    return lax.dot_general(a, b, (((0,), (0,)), ((), ())), preferred_element_type=F32, **kw)


def _silu(x):
    return x * jax.nn.sigmoid(x)


def _mixer_out_dtype(t):
    return BF16 if t % (2 * SUBLANES) == 0 else F32


def _seqs_per_step(nb, t, base):
    return math.gcd(nb, base * (4 if t < CHUNK else 1))


def _mod_kernel(c_ref, w_ref, b_ref, o_ref):
    a = _silu(c_ref[...]).astype(BF16)
    o_ref[...] = _dot(a, w_ref[...].astype(BF16)) + b_ref[...]


def _modulation(c_all, ada_w, ada_b, tn=1536):
    depth, d, n = ada_w.shape
    bp = c_all.shape[0]
    return pl.pallas_call(
        _mod_kernel,
        grid=(depth, n // tn),
        in_specs=[
            pl.BlockSpec((bp, d), lambda l, j: (0, 0)),
            pl.BlockSpec((None, d, tn), lambda l, j: (l, 0, j)),
            pl.BlockSpec((None, 1, tn), lambda l, j: (l, 0, j)),
        ],
        out_specs=pl.BlockSpec((None, bp, tn), lambda l, j: (l, 0, j)),
        out_shape=jax.ShapeDtypeStruct((depth, bp, n), F32),
        compiler_params=_cparams("parallel", "parallel"),
        name="modulation",
    )(c_all, ada_w, ada_b.reshape(depth, 1, n))


def _rms_mod(x, nw, sc, sh):
    ms = jnp.mean(x * x, axis=-1, keepdims=True)
    return (x * lax.rsqrt(ms + NORM_EPS) * nw) * (1.0 + sc) + sh


def _causal_conv_silu(xp_scr, seq, cw_ref, cg, nrows):
    cols = slice(cg * LANES, (cg + 1) * LANES)
    acc = None
    for i in range(CONV_WIDTH):
        start = SUBLANES - (CONV_WIDTH - 1) + i
        term = xp_scr[seq, cg, pl.ds(start, nrows, stride=1), :] * cw_ref[i:i + 1, cols]
        acc = term if acc is None else acc + term
    return _silu(acc)


def _inproj_kernel(x_ref, nw_ref, sh_ref, sc_ref, w_ref, ca_ref, sa_ref, cr_ref, sr_ref, *rest, tiles_per_seq):
    if tiles_per_seq:
        hist_ref, cw_ref = rest[:2]
        rest = rest[2:]
    qa_ref, ka_ref, va_ref, qr_ref, kr_ref, vr_ref, gr_ref, cv_ref, z_ref, ba_ref = rest[:10]
    tm = x_ref.shape[0]
    if tiles_per_seq:
        tail_ref, kt_ref, vt_ref, xp_scr = rest[10:]

        @pl.when(pl.program_id(0) % tiles_per_seq == 0)
        def _():
            tail_ref[...] = hist_ref[...]

    hb = _rms_mod(x_ref[...], nw_ref[...], sc_ref[...], sh_ref[...]).astype(BF16)

    def proj(c0, width):
        return _dot(hb, w_ref[:, c0:c0 + width])

    lane = lax.broadcasted_iota(jnp.int32, (tm, DSA_W), 1)
    first_half = (lane % DSA_HEAD_DIM) < (DSA_HEAD_DIM // 2)

    def rope(y, cos, sin_signed):
        partner = jnp.where(first_half, pltpu.roll(y, DSA_W - DSA_HEAD_DIM // 2, 1),
                            pltpu.roll(y, DSA_HEAD_DIM // 2, 1))
        return y * cos + partner * sin_signed

    wide = lambda ref: jnp.concatenate([ref[...]] * (DSA_W // LANES), axis=1)
    ca, sa, cr, sr = wide(ca_ref), wide(sa_ref), wide(cr_ref), wide(sr_ref)
    qa_ref[...] = rope(proj(_C_QA, DSA_W), ca, sa) * (DSA_HEAD_DIM ** -0.5 * LOG2_E)
    ka = rope(proj(_C_KA, DSA_W), ca, sa)
    va = proj(_C_VA, DSA_W)
    ka_ref[...] = ka
    va_ref[...] = va
    if tiles_per_seq:
        kt_ref[...] = ka.T
        vt_ref[...] = va.T

    qr_ref[...] = rope(proj(_C_QR, RET_W), cr, sr)
    kr_ref[...] = rope(proj(_C_KR, RET_W), cr, sr) * (RET_HEAD_DIM ** -0.5)
    vr_ref[...] = proj(_C_VR, RET_W)
    gr_ref[...] = proj(_C_GR, RET_W)
    for s in range(CONV_DIM // GDN_W):
        y = proj(_C_CV + s * GDN_W, GDN_W)
        if not tiles_per_seq:
            cv_ref[:, s * GDN_W:(s + 1) * GDN_W] = y
            continue
        for j in range(GDN_W // LANES):
            cg = s * (GDN_W // LANES) + j
            cols = slice(cg * LANES, (cg + 1) * LANES)
            yj = y[:, j * LANES:(j + 1) * LANES]
            xp_scr[0, cg, :SUBLANES, :] = tail_ref[:, cols]
            xp_scr[0, cg, SUBLANES:, :] = yj
            cv_ref[:, cols] = _causal_conv_silu(xp_scr, 0, cw_ref, cg, tm)
            tail_ref[:, cols] = yj[tm - SUBLANES:, :]
    z_ref[...] = proj(_C_Z, GDN_W)
    ba_ref[...] = proj(_C_BA, LANES)


def _mod_spec(tm, t, d, col):
    if t >= tm:
        return pl.BlockSpec((None, 1, d), lambda i: ((i * tm) // t, 0, col))
    return pl.BlockSpec((None, tm, d), lambda i: (i, 0, col))


def _conv_operands(conv_hist, conv_w):
    nb = conv_hist.shape[0]
    hist_pad = jnp.concatenate([jnp.zeros((nb, SUBLANES - (CONV_WIDTH - 1), CONV_DIM), F32), conv_hist], axis=1)
    cw_pad = jnp.concatenate([conv_w, jnp.zeros((SUBLANES - CONV_WIDTH, CONV_DIM), F32)], axis=0)
    return hist_pad, cw_pad


def _inproj(x2, t, mod3, norm_w, w_in_b, layer, tabs, tm, conv_hist, conv_w):
    m, d = x2.shape
    nt = tabs[0].shape[0] // tm
    widths = (DSA_W,) * 3 + (RET_W,) * 4 + (CONV_DIM, GDN_W, LANES)
    tab_spec = pl.BlockSpec((tm, LANES), lambda i: (i % nt, 0))
    tiles_per_seq = t // tm if t % tm == 0 else 0
    window_tiles = min(t, DSA_MAX_WINDOW) // tm
    in_specs = [
        pl.BlockSpec((tm, d), lambda i: (i, 0)),
        pl.BlockSpec((1, d), lambda i: (0, 0)),
        _mod_spec(tm, t, d, 0),
        _mod_spec(tm, t, d, 1),
        pl.BlockSpec((None, d, IN_COLS_PAD), lambda i: (layer, 0, 0), pipeline_mode=pl.Buffered(1)),
        tab_spec, tab_spec, tab_spec, tab_spec,
    ]
    operands = [x2, norm_w.reshape(1, d), mod3, mod3, w_in_b, *tabs]
    out_specs = [pl.BlockSpec((tm, w), lambda i: (i, 0)) for w in widths]
    out_shape = [jax.ShapeDtypeStruct((m, w), F32) for w in widths]
    scratch = []
    if tiles_per_seq:
        hist_pad, cw_pad = _conv_operands(conv_hist, conv_w)
        seq_tail = pl.BlockSpec((None, SUBLANES, CONV_DIM), lambda i: (i // tiles_per_seq, 0, 0))
        in_specs += [seq_tail, pl.BlockSpec(cw_pad.shape, lambda i: (0, 0))]
        operands += [hist_pad, cw_pad]
        out_specs.append(seq_tail)
        out_shape.append(jax.ShapeDtypeStruct((m // t, SUBLANES, CONV_DIM), F32))
        first_win = tiles_per_seq - window_tiles
        win_t = pl.BlockSpec((None, DSA_W, tm), lambda i: (i // tiles_per_seq, 0,
                                                            jnp.maximum(i % tiles_per_seq - first_win, 0)))
        out_specs += [win_t, win_t]
        out_shape += [jax.ShapeDtypeStruct((m // t, DSA_W, window_tiles * tm), F32)] * 2
        scratch.append(pltpu.VMEM((1, CONV_DIM // LANES, tm + SUBLANES, LANES), F32))
    outs = pl.pallas_call(
        functools.partial(_inproj_kernel, tiles_per_seq=tiles_per_seq),
        grid=(m // tm,),
        in_specs=in_specs,
        out_specs=out_specs,
        out_shape=out_shape,
        scratch_shapes=scratch,
        compiler_params=_cparams("arbitrary"),
        name="inproj",
    )(*operands)
    return (outs[:10], outs[10:]) if tiles_per_seq else (outs, None)


DSA_BLK = 128


DSA_QBLK = DSA_MAX_WINDOW
DSA_GROUP = 4


def _dsa_kernel(q_ref, kp_ref, kc_ref, vp_ref, vc_ref, o_ref, acc_scr, m_scr, l_scr):
    blk = pl.program_id(2)
    qblk = q_ref.shape[0]
    row = lax.broadcasted_iota(jnp.int32, (2 * DSA_BLK, 2 * DSA_BLK), 0) % DSA_BLK
    col = lax.broadcasted_iota(jnp.int32, (2 * DSA_BLK, 2 * DSA_BLK), 1)
    ok = jnp.logical_and(col >= row, col <= row + DSA_BLK)
    ok_first = jnp.logical_and(ok, jnp.logical_or(col >= DSA_BLK, blk > 0))
    lo = lax.broadcasted_iota(jnp.int32, (DSA_BLK, LANES), 1) < DSA_HEAD_DIM
    neg = -jnp.inf
    halves = lambda x: jnp.where(lo, x[:DSA_BLK], x[DSA_BLK:])

    def rows_of(dil, start, n):
        return pl.ds(start, n) if dil == 1 else pl.ds(start, n, stride=dil)

    def group_softmax(dil, subs):
        n = range(len(subs))
        idx = [rows_of(dil, ph + dil * DSA_BLK * j, DSA_BLK) for ph, j in subs]
        q = [q_ref[i, :] for i in idx]
        k, v = [], []
        for u, (ph, j) in enumerate(subs):
            if j == 0:
                band = rows_of(dil, qblk - dil * DSA_BLK + ph, DSA_BLK)
                k.append(jnp.concatenate([kp_ref[band, :], kc_ref[idx[u], :]], axis=0).astype(BF16))
                v.append(jnp.concatenate([vp_ref[band, :], vc_ref[idx[u], :]], axis=0).astype(BF16))
            else:
                both = rows_of(dil, ph + dil * DSA_BLK * (j - 1), 2 * DSA_BLK)
                k.append(kc_ref[both, :].astype(BF16))
                v.append(vc_ref[both, :].astype(BF16))
        q2 = [jnp.concatenate([jnp.where(lo, q[u], 0.0), jnp.where(lo, 0.0, q[u])], axis=0).astype(BF16)
              for u in n]
        s = [jnp.where(ok_first if subs[u][1] == 0 else ok, _dot_nt(q2[u], k[u]), neg) for u in n]
        mx = [jnp.max(s[u], axis=-1, keepdims=True) for u in n]
        p = [jnp.exp2(s[u] - mx[u]) for u in n]
        den = [jnp.sum(p[u], axis=-1, keepdims=True) for u in n]
        pv = [_dot(p[u].astype(BF16), v[u]) for u in n]
        return idx, [(halves(pv[u]), halves(jnp.broadcast_to(mx[u], (2 * DSA_BLK, LANES))),
                      halves(jnp.broadcast_to(den[u], (2 * DSA_BLK, LANES)))) for u in n]

    dils = sorted((d for _, d in DSA_PATTERNS), reverse=True)
    for pi, dil in enumerate(dils):
        subs = [(ph, j) for j in range(qblk // (dil * DSA_BLK)) for ph in range(dil)]
        for g0 in range(0, len(subs), DSA_GROUP):
            idx, tiles = group_softmax(dil, subs[g0:g0 + DSA_GROUP])
            for i, (pv, mx, den) in zip(idx, tiles):
                if pi > 0:
                    m_old = m_scr[i, :]
                    m_new = jnp.maximum(m_old, mx)
                    w_old = jnp.exp2(m_old - m_new)
                    w_cur = jnp.exp2(mx - m_new)
                    pv = acc_scr[i, :] * w_old + pv * w_cur
                    den = l_scr[i, :] * w_old + den * w_cur
                    mx = m_new
                if pi < len(dils) - 1:
                    acc_scr[i, :] = pv
                    m_scr[i, :] = mx
                    l_scr[i, :] = den
                else:
                    o_ref[i, :] = (pv / den).astype(o_ref.dtype)


def _dsa_prompt(q, k, v, nb, t):
    assert all(w // d == DSA_BLK for w, d in DSA_PATTERNS) and t % DSA_QBLK == 0
    ngrp = DSA_W // LANES
    r3 = lambda a: a.reshape(nb, t, DSA_W)
    cur = pl.BlockSpec((None, DSA_QBLK, LANES), lambda b, g, i: (b, i, g))
    prv = pl.BlockSpec((None, DSA_QBLK, LANES), lambda b, g, i: (b, jnp.maximum(i - 1, 0), g))
    return pl.pallas_call(
        _dsa_kernel,
        grid=(nb, ngrp, t // DSA_QBLK),
        in_specs=[cur, prv, cur, prv, cur],
        out_specs=cur,
        out_shape=jax.ShapeDtypeStruct((nb, t, DSA_W), _mixer_out_dtype(t)),
        scratch_shapes=[pltpu.VMEM((DSA_QBLK, LANES), F32)] * 3,
        compiler_params=_cparams("parallel", "parallel", "parallel"),
        name="dsa_prompt",
    )(r3(q), r3(k), r3(k), r3(v), r3(v)).reshape(nb * t, DSA_W)


DSA_STEP_SEQS = 2


def _multiplicity(dist):
    total = jnp.zeros(dist.shape, F32)
    for window, dil in DSA_PATTERNS:
        hit = (dist >= 0) & (dist <= window) & ((dist & (dil - 1)) == 0)
        total = total + hit.astype(F32)
    return total


def _dsa_step_kernel(q_ref, kc_ref, vc_ref, kn_ref, vn_ref, o_ref):
    nseq, tq = q_ref.shape[0], q_ref.shape[1]
    wb = kc_ref.shape[2]
    qi = lax.broadcasted_iota(jnp.int32, (2 * tq, wb), 0) % tq
    w_c = _multiplicity(wb + qi - lax.broadcasted_iota(jnp.int32, (2 * tq, wb), 1))
    qn = lax.broadcasted_iota(jnp.int32, (2 * tq, LANES), 0) % tq
    nn = lax.broadcasted_iota(jnp.int32, (2 * tq, LANES), 1)
    w_n = jnp.where(nn < tq, _multiplicity(qn - nn), 0.0)
    lo = lax.broadcasted_iota(jnp.int32, (tq, LANES), 1) < DSA_HEAD_DIM
    pad = jnp.zeros((LANES - tq, LANES), F32)
    neg = -jnp.inf
    for b, g in [(b, g) for b in range(nseq) for g in range(DSA_W // LANES)]:
        cols = slice(g * LANES, (g + 1) * LANES)
        q = q_ref[b, :, cols]
        q2 = jnp.concatenate([jnp.where(lo, q, 0.0), jnp.where(lo, 0.0, q)], axis=0).astype(BF16)
        kt_c, vt_c = kc_ref[b, cols, :].astype(BF16), vc_ref[b, cols, :].astype(BF16)
        k_n = jnp.concatenate([kn_ref[b, :, cols], pad], axis=0).astype(BF16)
        v_n = jnp.concatenate([vn_ref[b, :, cols], pad], axis=0).astype(BF16)
        s_c = jnp.where(w_c > 0, _dot(q2, kt_c), neg)
        s_n = jnp.where(w_n > 0, _dot_nt(q2, k_n), neg)
        mx = jnp.maximum(jnp.max(s_c, axis=-1, keepdims=True), jnp.max(s_n, axis=-1, keepdims=True))
        p_c = w_c * jnp.exp2(s_c - mx)
        p_n = w_n * jnp.exp2(s_n - mx)
        den = jnp.sum(p_c, axis=-1, keepdims=True) + jnp.sum(p_n, axis=-1, keepdims=True)
        o2 = (_dot_nt(p_c.astype(BF16), vt_c) + _dot(p_n.astype(BF16), v_n)) / den
        o_ref[b, :, cols] = jnp.where(lo, o2[:tq], o2[tq:])


def _dsa_step(q, k_new, v_new, k_cache, v_cache, layer, nb, t):
    wb = k_cache.shape[3]
    bblk = math.gcd(nb, DSA_STEP_SEQS)
    new = pl.BlockSpec((bblk, t, DSA_W), lambda b: (b, 0, 0))
    cache = pl.BlockSpec((None, bblk, DSA_W, wb), lambda b: (layer, b, 0, 0))
    r3 = lambda a: a.reshape(nb, t, DSA_W)
    return pl.pallas_call(
        _dsa_step_kernel,
        grid=(nb // bblk,),
        in_specs=[new, cache, cache, new, new],
        out_specs=new,
        out_shape=jax.ShapeDtypeStruct((nb, t, DSA_W), F32),
        compiler_params=_cparams("parallel"),
        name="dsa_step",
    )(r3(q), k_cache, v_cache, r3(k_new), r3(v_new)).reshape(nb * t, DSA_W)


RET_CHUNK = 256
RET_SEQS_PER_STEP = 2
RET_UNROLL = 4


def _ret_kernel(q_ref, k_ref, v_ref, g_ref, s0_ref, dec_ref, qd_ref, kd_ref, cd_ref, bd_ref, nw_ref,
                o_ref, so_ref, s_scr, *, chunk):
    tb = pl.program_id(1)
    ngrp = RET_W // LANES

    @pl.when(tb == 0)
    def _():
        s_scr[...] = s0_ref[...]

    lo = lax.broadcasted_iota(jnp.int32, (chunk, LANES), 1) < RET_HEAD_DIM
    nw = nw_ref[...]
    bd = bd_ref[...]

    def body(c, carry):
        rows = pl.ds(pl.multiple_of(c * chunk, chunk), chunk)
        for b in range(q_ref.shape[0]):
            for g in range(ngrp):
                cols = slice(g * LANES, (g + 1) * LANES)
                q, k, v = q_ref[b, rows, cols], k_ref[b, rows, cols], v_ref[b, rows, cols]
                kb, vb = k.astype(BF16), v.astype(BF16)
                parts = []
                for hh in range(2):
                    qm = jnp.where(lo if hh == 0 else jnp.logical_not(lo), q, 0.0).astype(BF16)
                    inner = _dot_nt(qm, kb) * dec_ref[2 * g + hh]
                    parts.append(_dot(inner.astype(BF16), vb))
                s_prev = s_scr[b, g]
                o = jnp.where(lo, parts[0], parts[1]) + _dot(q.astype(BF16), s_prev.astype(BF16)) * qd_ref[g]
                s_scr[b, g] = s_prev * cd_ref[g] + bd * _dot_tn((k * kd_ref[g]).astype(BF16), vb)
                o2 = o * o
                ms = jnp.where(lo, jnp.sum(jnp.where(lo, o2, 0.0), axis=-1, keepdims=True),
                               jnp.sum(jnp.where(lo, 0.0, o2), axis=-1, keepdims=True)) * (1.0 / RET_HEAD_DIM)
                o_ref[b, rows, cols] = (o * lax.rsqrt(ms + NORM_EPS) * nw * _silu(g_ref[b, rows, cols])).astype(
                    o_ref.dtype)
        return carry

    nchunk = q_ref.shape[1] // chunk
    lax.fori_loop(0, nchunk, body, 0, unroll=min(RET_UNROLL, nchunk))

    @pl.when(tb == pl.num_programs(1) - 1)
    def _():
        so_ref[...] = s_scr[...]


def _ret_tables(chunk):
    log_gamma = jnp.log(1.0 - 2.0 ** (-5.0 - jnp.arange(RET_HEADS, dtype=F32)))
    i = jnp.arange(chunk, dtype=F32)
    diff = i[:, None] - i[None, :]
    causal = diff >= 0
    decay = jnp.where(causal[None], jnp.exp(log_gamma[:, None, None] * jnp.where(causal, diff, 0.0)[None]), 0.0)
    per_lane = lambda a: jnp.repeat(a, RET_HEAD_DIM, axis=0).reshape(RET_W // LANES, LANES, -1)
    q_dec = per_lane(jnp.exp(log_gamma[:, None] * (i[None, :] + 1.0))).transpose(0, 2, 1)
    k_dec = per_lane(jnp.exp(log_gamma[:, None] * (chunk - 1.0 - i)[None, :])).transpose(0, 2, 1)
    c_dec = jnp.broadcast_to(per_lane(jnp.exp(log_gamma * chunk)[:, None]), (RET_W // LANES, LANES, LANES))
    head_of = jnp.arange(LANES) // RET_HEAD_DIM
    block_diag = (head_of[:, None] == head_of[None, :]).astype(F32)
    return decay, q_dec, k_dec, c_dec, block_diag


def _retention(q, k, v, gate, state_bd, ret_norm, nb, t):
    chunk = RET_CHUNK if t % RET_CHUNK == 0 else t
    tblk = min(t, 1024)
    ngrp = RET_W // LANES
    bblk = _seqs_per_step(nb, t, RET_SEQS_PER_STEP)
    decay, q_dec, k_dec, c_dec, block_diag = _ret_tables(chunk)
    r3 = lambda a: a.reshape(nb, t, RET_W)
    tok = pl.BlockSpec((bblk, tblk, RET_W), lambda b, i: (b, i, 0))
    st = pl.BlockSpec((bblk, ngrp, LANES, LANES), lambda b, i: (b, 0, 0, 0))
    full = lambda a: pl.BlockSpec(a.shape, lambda b, i: (0,) * a.ndim)
    nw = jnp.tile(ret_norm, LANES // RET_HEAD_DIM).reshape(1, LANES)
    o, s_new = pl.pallas_call(
        functools.partial(_ret_kernel, chunk=chunk),
        grid=(nb // bblk, t // tblk),
        in_specs=[tok, tok, tok, tok, st, full(decay), full(q_dec), full(k_dec), full(c_dec),
                  full(block_diag), full(nw)],
        out_specs=[tok, st],
        out_shape=[jax.ShapeDtypeStruct((nb, t, RET_W), _mixer_out_dtype(t)),
                   jax.ShapeDtypeStruct((nb, ngrp, LANES, LANES), F32)],
        scratch_shapes=[pltpu.VMEM((bblk, ngrp, LANES, LANES), F32)],
        compiler_params=_cparams("parallel", "arbitrary"),
        name="retention",
    )(r3(q), r3(k), r3(v), r3(gate), state_bd, decay, q_dec, k_dec, c_dec, block_diag, nw)
    return o.reshape(nb * t, RET_W), s_new


def _to_block_diag(s):
    nb = s.shape[0]
    s = s.reshape(nb, 2, 2, RET_HEAD_DIM, RET_HEAD_DIM)
    z = jnp.zeros_like(s[:, :, 0])
    top = jnp.concatenate([s[:, :, 0], z], axis=-1)
    bot = jnp.concatenate([z, s[:, :, 1]], axis=-1)
    return jnp.concatenate([top, bot], axis=-2)


def _from_block_diag(s):
    h = RET_HEAD_DIM
    return jnp.stack([s[:, :, :h, :h], s[:, :, h:, h:]], axis=2).reshape(s.shape[0], RET_HEADS, h, h)


GDN_PREP_CHUNKS_PER_ITER = 8
GDN_SCAN_SEQS_PER_STEP = 2
GDN_SCAN_UNROLL = 4


def _softplus(x):
    return jnp.maximum(x, 0.0) + jnp.log1p(jnp.exp(-jnp.abs(x)))


def _gdn_prep_kernel(cv_ref, hist_ref, ba_ref, cw_ref, ab_ref,
                     qg_ref, kd_ref, u_ref, w_ref, at_ref, el_ref, beta_scr, g_scr, *conv_scr, chunk, cpi):
    nseq, tblk = cv_ref.shape[0], cv_ref.shape[1]
    ncg = CONV_DIM // LANES
    ab = ab_ref[...]
    cs_scr = cv_ref
    for b in range(nseq):
        if conv_scr:
            xp_scr, cs_scr = conv_scr
            for cg in range(ncg):
                cols = slice(cg * LANES, (cg + 1) * LANES)
                xp_scr[b, cg, :SUBLANES, :] = hist_ref[b, :, cols]
                xp_scr[b, cg, SUBLANES:, :] = cv_ref[b, :, cols]
                cs_scr[b, :, cols] = _causal_conv_silu(xp_scr, b, cw_ref, cg, tblk)

        ba = ba_ref[b]
        beta_scr[b] = jax.nn.sigmoid(ba)
        g_scr[b] = -jnp.exp(ab[0:1, :]) * _softplus(ba + ab[1:2, :])

    grp = 2 if 2 * chunk == LANES else 1
    width = grp * chunk
    ri = lax.broadcasted_iota(jnp.int32, (chunk, width), 0)
    lane = lax.broadcasted_iota(jnp.int32, (chunk, width), 1)
    ci = lane % chunk
    first = lane < chunk
    incl = ri >= ci
    strict = ri > ci
    tri = (lax.broadcasted_iota(jnp.int32, (chunk, chunk), 0)
           >= lax.broadcasted_iota(jnp.int32, (chunk, chunk), 1)).astype(F32)
    lane_pick = (lax.broadcasted_iota(jnp.int32, (SUBLANES, LANES), 0)
                 == lax.broadcasted_iota(jnp.int32, (SUBLANES, LANES), 1)).astype(F32)
    base = min(SUBLANES, chunk)
    assert chunk % base == 0 and (chunk // base) & (chunk // base - 1) == 0
    same_blk = [(ri >> sh) == (ci >> sh) for sh in range(int(math.log2(base)), int(math.log2(chunk)) + 1)]

    def l2n(x):
        return x * lax.rsqrt(jnp.sum(x * x, axis=-1, keepdims=True) + NORM_EPS)

    def side_by_side(per_head):
        if grp == 1:
            return per_head[0][:, :width]
        return jnp.where(first, per_head[0][:, :width], per_head[1][:, :width])

    def block_diag(y):
        if grp == 1:
            return y.astype(BF16)
        return jnp.concatenate([jnp.where(first, y, 0.0), jnp.where(first, 0.0, y)], axis=0).astype(BF16)

    def block_rows(per_head):
        if grp == 1:
            return per_head[0].astype(BF16)
        a, b = per_head
        return jnp.concatenate([jnp.concatenate([a, jnp.zeros_like(b)], axis=1),
                                jnp.concatenate([jnp.zeros_like(a), b], axis=1)], axis=0).astype(BF16)

    def step(it, carry):
        probs = []
        for b, cc in [(b, cc) for b in range(nseq) for cc in range(cpi)]:
            c = it * cpi + cc
            rows = pl.ds(pl.multiple_of(c * chunk, chunk), chunk)
            beta_c = beta_scr[b, rows, :]
            gcol = _dot(tri, g_scr[b, rows, :], precision=HIGHEST)
            grow = _dot_nt(lane_pick, jnp.concatenate([gcol] * grp, axis=0), precision=HIGHEST)
            for h0 in range(0, GDN_HEADS, grp):
                hs = range(h0, h0 + grp)
                q = [l2n(cs_scr[b, rows, h * LANES:(h + 1) * LANES]) for h in hs]
                k = [l2n(cs_scr[b, rows, GDN_W + h * LANES:GDN_W + (h + 1) * LANES]) for h in hs]
                v = [cs_scr[b, rows, 2 * GDN_W + h * LANES:2 * GDN_W + (h + 1) * LANES] for h in hs]
                beta = [jnp.broadcast_to(beta_c[:, h:h + 1], (chunk, LANES)) for h in hs]
                gc = [jnp.broadcast_to(gcol[:, GDN_HEADS + h:GDN_HEADS + h + 1], (chunk, LANES)) for h in hs]
                g_row = side_by_side([jnp.broadcast_to(grow[GDN_HEADS + h:GDN_HEADS + h + 1, :], (chunk, width))
                                      for h in hs])
                gdiff = side_by_side(gc) - g_row
                dmask = jnp.where(incl, jnp.exp(jnp.where(incl, gdiff, 0.0)), 0.0)
                kb = [k[i] * beta[i] for i in range(grp)]
                k_rows = block_rows(k)
                lower = jnp.where(strict, _dot_nt(jnp.concatenate(kb, axis=1).astype(BF16), k_rows) * dmask, 0.0)
                npow = jnp.where(same_blk[0], -lower, 0.0)
                probs.append(dict(b=b, c=c, rows=rows, h0=h0, q=q, k=k, v=v, beta=beta, gc=gc, dmask=dmask,
                                  kb=kb, k_rows=k_rows, lower=lower, npow=npow, qmat=npow))
        for _ in range(int(math.log2(base)) - 1):
            for p in probs:
                p["npow"] = _dot(p["npow"].astype(BF16), block_diag(p["npow"]))
            for p in probs:
                p["qmat"] = p["qmat"] + p["npow"] + _dot(p["qmat"].astype(BF16), block_diag(p["npow"]))
        for lvl in range(1, len(same_blk)):
            for p in probs:
                cb = jnp.where(jnp.logical_and(same_blk[lvl], jnp.logical_not(same_blk[lvl - 1])),
                               p["lower"], 0.0)
                p["x"] = cb + _dot(p["qmat"].astype(BF16), block_diag(cb))
            for p in probs:
                p["qmat"] = p["qmat"] - p["x"] - _dot(p["x"].astype(BF16), block_diag(p["qmat"]))
        for p in probs:
            b, rows, h0, gc = p["b"], p["rows"], p["h0"], p["gc"]
            cols = slice(h0 * LANES, (h0 + grp) * LANES)
            idx = range(grp)
            eg = [jnp.exp(gc[i]) for i in idx]
            rhs_u = [p["v"][i] * p["beta"][i] for i in idx]
            rhs_w = [p["kb"][i] * eg[i] for i in idx]
            qb = p["qmat"].astype(BF16)
            qs = [p["q"][i] * (GDN_HEAD_DIM ** -0.5) for i in idx]
            g_last = [gc[i][chunk - 1:chunk, :] for i in idx]
            cat = lambda xs: jnp.concatenate(xs, axis=1)
            qg_ref[b, rows, cols] = cat([qs[i] * eg[i] for i in idx]).astype(qg_ref.dtype)
            kd_ref[b, rows, cols] = cat([p["k"][i] * jnp.exp(g_last[i] - gc[i]) for i in idx]).astype(kd_ref.dtype)
            u_ref[b, rows, cols] = cat(rhs_u) + _dot(qb, block_rows(rhs_u))
            w_ref[b, rows, cols] = (cat(rhs_w) + _dot(qb, block_rows(rhs_w))).astype(w_ref.dtype)
            at_ref[b, rows, h0 * chunk:(h0 + grp) * chunk] = (
                _dot_nt(cat(qs).astype(BF16), p["k_rows"]) * p["dmask"]).astype(at_ref.dtype)
            el_ref[b, pl.ds(pl.multiple_of(p["c"] * SUBLANES, SUBLANES), SUBLANES), cols] = cat(
                [jnp.broadcast_to(jnp.exp(g_last[i]), (SUBLANES, LANES)) for i in idx])
        return carry

    lax.fori_loop(0, tblk // (chunk * cpi), step, 0)


def _gdn_scan_kernel(qg_ref, kd_ref, u_ref, w_ref, at_ref, el_ref, z_ref, s0_ref, nw_ref,
                     o_ref, so_ref, s_scr, *, chunk):
    tb = pl.program_id(1)

    @pl.when(tb == 0)
    def _():
        s_scr[...] = s0_ref[...]

    nw = nw_ref[...]
    probs = [(b, h) for b in range(qg_ref.shape[0]) for h in range(GDN_HEADS)]

    def step(c, carry):
        rows = pl.ds(pl.multiple_of(c * chunk, chunk), chunk)
        erow = pl.ds(pl.multiple_of(c * SUBLANES, SUBLANES), SUBLANES)
        cols = lambda h: slice(h * LANES, (h + 1) * LANES)
        s_prev = [s_scr[b, h] for b, h in probs]
        sb = [s.astype(BF16) for s in s_prev]
        v_new = [u_ref[b, rows, cols(h)] - _dot(w_ref[b, rows, cols(h)].astype(BF16), sb[i])
                 for i, (b, h) in enumerate(probs)]
        vb = [v.astype(BF16) for v in v_new]
        for i, (b, h) in enumerate(probs):
            el = el_ref[b, erow, cols(h)][0:1, :]
            s_scr[b, h] = s_prev[i] * el + _dot_tn(kd_ref[b, rows, cols(h)].astype(BF16), vb[i])
        for i, (b, h) in enumerate(probs):
            attn = at_ref[b, rows, h * chunk:(h + 1) * chunk].astype(BF16)
            o = _dot(qg_ref[b, rows, cols(h)].astype(BF16), sb[i]) + _dot(attn, vb[i])
            ms = jnp.mean(o * o, axis=-1, keepdims=True)
            o_ref[b, rows, cols(h)] = (o * lax.rsqrt(ms + NORM_EPS) * nw * _silu(z_ref[b, rows, cols(h)])).astype(
                o_ref.dtype)
        return carry

    nchunk = qg_ref.shape[1] // chunk
    lax.fori_loop(0, nchunk, step, 0, unroll=min(GDN_SCAN_UNROLL, nchunk))

    @pl.when(tb == pl.num_programs(1) - 1)
    def _():
        so_ref[...] = s_scr[...]


def _gated_delta(cv, conv_done, z, ba, conv_hist, states, layer, conv_w, a_log, dt_bias, gdn_norm, nb, t):
    chunk = CHUNK if t % CHUNK == 0 else t
    tblk = min(t, 512)
    assert conv_done or tblk == t
    nchunk_blk = tblk // chunk
    hist_pad, cw_pad = _conv_operands(conv_hist, conv_w)
    ab = jnp.zeros((SUBLANES, LANES), F32)
    ab = ab.at[0, GDN_HEADS:2 * GDN_HEADS].set(a_log).at[1, GDN_HEADS:2 * GDN_HEADS].set(dt_bias)
    cv3 = cv.reshape(nb, t, CONV_DIM)
    pblk = _seqs_per_step(nb, t, 2)
    tok = lambda w: pl.BlockSpec((pblk, tblk, w), lambda b, i: (b, i, 0))
    full = lambda a: pl.BlockSpec(a.shape, lambda b, i: (0,) * a.ndim)
    el_spec =pl.BlockSpec((pblk, nchunk_blk * SUBLANES, GDN_W), lambda b, i: (b, i, 0))
    tok_shape = lambda w, dt=F32: jax.ShapeDtypeStruct((nb, t, w), dt)
    el_shape = jax.ShapeDtypeStruct((nb, (t // chunk) * SUBLANES, GDN_W), F32)
    opd = BF16 if chunk % (2 * SUBLANES) == 0 else F32
    qg, kd, u, w, attn, el = pl.pallas_call(
        functools.partial(_gdn_prep_kernel, chunk=chunk, cpi=min(GDN_PREP_CHUNKS_PER_ITER, nchunk_blk)),
        grid=(nb // pblk, t // tblk),
        in_specs=[tok(CONV_DIM), pl.BlockSpec((pblk, SUBLANES, CONV_DIM), lambda b, i: (b, 0, 0)),
                  tok(LANES), full(cw_pad), full(ab)],
        out_specs=[tok(GDN_W), tok(GDN_W), tok(GDN_W), tok(GDN_W), tok(GDN_HEADS * chunk), el_spec],
        out_shape=[tok_shape(GDN_W, opd), tok_shape(GDN_W, opd), tok_shape(GDN_W), tok_shape(GDN_W, opd),
                   tok_shape(GDN_HEADS * chunk, opd), el_shape],
        scratch_shapes=[pltpu.VMEM((pblk, tblk, LANES), F32), pltpu.VMEM((pblk, tblk, LANES), F32)] + (
            [] if conv_done else [pltpu.VMEM((pblk, CONV_DIM // LANES, tblk + SUBLANES, LANES), F32),
                                  pltpu.VMEM((pblk, tblk, CONV_DIM), F32)]),
        compiler_params=_cparams("parallel", "parallel"),
        name="gdn_prep",
    )(cv3, hist_pad, ba.reshape(nb, t, LANES), cw_pad, ab)

    bblk = _seqs_per_step(nb, t, GDN_SCAN_SEQS_PER_STEP)
    stok = lambda w: pl.BlockSpec((bblk, tblk, w), lambda b, i: (b, i, 0))
    sel_spec = pl.BlockSpec((bblk, nchunk_blk * SUBLANES, GDN_W), lambda b, i: (b, i, 0))
    st_in = pl.BlockSpec((None, bblk, GDN_HEADS, LANES, LANES), lambda b, i: (layer, b, 0, 0, 0))
    st = pl.BlockSpec((bblk, GDN_HEADS, LANES, LANES), lambda b, i: (b, 0, 0, 0))
    nw = gdn_norm.reshape(1, LANES)
    o, s_new = pl.pallas_call(
        functools.partial(_gdn_scan_kernel, chunk=chunk),
        grid=(nb // bblk, t // tblk),
        in_specs=[stok(GDN_W), stok(GDN_W), stok(GDN_W), stok(GDN_W), stok(GDN_HEADS * chunk), sel_spec,
                  stok(GDN_W), st_in, full(nw)],
        out_specs=[stok(GDN_W), st],
        out_shape=[tok_shape(GDN_W, _mixer_out_dtype(t)), jax.ShapeDtypeStruct((nb, GDN_HEADS, LANES, LANES), F32)],
        scratch_shapes=[pltpu.VMEM((bblk, GDN_HEADS, LANES, LANES), F32)],
        compiler_params=_cparams("parallel", "arbitrary"),
        name="gdn_scan",
    )(qg, kd, u, w, attn, el, z.reshape(nb, t, GDN_W), states, nw)
    return o.reshape(nb * t, GDN_W), s_new


FFN_TILE = 256


def _out_ffn_kernel(x_ref, oa_ref, orr_ref, oc_ref, g1_ref, sh2_ref, sc2_ref, g2_ref, nw_ref, wo_ref, wg_ref,
                    wu_ref, wd_ref, fn_ref, out_ref, *, final):
    mix = (_dot(oa_ref[...].astype(BF16), wo_ref[0:DSA_W, :])
           + _dot(orr_ref[...].astype(BF16), wo_ref[DSA_W:DSA_W + RET_W, :])
           + _dot(oc_ref[...].astype(BF16), wo_ref[DSA_W + RET_W:, :]))
    x1 = x_ref[...] + g1_ref[...] * mix
    hb = _rms_mod(x1, nw_ref[...], sc2_ref[...], sh2_ref[...]).astype(BF16)
    acc = None
    for j in range(wg_ref.shape[1] // FFN_TILE):
        cols = slice(j * FFN_TILE, (j + 1) * FFN_TILE)
        act = (_silu(_dot(hb, wg_ref[:, cols])) * _dot(hb, wu_ref[:, cols])).astype(BF16)
        down = _dot(act, wd_ref[cols, :])
        acc = down if acc is None else acc + down
    x2 = x1 + g2_ref[...] * acc
    if final:
        ms = jnp.mean(x2 * x2, axis=-1, keepdims=True)
        x2 = x2 * lax.rsqrt(ms + NORM_EPS) * fn_ref[...]
    out_ref[...] = x2


def _out_ffn(x2, t, oa, orr, oc, mod3, norm_w, w_out_b, wg_b, wu_b, wd_b, layer, final_norm, final, tm):
    m, d = x2.shape
    row = lambda w: pl.BlockSpec((tm, w), lambda i: (i, 0))
    const = lambda a: pl.BlockSpec(a.shape, lambda i: (0,) * a.ndim, pipeline_mode=pl.Buffered(1))
    wspec = lambda a: pl.BlockSpec((None,) + a.shape[1:], lambda i: (layer,) + (0,) * (a.ndim - 1),
                                   pipeline_mode=pl.Buffered(1))
    nw = norm_w.reshape(1, d)
    fn = final_norm.reshape(1, d)
    return pl.pallas_call(
        functools.partial(_out_ffn_kernel, final=final),
        grid=(m // tm,),
        in_specs=[row(d), row(DSA_W), row(RET_W), row(GDN_W),
                  _mod_spec(tm, t, d, 2), _mod_spec(tm, t, d, 3), _mod_spec(tm, t, d, 4), _mod_spec(tm, t, d, 5),
                  const(nw), wspec(w_out_b), wspec(wg_b), wspec(wu_b), wspec(wd_b), const(fn)],
        out_specs=row(d),
        out_shape=jax.ShapeDtypeStruct((m, d), F32),
        compiler_params=_cparams("parallel"),
        name="out_ffn",
    )(x2, oa, orr, oc, mod3, mod3, mod3, mod3, nw, w_out_b, wg_b, wu_b, wd_b, fn)


ROPE_SPLIT = 64


def _rope_tables(pos0, t, inv_freq):
    reps = LANES // (2 * inv_freq.shape[0])
    inv_lane = jnp.tile(jnp.concatenate([inv_freq, inv_freq]), reps)[None, :]
    sign_lane = jnp.tile(jnp.concatenate([-jnp.ones_like(inv_freq), jnp.ones_like(inv_freq)]), reps)[None, :]
    if t % ROPE_SPLIT:
        ang = (pos0 + jnp.arange(t, dtype=jnp.int32)).astype(F32)[:, None] * inv_lane
        return jnp.cos(ang), jnp.sin(ang) * sign_lane
    coarse = (pos0 + ROPE_SPLIT * jnp.arange(t // ROPE_SPLIT, dtype=jnp.int32)).astype(F32)[:, None] * inv_lane
    fine = jnp.arange(ROPE_SPLIT, dtype=jnp.int32).astype(F32)[:, None] * inv_lane
    cc, sc = jnp.cos(coarse)[:, None, :], jnp.sin(coarse)[:, None, :]
    cf, sf = jnp.cos(fine)[None], jnp.sin(fine)[None]
    cos = (cc * cf - sc * sf).reshape(t, LANES)
    sin = (sc * cf + cc * sf).reshape(t, LANES)
    return cos, sin * sign_lane


def _trunk(x, modp, pos0, k_hist, v_hist, s_ret, s_gdn, conv_hist, wts):
    (norm_mix, norm_ffn, w_in_b, ret_norm, conv_w, a_log, dt_bias, gdn_norm, w_out_b, wg_b, wu_b, wd_b,
     final_norm) = wts
    nb, t, d = x.shape
    m = nb * t
    depth = w_in_b.shape[0]
    tm = min(TOKEN_TILE, m)
    inv_a = 1.0 / (ROPE_THETA ** (jnp.arange(0, DSA_HEAD_DIM, 2, dtype=F32) / DSA_HEAD_DIM))
    inv_r = 1.0 / (10000.0 ** jnp.linspace(0.0, 1.0, RET_HEAD_DIM // 2, dtype=F32))
    tabs = _rope_tables(pos0, t, inv_a) + _rope_tables(pos0, t, inv_r)
    if t < tm:
        tabs = tuple(jnp.tile(a, (tm // t, 1)) for a in tabs)
    x2 = x.reshape(m, d)
    ks, vs, rs, gs, cs = [], [], [], [], []
    for l in range(depth):
        if t >= tm:
            mod3 = modp[l].reshape(nb, 1, 6 * d)
        else:
            mod3 = jnp.repeat(modp[l], t, axis=0).reshape(m // tm, tm, 6 * d)
        (qa, ka, va, qr, kr, vr, gr, cv, z, ba), fused = _inproj(
            x2, t, mod3, norm_mix[l], w_in_b, l, tabs, tm, conv_hist[l], conv_w[l])
        nhist = CONV_WIDTH - 1
        keep = min(t, DSA_MAX_WINDOW)
        if fused is None:
            cvn = jnp.concatenate([conv_hist[l], cv.reshape(nb, t, CONV_DIM)], axis=1)[:, -nhist:]
            window = lambda a: a.reshape(nb, t, DSA_W)[:, t - keep:].reshape(nb, keep, DSA_HEADS, DSA_HEAD_DIM)
            kwin, vwin = window(ka), window(va)
        else:
            conv_tail, k_t, v_t = fused
            cvn = conv_tail[:, SUBLANES - nhist:]
            untransposed = lambda a: a.reshape(nb, DSA_HEADS, DSA_HEAD_DIM, keep).transpose(0, 3, 1, 2)
            kwin, vwin = untransposed(k_t), untransposed(v_t)
        if k_hist is None:
            oa = _dsa_prompt(qa, ka, va, nb, t)
        else:
            oa = _dsa_step(qa, ka, va, k_hist, v_hist, l, nb, t)
        orr, sr = _retention(qr, kr, vr, gr, _to_block_diag(s_ret[l]), ret_norm[l], nb, t)
        oc, sg = _gated_delta(cv, fused is not None, z, ba, conv_hist[l], s_gdn, l, conv_w[l], a_log[l],
                              dt_bias[l], gdn_norm[l], nb, t)
        x2 = _out_ffn(x2, t, oa, orr, oc, mod3, norm_ffn[l], w_out_b, wg_b, wu_b, wd_b, l,
                      final_norm, l == depth - 1, tm)
        ks.append(kwin)
        vs.append(vwin)
        rs.append(_from_block_diag(sr))
        gs.append(sg)
        cs.append(cvn)
    return (x2.reshape(nb, t, d), jnp.stack(ks), jnp.stack(vs), jnp.stack(rs), jnp.stack(gs), jnp.stack(cs))


def kernel(x_prompt, x_sample, cache_win_k, cache_win_v, state_ret, state_gdn, state_conv, c_prompt, c_sample, ada_w, ada_b, norm_mix, norm_ffn, w_in, ret_norm, conv_w, a_log, dt_bias, gdn_norm, w_out, w_gate, w_up, w_down, final_norm):
    nb, t_p, d = x_prompt.shape
    db, t_s, _ = x_sample.shape
    depth = ada_w.shape[0]
    rows = nb + db
    rows_pad = -(-rows // SUBLANES) * SUBLANES
    c_all = jnp.concatenate([c_prompt, c_sample, jnp.zeros((rows_pad - rows, d), F32)], axis=0)
    mod = _modulation(c_all, ada_w, ada_b)
    w_in_b = jnp.pad(w_in, ((0, 0), (0, 0), (0, IN_COLS_PAD - IN_COLS))).astype(BF16)
    wts = (norm_mix, norm_ffn, w_in_b, ret_norm, conv_w, a_log, dt_bias, gdn_norm, w_out.astype(BF16),
           w_gate.astype(BF16), w_up.astype(BF16), w_down.astype(BF16), final_norm)

    zr = jnp.zeros((depth, nb, RET_HEADS, RET_HEAD_DIM, RET_HEAD_DIM), F32)
    zg = jnp.zeros((depth, nb, GDN_HEADS, GDN_HEAD_DIM, GDN_HEAD_DIM), F32)
    zc = jnp.zeros((depth, nb, CONV_WIDTH - 1, CONV_DIM), F32)
    y_p, kp, vp, rp, gp, cp = _trunk(x_prompt, mod[:, :nb], 0,
                                     None, None, zr, zg, zc, wts)
    wb = cache_win_k.shape[2]
    y_s, ks, vs, rs, gs, cs = _trunk(x_sample, mod[:, nb:rows], PAST_LEN,
                                     cache_win_k.reshape(depth, db, wb, DSA_W).transpose(0, 1, 3, 2),
                                     cache_win_v.reshape(depth, db, wb, DSA_W).transpose(0, 1, 3, 2),
                                     state_ret, state_gdn, state_conv, wts)
    return (y_p, y_s, kp, vp, rp, gp, cp, ks, vs, rs, gs, cs)
```

```python
import functools
import math

import jax
import jax.numpy as jnp
from jax import lax
from jax.experimental import pallas as pl
from jax.experimental.pallas import tpu as pltpu

F32 = jnp.float32
BF16 = jnp.bfloat16
HIGHEST = lax.Precision.HIGHEST

DSA_HEAD_DIM = 64
DSA_HEADS = 4
DSA_PATTERNS = ((128, 1), (512, 4), (2048, 16))
DSA_MAX_WINDOW = 2048
ROPE_THETA = 10000.0
RET_HEAD_DIM = 64
RET_HEADS = 4
GDN_HEAD_DIM = 128
GDN_HEADS = 4
CONV_WIDTH = 4
CHUNK = 64
NORM_EPS = 1e-6
PAST_LEN = 16384
LOG2_E = math.log2(math.e)

DSA_W = DSA_HEADS * DSA_HEAD_DIM
RET_W = RET_HEADS * RET_HEAD_DIM
GDN_W = GDN_HEADS * GDN_HEAD_DIM
CONV_DIM = 3 * GDN_W
LANES = 128
SUBLANES = 8
VMEM_LIMIT = 56 * 1024 * 1024
TOKEN_TILE = 512

_C_QA, _C_KA, _C_VA = 0, DSA_W, 2 * DSA_W
_C_QR = 3 * DSA_W
_C_KR, _C_VR, _C_GR = _C_QR + RET_W, _C_QR + 2 * RET_W, _C_QR + 3 * RET_W
_C_CV = _C_QR + 4 * RET_W
_C_Z = _C_CV + CONV_DIM
_C_BA = _C_Z + GDN_W
IN_COLS = _C_BA + 2 * GDN_HEADS


def _cparams(*sem):
    return pltpu.CompilerParams(dimension_semantics=sem, vmem_limit_bytes=VMEM_LIMIT)


def _dot(a, b, **kw):
    return jnp.dot(a, b, preferred_element_type=F32, **kw)


def _dot_nt(a, b, **kw):
    return lax.dot_general(a, b, (((1,), (1,)), ((), ())), preferred_element_type=F32, **kw)


def _dot_tn(a, b, **kw):
    return lax.dot_general(a, b, (((0,), (0,)), ((), ())), preferred_element_type=F32, **kw)


def _silu(x):
    return x * jax.nn.sigmoid(x)


def _mixer_out_dtype(t):
    return BF16 if t % (2 * SUBLANES) == 0 else F32


def _seqs_per_step(nb, t, base):
    return math.gcd(nb, base * (4 if t < CHUNK else 1))


def _mod_kernel(c_ref, w_ref, b_ref, o_ref):
    a = _silu(c_ref[...]).astype(BF16)
    o_ref[...] = _dot(a, w_ref[...].astype(BF16)) + b_ref[...]


def _modulation(c_all, ada_w, ada_b, tn=1536):
    depth, d, n = ada_w.shape
    bp = c_all.shape[0]
    return pl.pallas_call(
        _mod_kernel,
        grid=(depth, n // tn),
        in_specs=[
            pl.BlockSpec((bp, d), lambda l, j: (0, 0)),
            pl.BlockSpec((None, d, tn), lambda l, j: (l, 0, j)),
            pl.BlockSpec((None, 1, tn), lambda l, j: (l, 0, j)),
        ],
        out_specs=pl.BlockSpec((None, bp, tn), lambda l, j: (l, 0, j)),
        out_shape=jax.ShapeDtypeStruct((depth, bp, n), F32),
        compiler_params=_cparams("parallel", "parallel"),
        name="modulation",
    )(c_all, ada_w, ada_b.reshape(depth, 1, n))


def _rms_mod(x, nw, sc, sh):
    ms = jnp.mean(x * x, axis=-1, keepdims=True)
    return (x * lax.rsqrt(ms + NORM_EPS) * nw) * (1.0 + sc) + sh


def _causal_conv_silu(xp_scr, seq, cw_ref, cg, nrows):
    cols = slice(cg * LANES, (cg + 1) * LANES)
    acc = None
    for i in range(CONV_WIDTH):
        start = SUBLANES - (CONV_WIDTH - 1) + i
        term = xp_scr[seq, cg, pl.ds(start, nrows, stride=1), :] * cw_ref[i:i + 1, cols]
        acc = term if acc is None else acc + term
    return _silu(acc)


def _inproj_kernel(x_ref, nw_ref, sh_ref, sc_ref, w_ref, wba_ref, ca_ref, sa_ref, cr_ref, sr_ref, *rest,
                   tiles_per_seq):
    if tiles_per_seq:
        hist_ref, cw_ref = rest[:2]
        rest = rest[2:]
    qa_ref, ka_ref, va_ref, qr_ref, kr_ref, vr_ref, gr_ref, cv_ref, z_ref, ba_ref = rest[:10]
    tm = x_ref.shape[0]
    if tiles_per_seq:
        tail_ref, kt_ref, vt_ref, xp_scr = rest[10:]

        @pl.when(pl.program_id(0) % tiles_per_seq == 0)
        def _():
            tail_ref[...] = hist_ref[...]

    hb = _rms_mod(x_ref[...], nw_ref[...], sc_ref[...], sh_ref[...]).astype(BF16)

    def proj(c0, width):
        return _dot(hb, w_ref[:, c0:c0 + width])

    lane = lax.broadcasted_iota(jnp.int32, (tm, DSA_W), 1)
    first_half = (lane % DSA_HEAD_DIM) < (DSA_HEAD_DIM // 2)

    def rope(y, cos, sin_signed):
        partner = jnp.where(first_half, pltpu.roll(y, DSA_W - DSA_HEAD_DIM // 2, 1),
                            pltpu.roll(y, DSA_HEAD_DIM // 2, 1))
        return y * cos + partner * sin_signed

    wide = lambda ref: jnp.concatenate([ref[...]] * (DSA_W // LANES), axis=1)
    ca, sa, cr, sr = wide(ca_ref), wide(sa_ref), wide(cr_ref), wide(sr_ref)
    qa_ref[...] = rope(proj(_C_QA, DSA_W), ca, sa) * (DSA_HEAD_DIM ** -0.5 * LOG2_E)
    ka = rope(proj(_C_KA, DSA_W), ca, sa)
    va = proj(_C_VA, DSA_W)
    ka_ref[...] = ka
    va_ref[...] = va
    if tiles_per_seq:
        kt_ref[...] = ka.T
        vt_ref[...] = va.T

    qr_ref[...] = rope(proj(_C_QR, RET_W), cr, sr)
    kr_ref[...] = rope(proj(_C_KR, RET_W), cr, sr) * (RET_HEAD_DIM ** -0.5)
    vr_ref[...] = proj(_C_VR, RET_W)
    gr_ref[...] = proj(_C_GR, RET_W)
    for s in range(CONV_DIM // GDN_W):
        y = proj(_C_CV + s * GDN_W, GDN_W)
        if not tiles_per_seq:
            cv_ref[:, s * GDN_W:(s + 1) * GDN_W] = y
            continue
        for j in range(GDN_W // LANES):
            cg = s * (GDN_W // LANES) + j
            cols = slice(cg * LANES, (cg + 1) * LANES)
            yj = y[:, j * LANES:(j + 1) * LANES]
            xp_scr[0, cg, :SUBLANES, :] = tail_ref[:, cols]
            xp_scr[0, cg, SUBLANES:, :] = yj
            cv_ref[:, cols] = _causal_conv_silu(xp_scr, 0, cw_ref, cg, tm)
            tail_ref[:, cols] = yj[tm - SUBLANES:, :]
    z_ref[...] = proj(_C_Z, GDN_W)
    ba_ref[...] = _dot(hb, wba_ref[...])


def _mod_spec(tm, t, d, col):
    if t >= tm:
        return pl.BlockSpec((None, 1, d), lambda i: ((i * tm) // t, 0, col))
    return pl.BlockSpec((None, tm, d), lambda i: (i, 0, col))


def _conv_operands(conv_hist, conv_w):
    nb = conv_hist.shape[0]
    hist_pad = jnp.concatenate([jnp.zeros((nb, SUBLANES - (CONV_WIDTH - 1), CONV_DIM), F32), conv_hist], axis=1)
    cw_pad = jnp.concatenate([conv_w, jnp.zeros((SUBLANES - CONV_WIDTH, CONV_DIM), F32)], axis=0)
    return hist_pad, cw_pad


def _inproj(x2, t, mod3, norm_w, w_in_b, w_ba_b, layer, tabs, tm, conv_hist, conv_w):
    m, d = x2.shape
    nt = tabs[0].shape[0] // tm
    widths = (DSA_W,) * 3 + (RET_W,) * 4 + (CONV_DIM, GDN_W, LANES)
    tab_spec = pl.BlockSpec((tm, LANES), lambda i: (i % nt, 0))
    tiles_per_seq = t // tm if t % tm == 0 else 0
    window_tiles = min(t, DSA_MAX_WINDOW) // tm
    in_specs = [
        pl.BlockSpec((tm, d), lambda i: (i, 0)),
        pl.BlockSpec((1, d), lambda i: (0, 0)),
        _mod_spec(tm, t, d, 0),
        _mod_spec(tm, t, d, 1),
        pl.BlockSpec((None, d, IN_COLS), lambda i: (layer, 0, 0), pipeline_mode=pl.Buffered(1)),
        pl.BlockSpec((None, d, LANES), lambda i: (layer, 0, 0), pipeline_mode=pl.Buffered(1)),
        tab_spec, tab_spec, tab_spec, tab_spec,
    ]
    operands = [x2, norm_w.reshape(1, d), mod3, mod3, w_in_b, w_ba_b, *tabs]
    out_specs = [pl.BlockSpec((tm, w), lambda i: (i, 0)) for w in widths]
    out_shape = [jax.ShapeDtypeStruct((m, w), F32) for w in widths]
    scratch = []
    if tiles_per_seq:
        hist_pad, cw_pad = _conv_operands(conv_hist, conv_w)
        seq_tail = pl.BlockSpec((None, SUBLANES, CONV_DIM), lambda i: (i // tiles_per_seq, 0, 0))
        in_specs += [seq_tail, pl.BlockSpec(cw_pad.shape, lambda i: (0, 0))]
        operands += [hist_pad, cw_pad]
        out_specs.append(seq_tail)
        out_shape.append(jax.ShapeDtypeStruct((m // t, SUBLANES, CONV_DIM), F32))
        first_win = tiles_per_seq - window_tiles
        win_t = pl.BlockSpec((None, DSA_W, tm), lambda i: (i // tiles_per_seq, 0,
                                                            jnp.maximum(i % tiles_per_seq - first_win, 0)))
        out_specs += [win_t, win_t]
        out_shape += [jax.ShapeDtypeStruct((m // t, DSA_W, window_tiles * tm), F32)] * 2
        scratch.append(pltpu.VMEM((1, CONV_DIM // LANES, tm + SUBLANES, LANES), F32))
    outs = pl.pallas_call(
        functools.partial(_inproj_kernel, tiles_per_seq=tiles_per_seq),
        grid=(m // tm,),
        in_specs=in_specs,
        out_specs=out_specs,
        out_shape=out_shape,
        scratch_shapes=scratch,
        compiler_params=_cparams("arbitrary"),
        name="inproj",
    )(*operands)
    return (outs[:10], outs[10:]) if tiles_per_seq else (outs, None)


DSA_BLK = 128


DSA_QBLK = DSA_MAX_WINDOW
DSA_GROUP = 4


def _dsa_kernel(q_ref, kp_ref, kc_ref, vp_ref, vc_ref, o_ref, acc_scr, m_scr, l_scr):
    blk = pl.program_id(2)
    qblk = q_ref.shape[0]
    row = lax.broadcasted_iota(jnp.int32, (2 * DSA_BLK, 2 * DSA_BLK), 0) % DSA_BLK
    col = lax.broadcasted_iota(jnp.int32, (2 * DSA_BLK, 2 * DSA_BLK), 1)
    ok = jnp.logical_and(col >= row, col <= row + DSA_BLK)
    ok_first = jnp.logical_and(ok, jnp.logical_or(col >= DSA_BLK, blk > 0))
    lo = lax.broadcasted_iota(jnp.int32, (DSA_BLK, LANES), 1) < DSA_HEAD_DIM
    neg = -jnp.inf
    halves = lambda x: jnp.where(lo, x[:DSA_BLK], x[DSA_BLK:])

    def rows_of(dil, start, n):
        return pl.ds(start, n) if dil == 1 else pl.ds(start, n, stride=dil)

    def group_softmax(dil, subs):
        n = range(len(subs))
        idx = [rows_of(dil, ph + dil * DSA_BLK * j, DSA_BLK) for ph, j in subs]
        q = [q_ref[i, :] for i in idx]
        k, v = [], []
        for u, (ph, j) in enumerate(subs):
            if j == 0:
                band = rows_of(dil, qblk - dil * DSA_BLK + ph, DSA_BLK)
                k.append(jnp.concatenate([kp_ref[band, :], kc_ref[idx[u], :]], axis=0).astype(BF16))
                v.append(jnp.concatenate([vp_ref[band, :], vc_ref[idx[u], :]], axis=0).astype(BF16))
            else:
                both = rows_of(dil, ph + dil * DSA_BLK * (j - 1), 2 * DSA_BLK)
                k.append(kc_ref[both, :].astype(BF16))
                v.append(vc_ref[both, :].astype(BF16))
        q2 = [jnp.concatenate([jnp.where(lo, q[u], 0.0), jnp.where(lo, 0.0, q[u])], axis=0).astype(BF16)
              for u in n]
        s = [jnp.where(ok_first if subs[u][1] == 0 else ok, _dot_nt(q2[u], k[u]), neg) for u in n]
        mx = [jnp.max(s[u], axis=-1, keepdims=True) for u in n]
        p = [jnp.exp2(s[u] - mx[u]) for u in n]
        den = [jnp.sum(p[u], axis=-1, keepdims=True) for u in n]
        pv = [_dot(p[u].astype(BF16), v[u]) for u in n]
        return idx, [(halves(pv[u]), halves(jnp.broadcast_to(mx[u], (2 * DSA_BLK, LANES))),
                      halves(jnp.broadcast_to(den[u], (2 * DSA_BLK, LANES)))) for u in n]

    dils = sorted((d for _, d in DSA_PATTERNS), reverse=True)
    for pi, dil in enumerate(dils):
        subs = [(ph, j) for j in range(qblk // (dil * DSA_BLK)) for ph in range(dil)]
        for g0 in range(0, len(subs), DSA_GROUP):
            idx, tiles = group_softmax(dil, subs[g0:g0 + DSA_GROUP])
            for i, (pv, mx, den) in zip(idx, tiles):
                if pi > 0:
                    m_old = m_scr[i, :]
                    m_new = jnp.maximum(m_old, mx)
                    w_old = jnp.exp2(m_old - m_new)
                    w_cur = jnp.exp2(mx - m_new)
                    pv = acc_scr[i, :] * w_old + pv * w_cur
                    den = l_scr[i, :] * w_old + den * w_cur
                    mx = m_new
                if pi < len(dils) - 1:
                    acc_scr[i, :] = pv
                    m_scr[i, :] = mx
                    l_scr[i, :] = den
                else:
                    o_ref[i, :] = (pv / den).astype(o_ref.dtype)


def _dsa_prompt(q, k, v, nb, t):
    assert all(w // d == DSA_BLK for w, d in DSA_PATTERNS) and t % DSA_QBLK == 0
    ngrp = DSA_W // LANES
    r3 = lambda a: a.reshape(nb, t, DSA_W)
    cur = pl.BlockSpec((None, DSA_QBLK, LANES), lambda b, g, i: (b, i, g))
    prv = pl.BlockSpec((None, DSA_QBLK, LANES), lambda b, g, i: (b, jnp.maximum(i - 1, 0), g))
    return pl.pallas_call(
        _dsa_kernel,
        grid=(nb, ngrp, t // DSA_QBLK),
        in_specs=[cur, prv, cur, prv, cur],
        out_specs=cur,
        out_shape=jax.ShapeDtypeStruct((nb, t, DSA_W), _mixer_out_dtype(t)),
        scratch_shapes=[pltpu.VMEM((DSA_QBLK, LANES), F32)] * 3,
        compiler_params=_cparams("parallel", "parallel", "parallel"),
        name="dsa_prompt",
    )(r3(q), r3(k), r3(k), r3(v), r3(v)).reshape(nb * t, DSA_W)


DSA_STEP_SEQS = 2


def _multiplicity(dist):
    total = jnp.zeros(dist.shape, F32)
    for window, dil in DSA_PATTERNS:
        hit = (dist >= 0) & (dist <= window) & ((dist & (dil - 1)) == 0)
        total = total + hit.astype(F32)
    return total


def _dsa_step_kernel(q_ref, kc_ref, vc_ref, kn_ref, vn_ref, o_ref):
    nseq, tq = q_ref.shape[0], q_ref.shape[1]
    wb = kc_ref.shape[2]
    qi = lax.broadcasted_iota(jnp.int32, (2 * tq, wb), 0) % tq
    w_c = _multiplicity(wb + qi - lax.broadcasted_iota(jnp.int32, (2 * tq, wb), 1))
    qn = lax.broadcasted_iota(jnp.int32, (2 * tq, LANES), 0) % tq
    nn = lax.broadcasted_iota(jnp.int32, (2 * tq, LANES), 1)
    w_n = jnp.where(nn < tq, _multiplicity(qn - nn), 0.0)
    lo = lax.broadcasted_iota(jnp.int32, (tq, LANES), 1) < DSA_HEAD_DIM
    pad = jnp.zeros((LANES - tq, LANES), F32)
    neg = -jnp.inf
    for b, g in [(b, g) for b in range(nseq) for g in range(DSA_W // LANES)]:
        cols = slice(g * LANES, (g + 1) * LANES)
        q = q_ref[b, :, cols]
        q2 = jnp.concatenate([jnp.where(lo, q, 0.0), jnp.where(lo, 0.0, q)], axis=0).astype(BF16)
        kt_c, vt_c = kc_ref[b, cols, :].astype(BF16), vc_ref[b, cols, :].astype(BF16)
        k_n = jnp.concatenate([kn_ref[b, :, cols], pad], axis=0).astype(BF16)
        v_n = jnp.concatenate([vn_ref[b, :, cols], pad], axis=0).astype(BF16)
        s_c = jnp.where(w_c > 0, _dot(q2, kt_c), neg)
        s_n = jnp.where(w_n > 0, _dot_nt(q2, k_n), neg)
        mx = jnp.maximum(jnp.max(s_c, axis=-1, keepdims=True), jnp.max(s_n, axis=-1, keepdims=True))
        p_c = w_c * jnp.exp2(s_c - mx)
        p_n = w_n * jnp.exp2(s_n - mx)
        den = jnp.sum(p_c, axis=-1, keepdims=True) + jnp.sum(p_n, axis=-1, keepdims=True)
        o2 = (_dot_nt(p_c.astype(BF16), vt_c) + _dot(p_n.astype(BF16), v_n)) / den
        o_ref[b, :, cols] = jnp.where(lo, o2[:tq], o2[tq:])


def _dsa_step(q, k_new, v_new, k_cache, v_cache, layer, nb, t):
    wb = k_cache.shape[3]
    bblk = math.gcd(nb, DSA_STEP_SEQS)
    new = pl.BlockSpec((bblk, t, DSA_W), lambda b: (b, 0, 0))
    cache = pl.BlockSpec((None, bblk, DSA_W, wb), lambda b: (layer, b, 0, 0))
    r3 = lambda a: a.reshape(nb, t, DSA_W)
    return pl.pallas_call(
        _dsa_step_kernel,
        grid=(nb // bblk,),
        in_specs=[new, cache, cache, new, new],
        out_specs=new,
        out_shape=jax.ShapeDtypeStruct((nb, t, DSA_W), F32),
        compiler_params=_cparams("parallel"),
        name="dsa_step",
    )(r3(q), k_cache, v_cache, r3(k_new), r3(v_new)).reshape(nb * t, DSA_W)


RET_CHUNK = 256
RET_SEQS_PER_STEP = 2
RET_UNROLL = 4


def _ret_kernel(q_ref, k_ref, v_ref, g_ref, s0_ref, dec_ref, qd_ref, kd_ref, cd_ref, bd_ref, nw_ref,
                o_ref, so_ref, s_scr, *, chunk):
    tb = pl.program_id(1)
    ngrp = RET_W // LANES

    @pl.when(tb == 0)
    def _():
        s_scr[...] = s0_ref[...]

    lo = lax.broadcasted_iota(jnp.int32, (chunk, LANES), 1) < RET_HEAD_DIM
    nw = nw_ref[...]
    bd = bd_ref[...]

    def body(c, carry):
        rows = pl.ds(pl.multiple_of(c * chunk, chunk), chunk)
        for b in range(q_ref.shape[0]):
            for g in range(ngrp):
                cols = slice(g * LANES, (g + 1) * LANES)
                q, k, v = q_ref[b, rows, cols], k_ref[b, rows, cols], v_ref[b, rows, cols]
                kb, vb = k.astype(BF16), v.astype(BF16)
                parts = []
                for hh in range(2):
                    qm = jnp.where(lo if hh == 0 else jnp.logical_not(lo), q, 0.0).astype(BF16)
                    inner = _dot_nt(qm, kb) * dec_ref[2 * g + hh]
                    parts.append(_dot(inner.astype(BF16), vb))
                s_prev = s_scr[b, g]
                o = jnp.where(lo, parts[0], parts[1]) + _dot(q.astype(BF16), s_prev.astype(BF16)) * qd_ref[g]
                s_scr[b, g] = s_prev * cd_ref[g] + bd * _dot_tn((k * kd_ref[g]).astype(BF16), vb)
                o2 = o * o
                ms = jnp.where(lo, jnp.sum(jnp.where(lo, o2, 0.0), axis=-1, keepdims=True),
                               jnp.sum(jnp.where(lo, 0.0, o2), axis=-1, keepdims=True)) * (1.0 / RET_HEAD_DIM)
                o_ref[b, rows, cols] = (o * lax.rsqrt(ms + NORM_EPS) * nw * _silu(g_ref[b, rows, cols])).astype(
                    o_ref.dtype)
        return carry

    nchunk = q_ref.shape[1] // chunk
    lax.fori_loop(0, nchunk, body, 0, unroll=min(RET_UNROLL, nchunk))

    @pl.when(tb == pl.num_programs(1) - 1)
    def _():
        so_ref[...] = s_scr[...]


def _ret_tables(chunk):
    log_gamma = jnp.log(1.0 - 2.0 ** (-5.0 - jnp.arange(RET_HEADS, dtype=F32)))
    i = jnp.arange(chunk, dtype=F32)
    diff = i[:, None] - i[None, :]
    causal = diff >= 0
    decay = jnp.where(causal[None], jnp.exp(log_gamma[:, None, None] * jnp.where(causal, diff, 0.0)[None]), 0.0)
    per_lane = lambda a: jnp.repeat(a, RET_HEAD_DIM, axis=0).reshape(RET_W // LANES, LANES, -1)
    q_dec = per_lane(jnp.exp(log_gamma[:, None] * (i[None, :] + 1.0))).transpose(0, 2, 1)
    k_dec = per_lane(jnp.exp(log_gamma[:, None] * (chunk - 1.0 - i)[None, :])).transpose(0, 2, 1)
    c_dec = jnp.broadcast_to(per_lane(jnp.exp(log_gamma * chunk)[:, None]), (RET_W // LANES, LANES, LANES))
    head_of = jnp.arange(LANES) // RET_HEAD_DIM
    block_diag = (head_of[:, None] == head_of[None, :]).astype(F32)
    return decay, q_dec, k_dec, c_dec, block_diag


def _retention(q, k, v, gate, state_bd, ret_norm, nb, t):
    chunk = RET_CHUNK if t % RET_CHUNK == 0 else t
    tblk = min(t, 1024)
    ngrp = RET_W // LANES
    bblk = _seqs_per_step(nb, t, RET_SEQS_PER_STEP)
    decay, q_dec, k_dec, c_dec, block_diag = _ret_tables(chunk)
    r3 = lambda a: a.reshape(nb, t, RET_W)
    tok = pl.BlockSpec((bblk, tblk, RET_W), lambda b, i: (b, i, 0))
    st = pl.BlockSpec((bblk, ngrp, LANES, LANES), lambda b, i: (b, 0, 0, 0))
    full = lambda a: pl.BlockSpec(a.shape, lambda b, i: (0,) * a.ndim)
    nw = jnp.tile(ret_norm, LANES // RET_HEAD_DIM).reshape(1, LANES)
    o, s_new = pl.pallas_call(
        functools.partial(_ret_kernel, chunk=chunk),
        grid=(nb // bblk, t // tblk),
        in_specs=[tok, tok, tok, tok, st, full(decay), full(q_dec), full(k_dec), full(c_dec),
                  full(block_diag), full(nw)],
        out_specs=[tok, st],
        out_shape=[jax.ShapeDtypeStruct((nb, t, RET_W), _mixer_out_dtype(t)),
                   jax.ShapeDtypeStruct((nb, ngrp, LANES, LANES), F32)],
        scratch_shapes=[pltpu.VMEM((bblk, ngrp, LANES, LANES), F32)],
        compiler_params=_cparams("parallel", "arbitrary"),
        name="retention",
    )(r3(q), r3(k), r3(v), r3(gate), state_bd, decay, q_dec, k_dec, c_dec, block_diag, nw)
    return o.reshape(nb * t, RET_W), s_new


def _to_block_diag(s):
    nb = s.shape[0]
    s = s.reshape(nb, 2, 2, RET_HEAD_DIM, RET_HEAD_DIM)
    z = jnp.zeros_like(s[:, :, 0])
    top = jnp.concatenate([s[:, :, 0], z], axis=-1)
    bot = jnp.concatenate([z, s[:, :, 1]], axis=-1)
    return jnp.concatenate([top, bot], axis=-2)


def _from_block_diag(s):
    h = RET_HEAD_DIM
    return jnp.stack([s[:, :, :h, :h], s[:, :, h:, h:]], axis=2).reshape(s.shape[0], RET_HEADS, h, h)


GDN_PREP_CHUNKS_PER_ITER = 8
GDN_SCAN_SEQS_PER_STEP = 2
GDN_SCAN_UNROLL = 4


def _softplus(x):
    return jnp.maximum(x, 0.0) + jnp.log1p(jnp.exp(-jnp.abs(x)))


def _gdn_prep_kernel(cv_ref, hist_ref, ba_ref, cw_ref, ab_ref,
                     qg_ref, kd_ref, u_ref, w_ref, at_ref, el_ref, beta_scr, g_scr, *conv_scr, chunk, cpi):
    nseq, tblk = cv_ref.shape[0], cv_ref.shape[1]
    ncg = CONV_DIM // LANES
    ab = ab_ref[...]
    cs_scr = cv_ref
    for b in range(nseq):
        if conv_scr:
            xp_scr, cs_scr = conv_scr
            for cg in range(ncg):
                cols = slice(cg * LANES, (cg + 1) * LANES)
                xp_scr[b, cg, :SUBLANES, :] = hist_ref[b, :, cols]
                xp_scr[b, cg, SUBLANES:, :] = cv_ref[b, :, cols]
                cs_scr[b, :, cols] = _causal_conv_silu(xp_scr, b, cw_ref, cg, tblk)

        ba = ba_ref[b]
        beta_scr[b] = jax.nn.sigmoid(ba)
        g_scr[b] = -jnp.exp(ab[0:1, :]) * _softplus(ba + ab[1:2, :])

    grp = 2 if 2 * chunk == LANES else 1
    width = grp * chunk
    ri = lax.broadcasted_iota(jnp.int32, (chunk, width), 0)
    lane = lax.broadcasted_iota(jnp.int32, (chunk, width), 1)
    ci = lane % chunk
    first = lane < chunk
    incl = ri >= ci
    strict = ri > ci
    tri = (lax.broadcasted_iota(jnp.int32, (chunk, chunk), 0)
           >= lax.broadcasted_iota(jnp.int32, (chunk, chunk), 1)).astype(F32)
    lane_pick = (lax.broadcasted_iota(jnp.int32, (SUBLANES, LANES), 0)
                 == lax.broadcasted_iota(jnp.int32, (SUBLANES, LANES), 1)).astype(F32)
    base = min(SUBLANES, chunk)
    assert chunk % base == 0 and (chunk // base) & (chunk // base - 1) == 0
    same_blk = [(ri >> sh) == (ci >> sh) for sh in range(int(math.log2(base)), int(math.log2(chunk)) + 1)]

    def l2n(x):
        return x * lax.rsqrt(jnp.sum(x * x, axis=-1, keepdims=True) + NORM_EPS)

    def side_by_side(per_head):
        if grp == 1:
            return per_head[0][:, :width]
        return jnp.where(first, per_head[0][:, :width], per_head[1][:, :width])

    def block_diag(y):
        if grp == 1:
            return y.astype(BF16)
        return jnp.concatenate([jnp.where(first, y, 0.0), jnp.where(first, 0.0, y)], axis=0).astype(BF16)

    def block_rows(per_head):
        if grp == 1:
            return per_head[0].astype(BF16)
        a, b = per_head
        return jnp.concatenate([jnp.concatenate([a, jnp.zeros_like(b)], axis=1),
                                jnp.concatenate([jnp.zeros_like(a), b], axis=1)], axis=0).astype(BF16)

    def step(it, carry):
        probs = []
        for b, cc in [(b, cc) for b in range(nseq) for cc in range(cpi)]:
            c = it * cpi + cc
            rows = pl.ds(pl.multiple_of(c * chunk, chunk), chunk)
            beta_c = beta_scr[b, rows, :]
            gcol = _dot(tri, g_scr[b, rows, :], precision=HIGHEST)
            grow = _dot_nt(lane_pick, jnp.concatenate([gcol] * grp, axis=0), precision=HIGHEST)
            for h0 in range(0, GDN_HEADS, grp):
                hs = range(h0, h0 + grp)
                q = [l2n(cs_scr[b, rows, h * LANES:(h + 1) * LANES]) for h in hs]
                k = [l2n(cs_scr[b, rows, GDN_W + h * LANES:GDN_W + (h + 1) * LANES]) for h in hs]
                v = [cs_scr[b, rows, 2 * GDN_W + h * LANES:2 * GDN_W + (h + 1) * LANES] for h in hs]
                beta = [jnp.broadcast_to(beta_c[:, h:h + 1], (chunk, LANES)) for h in hs]
                gc = [jnp.broadcast_to(gcol[:, GDN_HEADS + h:GDN_HEADS + h + 1], (chunk, LANES)) for h in hs]
                g_row = side_by_side([jnp.broadcast_to(grow[GDN_HEADS + h:GDN_HEADS + h + 1, :], (chunk, width))
                                      for h in hs])
                gdiff = side_by_side(gc) - g_row
                dmask = jnp.where(incl, jnp.exp(jnp.where(incl, gdiff, 0.0)), 0.0)
                kb = [k[i] * beta[i] for i in range(grp)]
                k_rows = block_rows(k)
                lower = jnp.where(strict, _dot_nt(jnp.concatenate(kb, axis=1).astype(BF16), k_rows) * dmask, 0.0)
                npow = jnp.where(same_blk[0], -lower, 0.0)
                probs.append(dict(b=b, c=c, rows=rows, h0=h0, q=q, k=k, v=v, beta=beta, gc=gc, dmask=dmask,
                                  kb=kb, k_rows=k_rows, lower=lower, npow=npow, qmat=npow))
        for _ in range(int(math.log2(base)) - 1):
            for p in probs:
                p["npow"] = _dot(p["npow"].astype(BF16), block_diag(p["npow"]))
            for p in probs:
                p["qmat"] = p["qmat"] + p["npow"] + _dot(p["qmat"].astype(BF16), block_diag(p["npow"]))
        for lvl in range(1, len(same_blk)):
            for p in probs:
                cb = jnp.where(jnp.logical_and(same_blk[lvl], jnp.logical_not(same_blk[lvl - 1])),
                               p["lower"], 0.0)
                p["x"] = cb + _dot(p["qmat"].astype(BF16), block_diag(cb))
            for p in probs:
                p["qmat"] = p["qmat"] - p["x"] - _dot(p["x"].astype(BF16), block_diag(p["qmat"]))
        for p in probs:
            b, rows, h0, gc = p["b"], p["rows"], p["h0"], p["gc"]
            cols = slice(h0 * LANES, (h0 + grp) * LANES)
            idx = range(grp)
            eg = [jnp.exp(gc[i]) for i in idx]
            rhs_u = [p["v"][i] * p["beta"][i] for i in idx]
            rhs_w = [p["kb"][i] * eg[i] for i in idx]
            qb = p["qmat"].astype(BF16)
            qs = [p["q"][i] * (GDN_HEAD_DIM ** -0.5) for i in idx]
            g_last = [gc[i][chunk - 1:chunk, :] for i in idx]
            cat = lambda xs: jnp.concatenate(xs, axis=1)
            qg_ref[b, rows, cols] = cat([qs[i] * eg[i] for i in idx]).astype(qg_ref.dtype)
            kd_ref[b, rows, cols] = cat([p["k"][i] * jnp.exp(g_last[i] - gc[i]) for i in idx]).astype(kd_ref.dtype)
            u_ref[b, rows, cols] = cat(rhs_u) + _dot(qb, block_rows(rhs_u))
            w_ref[b, rows, cols] = (cat(rhs_w) + _dot(qb, block_rows(rhs_w))).astype(w_ref.dtype)
            at_ref[b, rows, h0 * chunk:(h0 + grp) * chunk] = (
                _dot_nt(cat(qs).astype(BF16), p["k_rows"]) * p["dmask"]).astype(at_ref.dtype)
            el_ref[b, pl.ds(pl.multiple_of(p["c"] * SUBLANES, SUBLANES), SUBLANES), cols] = cat(
                [jnp.broadcast_to(jnp.exp(g_last[i]), (SUBLANES, LANES)) for i in idx])
        return carry

    lax.fori_loop(0, tblk // (chunk * cpi), step, 0)


def _gdn_scan_kernel(qg_ref, kd_ref, u_ref, w_ref, at_ref, el_ref, z_ref, s0_ref, nw_ref,
                     o_ref, so_ref, s_scr, *, chunk):
    tb = pl.program_id(1)

    @pl.when(tb == 0)
    def _():
        s_scr[...] = s0_ref[...]

    nw = nw_ref[...]
    probs = [(b, h) for b in range(qg_ref.shape[0]) for h in range(GDN_HEADS)]

    def step(c, carry):
        rows = pl.ds(pl.multiple_of(c * chunk, chunk), chunk)
        erow = pl.ds(pl.multiple_of(c * SUBLANES, SUBLANES), SUBLANES)
        cols = lambda h: slice(h * LANES, (h + 1) * LANES)
        s_prev = [s_scr[b, h] for b, h in probs]
        sb = [s.astype(BF16) for s in s_prev]
        v_new = [u_ref[b, rows, cols(h)] - _dot(w_ref[b, rows, cols(h)].astype(BF16), sb[i])
                 for i, (b, h) in enumerate(probs)]
        vb = [v.astype(BF16) for v in v_new]
        for i, (b, h) in enumerate(probs):
            el = el_ref[b, erow, cols(h)][0:1, :]
            s_scr[b, h] = s_prev[i] * el + _dot_tn(kd_ref[b, rows, cols(h)].astype(BF16), vb[i])
        for i, (b, h) in enumerate(probs):
            attn = at_ref[b, rows, h * chunk:(h + 1) * chunk].astype(BF16)
            o = _dot(qg_ref[b, rows, cols(h)].astype(BF16), sb[i]) + _dot(attn, vb[i])
            ms = jnp.mean(o * o, axis=-1, keepdims=True)
            o_ref[b, rows, cols(h)] = (o * lax.rsqrt(ms + NORM_EPS) * nw * _silu(z_ref[b, rows, cols(h)])).astype(
                o_ref.dtype)
        return carry

    nchunk = qg_ref.shape[1] // chunk
    lax.fori_loop(0, nchunk, step, 0, unroll=min(GDN_SCAN_UNROLL, nchunk))

    @pl.when(tb == pl.num_programs(1) - 1)
    def _():
        so_ref[...] = s_scr[...]


def _gated_delta(cv, conv_done, z, ba, conv_hist, states, layer, conv_w, a_log, dt_bias, gdn_norm, nb, t):
    chunk = CHUNK if t % CHUNK == 0 else t
    tblk = min(t, 512)
    assert conv_done or tblk == t
    nchunk_blk = tblk // chunk
    hist_pad, cw_pad = _conv_operands(conv_hist, conv_w)
    ab = jnp.zeros((SUBLANES, LANES), F32)
    ab = ab.at[0, GDN_HEADS:2 * GDN_HEADS].set(a_log).at[1, GDN_HEADS:2 * GDN_HEADS].set(dt_bias)
    cv3 = cv.reshape(nb, t, CONV_DIM)
    pblk = _seqs_per_step(nb, t, 2)
    tok = lambda w: pl.BlockSpec((pblk, tblk, w), lambda b, i: (b, i, 0))
    full = lambda a: pl.BlockSpec(a.shape, lambda b, i: (0,) * a.ndim)
    el_spec =pl.BlockSpec((pblk, nchunk_blk * SUBLANES, GDN_W), lambda b, i: (b, i, 0))
    tok_shape = lambda w, dt=F32: jax.ShapeDtypeStruct((nb, t, w), dt)
    el_shape = jax.ShapeDtypeStruct((nb, (t // chunk) * SUBLANES, GDN_W), F32)
    opd = BF16 if chunk % (2 * SUBLANES) == 0 else F32
    qg, kd, u, w, attn, el = pl.pallas_call(
        functools.partial(_gdn_prep_kernel, chunk=chunk, cpi=min(GDN_PREP_CHUNKS_PER_ITER, nchunk_blk)),
        grid=(nb // pblk, t // tblk),
        in_specs=[tok(CONV_DIM), pl.BlockSpec((pblk, SUBLANES, CONV_DIM), lambda b, i: (b, 0, 0)),
                  tok(LANES), full(cw_pad), full(ab)],
        out_specs=[tok(GDN_W), tok(GDN_W), tok(GDN_W), tok(GDN_W), tok(GDN_HEADS * chunk), el_spec],
        out_shape=[tok_shape(GDN_W, opd), tok_shape(GDN_W, opd), tok_shape(GDN_W), tok_shape(GDN_W, opd),
                   tok_shape(GDN_HEADS * chunk, opd), el_shape],
        scratch_shapes=[pltpu.VMEM((pblk, tblk, LANES), F32), pltpu.VMEM((pblk, tblk, LANES), F32)] + (
            [] if conv_done else [pltpu.VMEM((pblk, CONV_DIM // LANES, tblk + SUBLANES, LANES), F32),
                                  pltpu.VMEM((pblk, tblk, CONV_DIM), F32)]),
        compiler_params=_cparams("parallel", "parallel"),
        name="gdn_prep",
    )(cv3, hist_pad, ba.reshape(nb, t, LANES), cw_pad, ab)

    bblk = _seqs_per_step(nb, t, GDN_SCAN_SEQS_PER_STEP)
    stok = lambda w: pl.BlockSpec((bblk, tblk, w), lambda b, i: (b, i, 0))
    sel_spec = pl.BlockSpec((bblk, nchunk_blk * SUBLANES, GDN_W), lambda b, i: (b, i, 0))
    st_in = pl.BlockSpec((None, bblk, GDN_HEADS, LANES, LANES), lambda b, i: (layer, b, 0, 0, 0))
    st = pl.BlockSpec((bblk, GDN_HEADS, LANES, LANES), lambda b, i: (b, 0, 0, 0))
    nw = gdn_norm.reshape(1, LANES)
    o, s_new = pl.pallas_call(
        functools.partial(_gdn_scan_kernel, chunk=chunk),
        grid=(nb // bblk, t // tblk),
        in_specs=[stok(GDN_W), stok(GDN_W), stok(GDN_W), stok(GDN_W), stok(GDN_HEADS * chunk), sel_spec,
                  stok(GDN_W), st_in, full(nw)],
        out_specs=[stok(GDN_W), st],
        out_shape=[tok_shape(GDN_W, _mixer_out_dtype(t)), jax.ShapeDtypeStruct((nb, GDN_HEADS, LANES, LANES), F32)],
        scratch_shapes=[pltpu.VMEM((bblk, GDN_HEADS, LANES, LANES), F32)],
        compiler_params=_cparams("parallel", "arbitrary"),
        name="gdn_scan",
    )(qg, kd, u, w, attn, el, z.reshape(nb, t, GDN_W), states, nw)
    return o.reshape(nb * t, GDN_W), s_new


FFN_TILE = 256


def _out_ffn_kernel(x_ref, oa_ref, orr_ref, oc_ref, g1_ref, sh2_ref, sc2_ref, g2_ref, nw_ref, wo_ref, wg_ref,
                    wu_ref, wd_ref, fn_ref, out_ref, *, final):
    mix = (_dot(oa_ref[...].astype(BF16), wo_ref[0:DSA_W, :])
           + _dot(orr_ref[...].astype(BF16), wo_ref[DSA_W:DSA_W + RET_W, :])
           + _dot(oc_ref[...].astype(BF16), wo_ref[DSA_W + RET_W:, :]))
    x1 = x_ref[...] + g1_ref[...] * mix
    hb = _rms_mod(x1, nw_ref[...], sc2_ref[...], sh2_ref[...]).astype(BF16)
    acc = None
    for j in range(wg_ref.shape[1] // FFN_TILE):
        cols = slice(j * FFN_TILE, (j + 1) * FFN_TILE)
        act = (_silu(_dot(hb, wg_ref[:, cols])) * _dot(hb, wu_ref[:, cols])).astype(BF16)
        down = _dot(act, wd_ref[cols, :])
        acc = down if acc is None else acc + down
    x2 = x1 + g2_ref[...] * acc
    if final:
        ms = jnp.mean(x2 * x2, axis=-1, keepdims=True)
        x2 = x2 * lax.rsqrt(ms + NORM_EPS) * fn_ref[...]
    out_ref[...] = x2


def _out_ffn(x2, t, oa, orr, oc, mod3, norm_w, w_out_b, wg_b, wu_b, wd_b, layer, final_norm, final, tm):
    m, d = x2.shape
    row = lambda w: pl.BlockSpec((tm, w), lambda i: (i, 0))
    const = lambda a: pl.BlockSpec(a.shape, lambda i: (0,) * a.ndim, pipeline_mode=pl.Buffered(1))
    wspec = lambda a: pl.BlockSpec((None,) + a.shape[1:], lambda i: (layer,) + (0,) * (a.ndim - 1),
                                   pipeline_mode=pl.Buffered(1))
    nw = norm_w.reshape(1, d)
    fn = final_norm.reshape(1, d)
    return pl.pallas_call(
        functools.partial(_out_ffn_kernel, final=final),
        grid=(m // tm,),
        in_specs=[row(d), row(DSA_W), row(RET_W), row(GDN_W),
                  _mod_spec(tm, t, d, 2), _mod_spec(tm, t, d, 3), _mod_spec(tm, t, d, 4), _mod_spec(tm, t, d, 5),
                  const(nw), wspec(w_out_b), wspec(wg_b), wspec(wu_b), wspec(wd_b), const(fn)],
        out_specs=row(d),
        out_shape=jax.ShapeDtypeStruct((m, d), F32),
        compiler_params=_cparams("parallel"),
        name="out_ffn",
    )(x2, oa, orr, oc, mod3, mod3, mod3, mod3, nw, w_out_b, wg_b, wu_b, wd_b, fn)


ROPE_SPLIT = 64


def _rope_tables(pos0, t, inv_freq):
    reps = LANES // (2 * inv_freq.shape[0])
    inv_lane = jnp.tile(jnp.concatenate([inv_freq, inv_freq]), reps)[None, :]
    sign_lane = jnp.tile(jnp.concatenate([-jnp.ones_like(inv_freq), jnp.ones_like(inv_freq)]), reps)[None, :]
    if t % ROPE_SPLIT:
        ang = (pos0 + jnp.arange(t, dtype=jnp.int32)).astype(F32)[:, None] * inv_lane
        return jnp.cos(ang), jnp.sin(ang) * sign_lane
    coarse = (pos0 + ROPE_SPLIT * jnp.arange(t // ROPE_SPLIT, dtype=jnp.int32)).astype(F32)[:, None] * inv_lane
    fine = jnp.arange(ROPE_SPLIT, dtype=jnp.int32).astype(F32)[:, None] * inv_lane
    cc, sc = jnp.cos(coarse)[:, None, :], jnp.sin(coarse)[:, None, :]
    cf, sf = jnp.cos(fine)[None], jnp.sin(fine)[None]
    cos = (cc * cf - sc * sf).reshape(t, LANES)
    sin = (sc * cf + cc * sf).reshape(t, LANES)
    return cos, sin * sign_lane


def _trunk(x, modp, pos0, k_hist, v_hist, s_ret, s_gdn, conv_hist, wts):
    (norm_mix, norm_ffn, w_in_b, w_ba_b, ret_norm, conv_w, a_log, dt_bias, gdn_norm, w_out_b, wg_b, wu_b, wd_b,
     final_norm) = wts
    nb, t, d = x.shape
    m = nb * t
    depth = w_in_b.shape[0]
    tm = min(TOKEN_TILE, m)
    inv_a = 1.0 / (ROPE_THETA ** (jnp.arange(0, DSA_HEAD_DIM, 2, dtype=F32) / DSA_HEAD_DIM))
    inv_r = 1.0 / (10000.0 ** jnp.linspace(0.0, 1.0, RET_HEAD_DIM // 2, dtype=F32))
    tabs = _rope_tables(pos0, t, inv_a) + _rope_tables(pos0, t, inv_r)
    if t < tm:
        tabs = tuple(jnp.tile(a, (tm // t, 1)) for a in tabs)
    x2 = x.reshape(m, d)
    ks, vs, rs, gs, cs = [], [], [], [], []
    for l in range(depth):
        if t >= tm:
            mod3 = modp[l].reshape(nb, 1, 6 * d)
        else:
            mod3 = jnp.repeat(modp[l], t, axis=0).reshape(m // tm, tm, 6 * d)
        (qa, ka, va, qr, kr, vr, gr, cv, z, ba), fused = _inproj(
            x2, t, mod3, norm_mix[l], w_in_b, w_ba_b, l, tabs, tm, conv_hist[l], conv_w[l])
        nhist = CONV_WIDTH - 1
        keep = min(t, DSA_MAX_WINDOW)
        if fused is None:
            cvn = jnp.concatenate([conv_hist[l], cv.reshape(nb, t, CONV_DIM)], axis=1)[:, -nhist:]
            window = lambda a: a.reshape(nb, t, DSA_W)[:, t - keep:].reshape(nb, keep, DSA_HEADS, DSA_HEAD_DIM)
            kwin, vwin = window(ka), window(va)
        else:
            conv_tail, k_t, v_t = fused
            cvn = conv_tail[:, SUBLANES - nhist:]
            untransposed = lambda a: a.reshape(nb, DSA_HEADS, DSA_HEAD_DIM, keep).transpose(0, 3, 1, 2)
            kwin, vwin = untransposed(k_t), untransposed(v_t)
        if k_hist is None:
            oa = _dsa_prompt(qa, ka, va, nb, t)
        else:
            oa = _dsa_step(qa, ka, va, k_hist, v_hist, l, nb, t)
        orr, sr = _retention(qr, kr, vr, gr, _to_block_diag(s_ret[l]), ret_norm[l], nb, t)
        oc, sg = _gated_delta(cv, fused is not None, z, ba, conv_hist[l], s_gdn, l, conv_w[l], a_log[l],
                              dt_bias[l], gdn_norm[l], nb, t)
        x2 = _out_ffn(x2, t, oa, orr, oc, mod3, norm_ffn[l], w_out_b, wg_b, wu_b, wd_b, l,
                      final_norm, l == depth - 1, tm)
        ks.append(kwin)
        vs.append(vwin)
        rs.append(_from_block_diag(sr))
        gs.append(sg)
        cs.append(cvn)
    return (x2.reshape(nb, t, d), jnp.stack(ks), jnp.stack(vs), jnp.stack(rs), jnp.stack(gs), jnp.stack(cs))


def kernel(x_prompt, x_sample, cache_win_k, cache_win_v, state_ret, state_gdn, state_conv, c_prompt, c_sample, ada_w, ada_b, norm_mix, norm_ffn, w_in, ret_norm, conv_w, a_log, dt_bias, gdn_norm, w_out, w_gate, w_up, w_down, final_norm):
    nb, t_p, d = x_prompt.shape
    db, t_s, _ = x_sample.shape
    depth = ada_w.shape[0]
    rows = nb + db
    rows_pad = -(-rows // SUBLANES) * SUBLANES
    c_all = jnp.concatenate([c_prompt, c_sample, jnp.zeros((rows_pad - rows, d), F32)], axis=0)
    mod = _modulation(c_all, ada_w, ada_b)
    w_ba_b = jnp.pad(w_in[:, :, _C_BA:], ((0, 0), (0, 0), (0, LANES - (IN_COLS - _C_BA)))).astype(BF16)
    wts = (norm_mix, norm_ffn, w_in.astype(BF16), w_ba_b, ret_norm, conv_w, a_log, dt_bias, gdn_norm, w_out.astype(BF16),
           w_gate.astype(BF16), w_up.astype(BF16), w_down.astype(BF16), final_norm)

    zr = jnp.zeros((depth, nb, RET_HEADS, RET_HEAD_DIM, RET_HEAD_DIM), F32)
    zg = jnp.zeros((depth, nb, GDN_HEADS, GDN_HEAD_DIM, GDN_HEAD_DIM), F32)
    zc = jnp.zeros((depth, nb, CONV_WIDTH - 1, CONV_DIM), F32)
    y_p, kp, vp, rp, gp, cp = _trunk(x_prompt, mod[:, :nb], 0,
                                     None, None, zr, zg, zc, wts)
    wb = cache_win_k.shape[2]
    y_s, ks, vs, rs, gs, cs = _trunk(x_sample, mod[:, nb:rows], PAST_LEN,
                                     cache_win_k.reshape(depth, db, wb, DSA_W).transpose(0, 1, 3, 2),
                                     cache_win_v.reshape(depth, db, wb, DSA_W).transpose(0, 1, 3, 2),
                                     state_ret, state_gdn, state_conv, wts)
    return (y_p, y_s, kp, vp, rp, gp, cp, ks, vs, rs, gs, cs)
```

```python
import functools
import math

import jax
import jax.numpy as jnp
from jax import lax
from jax.experimental import pallas as pl
from jax.experimental.pallas import tpu as pltpu

F32 = jnp.float32
BF16 = jnp.bfloat16
HIGHEST = lax.Precision.HIGHEST

DSA_HEAD_DIM = 64
DSA_HEADS = 4
DSA_PATTERNS = ((128, 1), (512, 4), (2048, 16))
DSA_MAX_WINDOW = 2048
ROPE_THETA = 10000.0
RET_HEAD_DIM = 64
RET_HEADS = 4
GDN_HEAD_DIM = 128
GDN_HEADS = 4
CONV_WIDTH = 4
CHUNK = 64
NORM_EPS = 1e-6
PAST_LEN = 16384
LOG2_E = math.log2(math.e)

DSA_W = DSA_HEADS * DSA_HEAD_DIM
RET_W = RET_HEADS * RET_HEAD_DIM
GDN_W = GDN_HEADS * GDN_HEAD_DIM
CONV_DIM = 3 * GDN_W
LANES = 128
SUBLANES = 8
VMEM_LIMIT = 56 * 1024 * 1024
TOKEN_TILE = 512

_C_QA, _C_KA, _C_VA = 0, DSA_W, 2 * DSA_W
_C_QR = 3 * DSA_W
_C_KR, _C_VR, _C_GR = _C_QR + RET_W, _C_QR + 2 * RET_W, _C_QR + 3 * RET_W
_C_CV = _C_QR + 4 * RET_W
_C_Z = _C_CV + CONV_DIM
_C_BA = _C_Z + GDN_W
IN_COLS = _C_BA + 2 * GDN_HEADS


def _cparams(*sem):
    return pltpu.CompilerParams(dimension_semantics=sem, vmem_limit_bytes=VMEM_LIMIT)


def _dot(a, b, **kw):
    return jnp.dot(a, b, preferred_element_type=F32, **kw)


def _dot_nt(a, b, **kw):
    return lax.dot_general(a, b, (((1,), (1,)), ((), ())), preferred_element_type=F32, **kw)


def _dot_tn(a, b, **kw):
    return lax.dot_general(a, b, (((0,), (0,)), ((), ())), preferred_element_type=F32, **kw)


def _silu(x):
    return x * jax.nn.sigmoid(x)


def _mixer_out_dtype(t):
    return BF16 if t % (2 * SUBLANES) == 0 else F32


def _seqs_per_step(nb, t, base):
    return math.gcd(nb, base * (4 if t < CHUNK else 1))


def _mod_kernel(c_ref, w_ref, b_ref, o_ref):
    a = _silu(c_ref[...]).astype(BF16)
    o_ref[...] = _dot(a, w_ref[...].astype(BF16)) + b_ref[...]


def _modulation(c_all, ada_w, ada_b, tn=1536):
    depth, d, n = ada_w.shape
    bp = c_all.shape[0]
    return pl.pallas_call(
        _mod_kernel,
        grid=(depth, n // tn),
        in_specs=[
            pl.BlockSpec((bp, d), lambda l, j: (0, 0)),
            pl.BlockSpec((None, d, tn), lambda l, j: (l, 0, j)),
            pl.BlockSpec((None, 1, tn), lambda l, j: (l, 0, j)),
        ],
        out_specs=pl.BlockSpec((None, bp, tn), lambda l, j: (l, 0, j)),
        out_shape=jax.ShapeDtypeStruct((depth, bp, n), F32),
        compiler_params=_cparams("parallel", "parallel"),
        name="modulation",
    )(c_all, ada_w, ada_b.reshape(depth, 1, n))


def _rms_mod(x, nw, sc, sh):
    ms = jnp.mean(x * x, axis=-1, keepdims=True)
    return (x * lax.rsqrt(ms + NORM_EPS) * nw) * (1.0 + sc) + sh


def _causal_conv_silu(xp_scr, seq, cw_ref, cg, nrows):
    cols = slice(cg * LANES, (cg + 1) * LANES)
    acc = None
    for i in range(CONV_WIDTH):
        start = SUBLANES - (CONV_WIDTH - 1) + i
        term = xp_scr[seq, cg, pl.ds(start, nrows, stride=1), :] * cw_ref[i:i + 1, cols]
        acc = term if acc is None else acc + term
    return _silu(acc)


def _inproj_kernel(x_ref, nw_ref, sh_ref, sc_ref, w_ref, wba_ref, ca_ref, sa_ref, cr_ref, sr_ref, *rest,
                   tiles_per_seq):
    if tiles_per_seq:
        hist_ref, cw_ref = rest[:2]
        rest = rest[2:]
    qa_ref, ka_ref, va_ref, qr_ref, kr_ref, vr_ref, gr_ref, cv_ref, z_ref, ba_ref = rest[:10]
    tm = x_ref.shape[0]
    if tiles_per_seq:
        tail_ref, kt_ref, vt_ref, xp_scr = rest[10:]

        @pl.when(pl.program_id(0) % tiles_per_seq == 0)
        def _():
            tail_ref[...] = hist_ref[...]

    hb = _rms_mod(x_ref[...], nw_ref[...], sc_ref[...], sh_ref[...]).astype(BF16)

    def proj(c0, width):
        return _dot(hb, w_ref[:, c0:c0 + width])

    lane = lax.broadcasted_iota(jnp.int32, (tm, DSA_W), 1)
    first_half = (lane % DSA_HEAD_DIM) < (DSA_HEAD_DIM // 2)

    def rope(y, cos, sin_signed):
        partner = jnp.where(first_half, pltpu.roll(y, DSA_W - DSA_HEAD_DIM // 2, 1),
                            pltpu.roll(y, DSA_HEAD_DIM // 2, 1))
        return y * cos + partner * sin_signed

    wide = lambda ref: jnp.concatenate([ref[...]] * (DSA_W // LANES), axis=1)
    ca, sa, cr, sr = wide(ca_ref), wide(sa_ref), wide(cr_ref), wide(sr_ref)
    qa_ref[...] = rope(proj(_C_QA, DSA_W), ca, sa) * (DSA_HEAD_DIM ** -0.5 * LOG2_E)
    ka = rope(proj(_C_KA, DSA_W), ca, sa)
    va = proj(_C_VA, DSA_W)
    ka_ref[...] = ka
    va_ref[...] = va
    if tiles_per_seq:
        kt_ref[...] = ka.T
        vt_ref[...] = va.T

    qr_ref[...] = rope(proj(_C_QR, RET_W), cr, sr)
    kr_ref[...] = rope(proj(_C_KR, RET_W), cr, sr) * (RET_HEAD_DIM ** -0.5)
    vr_ref[...] = proj(_C_VR, RET_W)
    gr_ref[...] = proj(_C_GR, RET_W)
    for s in range(CONV_DIM // GDN_W):
        y = proj(_C_CV + s * GDN_W, GDN_W)
        if not tiles_per_seq:
            cv_ref[:, s * GDN_W:(s + 1) * GDN_W] = y
            continue
        for j in range(GDN_W // LANES):
            cg = s * (GDN_W // LANES) + j
            cols = slice(cg * LANES, (cg + 1) * LANES)
            yj = y[:, j * LANES:(j + 1) * LANES]
            xp_scr[0, cg, :SUBLANES, :] = tail_ref[:, cols]
            xp_scr[0, cg, SUBLANES:, :] = yj
            cv_ref[:, cols] = _causal_conv_silu(xp_scr, 0, cw_ref, cg, tm)
            tail_ref[:, cols] = yj[tm - SUBLANES:, :]
    z_ref[...] = proj(_C_Z, GDN_W)
    ba_ref[...] = _dot(hb, wba_ref[...])


def _mod_spec(tm, t, d, col):
    if t >= tm:
        return pl.BlockSpec((None, 1, d), lambda i: ((i * tm) // t, 0, col))
    return pl.BlockSpec((None, tm, d), lambda i: (i, 0, col))


def _conv_operands(conv_hist, conv_w):
    nb = conv_hist.shape[0]
    hist_pad = jnp.concatenate([jnp.zeros((nb, SUBLANES - (CONV_WIDTH - 1), CONV_DIM), F32), conv_hist], axis=1)
    cw_pad = jnp.concatenate([conv_w, jnp.zeros((SUBLANES - CONV_WIDTH, CONV_DIM), F32)], axis=0)
    return hist_pad, cw_pad


def _inproj(x2, t, mod3, norm_w, w_in_b, w_ba_b, layer, tabs, tm, conv_hist, conv_w):
    m, d = x2.shape
    nt = tabs[0].shape[0] // tm
    widths = (DSA_W,) * 3 + (RET_W,) * 4 + (CONV_DIM, GDN_W, LANES)
    tab_spec = pl.BlockSpec((tm, LANES), lambda i: (i % nt, 0))
    tiles_per_seq = t // tm if t % tm == 0 else 0
    window_tiles = min(t, DSA_MAX_WINDOW) // tm
    in_specs = [
        pl.BlockSpec((tm, d), lambda i: (i, 0)),
        pl.BlockSpec((1, d), lambda i: (0, 0)),
        _mod_spec(tm, t, d, 0),
        _mod_spec(tm, t, d, 1),
        pl.BlockSpec((None, d, IN_COLS), lambda i: (layer, 0, 0), pipeline_mode=pl.Buffered(1)),
        pl.BlockSpec((None, d, LANES), lambda i: (layer, 0, 0), pipeline_mode=pl.Buffered(1)),
        tab_spec, tab_spec, tab_spec, tab_spec,
    ]
    operands = [x2, norm_w.reshape(1, d), mod3, mod3, w_in_b, w_ba_b, *tabs]
    out_specs = [pl.BlockSpec((tm, w), lambda i: (i, 0)) for w in widths]
    out_shape = [jax.ShapeDtypeStruct((m, w), F32) for w in widths]
    scratch = []
    if tiles_per_seq:
        hist_pad, cw_pad = _conv_operands(conv_hist, conv_w)
        seq_tail = pl.BlockSpec((None, SUBLANES, CONV_DIM), lambda i: (i // tiles_per_seq, 0, 0))
        in_specs += [seq_tail, pl.BlockSpec(cw_pad.shape, lambda i: (0, 0))]
        operands += [hist_pad, cw_pad]
        out_specs.append(seq_tail)
        out_shape.append(jax.ShapeDtypeStruct((m // t, SUBLANES, CONV_DIM), F32))
        first_win = tiles_per_seq - window_tiles
        win_t = pl.BlockSpec((None, DSA_W, tm), lambda i: (i // tiles_per_seq, 0,
                                                            jnp.maximum(i % tiles_per_seq - first_win, 0)))
        out_specs += [win_t, win_t]
        out_shape += [jax.ShapeDtypeStruct((m // t, DSA_W, window_tiles * tm), F32)] * 2
        scratch.append(pltpu.VMEM((1, CONV_DIM // LANES, tm + SUBLANES, LANES), F32))
    outs = pl.pallas_call(
        functools.partial(_inproj_kernel, tiles_per_seq=tiles_per_seq),
        grid=(m // tm,),
        in_specs=in_specs,
        out_specs=out_specs,
        out_shape=out_shape,
        scratch_shapes=scratch,
        compiler_params=_cparams("arbitrary"),
        name="inproj",
    )(*operands)
    return (outs[:10], outs[10:]) if tiles_per_seq else (outs, None)


DSA_BLK = 128


DSA_QBLK = DSA_MAX_WINDOW
DSA_GROUP = 4


def _dsa_kernel(q_ref, kp_ref, kc_ref, vp_ref, vc_ref, o_ref, acc_scr, m_scr, l_scr):
    blk = pl.program_id(2)
    qblk = q_ref.shape[0]
    row = lax.broadcasted_iota(jnp.int32, (2 * DSA_BLK, 2 * DSA_BLK), 0) % DSA_BLK
    col = lax.broadcasted_iota(jnp.int32, (2 * DSA_BLK, 2 * DSA_BLK), 1)
    ok = jnp.logical_and(col >= row, col <= row + DSA_BLK)
    ok_first = jnp.logical_and(ok, jnp.logical_or(col >= DSA_BLK, blk > 0))
    lo = lax.broadcasted_iota(jnp.int32, (DSA_BLK, LANES), 1) < DSA_HEAD_DIM
    neg = -jnp.inf
    halves = lambda x: jnp.where(lo, x[:DSA_BLK], x[DSA_BLK:])

    def rows_of(dil, start, n):
        return pl.ds(start, n) if dil == 1 else pl.ds(start, n, stride=dil)

    def group_softmax(dil, subs):
        n = range(len(subs))
        idx = [rows_of(dil, ph + dil * DSA_BLK * j, DSA_BLK) for ph, j in subs]
        q = [q_ref[i, :] for i in idx]
        k, v = [], []
        for u, (ph, j) in enumerate(subs):
            if j == 0:
                band = rows_of(dil, qblk - dil * DSA_BLK + ph, DSA_BLK)
                k.append(jnp.concatenate([kp_ref[band, :], kc_ref[idx[u], :]], axis=0).astype(BF16))
                v.append(jnp.concatenate([vp_ref[band, :], vc_ref[idx[u], :]], axis=0).astype(BF16))
            else:
                both = rows_of(dil, ph + dil * DSA_BLK * (j - 1), 2 * DSA_BLK)
                k.append(kc_ref[both, :].astype(BF16))
                v.append(vc_ref[both, :].astype(BF16))
        q2 = [jnp.concatenate([jnp.where(lo, q[u], 0.0), jnp.where(lo, 0.0, q[u])], axis=0).astype(BF16)
              for u in n]
        s = [jnp.where(ok_first if subs[u][1] == 0 else ok, _dot_nt(q2[u], k[u]), neg) for u in n]
        mx = [jnp.max(s[u], axis=-1, keepdims=True) for u in n]
        p = [jnp.exp2(s[u] - mx[u]) for u in n]
        den = [jnp.sum(p[u], axis=-1, keepdims=True) for u in n]
        pv = [_dot(p[u].astype(BF16), v[u]) for u in n]
        return idx, [(halves(pv[u]), halves(jnp.broadcast_to(mx[u], (2 * DSA_BLK, LANES))),
                      halves(jnp.broadcast_to(den[u], (2 * DSA_BLK, LANES)))) for u in n]

    dils = sorted((d for _, d in DSA_PATTERNS), reverse=True)
    for pi, dil in enumerate(dils):
        subs = [(ph, j) for j in range(qblk // (dil * DSA_BLK)) for ph in range(dil)]
        for g0 in range(0, len(subs), DSA_GROUP):
            idx, tiles = group_softmax(dil, subs[g0:g0 + DSA_GROUP])
            for i, (pv, mx, den) in zip(idx, tiles):
                if pi > 0:
                    m_old = m_scr[i, :]
                    m_new = jnp.maximum(m_old, mx)
                    w_old = jnp.exp2(m_old - m_new)
                    w_cur = jnp.exp2(mx - m_new)
                    pv = acc_scr[i, :] * w_old + pv * w_cur
                    den = l_scr[i, :] * w_old + den * w_cur
                    mx = m_new
                if pi < len(dils) - 1:
                    acc_scr[i, :] = pv
                    m_scr[i, :] = mx
                    l_scr[i, :] = den
                else:
                    o_ref[i, :] = (pv / den).astype(o_ref.dtype)


def _dsa_prompt(q, k, v, nb, t):
    assert all(w // d == DSA_BLK for w, d in DSA_PATTERNS) and t % DSA_QBLK == 0
    ngrp = DSA_W // LANES
    r3 = lambda a: a.reshape(nb, t, DSA_W)
    cur = pl.BlockSpec((None, DSA_QBLK, LANES), lambda b, g, i: (b, i, g))
    prv = pl.BlockSpec((None, DSA_QBLK, LANES), lambda b, g, i: (b, jnp.maximum(i - 1, 0), g))
    return pl.pallas_call(
        _dsa_kernel,
        grid=(nb, ngrp, t // DSA_QBLK),
        in_specs=[cur, prv, cur, prv, cur],
        out_specs=cur,
        out_shape=jax.ShapeDtypeStruct((nb, t, DSA_W), _mixer_out_dtype(t)),
        scratch_shapes=[pltpu.VMEM((DSA_QBLK, LANES), F32)] * 3,
        compiler_params=_cparams("parallel", "parallel", "parallel"),
        name="dsa_prompt",
    )(r3(q), r3(k), r3(k), r3(v), r3(v)).reshape(nb * t, DSA_W)


DSA_STEP_SEQS = 2


def _multiplicity(dist):
    total = jnp.zeros(dist.shape, F32)
    for window, dil in DSA_PATTERNS:
        hit = (dist >= 0) & (dist <= window) & ((dist & (dil - 1)) == 0)
        total = total + hit.astype(F32)
    return total


def _dsa_step_kernel(q_ref, kc_ref, vc_ref, kn_ref, vn_ref, o_ref):
    nseq, tq = q_ref.shape[0], q_ref.shape[1]
    wb = kc_ref.shape[2]
    qi = lax.broadcasted_iota(jnp.int32, (2 * tq, wb), 0) % tq
    w_c = _multiplicity(wb + qi - lax.broadcasted_iota(jnp.int32, (2 * tq, wb), 1))
    qn = lax.broadcasted_iota(jnp.int32, (2 * tq, LANES), 0) % tq
    nn = lax.broadcasted_iota(jnp.int32, (2 * tq, LANES), 1)
    w_n = jnp.where(nn < tq, _multiplicity(qn - nn), 0.0)
    lo = lax.broadcasted_iota(jnp.int32, (tq, LANES), 1) < DSA_HEAD_DIM
    pad = jnp.zeros((LANES - tq, LANES), F32)
    neg = -jnp.inf
    for b, g in [(b, g) for b in range(nseq) for g in range(DSA_W // LANES)]:
        cols = slice(g * LANES, (g + 1) * LANES)
        q = q_ref[b, :, cols]
        q2 = jnp.concatenate([jnp.where(lo, q, 0.0), jnp.where(lo, 0.0, q)], axis=0).astype(BF16)
        kt_c, vt_c = kc_ref[b, cols, :].astype(BF16), vc_ref[b, cols, :].astype(BF16)
        k_n = jnp.concatenate([kn_ref[b, :, cols], pad], axis=0).astype(BF16)
        v_n = jnp.concatenate([vn_ref[b, :, cols], pad], axis=0).astype(BF16)
        s_c = jnp.where(w_c > 0, _dot(q2, kt_c), neg)
        s_n = jnp.where(w_n > 0, _dot_nt(q2, k_n), neg)
        mx = jnp.maximum(jnp.max(s_c, axis=-1, keepdims=True), jnp.max(s_n, axis=-1, keepdims=True))
        p_c = w_c * jnp.exp2(s_c - mx)
        p_n = w_n * jnp.exp2(s_n - mx)
        den = jnp.sum(p_c, axis=-1, keepdims=True) + jnp.sum(p_n, axis=-1, keepdims=True)
        o2 = (_dot_nt(p_c.astype(BF16), vt_c) + _dot(p_n.astype(BF16), v_n)) / den
        o_ref[b, :, cols] = jnp.where(lo, o2[:tq], o2[tq:])


def _dsa_step(q, k_new, v_new, k_cache, v_cache, layer, nb, t):
    wb = k_cache.shape[3]
    bblk = math.gcd(nb, DSA_STEP_SEQS)
    new = pl.BlockSpec((bblk, t, DSA_W), lambda b: (b, 0, 0))
    cache = pl.BlockSpec((None, bblk, DSA_W, wb), lambda b: (layer, b, 0, 0))
    r3 = lambda a: a.reshape(nb, t, DSA_W)
    return pl.pallas_call(
        _dsa_step_kernel,
        grid=(nb // bblk,),
        in_specs=[new, cache, cache, new, new],
        out_specs=new,
        out_shape=jax.ShapeDtypeStruct((nb, t, DSA_W), F32),
        compiler_params=_cparams("parallel"),
        name="dsa_step",
    )(r3(q), k_cache, v_cache, r3(k_new), r3(v_new)).reshape(nb * t, DSA_W)


RET_CHUNK = 256
RET_SEQS_PER_STEP = 2
RET_UNROLL = 4


def _ret_kernel(q_ref, k_ref, v_ref, g_ref, s0_ref, dec_ref, qd_ref, kd_ref, cd_ref, bd_ref, nw_ref,
                o_ref, so_ref, s_scr, *, chunk):
    tb = pl.program_id(1)
    ngrp = RET_W // LANES

    @pl.when(tb == 0)
    def _():
        s_scr[...] = s0_ref[...]

    lo = lax.broadcasted_iota(jnp.int32, (chunk, LANES), 1) < RET_HEAD_DIM
    nw = nw_ref[...]
    bd = bd_ref[...]

    def body(c, carry):
        rows = pl.ds(pl.multiple_of(c * chunk, chunk), chunk)
        for b in range(q_ref.shape[0]):
            for g in range(ngrp):
                cols = slice(g * LANES, (g + 1) * LANES)
                q, k, v = q_ref[b, rows, cols], k_ref[b, rows, cols], v_ref[b, rows, cols]
                kb, vb = k.astype(BF16), v.astype(BF16)
                parts = []
                for hh in range(2):
                    qm = jnp.where(lo if hh == 0 else jnp.logical_not(lo), q, 0.0).astype(BF16)
                    inner = _dot_nt(qm, kb) * dec_ref[2 * g + hh]
                    parts.append(_dot(inner.astype(BF16), vb))
                s_prev = s_scr[b, g]
                o = jnp.where(lo, parts[0], parts[1]) + _dot(q.astype(BF16), s_prev.astype(BF16)) * qd_ref[g]
                s_scr[b, g] = s_prev * cd_ref[g] + bd * _dot_tn((k * kd_ref[g]).astype(BF16), vb)
                o2 = o * o
                ms = jnp.where(lo, jnp.sum(jnp.where(lo, o2, 0.0), axis=-1, keepdims=True),
                               jnp.sum(jnp.where(lo, 0.0, o2), axis=-1, keepdims=True)) * (1.0 / RET_HEAD_DIM)
                o_ref[b, rows, cols] = (o * lax.rsqrt(ms + NORM_EPS) * nw * _silu(g_ref[b, rows, cols])).astype(
                    o_ref.dtype)
        return carry

    nchunk = q_ref.shape[1] // chunk
    lax.fori_loop(0, nchunk, body, 0, unroll=min(RET_UNROLL, nchunk))

    @pl.when(tb == pl.num_programs(1) - 1)
    def _():
        so_ref[...] = s_scr[...]


def _ret_tables(chunk):
    log_gamma = jnp.log(1.0 - 2.0 ** (-5.0 - jnp.arange(RET_HEADS, dtype=F32)))
    i = jnp.arange(chunk, dtype=F32)
    diff = i[:, None] - i[None, :]
    causal = diff >= 0
    decay = jnp.where(causal[None], jnp.exp(log_gamma[:, None, None] * jnp.where(causal, diff, 0.0)[None]), 0.0)
    per_lane = lambda a: jnp.repeat(a, RET_HEAD_DIM, axis=0).reshape(RET_W // LANES, LANES, -1)
    q_dec = per_lane(jnp.exp(log_gamma[:, None] * (i[None, :] + 1.0))).transpose(0, 2, 1)
    k_dec = per_lane(jnp.exp(log_gamma[:, None] * (chunk - 1.0 - i)[None, :])).transpose(0, 2, 1)
    c_dec = jnp.broadcast_to(per_lane(jnp.exp(log_gamma * chunk)[:, None]), (RET_W // LANES, LANES, LANES))
    head_of = jnp.arange(LANES) // RET_HEAD_DIM
    block_diag = (head_of[:, None] == head_of[None, :]).astype(F32)
    return decay, q_dec, k_dec, c_dec, block_diag


def _retention(q, k, v, gate, state_bd, ret_norm, nb, t):
    chunk = RET_CHUNK if t % RET_CHUNK == 0 else t
    tblk = min(t, 1024)
    ngrp = RET_W // LANES
    bblk = _seqs_per_step(nb, t, RET_SEQS_PER_STEP)
    decay, q_dec, k_dec, c_dec, block_diag = _ret_tables(chunk)
    r3 = lambda a: a.reshape(nb, t, RET_W)
    tok = pl.BlockSpec((bblk, tblk, RET_W), lambda b, i: (b, i, 0))
    st = pl.BlockSpec((bblk, ngrp, LANES, LANES), lambda b, i: (b, 0, 0, 0))
    full = lambda a: pl.BlockSpec(a.shape, lambda b, i: (0,) * a.ndim)
    nw = jnp.tile(ret_norm, LANES // RET_HEAD_DIM).reshape(1, LANES)
    o, s_new = pl.pallas_call(
        functools.partial(_ret_kernel, chunk=chunk),
        grid=(nb // bblk, t // tblk),
        in_specs=[tok, tok, tok, tok, st, full(decay), full(q_dec), full(k_dec), full(c_dec),
                  full(block_diag), full(nw)],
        out_specs=[tok, st],
        out_shape=[jax.ShapeDtypeStruct((nb, t, RET_W), _mixer_out_dtype(t)),
                   jax.ShapeDtypeStruct((nb, ngrp, LANES, LANES), F32)],
        scratch_shapes=[pltpu.VMEM((bblk, ngrp, LANES, LANES), F32)],
        compiler_params=_cparams("parallel", "arbitrary"),
        name="retention",
    )(r3(q), r3(k), r3(v), r3(gate), state_bd, decay, q_dec, k_dec, c_dec, block_diag, nw)
    return o.reshape(nb * t, RET_W), s_new


def _to_block_diag(s):
    nb = s.shape[0]
    s = s.reshape(nb, 2, 2, RET_HEAD_DIM, RET_HEAD_DIM)
    z = jnp.zeros_like(s[:, :, 0])
    top = jnp.concatenate([s[:, :, 0], z], axis=-1)
    bot = jnp.concatenate([z, s[:, :, 1]], axis=-1)
    return jnp.concatenate([top, bot], axis=-2)


def _from_block_diag(s):
    h = RET_HEAD_DIM
    return jnp.stack([s[:, :, :h, :h], s[:, :, h:, h:]], axis=2).reshape(s.shape[0], RET_HEADS, h, h)


GDN_PREP_CHUNKS_PER_ITER = 8
GDN_SCAN_SEQS_PER_STEP = 2
GDN_SCAN_UNROLL = 8


def _softplus(x):
    return jnp.maximum(x, 0.0) + jnp.log1p(jnp.exp(-jnp.abs(x)))


def _gdn_prep_kernel(cv_ref, hist_ref, ba_ref, cw_ref, ab_ref,
                     qg_ref, kd_ref, u_ref, w_ref, at_ref, el_ref, beta_scr, g_scr, *conv_scr, chunk, cpi):
    nseq, tblk = cv_ref.shape[0], cv_ref.shape[1]
    ncg = CONV_DIM // LANES
    ab = ab_ref[...]
    cs_scr = cv_ref
    for b in range(nseq):
        if conv_scr:
            xp_scr, cs_scr = conv_scr
            for cg in range(ncg):
                cols = slice(cg * LANES, (cg + 1) * LANES)
                xp_scr[b, cg, :SUBLANES, :] = hist_ref[b, :, cols]
                xp_scr[b, cg, SUBLANES:, :] = cv_ref[b, :, cols]
                cs_scr[b, :, cols] = _causal_conv_silu(xp_scr, b, cw_ref, cg, tblk)

        ba = ba_ref[b]
        beta_scr[b] = jax.nn.sigmoid(ba)
        g_scr[b] = -jnp.exp(ab[0:1, :]) * _softplus(ba + ab[1:2, :]) * LOG2_E

    grp = 2 if 2 * chunk == LANES else 1
    width = grp * chunk
    ri = lax.broadcasted_iota(jnp.int32, (chunk, width), 0)
    lane = lax.broadcasted_iota(jnp.int32, (chunk, width), 1)
    ci = lane % chunk
    first = lane < chunk
    incl = ri >= ci
    strict = ri > ci
    tri = (lax.broadcasted_iota(jnp.int32, (chunk, chunk), 0)
           >= lax.broadcasted_iota(jnp.int32, (chunk, chunk), 1)).astype(F32)
    lane_pick = (lax.broadcasted_iota(jnp.int32, (SUBLANES, LANES), 0)
                 == lax.broadcasted_iota(jnp.int32, (SUBLANES, LANES), 1)).astype(F32)
    base = min(SUBLANES, chunk)
    assert chunk % base == 0 and (chunk // base) & (chunk // base - 1) == 0
    same_blk = [(ri >> sh) == (ci >> sh) for sh in range(int(math.log2(base)), int(math.log2(chunk)) + 1)]

    def l2n(x):
        return x * lax.rsqrt(jnp.sum(x * x, axis=-1, keepdims=True) + NORM_EPS)

    def side_by_side(per_head):
        if grp == 1:
            return per_head[0][:, :width]
        return jnp.where(first, per_head[0][:, :width], per_head[1][:, :width])

    def block_diag(y):
        if grp == 1:
            return y.astype(BF16)
        return jnp.concatenate([jnp.where(first, y, 0.0), jnp.where(first, 0.0, y)], axis=0).astype(BF16)

    def block_rows(per_head):
        if grp == 1:
            return per_head[0].astype(BF16)
        a, b = per_head
        return jnp.concatenate([jnp.concatenate([a, jnp.zeros_like(b)], axis=1),
                                jnp.concatenate([jnp.zeros_like(a), b], axis=1)], axis=0).astype(BF16)

    def step(it, carry):
        probs = []
        for b, cc in [(b, cc) for b in range(nseq) for cc in range(cpi)]:
            c = it * cpi + cc
            rows = pl.ds(pl.multiple_of(c * chunk, chunk), chunk)
            beta_c = beta_scr[b, rows, :]
            gcol = _dot(tri, g_scr[b, rows, :], precision=HIGHEST)
            grow = _dot_nt(lane_pick, jnp.concatenate([gcol] * grp, axis=0), precision=HIGHEST)
            for h0 in range(0, GDN_HEADS, grp):
                hs = range(h0, h0 + grp)
                q = [l2n(cs_scr[b, rows, h * LANES:(h + 1) * LANES]) for h in hs]
                k = [l2n(cs_scr[b, rows, GDN_W + h * LANES:GDN_W + (h + 1) * LANES]) for h in hs]
                v = [cs_scr[b, rows, 2 * GDN_W + h * LANES:2 * GDN_W + (h + 1) * LANES] for h in hs]
                beta = [jnp.broadcast_to(beta_c[:, h:h + 1], (chunk, LANES)) for h in hs]
                gc = [jnp.broadcast_to(gcol[:, GDN_HEADS + h:GDN_HEADS + h + 1], (chunk, LANES)) for h in hs]
                g_row = side_by_side([jnp.broadcast_to(grow[GDN_HEADS + h:GDN_HEADS + h + 1, :], (chunk, width))
                                      for h in hs])
                gdiff = side_by_side(gc) - g_row
                dmask = jnp.where(incl, jnp.exp2(jnp.where(incl, gdiff, 0.0)), 0.0)
                kb = [k[i] * beta[i] for i in range(grp)]
                k_rows = block_rows(k)
                lower = jnp.where(strict, _dot_nt(jnp.concatenate(kb, axis=1).astype(BF16), k_rows) * dmask, 0.0)
                npow = jnp.where(same_blk[0], -lower, 0.0)
                probs.append(dict(b=b, c=c, rows=rows, h0=h0, q=q, k=k, v=v, beta=beta, gc=gc, dmask=dmask,
                                  kb=kb, k_rows=k_rows, lower=lower, npow=npow, qmat=npow))
        for _ in range(int(math.log2(base)) - 1):
            for p in probs:
                p["npow"] = _dot(p["npow"].astype(BF16), block_diag(p["npow"]))
            for p in probs:
                p["qmat"] = p["qmat"] + p["npow"] + _dot(p["qmat"].astype(BF16), block_diag(p["npow"]))
        for lvl in range(1, len(same_blk)):
            for p in probs:
                cb = jnp.where(jnp.logical_and(same_blk[lvl], jnp.logical_not(same_blk[lvl - 1])),
                               p["lower"], 0.0)
                p["x"] = cb + _dot(p["qmat"].astype(BF16), block_diag(cb))
            for p in probs:
                p["qmat"] = p["qmat"] - p["x"] - _dot(p["x"].astype(BF16), block_diag(p["qmat"]))
        for p in probs:
            b, rows, h0, gc = p["b"], p["rows"], p["h0"], p["gc"]
            cols = slice(h0 * LANES, (h0 + grp) * LANES)
            idx = range(grp)
            eg = [jnp.exp2(gc[i]) for i in idx]
            rhs_u = [p["v"][i] * p["beta"][i] for i in idx]
            rhs_w = [p["kb"][i] * eg[i] for i in idx]
            qb = p["qmat"].astype(BF16)
            qs = [p["q"][i] * (GDN_HEAD_DIM ** -0.5) for i in idx]
            g_last = [gc[i][chunk - 1:chunk, :] for i in idx]
            cat = lambda xs: jnp.concatenate(xs, axis=1)
            qg_ref[b, rows, cols] = cat([qs[i] * eg[i] for i in idx]).astype(qg_ref.dtype)
            kd_ref[b, rows, cols] = cat([p["k"][i] * jnp.exp2(g_last[i] - gc[i]) for i in idx]).astype(kd_ref.dtype)
            u_ref[b, rows, cols] = cat(rhs_u) + _dot(qb, block_rows(rhs_u))
            w_ref[b, rows, cols] = (cat(rhs_w) + _dot(qb, block_rows(rhs_w))).astype(w_ref.dtype)
            at_ref[b, rows, h0 * chunk:(h0 + grp) * chunk] = (
                _dot_nt(cat(qs).astype(BF16), p["k_rows"]) * p["dmask"]).astype(at_ref.dtype)
            el_ref[b, pl.ds(pl.multiple_of(p["c"] * SUBLANES, SUBLANES), SUBLANES), cols] = cat(
                [jnp.broadcast_to(jnp.exp2(g_last[i]), (SUBLANES, LANES)) for i in idx])
        return carry

    lax.fori_loop(0, tblk // (chunk * cpi), step, 0)


def _gdn_scan_kernel(qg_ref, kd_ref, u_ref, w_ref, at_ref, el_ref, z_ref, s0_ref, nw_ref,
                     o_ref, so_ref, s_scr, *, chunk):
    tb = pl.program_id(1)

    @pl.when(tb == 0)
    def _():
        s_scr[...] = s0_ref[...]

    nw = nw_ref[...]
    probs = [(b, h) for b in range(qg_ref.shape[0]) for h in range(GDN_HEADS)]

    def step(c, carry):
        rows = pl.ds(pl.multiple_of(c * chunk, chunk), chunk)
        erow = pl.ds(pl.multiple_of(c * SUBLANES, SUBLANES), SUBLANES)
        cols = lambda h: slice(h * LANES, (h + 1) * LANES)
        s_prev = [s_scr[b, h] for b, h in probs]
        sb = [s.astype(BF16) for s in s_prev]
        v_new = [u_ref[b, rows, cols(h)] - _dot(w_ref[b, rows, cols(h)].astype(BF16), sb[i])
                 for i, (b, h) in enumerate(probs)]
        vb = [v.astype(BF16) for v in v_new]
        for i, (b, h) in enumerate(probs):
            el = el_ref[b, erow, cols(h)][0:1, :]
            s_scr[b, h] = s_prev[i] * el + _dot_tn(kd_ref[b, rows, cols(h)].astype(BF16), vb[i])
        for i, (b, h) in enumerate(probs):
            attn = at_ref[b, rows, h * chunk:(h + 1) * chunk].astype(BF16)
            o = _dot(qg_ref[b, rows, cols(h)].astype(BF16), sb[i]) + _dot(attn, vb[i])
            ms = jnp.mean(o * o, axis=-1, keepdims=True)
            o_ref[b, rows, cols(h)] = (o * lax.rsqrt(ms + NORM_EPS) * nw * _silu(z_ref[b, rows, cols(h)])).astype(
                o_ref.dtype)
        return carry

    nchunk = qg_ref.shape[1] // chunk
    lax.fori_loop(0, nchunk, step, 0, unroll=min(GDN_SCAN_UNROLL, nchunk))

    @pl.when(tb == pl.num_programs(1) - 1)
    def _():
        so_ref[...] = s_scr[...]


def _gated_delta(cv, conv_done, z, ba, conv_hist, states, layer, conv_w, a_log, dt_bias, gdn_norm, nb, t):
    chunk = CHUNK if t % CHUNK == 0 else t
    tblk = min(t, 512)
    assert conv_done or tblk == t
    nchunk_blk = tblk // chunk
    hist_pad, cw_pad = _conv_operands(conv_hist, conv_w)
    ab = jnp.zeros((SUBLANES, LANES), F32)
    ab = ab.at[0, GDN_HEADS:2 * GDN_HEADS].set(a_log).at[1, GDN_HEADS:2 * GDN_HEADS].set(dt_bias)
    cv3 = cv.reshape(nb, t, CONV_DIM)
    pblk = _seqs_per_step(nb, t, 2)
    tok = lambda w: pl.BlockSpec((pblk, tblk, w), lambda b, i: (b, i, 0))
    full = lambda a: pl.BlockSpec(a.shape, lambda b, i: (0,) * a.ndim)
    el_spec =pl.BlockSpec((pblk, nchunk_blk * SUBLANES, GDN_W), lambda b, i: (b, i, 0))
    tok_shape = lambda w, dt=F32: jax.ShapeDtypeStruct((nb, t, w), dt)
    el_shape = jax.ShapeDtypeStruct((nb, (t // chunk) * SUBLANES, GDN_W), F32)
    opd = BF16 if chunk % (2 * SUBLANES) == 0 else F32
    qg, kd, u, w, attn, el = pl.pallas_call(
        functools.partial(_gdn_prep_kernel, chunk=chunk, cpi=min(GDN_PREP_CHUNKS_PER_ITER, nchunk_blk)),
        grid=(nb // pblk, t // tblk),
        in_specs=[tok(CONV_DIM), pl.BlockSpec((pblk, SUBLANES, CONV_DIM), lambda b, i: (b, 0, 0)),
                  tok(LANES), full(cw_pad), full(ab)],
        out_specs=[tok(GDN_W), tok(GDN_W), tok(GDN_W), tok(GDN_W), tok(GDN_HEADS * chunk), el_spec],
        out_shape=[tok_shape(GDN_W, opd), tok_shape(GDN_W, opd), tok_shape(GDN_W), tok_shape(GDN_W, opd),
                   tok_shape(GDN_HEADS * chunk, opd), el_shape],
        scratch_shapes=[pltpu.VMEM((pblk, tblk, LANES), F32), pltpu.VMEM((pblk, tblk, LANES), F32)] + (
            [] if conv_done else [pltpu.VMEM((pblk, CONV_DIM // LANES, tblk + SUBLANES, LANES), F32),
                                  pltpu.VMEM((pblk, tblk, CONV_DIM), F32)]),
        compiler_params=_cparams("parallel", "parallel"),
        name="gdn_prep",
    )(cv3, hist_pad, ba.reshape(nb, t, LANES), cw_pad, ab)

    bblk = _seqs_per_step(nb, t, GDN_SCAN_SEQS_PER_STEP)
    stok = lambda w: pl.BlockSpec((bblk, tblk, w), lambda b, i: (b, i, 0))
    sel_spec = pl.BlockSpec((bblk, nchunk_blk * SUBLANES, GDN_W), lambda b, i: (b, i, 0))
    st_in = pl.BlockSpec((None, bblk, GDN_HEADS, LANES, LANES), lambda b, i: (layer, b, 0, 0, 0))
    st = pl.BlockSpec((bblk, GDN_HEADS, LANES, LANES), lambda b, i: (b, 0, 0, 0))
    nw = gdn_norm.reshape(1, LANES)
    o, s_new = pl.pallas_call(
        functools.partial(_gdn_scan_kernel, chunk=chunk),
        grid=(nb // bblk, t // tblk),
        in_specs=[stok(GDN_W), stok(GDN_W), stok(GDN_W), stok(GDN_W), stok(GDN_HEADS * chunk), sel_spec,
                  stok(GDN_W), st_in, full(nw)],
        out_specs=[stok(GDN_W), st],
        out_shape=[tok_shape(GDN_W, _mixer_out_dtype(t)), jax.ShapeDtypeStruct((nb, GDN_HEADS, LANES, LANES), F32)],
        scratch_shapes=[pltpu.VMEM((bblk, GDN_HEADS, LANES, LANES), F32)],
        compiler_params=_cparams("parallel", "arbitrary"),
        name="gdn_scan",
    )(qg, kd, u, w, attn, el, z.reshape(nb, t, GDN_W), states, nw)
    return o.reshape(nb * t, GDN_W), s_new


FFN_TILE = 256


def _out_ffn_kernel(x_ref, oa_ref, orr_ref, oc_ref, g1_ref, sh2_ref, sc2_ref, g2_ref, nw_ref, wo_ref, wg_ref,
                    wu_ref, wd_ref, fn_ref, out_ref, *, final):
    mix = (_dot(oa_ref[...].astype(BF16), wo_ref[0:DSA_W, :])
           + _dot(orr_ref[...].astype(BF16), wo_ref[DSA_W:DSA_W + RET_W, :])
           + _dot(oc_ref[...].astype(BF16), wo_ref[DSA_W + RET_W:, :]))
    x1 = x_ref[...] + g1_ref[...] * mix
    hb = _rms_mod(x1, nw_ref[...], sc2_ref[...], sh2_ref[...]).astype(BF16)
    acc = None
    for j in range(wg_ref.shape[1] // FFN_TILE):
        cols = slice(j * FFN_TILE, (j + 1) * FFN_TILE)
        act = (_silu(_dot(hb, wg_ref[:, cols])) * _dot(hb, wu_ref[:, cols])).astype(BF16)
        down = _dot(act, wd_ref[cols, :])
        acc = down if acc is None else acc + down
    x2 = x1 + g2_ref[...] * acc
    if final:
        ms = jnp.mean(x2 * x2, axis=-1, keepdims=True)
        x2 = x2 * lax.rsqrt(ms + NORM_EPS) * fn_ref[...]
    out_ref[...] = x2


def _out_ffn(x2, t, oa, orr, oc, mod3, norm_w, w_out_b, wg_b, wu_b, wd_b, layer, final_norm, final, tm):
    m, d = x2.shape
    row = lambda w: pl.BlockSpec((tm, w), lambda i: (i, 0))
    const = lambda a: pl.BlockSpec(a.shape, lambda i: (0,) * a.ndim, pipeline_mode=pl.Buffered(1))
    wspec = lambda a: pl.BlockSpec((None,) + a.shape[1:], lambda i: (layer,) + (0,) * (a.ndim - 1),
                                   pipeline_mode=pl.Buffered(1))
    nw = norm_w.reshape(1, d)
    fn = final_norm.reshape(1, d)
    return pl.pallas_call(
        functools.partial(_out_ffn_kernel, final=final),
        grid=(m // tm,),
        in_specs=[row(d), row(DSA_W), row(RET_W), row(GDN_W),
                  _mod_spec(tm, t, d, 2), _mod_spec(tm, t, d, 3), _mod_spec(tm, t, d, 4), _mod_spec(tm, t, d, 5),
                  const(nw), wspec(w_out_b), wspec(wg_b), wspec(wu_b), wspec(wd_b), const(fn)],
        out_specs=row(d),
        out_shape=jax.ShapeDtypeStruct((m, d), F32),
        compiler_params=_cparams("parallel"),
        name="out_ffn",
    )(x2, oa, orr, oc, mod3, mod3, mod3, mod3, nw, w_out_b, wg_b, wu_b, wd_b, fn)


ROPE_SPLIT = 64


def _rope_tables(pos0, t, inv_freq):
    reps = LANES // (2 * inv_freq.shape[0])
    inv_lane = jnp.tile(jnp.concatenate([inv_freq, inv_freq]), reps)[None, :]
    sign_lane = jnp.tile(jnp.concatenate([-jnp.ones_like(inv_freq), jnp.ones_like(inv_freq)]), reps)[None, :]
    if t % ROPE_SPLIT:
        ang = (pos0 + jnp.arange(t, dtype=jnp.int32)).astype(F32)[:, None] * inv_lane
        return jnp.cos(ang), jnp.sin(ang) * sign_lane
    coarse = (pos0 + ROPE_SPLIT * jnp.arange(t // ROPE_SPLIT, dtype=jnp.int32)).astype(F32)[:, None] * inv_lane
    fine = jnp.arange(ROPE_SPLIT, dtype=jnp.int32).astype(F32)[:, None] * inv_lane
    cc, sc = jnp.cos(coarse)[:, None, :], jnp.sin(coarse)[:, None, :]
    cf, sf = jnp.cos(fine)[None], jnp.sin(fine)[None]
    cos = (cc * cf - sc * sf).reshape(t, LANES)
    sin = (sc * cf + cc * sf).reshape(t, LANES)
    return cos, sin * sign_lane


def _trunk(x, modp, pos0, k_hist, v_hist, s_ret, s_gdn, conv_hist, wts):
    (norm_mix, norm_ffn, w_in_b, w_ba_b, ret_norm, conv_w, a_log, dt_bias, gdn_norm, w_out_b, wg_b, wu_b, wd_b,
     final_norm) = wts
    nb, t, d = x.shape
    m = nb * t
    depth = w_in_b.shape[0]
    tm = min(TOKEN_TILE, m)
    inv_a = 1.0 / (ROPE_THETA ** (jnp.arange(0, DSA_HEAD_DIM, 2, dtype=F32) / DSA_HEAD_DIM))
    inv_r = 1.0 / (10000.0 ** jnp.linspace(0.0, 1.0, RET_HEAD_DIM // 2, dtype=F32))
    tabs = _rope_tables(pos0, t, inv_a) + _rope_tables(pos0, t, inv_r)
    if t < tm:
        tabs = tuple(jnp.tile(a, (tm // t, 1)) for a in tabs)
    x2 = x.reshape(m, d)
    ks, vs, rs, gs, cs = [], [], [], [], []
    for l in range(depth):
        if t >= tm:
            mod3 = modp[l].reshape(nb, 1, 6 * d)
        else:
            mod3 = jnp.repeat(modp[l], t, axis=0).reshape(m // tm, tm, 6 * d)
        (qa, ka, va, qr, kr, vr, gr, cv, z, ba), fused = _inproj(
            x2, t, mod3, norm_mix[l], w_in_b, w_ba_b, l, tabs, tm, conv_hist[l], conv_w[l])
        nhist = CONV_WIDTH - 1
        keep = min(t, DSA_MAX_WINDOW)
        if fused is None:
            cvn = jnp.concatenate([conv_hist[l], cv.reshape(nb, t, CONV_DIM)], axis=1)[:, -nhist:]
            window = lambda a: a.reshape(nb, t, DSA_W)[:, t - keep:].reshape(nb, keep, DSA_HEADS, DSA_HEAD_DIM)
            kwin, vwin = window(ka), window(va)
        else:
            conv_tail, k_t, v_t = fused
            cvn = conv_tail[:, SUBLANES - nhist:]
            untransposed = lambda a: a.reshape(nb, DSA_HEADS, DSA_HEAD_DIM, keep).transpose(0, 3, 1, 2)
            kwin, vwin = untransposed(k_t), untransposed(v_t)
        if k_hist is None:
            oa = _dsa_prompt(qa, ka, va, nb, t)
        else:
            oa = _dsa_step(qa, ka, va, k_hist, v_hist, l, nb, t)
        orr, sr = _retention(qr, kr, vr, gr, _to_block_diag(s_ret[l]), ret_norm[l], nb, t)
        oc, sg = _gated_delta(cv, fused is not None, z, ba, conv_hist[l], s_gdn, l, conv_w[l], a_log[l],
                              dt_bias[l], gdn_norm[l], nb, t)
        x2 = _out_ffn(x2, t, oa, orr, oc, mod3, norm_ffn[l], w_out_b, wg_b, wu_b, wd_b, l,
                      final_norm, l == depth - 1, tm)
        ks.append(kwin)
        vs.append(vwin)
        rs.append(_from_block_diag(sr))
        gs.append(sg)
        cs.append(cvn)
    return (x2.reshape(nb, t, d), jnp.stack(ks), jnp.stack(vs), jnp.stack(rs), jnp.stack(gs), jnp.stack(cs))


def kernel(x_prompt, x_sample, cache_win_k, cache_win_v, state_ret, state_gdn, state_conv, c_prompt, c_sample, ada_w, ada_b, norm_mix, norm_ffn, w_in, ret_norm, conv_w, a_log, dt_bias, gdn_norm, w_out, w_gate, w_up, w_down, final_norm):
    nb, t_p, d = x_prompt.shape
    db, t_s, _ = x_sample.shape
    depth = ada_w.shape[0]
    rows = nb + db
    rows_pad = -(-rows // SUBLANES) * SUBLANES
    c_all = jnp.concatenate([c_prompt, c_sample, jnp.zeros((rows_pad - rows, d), F32)], axis=0)
    mod = _modulation(c_all, ada_w, ada_b)
    w_ba_b = jnp.pad(w_in[:, :, _C_BA:], ((0, 0), (0, 0), (0, LANES - (IN_COLS - _C_BA)))).astype(BF16)
    wts = (norm_mix, norm_ffn, w_in.astype(BF16), w_ba_b, ret_norm, conv_w, a_log, dt_bias, gdn_norm, w_out.astype(BF16),
           w_gate.astype(BF16), w_up.astype(BF16), w_down.astype(BF16), final_norm)

    zr = jnp.zeros((depth, nb, RET_HEADS, RET_HEAD_DIM, RET_HEAD_DIM), F32)
    zg = jnp.zeros((depth, nb, GDN_HEADS, GDN_HEAD_DIM, GDN_HEAD_DIM), F32)
    zc = jnp.zeros((depth, nb, CONV_WIDTH - 1, CONV_DIM), F32)
    y_p, kp, vp, rp, gp, cp = _trunk(x_prompt, mod[:, :nb], 0,
                                     None, None, zr, zg, zc, wts)
    wb = cache_win_k.shape[2]
    y_s, ks, vs, rs, gs, cs = _trunk(x_sample, mod[:, nb:rows], PAST_LEN,
                                     cache_win_k.reshape(depth, db, wb, DSA_W).transpose(0, 1, 3, 2),
                                     cache_win_v.reshape(depth, db, wb, DSA_W).transpose(0, 1, 3, 2),
                                     state_ret, state_gdn, state_conv, wts)
    return (y_p, y_s, kp, vp, rp, gp, cp, ks, vs, rs, gs, cs)
```

```python
import functools
import math

import jax
import jax.numpy as jnp
from jax import lax
from jax.experimental import pallas as pl
from jax.experimental.pallas import tpu as pltpu

F32 = jnp.float32
BF16 = jnp.bfloat16
HIGHEST = lax.Precision.HIGHEST

DSA_HEAD_DIM = 64
DSA_HEADS = 4
DSA_PATTERNS = ((128, 1), (512, 4), (2048, 16))
DSA_MAX_WINDOW = 2048
ROPE_THETA = 10000.0
RET_HEAD_DIM = 64
RET_HEADS = 4
GDN_HEAD_DIM = 128
GDN_HEADS = 4
CONV_WIDTH = 4
CHUNK = 64
NORM_EPS = 1e-6
PAST_LEN = 16384
LOG2_E = math.log2(math.e)

DSA_W = DSA_HEADS * DSA_HEAD_DIM
RET_W = RET_HEADS * RET_HEAD_DIM
GDN_W = GDN_HEADS * GDN_HEAD_DIM
CONV_DIM = 3 * GDN_W
LANES = 128
SUBLANES = 8
VMEM_LIMIT = 56 * 1024 * 1024
TOKEN_TILE = 512

_C_QA, _C_KA, _C_VA = 0, DSA_W, 2 * DSA_W
_C_QR = 3 * DSA_W
_C_KR, _C_VR, _C_GR = _C_QR + RET_W, _C_QR + 2 * RET_W, _C_QR + 3 * RET_W
_C_CV = _C_QR + 4 * RET_W
_C_Z = _C_CV + CONV_DIM
_C_BA = _C_Z + GDN_W
IN_COLS = _C_BA + 2 * GDN_HEADS


def _cparams(*sem):
    return pltpu.CompilerParams(dimension_semantics=sem, vmem_limit_bytes=VMEM_LIMIT)


def _dot(a, b, **kw):
    return jnp.dot(a, b, preferred_element_type=F32, **kw)


def _dot_nt(a, b, **kw):
    return lax.dot_general(a, b, (((1,), (1,)), ((), ())), preferred_element_type=F32, **kw)


def _dot_tn(a, b, **kw):
    return lax.dot_general(a, b, (((0,), (0,)), ((), ())), preferred_element_type=F32, **kw)


def _silu(x):
    return x * jax.nn.sigmoid(x)


def _mixer_out_dtype(t):
    return BF16 if t % (2 * SUBLANES) == 0 else F32


def _seqs_per_step(nb, t, base):
    return math.gcd(nb, base * (4 if t < CHUNK else 1))


def _mod_kernel(c_ref, w_ref, b_ref, o_ref):
    a = _silu(c_ref[...]).astype(BF16)
    o_ref[...] = _dot(a, w_ref[...].astype(BF16)) + b_ref[...]


def _modulation(c_all, ada_w, ada_b, tn=1536):
    depth, d, n = ada_w.shape
    bp = c_all.shape[0]
    return pl.pallas_call(
        _mod_kernel,
        grid=(depth, n // tn),
        in_specs=[
            pl.BlockSpec((bp, d), lambda l, j: (0, 0)),
            pl.BlockSpec((None, d, tn), lambda l, j: (l, 0, j)),
            pl.BlockSpec((None, 1, tn), lambda l, j: (l, 0, j)),
        ],
        out_specs=pl.BlockSpec((None, bp, tn), lambda l, j: (l, 0, j)),
        out_shape=jax.ShapeDtypeStruct((depth, bp, n), F32),
        compiler_params=_cparams("parallel", "parallel"),
        name="modulation",
    )(c_all, ada_w, ada_b.reshape(depth, 1, n))


def _rms_mod(x, nw, sc, sh):
    ms = jnp.mean(x * x, axis=-1, keepdims=True)
    return (x * lax.rsqrt(ms + NORM_EPS) * nw) * (1.0 + sc) + sh


def _causal_conv_silu(xp_scr, seq, cw_ref, cg, nrows):
    cols = slice(cg * LANES, (cg + 1) * LANES)
    acc = None
    for i in range(CONV_WIDTH):
        start = SUBLANES - (CONV_WIDTH - 1) + i
        term = xp_scr[seq, cg, pl.ds(start, nrows, stride=1), :] * cw_ref[i:i + 1, cols]
        acc = term if acc is None else acc + term
    return _silu(acc)


def _inproj_kernel(x_ref, nw_ref, sh_ref, sc_ref, w_ref, wba_ref, ca_ref, sa_ref, cr_ref, sr_ref, *rest,
                   tiles_per_seq):
    if tiles_per_seq:
        hist_ref, cw_ref = rest[:2]
        rest = rest[2:]
    qa_ref, ka_ref, va_ref, qr_ref, kr_ref, vr_ref, gr_ref, cv_ref, z_ref, ba_ref = rest[:10]
    tm = x_ref.shape[0]
    if tiles_per_seq:
        tail_ref, kt_ref, vt_ref, xp_scr = rest[10:]

        @pl.when(pl.program_id(0) % tiles_per_seq == 0)
        def _():
            tail_ref[...] = hist_ref[...]

    hb = _rms_mod(x_ref[...], nw_ref[...], sc_ref[...], sh_ref[...]).astype(BF16)

    def proj(c0, width):
        return _dot(hb, w_ref[:, c0:c0 + width])

    lane = lax.broadcasted_iota(jnp.int32, (tm, DSA_W), 1)
    first_half = (lane % DSA_HEAD_DIM) < (DSA_HEAD_DIM // 2)

    def rope(y, cos, sin_signed):
        partner = jnp.where(first_half, pltpu.roll(y, DSA_W - DSA_HEAD_DIM // 2, 1),
                            pltpu.roll(y, DSA_HEAD_DIM // 2, 1))
        return y * cos + partner * sin_signed

    wide = lambda ref: jnp.concatenate([ref[...]] * (DSA_W // LANES), axis=1)
    ca, sa, cr, sr = wide(ca_ref), wide(sa_ref), wide(cr_ref), wide(sr_ref)
    qa_ref[...] = rope(proj(_C_QA, DSA_W), ca, sa) * (DSA_HEAD_DIM ** -0.5 * LOG2_E)
    ka = rope(proj(_C_KA, DSA_W), ca, sa)
    va = proj(_C_VA, DSA_W)
    ka_ref[...] = ka
    va_ref[...] = va
    if tiles_per_seq:
        kt_ref[...] = ka.T
        vt_ref[...] = va.T

    qr_ref[...] = rope(proj(_C_QR, RET_W), cr, sr)
    kr_ref[...] = rope(proj(_C_KR, RET_W), cr, sr) * (RET_HEAD_DIM ** -0.5)
    vr_ref[...] = proj(_C_VR, RET_W)
    gr_ref[...] = proj(_C_GR, RET_W)
    for s in range(CONV_DIM // GDN_W):
        y = proj(_C_CV + s * GDN_W, GDN_W)
        if not tiles_per_seq:
            cv_ref[:, s * GDN_W:(s + 1) * GDN_W] = y
            continue
        for j in range(GDN_W // LANES):
            cg = s * (GDN_W // LANES) + j
            cols = slice(cg * LANES, (cg + 1) * LANES)
            yj = y[:, j * LANES:(j + 1) * LANES]
            xp_scr[0, cg, :SUBLANES, :] = tail_ref[:, cols]
            xp_scr[0, cg, SUBLANES:, :] = yj
            cv_ref[:, cols] = _causal_conv_silu(xp_scr, 0, cw_ref, cg, tm)
            tail_ref[:, cols] = yj[tm - SUBLANES:, :]
    z_ref[...] = proj(_C_Z, GDN_W)
    ba_ref[...] = _dot(hb, wba_ref[...])


def _mod_spec(tm, t, d, col):
    if t >= tm:
        return pl.BlockSpec((None, 1, d), lambda i: ((i * tm) // t, 0, col))
    return pl.BlockSpec((None, tm, d), lambda i: (i, 0, col))


def _conv_operands(conv_hist, conv_w):
    nb = conv_hist.shape[0]
    hist_pad = jnp.concatenate([jnp.zeros((nb, SUBLANES - (CONV_WIDTH - 1), CONV_DIM), F32), conv_hist], axis=1)
    cw_pad = jnp.concatenate([conv_w, jnp.zeros((SUBLANES - CONV_WIDTH, CONV_DIM), F32)], axis=0)
    return hist_pad, cw_pad


def _inproj(x2, t, mod3, norm_w, w_in_b, w_ba_b, layer, tabs, tm, conv_hist, conv_w):
    m, d = x2.shape
    nt = tabs[0].shape[0] // tm
    widths = (DSA_W,) * 3 + (RET_W,) * 4 + (CONV_DIM, GDN_W, LANES)
    tab_spec = pl.BlockSpec((tm, LANES), lambda i: (i % nt, 0))
    tiles_per_seq = t // tm if t % tm == 0 else 0
    window_tiles = min(t, DSA_MAX_WINDOW) // tm
    in_specs = [
        pl.BlockSpec((tm, d), lambda i: (i, 0)),
        pl.BlockSpec((1, d), lambda i: (0, 0)),
        _mod_spec(tm, t, d, 0),
        _mod_spec(tm, t, d, 1),
        pl.BlockSpec((None, d, IN_COLS), lambda i: (layer, 0, 0), pipeline_mode=pl.Buffered(1)),
        pl.BlockSpec((None, d, LANES), lambda i: (layer, 0, 0), pipeline_mode=pl.Buffered(1)),
        tab_spec, tab_spec, tab_spec, tab_spec,
    ]
    operands = [x2, norm_w.reshape(1, d), mod3, mod3, w_in_b, w_ba_b, *tabs]
    out_specs = [pl.BlockSpec((tm, w), lambda i: (i, 0)) for w in widths]
    out_shape = [jax.ShapeDtypeStruct((m, w), F32) for w in widths]
    scratch = []
    if tiles_per_seq:
        hist_pad, cw_pad = _conv_operands(conv_hist, conv_w)
        seq_tail = pl.BlockSpec((None, SUBLANES, CONV_DIM), lambda i: (i // tiles_per_seq, 0, 0))
        in_specs += [seq_tail, pl.BlockSpec(cw_pad.shape, lambda i: (0, 0))]
        operands += [hist_pad, cw_pad]
        out_specs.append(seq_tail)
        out_shape.append(jax.ShapeDtypeStruct((m // t, SUBLANES, CONV_DIM), F32))
        first_win = tiles_per_seq - window_tiles
        win_t = pl.BlockSpec((None, DSA_W, tm), lambda i: (i // tiles_per_seq, 0,
                                                            jnp.maximum(i % tiles_per_seq - first_win, 0)))
        out_specs += [win_t, win_t]
        out_shape += [jax.ShapeDtypeStruct((m // t, DSA_W, window_tiles * tm), F32)] * 2
        scratch.append(pltpu.VMEM((1, CONV_DIM // LANES, tm + SUBLANES, LANES), F32))
    outs = pl.pallas_call(
        functools.partial(_inproj_kernel, tiles_per_seq=tiles_per_seq),
        grid=(m // tm,),
        in_specs=in_specs,
        out_specs=out_specs,
        out_shape=out_shape,
        scratch_shapes=scratch,
        compiler_params=_cparams("arbitrary"),
        name="inproj",
    )(*operands)
    return (outs[:10], outs[10:]) if tiles_per_seq else (outs, None)


DSA_BLK = 128


DSA_QBLK = DSA_MAX_WINDOW
DSA_GROUP = 4


def _dsa_kernel(q_ref, kp_ref, kc_ref, vp_ref, vc_ref, o_ref, acc_scr, m_scr, l_scr):
    blk = pl.program_id(2)
    qblk = q_ref.shape[0]
    row = lax.broadcasted_iota(jnp.int32, (2 * DSA_BLK, 2 * DSA_BLK), 0) % DSA_BLK
    col = lax.broadcasted_iota(jnp.int32, (2 * DSA_BLK, 2 * DSA_BLK), 1)
    ok = jnp.logical_and(col >= row, col <= row + DSA_BLK)
    ok_first = jnp.logical_and(ok, jnp.logical_or(col >= DSA_BLK, blk > 0))
    lo = lax.broadcasted_iota(jnp.int32, (DSA_BLK, LANES), 1) < DSA_HEAD_DIM
    neg = -jnp.inf
    halves = lambda x: jnp.where(lo, x[:DSA_BLK], x[DSA_BLK:])

    def rows_of(dil, start, n):
        return pl.ds(start, n) if dil == 1 else pl.ds(start, n, stride=dil)

    def group_softmax(dil, subs):
        n = range(len(subs))
        idx = [rows_of(dil, ph + dil * DSA_BLK * j, DSA_BLK) for ph, j in subs]
        q = [q_ref[i, :] for i in idx]
        k, v = [], []
        for u, (ph, j) in enumerate(subs):
            if j == 0:
                band = rows_of(dil, qblk - dil * DSA_BLK + ph, DSA_BLK)
                k.append(jnp.concatenate([kp_ref[band, :], kc_ref[idx[u], :]], axis=0).astype(BF16))
                v.append(jnp.concatenate([vp_ref[band, :], vc_ref[idx[u], :]], axis=0).astype(BF16))
            else:
                both = rows_of(dil, ph + dil * DSA_BLK * (j - 1), 2 * DSA_BLK)
                k.append(kc_ref[both, :].astype(BF16))
                v.append(vc_ref[both, :].astype(BF16))
        q2 = [jnp.concatenate([jnp.where(lo, q[u], 0.0), jnp.where(lo, 0.0, q[u])], axis=0).astype(BF16)
              for u in n]
        s = [jnp.where(ok_first if subs[u][1] == 0 else ok, _dot_nt(q2[u], k[u]), neg) for u in n]
        mx = [jnp.max(s[u], axis=-1, keepdims=True) for u in n]
        p = [jnp.exp2(s[u] - mx[u]) for u in n]
        den = [jnp.sum(p[u], axis=-1, keepdims=True) for u in n]
        pv = [_dot(p[u].astype(BF16), v[u]) for u in n]
        return idx, [(halves(pv[u]), halves(jnp.broadcast_to(mx[u], (2 * DSA_BLK, LANES))),
                      halves(jnp.broadcast_to(den[u], (2 * DSA_BLK, LANES)))) for u in n]

    dils = sorted((d for _, d in DSA_PATTERNS), reverse=True)
    for pi, dil in enumerate(dils):
        subs = [(ph, j) for j in range(qblk // (dil * DSA_BLK)) for ph in range(dil)]
        for g0 in range(0, len(subs), DSA_GROUP):
            idx, tiles = group_softmax(dil, subs[g0:g0 + DSA_GROUP])
            for i, (pv, mx, den) in zip(idx, tiles):
                if pi > 0:
                    m_old = m_scr[i, :]
                    m_new = jnp.maximum(m_old, mx)
                    w_old = jnp.exp2(m_old - m_new)
                    w_cur = jnp.exp2(mx - m_new)
                    pv = acc_scr[i, :] * w_old + pv * w_cur
                    den = l_scr[i, :] * w_old + den * w_cur
                    mx = m_new
                if pi < len(dils) - 1:
                    acc_scr[i, :] = pv
                    m_scr[i, :] = mx
                    l_scr[i, :] = den
                else:
                    o_ref[i, :] = (pv / den).astype(o_ref.dtype)


def _dsa_prompt(q, k, v, nb, t):
    assert all(w // d == DSA_BLK for w, d in DSA_PATTERNS) and t % DSA_QBLK == 0
    ngrp = DSA_W // LANES
    r3 = lambda a: a.reshape(nb, t, DSA_W)
    cur = pl.BlockSpec((None, DSA_QBLK, LANES), lambda b, g, i: (b, i, g))
    prv = pl.BlockSpec((None, DSA_QBLK, LANES), lambda b, g, i: (b, jnp.maximum(i - 1, 0), g))
    return pl.pallas_call(
        _dsa_kernel,
        grid=(nb, ngrp, t // DSA_QBLK),
        in_specs=[cur, prv, cur, prv, cur],
        out_specs=cur,
        out_shape=jax.ShapeDtypeStruct((nb, t, DSA_W), _mixer_out_dtype(t)),
        scratch_shapes=[pltpu.VMEM((DSA_QBLK, LANES), F32)] * 3,
        compiler_params=_cparams("parallel", "parallel", "parallel"),
        name="dsa_prompt",
    )(r3(q), r3(k), r3(k), r3(v), r3(v)).reshape(nb * t, DSA_W)


DSA_STEP_SEQS = 2


def _multiplicity(dist):
    total = jnp.zeros(dist.shape, F32)
    for window, dil in DSA_PATTERNS:
        hit = (dist >= 0) & (dist <= window) & ((dist & (dil - 1)) == 0)
        total = total + hit.astype(F32)
    return total


def _dsa_step_kernel(q_ref, kc_ref, vc_ref, kn_ref, vn_ref, o_ref):
    nseq, tq = q_ref.shape[0], q_ref.shape[1]
    wb = kc_ref.shape[2]
    qi = lax.broadcasted_iota(jnp.int32, (2 * tq, wb), 0) % tq
    w_c = _multiplicity(wb + qi - lax.broadcasted_iota(jnp.int32, (2 * tq, wb), 1))
    qn = lax.broadcasted_iota(jnp.int32, (2 * tq, LANES), 0) % tq
    nn = lax.broadcasted_iota(jnp.int32, (2 * tq, LANES), 1)
    w_n = jnp.where(nn < tq, _multiplicity(qn - nn), 0.0)
    lo = lax.broadcasted_iota(jnp.int32, (tq, LANES), 1) < DSA_HEAD_DIM
    pad = jnp.zeros((LANES - tq, LANES), F32)
    neg = -jnp.inf
    for b, g in [(b, g) for b in range(nseq) for g in range(DSA_W // LANES)]:
        cols = slice(g * LANES, (g + 1) * LANES)
        q = q_ref[b, :, cols]
        q2 = jnp.concatenate([jnp.where(lo, q, 0.0), jnp.where(lo, 0.0, q)], axis=0).astype(BF16)
        kt_c, vt_c = kc_ref[b, cols, :].astype(BF16), vc_ref[b, cols, :].astype(BF16)
        k_n = jnp.concatenate([kn_ref[b, :, cols], pad], axis=0).astype(BF16)
        v_n = jnp.concatenate([vn_ref[b, :, cols], pad], axis=0).astype(BF16)
        s_c = jnp.where(w_c > 0, _dot(q2, kt_c), neg)
        s_n = jnp.where(w_n > 0, _dot_nt(q2, k_n), neg)
        mx = jnp.maximum(jnp.max(s_c, axis=-1, keepdims=True), jnp.max(s_n, axis=-1, keepdims=True))
        p_c = w_c * jnp.exp2(s_c - mx)
        p_n = w_n * jnp.exp2(s_n - mx)
        den = jnp.sum(p_c, axis=-1, keepdims=True) + jnp.sum(p_n, axis=-1, keepdims=True)
        o2 = (_dot_nt(p_c.astype(BF16), vt_c) + _dot(p_n.astype(BF16), v_n)) / den
        o_ref[b, :, cols] = jnp.where(lo, o2[:tq], o2[tq:])


def _dsa_step(q, k_new, v_new, k_cache, v_cache, layer, nb, t):
    wb = k_cache.shape[3]
    bblk = math.gcd(nb, DSA_STEP_SEQS)
    new = pl.BlockSpec((bblk, t, DSA_W), lambda b: (b, 0, 0))
    cache = pl.BlockSpec((None, bblk, DSA_W, wb), lambda b: (layer, b, 0, 0))
    r3 = lambda a: a.reshape(nb, t, DSA_W)
    return pl.pallas_call(
        _dsa_step_kernel,
        grid=(nb // bblk,),
        in_specs=[new, cache, cache, new, new],
        out_specs=new,
        out_shape=jax.ShapeDtypeStruct((nb, t, DSA_W), F32),
        compiler_params=_cparams("parallel"),
        name="dsa_step",
    )(r3(q), k_cache, v_cache, r3(k_new), r3(v_new)).reshape(nb * t, DSA_W)


RET_CHUNK = 256
RET_SEQS_PER_STEP = 2
RET_UNROLL = 4


def _ret_kernel(q_ref, k_ref, v_ref, g_ref, s0_ref, dec_ref, qd_ref, kd_ref, cd_ref, bd_ref, nw_ref,
                o_ref, so_ref, s_scr, *, chunk):
    tb = pl.program_id(1)
    ngrp = RET_W // LANES

    @pl.when(tb == 0)
    def _():
        s_scr[...] = s0_ref[...]

    lo = lax.broadcasted_iota(jnp.int32, (chunk, LANES), 1) < RET_HEAD_DIM
    nw = nw_ref[...]
    bd = bd_ref[...]

    def body(c, carry):
        rows = pl.ds(pl.multiple_of(c * chunk, chunk), chunk)
        for b in range(q_ref.shape[0]):
            for g in range(ngrp):
                cols = slice(g * LANES, (g + 1) * LANES)
                q, k, v = q_ref[b, rows, cols], k_ref[b, rows, cols], v_ref[b, rows, cols]
                kb, vb = k.astype(BF16), v.astype(BF16)
                parts = []
                for hh in range(2):
                    qm = jnp.where(lo if hh == 0 else jnp.logical_not(lo), q, 0.0).astype(BF16)
                    inner = _dot_nt(qm, kb) * dec_ref[2 * g + hh]
                    parts.append(_dot(inner.astype(BF16), vb))
                s_prev = s_scr[b, g]
                o = jnp.where(lo, parts[0], parts[1]) + _dot(q.astype(BF16), s_prev.astype(BF16)) * qd_ref[g]
                s_scr[b, g] = s_prev * cd_ref[g] + bd * _dot_tn((k * kd_ref[g]).astype(BF16), vb)
                o2 = o * o
                ms = jnp.where(lo, jnp.sum(jnp.where(lo, o2, 0.0), axis=-1, keepdims=True),
                               jnp.sum(jnp.where(lo, 0.0, o2), axis=-1, keepdims=True)) * (1.0 / RET_HEAD_DIM)
                o_ref[b, rows, cols] = (o * lax.rsqrt(ms + NORM_EPS) * nw * _silu(g_ref[b, rows, cols])).astype(
                    o_ref.dtype)
        return carry

    nchunk = q_ref.shape[1] // chunk
    lax.fori_loop(0, nchunk, body, 0, unroll=min(RET_UNROLL, nchunk))

    @pl.when(tb == pl.num_programs(1) - 1)
    def _():
        so_ref[...] = s_scr[...]


def _ret_tables(chunk):
    log_gamma = jnp.log(1.0 - 2.0 ** (-5.0 - jnp.arange(RET_HEADS, dtype=F32)))
    i = jnp.arange(chunk, dtype=F32)
    diff = i[:, None] - i[None, :]
    causal = diff >= 0
    decay = jnp.where(causal[None], jnp.exp(log_gamma[:, None, None] * jnp.where(causal, diff, 0.0)[None]), 0.0)
    per_lane = lambda a: jnp.repeat(a, RET_HEAD_DIM, axis=0).reshape(RET_W // LANES, LANES, -1)
    q_dec = per_lane(jnp.exp(log_gamma[:, None] * (i[None, :] + 1.0))).transpose(0, 2, 1)
    k_dec = per_lane(jnp.exp(log_gamma[:, None] * (chunk - 1.0 - i)[None, :])).transpose(0, 2, 1)
    c_dec = jnp.broadcast_to(per_lane(jnp.exp(log_gamma * chunk)[:, None]), (RET_W // LANES, LANES, LANES))
    head_of = jnp.arange(LANES) // RET_HEAD_DIM
    block_diag = (head_of[:, None] == head_of[None, :]).astype(F32)
    return decay, q_dec, k_dec, c_dec, block_diag


def _retention(q, k, v, gate, state_bd, ret_norm, nb, t):
    chunk = RET_CHUNK if t % RET_CHUNK == 0 else t
    tblk = min(t, 1024)
    ngrp = RET_W // LANES
    bblk = _seqs_per_step(nb, t, RET_SEQS_PER_STEP)
    decay, q_dec, k_dec, c_dec, block_diag = _ret_tables(chunk)
    r3 = lambda a: a.reshape(nb, t, RET_W)
    tok = pl.BlockSpec((bblk, tblk, RET_W), lambda b, i: (b, i, 0))
    st = pl.BlockSpec((bblk, ngrp, LANES, LANES), lambda b, i: (b, 0, 0, 0))
    full = lambda a: pl.BlockSpec(a.shape, lambda b, i: (0,) * a.ndim)
    nw = jnp.tile(ret_norm, LANES // RET_HEAD_DIM).reshape(1, LANES)
    o, s_new = pl.pallas_call(
        functools.partial(_ret_kernel, chunk=chunk),
        grid=(nb // bblk, t // tblk),
        in_specs=[tok, tok, tok, tok, st, full(decay), full(q_dec), full(k_dec), full(c_dec),
                  full(block_diag), full(nw)],
        out_specs=[tok, st],
        out_shape=[jax.ShapeDtypeStruct((nb, t, RET_W), _mixer_out_dtype(t)),
                   jax.ShapeDtypeStruct((nb, ngrp, LANES, LANES), F32)],
        scratch_shapes=[pltpu.VMEM((bblk, ngrp, LANES, LANES), F32)],
        compiler_params=_cparams("parallel", "arbitrary"),
        name="retention",
    )(r3(q), r3(k), r3(v), r3(gate), state_bd, decay, q_dec, k_dec, c_dec, block_diag, nw)
    return o.reshape(nb * t, RET_W), s_new


def _to_block_diag(s):
    nb = s.shape[0]
    s = s.reshape(nb, 2, 2, RET_HEAD_DIM, RET_HEAD_DIM)
    z = jnp.zeros_like(s[:, :, 0])
    top = jnp.concatenate([s[:, :, 0], z], axis=-1)
    bot = jnp.concatenate([z, s[:, :, 1]], axis=-1)
    return jnp.concatenate([top, bot], axis=-2)


def _from_block_diag(s):
    h = RET_HEAD_DIM
    return jnp.stack([s[:, :, :h, :h], s[:, :, h:, h:]], axis=2).reshape(s.shape[0], RET_HEADS, h, h)


GDN_PREP_CHUNKS_PER_ITER = 8
GDN_SCAN_SEQS_PER_STEP = 2
GDN_SCAN_UNROLL = 4


def _softplus(x):
    return jnp.maximum(x, 0.0) + jnp.log1p(jnp.exp(-jnp.abs(x)))


def _gdn_prep_kernel(cv_ref, hist_ref, ba_ref, cw_ref, ab_ref,
                     qg_ref, kd_ref, u_ref, w_ref, at_ref, el_ref, beta_scr, g_scr, *conv_scr, chunk, cpi):
    nseq, tblk = cv_ref.shape[0], cv_ref.shape[1]
    ncg = CONV_DIM // LANES
    ab = ab_ref[...]
    cs_scr = cv_ref
    for b in range(nseq):
        if conv_scr:
            xp_scr, cs_scr = conv_scr
            for cg in range(ncg):
                cols = slice(cg * LANES, (cg + 1) * LANES)
                xp_scr[b, cg, :SUBLANES, :] = hist_ref[b, :, cols]
                xp_scr[b, cg, SUBLANES:, :] = cv_ref[b, :, cols]
                cs_scr[b, :, cols] = _causal_conv_silu(xp_scr, b, cw_ref, cg, tblk)

        ba = ba_ref[b]
        beta_scr[b] = jax.nn.sigmoid(ba)
        g_scr[b] = -jnp.exp(ab[0:1, :]) * _softplus(ba + ab[1:2, :])

    grp = 2 if 2 * chunk == LANES else 1
    width = grp * chunk
    ri = lax.broadcasted_iota(jnp.int32, (chunk, width), 0)
    lane = lax.broadcasted_iota(jnp.int32, (chunk, width), 1)
    ci = lane % chunk
    first = lane < chunk
    incl = ri >= ci
    strict = ri > ci
    tri = (lax.broadcasted_iota(jnp.int32, (chunk, chunk), 0)
           >= lax.broadcasted_iota(jnp.int32, (chunk, chunk), 1)).astype(F32)
    lane_pick = (lax.broadcasted_iota(jnp.int32, (SUBLANES, LANES), 0)
                 == lax.broadcasted_iota(jnp.int32, (SUBLANES, LANES), 1)).astype(F32)
    base = min(SUBLANES, chunk)
    assert chunk % base == 0 and (chunk // base) & (chunk // base - 1) == 0
    same_blk = [(ri >> sh) == (ci >> sh) for sh in range(int(math.log2(base)), int(math.log2(chunk)) + 1)]

    def l2n(x):
        return x * lax.rsqrt(jnp.sum(x * x, axis=-1, keepdims=True) + NORM_EPS)

    def side_by_side(per_head):
        if grp == 1:
            return per_head[0][:, :width]
        return jnp.where(first, per_head[0][:, :width], per_head[1][:, :width])

    def block_diag(y):
        if grp == 1:
            return y.astype(BF16)
        return jnp.concatenate([jnp.where(first, y, 0.0), jnp.where(first, 0.0, y)], axis=0).astype(BF16)

    def block_rows(per_head):
        if grp == 1:
            return per_head[0].astype(BF16)
        a, b = per_head
        return jnp.concatenate([jnp.concatenate([a, jnp.zeros_like(b)], axis=1),
                                jnp.concatenate([jnp.zeros_like(a), b], axis=1)], axis=0).astype(BF16)

    def step(it, carry):
        probs = []
        for b, cc in [(b, cc) for b in range(nseq) for cc in range(cpi)]:
            c = it * cpi + cc
            rows = pl.ds(pl.multiple_of(c * chunk, chunk), chunk)
            beta_c = beta_scr[b, rows, :]
            gcol = _dot(tri, g_scr[b, rows, :], precision=HIGHEST)
            grow = _dot_nt(lane_pick, jnp.concatenate([gcol] * grp, axis=0), precision=HIGHEST)
            for h0 in range(0, GDN_HEADS, grp):
                hs = range(h0, h0 + grp)
                q = [l2n(cs_scr[b, rows, h * LANES:(h + 1) * LANES]) for h in hs]
                k = [l2n(cs_scr[b, rows, GDN_W + h * LANES:GDN_W + (h + 1) * LANES]) for h in hs]
                v = [cs_scr[b, rows, 2 * GDN_W + h * LANES:2 * GDN_W + (h + 1) * LANES] for h in hs]
                beta = [jnp.broadcast_to(beta_c[:, h:h + 1], (chunk, LANES)) for h in hs]
                gc = [jnp.broadcast_to(gcol[:, GDN_HEADS + h:GDN_HEADS + h + 1], (chunk, LANES)) for h in hs]
                g_row = side_by_side([jnp.broadcast_to(grow[GDN_HEADS + h:GDN_HEADS + h + 1, :], (chunk, width))
                                      for h in hs])
                gdiff = side_by_side(gc) - g_row
                dmask = jnp.where(incl, jnp.exp(jnp.where(incl, gdiff, 0.0)), 0.0)
                kb = [k[i] * beta[i] for i in range(grp)]
                k_rows = block_rows(k)
                lower = jnp.where(strict, _dot_nt(jnp.concatenate(kb, axis=1).astype(BF16), k_rows) * dmask, 0.0)
                npow = jnp.where(same_blk[0], -lower, 0.0)
                probs.append(dict(b=b, c=c, rows=rows, h0=h0, q=q, k=k, v=v, beta=beta, gc=gc, dmask=dmask,
                                  kb=kb, k_rows=k_rows, lower=lower, npow=npow, qmat=npow))
        for _ in range(int(math.log2(base)) - 1):
            for p in probs:
                p["npow"] = _dot(p["npow"].astype(BF16), block_diag(p["npow"]))
            for p in probs:
                p["qmat"] = p["qmat"] + p["npow"] + _dot(p["qmat"].astype(BF16), block_diag(p["npow"]))
        for lvl in range(1, len(same_blk)):
            for p in probs:
                cb = jnp.where(jnp.logical_and(same_blk[lvl], jnp.logical_not(same_blk[lvl - 1])),
                               p["lower"], 0.0)
                p["x"] = cb + _dot(p["qmat"].astype(BF16), block_diag(cb))
            for p in probs:
                p["qmat"] = p["qmat"] - p["x"] - _dot(p["x"].astype(BF16), block_diag(p["qmat"]))
        for p in probs:
            b, rows, h0, gc = p["b"], p["rows"], p["h0"], p["gc"]
            cols = slice(h0 * LANES, (h0 + grp) * LANES)
            idx = range(grp)
            eg = [jnp.exp(gc[i]) for i in idx]
            rhs_u = [p["v"][i] * p["beta"][i] for i in idx]
            rhs_w = [p["kb"][i] * eg[i] for i in idx]
            qb = p["qmat"].astype(BF16)
            qs = [p["q"][i] * (GDN_HEAD_DIM ** -0.5) for i in idx]
            g_last = [gc[i][chunk - 1:chunk, :] for i in idx]
            cat = lambda xs: jnp.concatenate(xs, axis=1)
            qg_ref[b, rows, cols] = cat([qs[i] * eg[i] for i in idx]).astype(qg_ref.dtype)
            kd_ref[b, rows, cols] = cat([p["k"][i] * jnp.exp(g_last[i] - gc[i]) for i in idx]).astype(kd_ref.dtype)
            u_ref[b, rows, cols] = cat(rhs_u) + _dot(qb, block_rows(rhs_u))
            w_ref[b, rows, cols] = (cat(rhs_w) + _dot(qb, block_rows(rhs_w))).astype(w_ref.dtype)
            at_ref[b, rows, h0 * chunk:(h0 + grp) * chunk] = (
                _dot_nt(cat(qs).astype(BF16), p["k_rows"]) * p["dmask"]).astype(at_ref.dtype)
            el_ref[b, pl.ds(pl.multiple_of(p["c"] * SUBLANES, SUBLANES), SUBLANES), cols] = cat(
                [jnp.broadcast_to(jnp.exp(g_last[i]), (SUBLANES, LANES)) for i in idx])
        return carry

    lax.fori_loop(0, tblk // (chunk * cpi), step, 0)


def _gdn_scan_kernel(qg_ref, kd_ref, u_ref, w_ref, at_ref, el_ref, z_ref, s0_ref, nw_ref, *rest, chunk):
    o_ref, so_ref, s_scr = rest[-3:]
    tb = pl.program_id(1)

    @pl.when(tb == 0)
    def _():
        s_scr[...] = s0_ref[...]

    nw = nw_ref[...]
    probs = [(b, h) for b in range(qg_ref.shape[0]) for h in range(GDN_HEADS)]

    def step(c, carry):
        rows = pl.ds(pl.multiple_of(c * chunk, chunk), chunk)
        erow = pl.ds(pl.multiple_of(c * SUBLANES, SUBLANES), SUBLANES)
        cols = lambda h: slice(h * LANES, (h + 1) * LANES)
        s_prev = [s_scr[b, h] for b, h in probs]
        sb = [s.astype(BF16) for s in s_prev]
        v_new = [u_ref[b, rows, cols(h)] - _dot(w_ref[b, rows, cols(h)].astype(BF16), sb[i])
                 for i, (b, h) in enumerate(probs)]
        vb = [v.astype(BF16) for v in v_new]
        for i, (b, h) in enumerate(probs):
            el = el_ref[b, erow, cols(h)][0:1, :]
            s_scr[b, h] = s_prev[i] * el + _dot_tn(kd_ref[b, rows, cols(h)].astype(BF16), vb[i])
        for i, (b, h) in enumerate(probs):
            attn = at_ref[b, rows, h * chunk:(h + 1) * chunk].astype(BF16)
            o = _dot(qg_ref[b, rows, cols(h)].astype(BF16), sb[i]) + _dot(attn, vb[i])
            ms = jnp.mean(o * o, axis=-1, keepdims=True)
            o_ref[b, rows, cols(h)] = (o * lax.rsqrt(ms + NORM_EPS) * nw * _silu(z_ref[b, rows, cols(h)])).astype(
                o_ref.dtype)
        return carry

    nchunk = qg_ref.shape[1] // chunk
    lax.fori_loop(0, nchunk, step, 0, unroll=min(GDN_SCAN_UNROLL, nchunk))

    @pl.when(tb == pl.num_programs(1) - 1)
    def _():
        so_ref[...] = s_scr[...]


def _gated_delta(cv, conv_done, z, ba, conv_hist, states, layer, conv_w, a_log, dt_bias, gdn_norm, nb, t,
                 new_states):
    chunk = CHUNK if t % CHUNK == 0 else t
    tblk = min(t, 512)
    assert conv_done or tblk == t
    nchunk_blk = tblk // chunk
    hist_pad, cw_pad = _conv_operands(conv_hist, conv_w)
    ab = jnp.zeros((SUBLANES, LANES), F32)
    ab = ab.at[0, GDN_HEADS:2 * GDN_HEADS].set(a_log).at[1, GDN_HEADS:2 * GDN_HEADS].set(dt_bias)
    cv3 = cv.reshape(nb, t, CONV_DIM)
    pblk = _seqs_per_step(nb, t, 2)
    tok = lambda w: pl.BlockSpec((pblk, tblk, w), lambda b, i: (b, i, 0))
    full = lambda a: pl.BlockSpec(a.shape, lambda b, i: (0,) * a.ndim)
    el_spec =pl.BlockSpec((pblk, nchunk_blk * SUBLANES, GDN_W), lambda b, i: (b, i, 0))
    tok_shape = lambda w, dt=F32: jax.ShapeDtypeStruct((nb, t, w), dt)
    el_shape = jax.ShapeDtypeStruct((nb, (t // chunk) * SUBLANES, GDN_W), F32)
    opd = BF16 if chunk % (2 * SUBLANES) == 0 else F32
    qg, kd, u, w, attn, el = pl.pallas_call(
        functools.partial(_gdn_prep_kernel, chunk=chunk, cpi=min(GDN_PREP_CHUNKS_PER_ITER, nchunk_blk)),
        grid=(nb // pblk, t // tblk),
        in_specs=[tok(CONV_DIM), pl.BlockSpec((pblk, SUBLANES, CONV_DIM), lambda b, i: (b, 0, 0)),
                  tok(LANES), full(cw_pad), full(ab)],
        out_specs=[tok(GDN_W), tok(GDN_W), tok(GDN_W), tok(GDN_W), tok(GDN_HEADS * chunk), el_spec],
        out_shape=[tok_shape(GDN_W, opd), tok_shape(GDN_W, opd), tok_shape(GDN_W), tok_shape(GDN_W, opd),
                   tok_shape(GDN_HEADS * chunk, opd), el_shape],
        scratch_shapes=[pltpu.VMEM((pblk, tblk, LANES), F32), pltpu.VMEM((pblk, tblk, LANES), F32)] + (
            [] if conv_done else [pltpu.VMEM((pblk, CONV_DIM // LANES, tblk + SUBLANES, LANES), F32),
                                  pltpu.VMEM((pblk, tblk, CONV_DIM), F32)]),
        compiler_params=_cparams("parallel", "parallel"),
        name="gdn_prep",
    )(cv3, hist_pad, ba.reshape(nb, t, LANES), cw_pad, ab)

    bblk = _seqs_per_step(nb, t, GDN_SCAN_SEQS_PER_STEP)
    stok = lambda w: pl.BlockSpec((bblk, tblk, w), lambda b, i: (b, i, 0))
    sel_spec = pl.BlockSpec((bblk, nchunk_blk * SUBLANES, GDN_W), lambda b, i: (b, i, 0))
    st_in = pl.BlockSpec((None, bblk, GDN_HEADS, LANES, LANES), lambda b, i: (layer, b, 0, 0, 0))
    st = pl.BlockSpec((None, bblk, GDN_HEADS, LANES, LANES), lambda b, i: (layer, b, 0, 0, 0))
    nw = gdn_norm.reshape(1, LANES)
    carried = [] if new_states is None else [new_states]
    o, s_new = pl.pallas_call(
        functools.partial(_gdn_scan_kernel, chunk=chunk),
        grid=(nb // bblk, t // tblk),
        in_specs=[stok(GDN_W), stok(GDN_W), stok(GDN_W), stok(GDN_W), stok(GDN_HEADS * chunk), sel_spec,
                  stok(GDN_W), st_in, full(nw)] + [pl.BlockSpec(memory_space=pl.ANY)] * len(carried),
        out_specs=[stok(GDN_W), st],
        out_shape=[tok_shape(GDN_W, _mixer_out_dtype(t)), jax.ShapeDtypeStruct(states.shape, F32)],
        input_output_aliases={9: 1} if carried else {},
        scratch_shapes=[pltpu.VMEM((bblk, GDN_HEADS, LANES, LANES), F32)],
        compiler_params=_cparams("parallel", "arbitrary"),
        name="gdn_scan",
    )(qg, kd, u, w, attn, el, z.reshape(nb, t, GDN_W), states, nw, *carried)
    return o.reshape(nb * t, GDN_W), s_new


FFN_TILE = 256


def _out_ffn_kernel(x_ref, oa_ref, orr_ref, oc_ref, g1_ref, sh2_ref, sc2_ref, g2_ref, nw_ref, wo_ref, wg_ref,
                    wu_ref, wd_ref, fn_ref, out_ref, *, final):
    mix = (_dot(oa_ref[...].astype(BF16), wo_ref[0:DSA_W, :])
           + _dot(orr_ref[...].astype(BF16), wo_ref[DSA_W:DSA_W + RET_W, :])
           + _dot(oc_ref[...].astype(BF16), wo_ref[DSA_W + RET_W:, :]))
    x1 = x_ref[...] + g1_ref[...] * mix
    hb = _rms_mod(x1, nw_ref[...], sc2_ref[...], sh2_ref[...]).astype(BF16)
    acc = None
    for j in range(wg_ref.shape[1] // FFN_TILE):
        cols = slice(j * FFN_TILE, (j + 1) * FFN_TILE)
        act = (_silu(_dot(hb, wg_ref[:, cols])) * _dot(hb, wu_ref[:, cols])).astype(BF16)
        down = _dot(act, wd_ref[cols, :])
        acc = down if acc is None else acc + down
    x2 = x1 + g2_ref[...] * acc
    if final:
        ms = jnp.mean(x2 * x2, axis=-1, keepdims=True)
        x2 = x2 * lax.rsqrt(ms + NORM_EPS) * fn_ref[...]
    out_ref[...] = x2


def _out_ffn(x2, t, oa, orr, oc, mod3, norm_w, w_out_b, wg_b, wu_b, wd_b, layer, final_norm, final, tm):
    m, d = x2.shape
    row = lambda w: pl.BlockSpec((tm, w), lambda i: (i, 0))
    const = lambda a: pl.BlockSpec(a.shape, lambda i: (0,) * a.ndim, pipeline_mode=pl.Buffered(1))
    wspec = lambda a: pl.BlockSpec((None,) + a.shape[1:], lambda i: (layer,) + (0,) * (a.ndim - 1),
                                   pipeline_mode=pl.Buffered(1))
    nw = norm_w.reshape(1, d)
    fn = final_norm.reshape(1, d)
    return pl.pallas_call(
        functools.partial(_out_ffn_kernel, final=final),
        grid=(m // tm,),
        in_specs=[row(d), row(DSA_W), row(RET_W), row(GDN_W),
                  _mod_spec(tm, t, d, 2), _mod_spec(tm, t, d, 3), _mod_spec(tm, t, d, 4), _mod_spec(tm, t, d, 5),
                  const(nw), wspec(w_out_b), wspec(wg_b), wspec(wu_b), wspec(wd_b), const(fn)],
        out_specs=row(d),
        out_shape=jax.ShapeDtypeStruct((m, d), F32),
        compiler_params=_cparams("parallel"),
        name="out_ffn",
    )(x2, oa, orr, oc, mod3, mod3, mod3, mod3, nw, w_out_b, wg_b, wu_b, wd_b, fn)


ROPE_SPLIT = 64


def _rope_tables(pos0, t, inv_freq):
    reps = LANES // (2 * inv_freq.shape[0])
    inv_lane = jnp.tile(jnp.concatenate([inv_freq, inv_freq]), reps)[None, :]
    sign_lane = jnp.tile(jnp.concatenate([-jnp.ones_like(inv_freq), jnp.ones_like(inv_freq)]), reps)[None, :]
    if t % ROPE_SPLIT:
        ang = (pos0 + jnp.arange(t, dtype=jnp.int32)).astype(F32)[:, None] * inv_lane
        return jnp.cos(ang), jnp.sin(ang) * sign_lane
    coarse = (pos0 + ROPE_SPLIT * jnp.arange(t // ROPE_SPLIT, dtype=jnp.int32)).astype(F32)[:, None] * inv_lane
    fine = jnp.arange(ROPE_SPLIT, dtype=jnp.int32).astype(F32)[:, None] * inv_lane
    cc, sc = jnp.cos(coarse)[:, None, :], jnp.sin(coarse)[:, None, :]
    cf, sf = jnp.cos(fine)[None], jnp.sin(fine)[None]
    cos = (cc * cf - sc * sf).reshape(t, LANES)
    sin = (sc * cf + cc * sf).reshape(t, LANES)
    return cos, sin * sign_lane


def _trunk(x, modp, pos0, k_hist, v_hist, s_ret, s_gdn, conv_hist, wts):
    (norm_mix, norm_ffn, w_in_b, w_ba_b, ret_norm, conv_w, a_log, dt_bias, gdn_norm, w_out_b, wg_b, wu_b, wd_b,
     final_norm) = wts
    nb, t, d = x.shape
    m = nb * t
    depth = w_in_b.shape[0]
    tm = min(TOKEN_TILE, m)
    inv_a = 1.0 / (ROPE_THETA ** (jnp.arange(0, DSA_HEAD_DIM, 2, dtype=F32) / DSA_HEAD_DIM))
    inv_r = 1.0 / (10000.0 ** jnp.linspace(0.0, 1.0, RET_HEAD_DIM // 2, dtype=F32))
    tabs = _rope_tables(pos0, t, inv_a) + _rope_tables(pos0, t, inv_r)
    if t < tm:
        tabs = tuple(jnp.tile(a, (tm // t, 1)) for a in tabs)
    x2 = x.reshape(m, d)
    ks, vs, rs, cs = [], [], [], []
    gs = None
    for l in range(depth):
        if t >= tm:
            mod3 = modp[l].reshape(nb, 1, 6 * d)
        else:
            mod3 = jnp.repeat(modp[l], t, axis=0).reshape(m // tm, tm, 6 * d)
        (qa, ka, va, qr, kr, vr, gr, cv, z, ba), fused = _inproj(
            x2, t, mod3, norm_mix[l], w_in_b, w_ba_b, l, tabs, tm, conv_hist[l], conv_w[l])
        nhist = CONV_WIDTH - 1
        keep = min(t, DSA_MAX_WINDOW)
        if fused is None:
            cvn = jnp.concatenate([conv_hist[l], cv.reshape(nb, t, CONV_DIM)], axis=1)[:, -nhist:]
            window = lambda a: a.reshape(nb, t, DSA_W)[:, t - keep:].reshape(nb, keep, DSA_HEADS, DSA_HEAD_DIM)
            kwin, vwin = window(ka), window(va)
        else:
            conv_tail, k_t, v_t = fused
            cvn = conv_tail[:, SUBLANES - nhist:]
            untransposed = lambda a: a.reshape(nb, DSA_HEADS, DSA_HEAD_DIM, keep).transpose(0, 3, 1, 2)
            kwin, vwin = untransposed(k_t), untransposed(v_t)
        if k_hist is None:
            oa = _dsa_prompt(qa, ka, va, nb, t)
        else:
            oa = _dsa_step(qa, ka, va, k_hist, v_hist, l, nb, t)
        orr, sr = _retention(qr, kr, vr, gr, _to_block_diag(s_ret[l]), ret_norm[l], nb, t)
        oc, gs = _gated_delta(cv, fused is not None, z, ba, conv_hist[l], s_gdn, l, conv_w[l], a_log[l],
                              dt_bias[l], gdn_norm[l], nb, t, gs)
        x2 = _out_ffn(x2, t, oa, orr, oc, mod3, norm_ffn[l], w_out_b, wg_b, wu_b, wd_b, l,
                      final_norm, l == depth - 1, tm)
        ks.append(kwin)
        vs.append(vwin)
        rs.append(_from_block_diag(sr))
        cs.append(cvn)
    return (x2.reshape(nb, t, d), jnp.stack(ks), jnp.stack(vs), jnp.stack(rs), gs, jnp.stack(cs))


def kernel(x_prompt, x_sample, cache_win_k, cache_win_v, state_ret, state_gdn, state_conv, c_prompt, c_sample, ada_w, ada_b, norm_mix, norm_ffn, w_in, ret_norm, conv_w, a_log, dt_bias, gdn_norm, w_out, w_gate, w_up, w_down, final_norm):
    nb, t_p, d = x_prompt.shape
    db, t_s, _ = x_sample.shape
    depth = ada_w.shape[0]
    rows = nb + db
    rows_pad = -(-rows // SUBLANES) * SUBLANES
    c_all = jnp.concatenate([c_prompt, c_sample, jnp.zeros((rows_pad - rows, d), F32)], axis=0)
    mod = _modulation(c_all, ada_w, ada_b)
    w_ba_b = jnp.pad(w_in[:, :, _C_BA:], ((0, 0), (0, 0), (0, LANES - (IN_COLS - _C_BA)))).astype(BF16)
    wts = (norm_mix, norm_ffn, w_in.astype(BF16), w_ba_b, ret_norm, conv_w, a_log, dt_bias, gdn_norm, w_out.astype(BF16),
           w_gate.astype(BF16), w_up.astype(BF16), w_down.astype(BF16), final_norm)

    zr = jnp.zeros((depth, nb, RET_HEADS, RET_HEAD_DIM, RET_HEAD_DIM), F32)
    zg = jnp.zeros((depth, nb, GDN_HEADS, GDN_HEAD_DIM, GDN_HEAD_DIM), F32)
    zc = jnp.zeros((depth, nb, CONV_WIDTH - 1, CONV_DIM), F32)
    y_p, kp, vp, rp, gp, cp = _trunk(x_prompt, mod[:, :nb], 0,
                                     None, None, zr, zg, zc, wts)
    wb = cache_win_k.shape[2]
    y_s, ks, vs, rs, gs, cs = _trunk(x_sample, mod[:, nb:rows], PAST_LEN,
                                     cache_win_k.reshape(depth, db, wb, DSA_W).transpose(0, 1, 3, 2),
                                     cache_win_v.reshape(depth, db, wb, DSA_W).transpose(0, 1, 3, 2),
                                     state_ret, state_gdn, state_conv, wts)
    return (y_p, y_s, kp, vp, rp, gp, cp, ks, vs, rs, gs, cs)
```

```python
import functools
import math

import jax
import jax.numpy as jnp
from jax import lax
from jax.experimental import pallas as pl
from jax.experimental.pallas import tpu as pltpu

F32 = jnp.float32
BF16 = jnp.bfloat16
HIGHEST = lax.Precision.HIGHEST

DSA_HEAD_DIM = 64
DSA_HEADS = 4
DSA_PATTERNS = ((128, 1), (512, 4), (2048, 16))
DSA_MAX_WINDOW = 2048
ROPE_THETA = 10000.0
RET_HEAD_DIM = 64
RET_HEADS = 4
GDN_HEAD_DIM = 128
GDN_HEADS = 4
CONV_WIDTH = 4
CHUNK = 64
NORM_EPS = 1e-6
PAST_LEN = 16384
LOG2_E = math.log2(math.e)

DSA_W = DSA_HEADS * DSA_HEAD_DIM
RET_W = RET_HEADS * RET_HEAD_DIM
GDN_W = GDN_HEADS * GDN_HEAD_DIM
CONV_DIM = 3 * GDN_W
LANES = 128
SUBLANES = 8
VMEM_LIMIT = 56 * 1024 * 1024
TOKEN_TILE = 512

_C_QA, _C_KA, _C_VA = 0, DSA_W, 2 * DSA_W
_C_QR = 3 * DSA_W
_C_KR, _C_VR, _C_GR = _C_QR + RET_W, _C_QR + 2 * RET_W, _C_QR + 3 * RET_W
_C_CV = _C_QR + 4 * RET_W
_C_Z = _C_CV + CONV_DIM
_C_BA = _C_Z + GDN_W
IN_COLS = _C_BA + 2 * GDN_HEADS


def _cparams(*sem):
    return pltpu.CompilerParams(dimension_semantics=sem, vmem_limit_bytes=VMEM_LIMIT)


def _dot(a, b, **kw):
    return jnp.dot(a, b, preferred_element_type=F32, **kw)


def _dot_nt(a, b, **kw):
    return lax.dot_general(a, b, (((1,), (1,)), ((), ())), preferred_element_type=F32, **kw)


def _dot_tn(a, b, **kw):
    return lax.dot_general(a, b, (((0,), (0,)), ((), ())), preferred_element_type=F32, **kw)


def _silu(x):
    return x * jax.nn.sigmoid(x)


def _mixer_out_dtype(t):
    return BF16 if t % (2 * SUBLANES) == 0 else F32


def _seqs_per_step(nb, t, base):
    return math.gcd(nb, base * (4 if t < CHUNK else 1))


def _mod_kernel(c_ref, w_ref, b_ref, o_ref):
    a = _silu(c_ref[...]).astype(BF16)
    o_ref[...] = _dot(a, w_ref[...].astype(BF16)) + b_ref[...]


def _modulation(c_all, ada_w, ada_b, tn=1536):
    depth, d, n = ada_w.shape
    bp = c_all.shape[0]
    return pl.pallas_call(
        _mod_kernel,
        grid=(depth, n // tn),
        in_specs=[
            pl.BlockSpec((bp, d), lambda l, j: (0, 0)),
            pl.BlockSpec((None, d, tn), lambda l, j: (l, 0, j)),
            pl.BlockSpec((None, 1, tn), lambda l, j: (l, 0, j)),
        ],
        out_specs=pl.BlockSpec((None, bp, tn), lambda l, j: (l, 0, j)),
        out_shape=jax.ShapeDtypeStruct((depth, bp, n), F32),
        compiler_params=_cparams("parallel", "parallel"),
        name="modulation",
    )(c_all, ada_w, ada_b.reshape(depth, 1, n))


def _rms_mod(x, nw, sc, sh):
    ms = jnp.mean(x * x, axis=-1, keepdims=True)
    return (x * lax.rsqrt(ms + NORM_EPS) * nw) * (1.0 + sc) + sh


def _causal_conv_silu(xp_scr, seq, cw_ref, cg, nrows):
    cols = slice(cg * LANES, (cg + 1) * LANES)
    acc = None
    for i in range(CONV_WIDTH):
        start = SUBLANES - (CONV_WIDTH - 1) + i
        term = xp_scr[seq, cg, pl.ds(start, nrows, stride=1), :] * cw_ref[i:i + 1, cols]
        acc = term if acc is None else acc + term
    return _silu(acc)


def _inproj_kernel(x_ref, nw_ref, sh_ref, sc_ref, w_ref, wba_ref, ca_ref, sa_ref, cr_ref, sr_ref, *rest,
                   tiles_per_seq, n_carried):
    if tiles_per_seq:
        hist_ref, cw_ref = rest[:2]
        rest = rest[2 + n_carried:]
    qa_ref, ka_ref, va_ref, qr_ref, kr_ref, vr_ref, gr_ref, cv_ref, z_ref, ba_ref = rest[:10]
    tm = x_ref.shape[0]
    if tiles_per_seq:
        tail_ref, kt_ref, vt_ref, xp_scr = rest[10:]

        @pl.when(pl.program_id(0) % tiles_per_seq == 0)
        def _():
            tail_ref[...] = hist_ref[...]

    hb = _rms_mod(x_ref[...], nw_ref[...], sc_ref[...], sh_ref[...]).astype(BF16)

    def proj(c0, width):
        return _dot(hb, w_ref[:, c0:c0 + width])

    lane = lax.broadcasted_iota(jnp.int32, (tm, DSA_W), 1)
    first_half = (lane % DSA_HEAD_DIM) < (DSA_HEAD_DIM // 2)

    def rope(y, cos, sin_signed):
        partner = jnp.where(first_half, pltpu.roll(y, DSA_W - DSA_HEAD_DIM // 2, 1),
                            pltpu.roll(y, DSA_HEAD_DIM // 2, 1))
        return y * cos + partner * sin_signed

    wide = lambda ref: jnp.concatenate([ref[...]] * (DSA_W // LANES), axis=1)
    ca, sa, cr, sr = wide(ca_ref), wide(sa_ref), wide(cr_ref), wide(sr_ref)
    qa_ref[...] = rope(proj(_C_QA, DSA_W), ca, sa) * (DSA_HEAD_DIM ** -0.5 * LOG2_E)
    ka = rope(proj(_C_KA, DSA_W), ca, sa)
    va = proj(_C_VA, DSA_W)
    ka_ref[...] = ka
    va_ref[...] = va
    if tiles_per_seq:
        kt_ref[...] = ka.T
        vt_ref[...] = va.T

    qr_ref[...] = rope(proj(_C_QR, RET_W), cr, sr)
    kr_ref[...] = rope(proj(_C_KR, RET_W), cr, sr) * (RET_HEAD_DIM ** -0.5)
    vr_ref[...] = proj(_C_VR, RET_W)
    gr_ref[...] = proj(_C_GR, RET_W)
    for s in range(CONV_DIM // GDN_W):
        y = proj(_C_CV + s * GDN_W, GDN_W)
        if not tiles_per_seq:
            cv_ref[:, s * GDN_W:(s + 1) * GDN_W] = y
            continue
        for j in range(GDN_W // LANES):
            cg = s * (GDN_W // LANES) + j
            cols = slice(cg * LANES, (cg + 1) * LANES)
            yj = y[:, j * LANES:(j + 1) * LANES]
            xp_scr[0, cg, :SUBLANES, :] = tail_ref[:, cols]
            xp_scr[0, cg, SUBLANES:, :] = yj
            cv_ref[:, cols] = _causal_conv_silu(xp_scr, 0, cw_ref, cg, tm)
            tail_ref[:, cols] = yj[tm - SUBLANES:, :]
    z_ref[...] = proj(_C_Z, GDN_W)
    ba_ref[...] = _dot(hb, wba_ref[...])


def _mod_spec(tm, t, d, col):
    if t >= tm:
        return pl.BlockSpec((None, 1, d), lambda i: ((i * tm) // t, 0, col))
    return pl.BlockSpec((None, tm, d), lambda i: (i, 0, col))


def _conv_operands(conv_hist, conv_w):
    nb = conv_hist.shape[0]
    hist_pad = jnp.concatenate([jnp.zeros((nb, SUBLANES - (CONV_WIDTH - 1), CONV_DIM), F32), conv_hist], axis=1)
    cw_pad = jnp.concatenate([conv_w, jnp.zeros((SUBLANES - CONV_WIDTH, CONV_DIM), F32)], axis=0)
    return hist_pad, cw_pad


def _inproj(x2, t, mod3, norm_w, w_in_b, w_ba_b, layer, tabs, tm, conv_hist, conv_w, carried):
    m, d = x2.shape
    nt = tabs[0].shape[0] // tm
    widths = (DSA_W,) * 3 + (RET_W,) * 4 + (CONV_DIM, GDN_W, LANES)
    tab_spec = pl.BlockSpec((tm, LANES), lambda i: (i % nt, 0))
    tiles_per_seq = t // tm if t % tm == 0 else 0
    window_tiles = min(t, DSA_MAX_WINDOW) // tm
    in_specs = [
        pl.BlockSpec((tm, d), lambda i: (i, 0)),
        pl.BlockSpec((1, d), lambda i: (0, 0)),
        _mod_spec(tm, t, d, 0),
        _mod_spec(tm, t, d, 1),
        pl.BlockSpec((None, d, IN_COLS), lambda i: (layer, 0, 0), pipeline_mode=pl.Buffered(1)),
        pl.BlockSpec((None, d, LANES), lambda i: (layer, 0, 0), pipeline_mode=pl.Buffered(1)),
        tab_spec, tab_spec, tab_spec, tab_spec,
    ]
    operands = [x2, norm_w.reshape(1, d), mod3, mod3, w_in_b, w_ba_b, *tabs]
    out_specs = [pl.BlockSpec((tm, w), lambda i: (i, 0)) for w in widths]
    out_shape = [jax.ShapeDtypeStruct((m, w), F32) for w in widths]
    scratch = []
    aliases = {}
    if tiles_per_seq:
        hist_pad, cw_pad = _conv_operands(conv_hist, conv_w)
        seq_tail = pl.BlockSpec((None, SUBLANES, CONV_DIM), lambda i: (i // tiles_per_seq, 0, 0))
        in_specs += [seq_tail, pl.BlockSpec(cw_pad.shape, lambda i: (0, 0))]
        operands += [hist_pad, cw_pad]
        out_specs.append(seq_tail)
        out_shape.append(jax.ShapeDtypeStruct((m // t, SUBLANES, CONV_DIM), F32))
        first_win = tiles_per_seq - window_tiles
        win_t = pl.BlockSpec((None, None, DSA_W, tm), lambda i: (layer, i // tiles_per_seq, 0,
                                                                  jnp.maximum(i % tiles_per_seq - first_win, 0)))
        out_specs += [win_t, win_t]
        depth = w_in_b.shape[0]
        out_shape += [jax.ShapeDtypeStruct((depth, m // t, DSA_W, window_tiles * tm), F32)] * 2
        aliases = {len(operands) + j: 11 + j for j in range(len(carried))}
        in_specs += [pl.BlockSpec(memory_space=pl.ANY)] * len(carried)
        operands += list(carried)
        scratch.append(pltpu.VMEM((1, CONV_DIM // LANES, tm + SUBLANES, LANES), F32))
    outs = pl.pallas_call(
        functools.partial(_inproj_kernel, tiles_per_seq=tiles_per_seq, n_carried=len(carried) if tiles_per_seq else 0),
        grid=(m // tm,),
        in_specs=in_specs,
        out_specs=out_specs,
        out_shape=out_shape,
        scratch_shapes=scratch,
        input_output_aliases=aliases,
        compiler_params=_cparams("arbitrary"),
        name="inproj",
    )(*operands)
    return (outs[:10], outs[10:]) if tiles_per_seq else (outs, None)


DSA_BLK = 128


DSA_QBLK = DSA_MAX_WINDOW
DSA_GROUP = 4


def _dsa_kernel(q_ref, kp_ref, kc_ref, vp_ref, vc_ref, o_ref, acc_scr, m_scr, l_scr):
    blk = pl.program_id(2)
    qblk = q_ref.shape[0]
    row = lax.broadcasted_iota(jnp.int32, (2 * DSA_BLK, 2 * DSA_BLK), 0) % DSA_BLK
    col = lax.broadcasted_iota(jnp.int32, (2 * DSA_BLK, 2 * DSA_BLK), 1)
    ok = jnp.logical_and(col >= row, col <= row + DSA_BLK)
    ok_first = jnp.logical_and(ok, jnp.logical_or(col >= DSA_BLK, blk > 0))
    lo = lax.broadcasted_iota(jnp.int32, (DSA_BLK, LANES), 1) < DSA_HEAD_DIM
    neg = -jnp.inf
    halves = lambda x: jnp.where(lo, x[:DSA_BLK], x[DSA_BLK:])

    def rows_of(dil, start, n):
        return pl.ds(start, n) if dil == 1 else pl.ds(start, n, stride=dil)

    def group_softmax(dil, subs):
        n = range(len(subs))
        idx = [rows_of(dil, ph + dil * DSA_BLK * j, DSA_BLK) for ph, j in subs]
        q = [q_ref[i, :] for i in idx]
        k, v = [], []
        for u, (ph, j) in enumerate(subs):
            if j == 0:
                band = rows_of(dil, qblk - dil * DSA_BLK + ph, DSA_BLK)
                k.append(jnp.concatenate([kp_ref[band, :], kc_ref[idx[u], :]], axis=0).astype(BF16))
                v.append(jnp.concatenate([vp_ref[band, :], vc_ref[idx[u], :]], axis=0).astype(BF16))
            else:
                both = rows_of(dil, ph + dil * DSA_BLK * (j - 1), 2 * DSA_BLK)
                k.append(kc_ref[both, :].astype(BF16))
                v.append(vc_ref[both, :].astype(BF16))
        q2 = [jnp.concatenate([jnp.where(lo, q[u], 0.0), jnp.where(lo, 0.0, q[u])], axis=0).astype(BF16)
              for u in n]
        s = [jnp.where(ok_first if subs[u][1] == 0 else ok, _dot_nt(q2[u], k[u]), neg) for u in n]
        mx = [jnp.max(s[u], axis=-1, keepdims=True) for u in n]
        p = [jnp.exp2(s[u] - mx[u]) for u in n]
        den = [jnp.sum(p[u], axis=-1, keepdims=True) for u in n]
        pv = [_dot(p[u].astype(BF16), v[u]) for u in n]
        return idx, [(halves(pv[u]), halves(jnp.broadcast_to(mx[u], (2 * DSA_BLK, LANES))),
                      halves(jnp.broadcast_to(den[u], (2 * DSA_BLK, LANES)))) for u in n]

    dils = sorted((d for _, d in DSA_PATTERNS), reverse=True)
    for pi, dil in enumerate(dils):
        subs = [(ph, j) for j in range(qblk // (dil * DSA_BLK)) for ph in range(dil)]
        for g0 in range(0, len(subs), DSA_GROUP):
            idx, tiles = group_softmax(dil, subs[g0:g0 + DSA_GROUP])
            for i, (pv, mx, den) in zip(idx, tiles):
                if pi > 0:
                    m_old = m_scr[i, :]
                    m_new = jnp.maximum(m_old, mx)
                    w_old = jnp.exp2(m_old - m_new)
                    w_cur = jnp.exp2(mx - m_new)
                    pv = acc_scr[i, :] * w_old + pv * w_cur
                    den = l_scr[i, :] * w_old + den * w_cur
                    mx = m_new
                if pi < len(dils) - 1:
                    acc_scr[i, :] = pv
                    m_scr[i, :] = mx
                    l_scr[i, :] = den
                else:
                    o_ref[i, :] = (pv / den).astype(o_ref.dtype)


def _dsa_prompt(q, k, v, nb, t):
    assert all(w // d == DSA_BLK for w, d in DSA_PATTERNS) and t % DSA_QBLK == 0
    ngrp = DSA_W // LANES
    r3 = lambda a: a.reshape(nb, t, DSA_W)
    cur = pl.BlockSpec((None, DSA_QBLK, LANES), lambda b, g, i: (b, i, g))
    prv = pl.BlockSpec((None, DSA_QBLK, LANES), lambda b, g, i: (b, jnp.maximum(i - 1, 0), g))
    return pl.pallas_call(
        _dsa_kernel,
        grid=(nb, ngrp, t // DSA_QBLK),
        in_specs=[cur, prv, cur, prv, cur],
        out_specs=cur,
        out_shape=jax.ShapeDtypeStruct((nb, t, DSA_W), _mixer_out_dtype(t)),
        scratch_shapes=[pltpu.VMEM((DSA_QBLK, LANES), F32)] * 3,
        compiler_params=_cparams("parallel", "parallel", "parallel"),
        name="dsa_prompt",
    )(r3(q), r3(k), r3(k), r3(v), r3(v)).reshape(nb * t, DSA_W)


DSA_STEP_SEQS = 2


def _multiplicity(dist):
    total = jnp.zeros(dist.shape, F32)
    for window, dil in DSA_PATTERNS:
        hit = (dist >= 0) & (dist <= window) & ((dist & (dil - 1)) == 0)
        total = total + hit.astype(F32)
    return total


def _dsa_step_kernel(q_ref, kc_ref, vc_ref, kn_ref, vn_ref, o_ref):
    nseq, tq = q_ref.shape[0], q_ref.shape[1]
    wb = kc_ref.shape[2]
    qi = lax.broadcasted_iota(jnp.int32, (2 * tq, wb), 0) % tq
    w_c = _multiplicity(wb + qi - lax.broadcasted_iota(jnp.int32, (2 * tq, wb), 1))
    qn = lax.broadcasted_iota(jnp.int32, (2 * tq, LANES), 0) % tq
    nn = lax.broadcasted_iota(jnp.int32, (2 * tq, LANES), 1)
    w_n = jnp.where(nn < tq, _multiplicity(qn - nn), 0.0)
    lo = lax.broadcasted_iota(jnp.int32, (tq, LANES), 1) < DSA_HEAD_DIM
    pad = jnp.zeros((LANES - tq, LANES), F32)
    neg = -jnp.inf
    for b, g in [(b, g) for b in range(nseq) for g in range(DSA_W // LANES)]:
        cols = slice(g * LANES, (g + 1) * LANES)
        q = q_ref[b, :, cols]
        q2 = jnp.concatenate([jnp.where(lo, q, 0.0), jnp.where(lo, 0.0, q)], axis=0).astype(BF16)
        kt_c, vt_c = kc_ref[b, cols, :].astype(BF16), vc_ref[b, cols, :].astype(BF16)
        k_n = jnp.concatenate([kn_ref[b, :, cols], pad], axis=0).astype(BF16)
        v_n = jnp.concatenate([vn_ref[b, :, cols], pad], axis=0).astype(BF16)
        s_c = jnp.where(w_c > 0, _dot(q2, kt_c), neg)
        s_n = jnp.where(w_n > 0, _dot_nt(q2, k_n), neg)
        mx = jnp.maximum(jnp.max(s_c, axis=-1, keepdims=True), jnp.max(s_n, axis=-1, keepdims=True))
        p_c = w_c * jnp.exp2(s_c - mx)
        p_n = w_n * jnp.exp2(s_n - mx)
        den = jnp.sum(p_c, axis=-1, keepdims=True) + jnp.sum(p_n, axis=-1, keepdims=True)
        o2 = (_dot_nt(p_c.astype(BF16), vt_c) + _dot(p_n.astype(BF16), v_n)) / den
        o_ref[b, :, cols] = jnp.where(lo, o2[:tq], o2[tq:])


def _dsa_step(q, k_new, v_new, k_cache, v_cache, layer, nb, t):
    wb = k_cache.shape[3]
    bblk = math.gcd(nb, DSA_STEP_SEQS)
    new = pl.BlockSpec((bblk, t, DSA_W), lambda b: (b, 0, 0))
    cache = pl.BlockSpec((None, bblk, DSA_W, wb), lambda b: (layer, b, 0, 0))
    r3 = lambda a: a.reshape(nb, t, DSA_W)
    return pl.pallas_call(
        _dsa_step_kernel,
        grid=(nb // bblk,),
        in_specs=[new, cache, cache, new, new],
        out_specs=new,
        out_shape=jax.ShapeDtypeStruct((nb, t, DSA_W), F32),
        compiler_params=_cparams("parallel"),
        name="dsa_step",
    )(r3(q), k_cache, v_cache, r3(k_new), r3(v_new)).reshape(nb * t, DSA_W)


RET_CHUNK = 256
RET_SEQS_PER_STEP = 2
RET_UNROLL = 4


def _ret_kernel(q_ref, k_ref, v_ref, g_ref, s0_ref, dec_ref, qd_ref, kd_ref, cd_ref, bd_ref, nw_ref,
                o_ref, so_ref, s_scr, *, chunk):
    tb = pl.program_id(1)
    ngrp = RET_W // LANES

    @pl.when(tb == 0)
    def _():
        s_scr[...] = s0_ref[...]

    lo = lax.broadcasted_iota(jnp.int32, (chunk, LANES), 1) < RET_HEAD_DIM
    nw = nw_ref[...]
    bd = bd_ref[...]

    def body(c, carry):
        rows = pl.ds(pl.multiple_of(c * chunk, chunk), chunk)
        for b in range(q_ref.shape[0]):
            for g in range(ngrp):
                cols = slice(g * LANES, (g + 1) * LANES)
                q, k, v = q_ref[b, rows, cols], k_ref[b, rows, cols], v_ref[b, rows, cols]
                kb, vb = k.astype(BF16), v.astype(BF16)
                parts = []
                for hh in range(2):
                    qm = jnp.where(lo if hh == 0 else jnp.logical_not(lo), q, 0.0).astype(BF16)
                    inner = _dot_nt(qm, kb) * dec_ref[2 * g + hh]
                    parts.append(_dot(inner.astype(BF16), vb))
                s_prev = s_scr[b, g]
                o = jnp.where(lo, parts[0], parts[1]) + _dot(q.astype(BF16), s_prev.astype(BF16)) * qd_ref[g]
                s_scr[b, g] = s_prev * cd_ref[g] + bd * _dot_tn((k * kd_ref[g]).astype(BF16), vb)
                o2 = o * o
                ms = jnp.where(lo, jnp.sum(jnp.where(lo, o2, 0.0), axis=-1, keepdims=True),
                               jnp.sum(jnp.where(lo, 0.0, o2), axis=-1, keepdims=True)) * (1.0 / RET_HEAD_DIM)
                o_ref[b, rows, cols] = (o * lax.rsqrt(ms + NORM_EPS) * nw * _silu(g_ref[b, rows, cols])).astype(
                    o_ref.dtype)
        return carry

    nchunk = q_ref.shape[1] // chunk
    lax.fori_loop(0, nchunk, body, 0, unroll=min(RET_UNROLL, nchunk))

    @pl.when(tb == pl.num_programs(1) - 1)
    def _():
        so_ref[...] = s_scr[...]


def _ret_tables(chunk):
    log_gamma = jnp.log(1.0 - 2.0 ** (-5.0 - jnp.arange(RET_HEADS, dtype=F32)))
    i = jnp.arange(chunk, dtype=F32)
    diff = i[:, None] - i[None, :]
    causal = diff >= 0
    decay = jnp.where(causal[None], jnp.exp(log_gamma[:, None, None] * jnp.where(causal, diff, 0.0)[None]), 0.0)
    per_lane = lambda a: jnp.repeat(a, RET_HEAD_DIM, axis=0).reshape(RET_W // LANES, LANES, -1)
    q_dec = per_lane(jnp.exp(log_gamma[:, None] * (i[None, :] + 1.0))).transpose(0, 2, 1)
    k_dec = per_lane(jnp.exp(log_gamma[:, None] * (chunk - 1.0 - i)[None, :])).transpose(0, 2, 1)
    c_dec = jnp.broadcast_to(per_lane(jnp.exp(log_gamma * chunk)[:, None]), (RET_W // LANES, LANES, LANES))
    head_of = jnp.arange(LANES) // RET_HEAD_DIM
    block_diag = (head_of[:, None] == head_of[None, :]).astype(F32)
    return decay, q_dec, k_dec, c_dec, block_diag


def _retention(q, k, v, gate, state_bd, ret_norm, nb, t):
    chunk = RET_CHUNK if t % RET_CHUNK == 0 else t
    tblk = min(t, 1024)
    ngrp = RET_W // LANES
    bblk = _seqs_per_step(nb, t, RET_SEQS_PER_STEP)
    decay, q_dec, k_dec, c_dec, block_diag = _ret_tables(chunk)
    r3 = lambda a: a.reshape(nb, t, RET_W)
    tok = pl.BlockSpec((bblk, tblk, RET_W), lambda b, i: (b, i, 0))
    st = pl.BlockSpec((bblk, ngrp, LANES, LANES), lambda b, i: (b, 0, 0, 0))
    full = lambda a: pl.BlockSpec(a.shape, lambda b, i: (0,) * a.ndim)
    nw = jnp.tile(ret_norm, LANES // RET_HEAD_DIM).reshape(1, LANES)
    o, s_new = pl.pallas_call(
        functools.partial(_ret_kernel, chunk=chunk),
        grid=(nb // bblk, t // tblk),
        in_specs=[tok, tok, tok, tok, st, full(decay), full(q_dec), full(k_dec), full(c_dec),
                  full(block_diag), full(nw)],
        out_specs=[tok, st],
        out_shape=[jax.ShapeDtypeStruct((nb, t, RET_W), _mixer_out_dtype(t)),
                   jax.ShapeDtypeStruct((nb, ngrp, LANES, LANES), F32)],
        scratch_shapes=[pltpu.VMEM((bblk, ngrp, LANES, LANES), F32)],
        compiler_params=_cparams("parallel", "arbitrary"),
        name="retention",
    )(r3(q), r3(k), r3(v), r3(gate), state_bd, decay, q_dec, k_dec, c_dec, block_diag, nw)
    return o.reshape(nb * t, RET_W), s_new


def _to_block_diag(s):
    nb = s.shape[0]
    s = s.reshape(nb, 2, 2, RET_HEAD_DIM, RET_HEAD_DIM)
    z = jnp.zeros_like(s[:, :, 0])
    top = jnp.concatenate([s[:, :, 0], z], axis=-1)
    bot = jnp.concatenate([z, s[:, :, 1]], axis=-1)
    return jnp.concatenate([top, bot], axis=-2)


def _from_block_diag(s):
    h = RET_HEAD_DIM
    return jnp.stack([s[:, :, :h, :h], s[:, :, h:, h:]], axis=2).reshape(s.shape[0], RET_HEADS, h, h)


GDN_PREP_CHUNKS_PER_ITER = 8
GDN_SCAN_SEQS_PER_STEP = 2
GDN_SCAN_UNROLL = 4


def _softplus(x):
    return jnp.maximum(x, 0.0) + jnp.log1p(jnp.exp(-jnp.abs(x)))


def _gdn_prep_kernel(cv_ref, hist_ref, ba_ref, cw_ref, ab_ref,
                     qg_ref, kd_ref, u_ref, w_ref, at_ref, el_ref, beta_scr, g_scr, *conv_scr, chunk, cpi):
    nseq, tblk = cv_ref.shape[0], cv_ref.shape[1]
    ncg = CONV_DIM // LANES
    ab = ab_ref[...]
    cs_scr = cv_ref
    for b in range(nseq):
        if conv_scr:
            xp_scr, cs_scr = conv_scr
            for cg in range(ncg):
                cols = slice(cg * LANES, (cg + 1) * LANES)
                xp_scr[b, cg, :SUBLANES, :] = hist_ref[b, :, cols]
                xp_scr[b, cg, SUBLANES:, :] = cv_ref[b, :, cols]
                cs_scr[b, :, cols] = _causal_conv_silu(xp_scr, b, cw_ref, cg, tblk)

        ba = ba_ref[b]
        beta_scr[b] = jax.nn.sigmoid(ba)
        g_scr[b] = -jnp.exp(ab[0:1, :]) * _softplus(ba + ab[1:2, :])

    grp = 2 if 2 * chunk == LANES else 1
    width = grp * chunk
    ri = lax.broadcasted_iota(jnp.int32, (chunk, width), 0)
    lane = lax.broadcasted_iota(jnp.int32, (chunk, width), 1)
    ci = lane % chunk
    first = lane < chunk
    incl = ri >= ci
    strict = ri > ci
    tri = (lax.broadcasted_iota(jnp.int32, (chunk, chunk), 0)
           >= lax.broadcasted_iota(jnp.int32, (chunk, chunk), 1)).astype(F32)
    lane_pick = (lax.broadcasted_iota(jnp.int32, (SUBLANES, LANES), 0)
                 == lax.broadcasted_iota(jnp.int32, (SUBLANES, LANES), 1)).astype(F32)
    base = min(SUBLANES, chunk)
    assert chunk % base == 0 and (chunk // base) & (chunk // base - 1) == 0
    same_blk = [(ri >> sh) == (ci >> sh) for sh in range(int(math.log2(base)), int(math.log2(chunk)) + 1)]

    def l2n(x):
        return x * lax.rsqrt(jnp.sum(x * x, axis=-1, keepdims=True) + NORM_EPS)

    def side_by_side(per_head):
        if grp == 1:
            return per_head[0][:, :width]
        return jnp.where(first, per_head[0][:, :width], per_head[1][:, :width])

    def block_diag(y):
        if grp == 1:
            return y.astype(BF16)
        return jnp.concatenate([jnp.where(first, y, 0.0), jnp.where(first, 0.0, y)], axis=0).astype(BF16)

    def block_rows(per_head):
        if grp == 1:
            return per_head[0].astype(BF16)
        a, b = per_head
        return jnp.concatenate([jnp.concatenate([a, jnp.zeros_like(b)], axis=1),
                                jnp.concatenate([jnp.zeros_like(a), b], axis=1)], axis=0).astype(BF16)

    def step(it, carry):
        probs = []
        for b, cc in [(b, cc) for b in range(nseq) for cc in range(cpi)]:
            c = it * cpi + cc
            rows = pl.ds(pl.multiple_of(c * chunk, chunk), chunk)
            beta_c = beta_scr[b, rows, :]
            gcol = _dot(tri, g_scr[b, rows, :], precision=HIGHEST)
            grow = _dot_nt(lane_pick, jnp.concatenate([gcol] * grp, axis=0), precision=HIGHEST)
            for h0 in range(0, GDN_HEADS, grp):
                hs = range(h0, h0 + grp)
                q = [l2n(cs_scr[b, rows, h * LANES:(h + 1) * LANES]) for h in hs]
                k = [l2n(cs_scr[b, rows, GDN_W + h * LANES:GDN_W + (h + 1) * LANES]) for h in hs]
                v = [cs_scr[b, rows, 2 * GDN_W + h * LANES:2 * GDN_W + (h + 1) * LANES] for h in hs]
                beta = [jnp.broadcast_to(beta_c[:, h:h + 1], (chunk, LANES)) for h in hs]
                gc = [jnp.broadcast_to(gcol[:, GDN_HEADS + h:GDN_HEADS + h + 1], (chunk, LANES)) for h in hs]
                g_row = side_by_side([jnp.broadcast_to(grow[GDN_HEADS + h:GDN_HEADS + h + 1, :], (chunk, width))
                                      for h in hs])
                gdiff = side_by_side(gc) - g_row
                dmask = jnp.where(incl, jnp.exp(jnp.where(incl, gdiff, 0.0)), 0.0)
                kb = [k[i] * beta[i] for i in range(grp)]
                k_rows = block_rows(k)
                lower = jnp.where(strict, _dot_nt(jnp.concatenate(kb, axis=1).astype(BF16), k_rows) * dmask, 0.0)
                npow = jnp.where(same_blk[0], -lower, 0.0)
                probs.append(dict(b=b, c=c, rows=rows, h0=h0, q=q, k=k, v=v, beta=beta, gc=gc, dmask=dmask,
                                  kb=kb, k_rows=k_rows, lower=lower, npow=npow, qmat=npow))
        for _ in range(int(math.log2(base)) - 1):
            for p in probs:
                p["npow"] = _dot(p["npow"].astype(BF16), block_diag(p["npow"]))
            for p in probs:
                p["qmat"] = p["qmat"] + p["npow"] + _dot(p["qmat"].astype(BF16), block_diag(p["npow"]))
        for lvl in range(1, len(same_blk)):
            for p in probs:
                cb = jnp.where(jnp.logical_and(same_blk[lvl], jnp.logical_not(same_blk[lvl - 1])),
                               p["lower"], 0.0)
                p["x"] = cb + _dot(p["qmat"].astype(BF16), block_diag(cb))
            for p in probs:
                p["qmat"] = p["qmat"] - p["x"] - _dot(p["x"].astype(BF16), block_diag(p["qmat"]))
        for p in probs:
            b, rows, h0, gc = p["b"], p["rows"], p["h0"], p["gc"]
            cols = slice(h0 * LANES, (h0 + grp) * LANES)
            idx = range(grp)
            eg = [jnp.exp(gc[i]) for i in idx]
            rhs_u = [p["v"][i] * p["beta"][i] for i in idx]
            rhs_w = [p["kb"][i] * eg[i] for i in idx]
            qb = p["qmat"].astype(BF16)
            qs = [p["q"][i] * (GDN_HEAD_DIM ** -0.5) for i in idx]
            g_last = [gc[i][chunk - 1:chunk, :] for i in idx]
            cat = lambda xs: jnp.concatenate(xs, axis=1)
            qg_ref[b, rows, cols] = cat([qs[i] * eg[i] for i in idx]).astype(qg_ref.dtype)
            kd_ref[b, rows, cols] = cat([p["k"][i] * jnp.exp(g_last[i] - gc[i]) for i in idx]).astype(kd_ref.dtype)
            u_ref[b, rows, cols] = cat(rhs_u) + _dot(qb, block_rows(rhs_u))
            w_ref[b, rows, cols] = (cat(rhs_w) + _dot(qb, block_rows(rhs_w))).astype(w_ref.dtype)
            at_ref[b, rows, h0 * chunk:(h0 + grp) * chunk] = (
                _dot_nt(cat(qs).astype(BF16), p["k_rows"]) * p["dmask"]).astype(at_ref.dtype)
            el_ref[b, pl.ds(pl.multiple_of(p["c"] * SUBLANES, SUBLANES), SUBLANES), cols] = cat(
                [jnp.broadcast_to(jnp.exp(g_last[i]), (SUBLANES, LANES)) for i in idx])
        return carry

    lax.fori_loop(0, tblk // (chunk * cpi), step, 0)


def _gdn_scan_kernel(qg_ref, kd_ref, u_ref, w_ref, at_ref, el_ref, z_ref, s0_ref, nw_ref, *rest, chunk):
    o_ref, so_ref, s_scr = rest[-3:]
    tb = pl.program_id(1)

    @pl.when(tb == 0)
    def _():
        s_scr[...] = s0_ref[...]

    nw = nw_ref[...]
    probs = [(b, h) for b in range(qg_ref.shape[0]) for h in range(GDN_HEADS)]

    def step(c, carry):
        rows = pl.ds(pl.multiple_of(c * chunk, chunk), chunk)
        erow = pl.ds(pl.multiple_of(c * SUBLANES, SUBLANES), SUBLANES)
        cols = lambda h: slice(h * LANES, (h + 1) * LANES)
        s_prev = [s_scr[b, h] for b, h in probs]
        sb = [s.astype(BF16) for s in s_prev]
        v_new = [u_ref[b, rows, cols(h)] - _dot(w_ref[b, rows, cols(h)].astype(BF16), sb[i])
                 for i, (b, h) in enumerate(probs)]
        vb = [v.astype(BF16) for v in v_new]
        for i, (b, h) in enumerate(probs):
            el = el_ref[b, erow, cols(h)][0:1, :]
            s_scr[b, h] = s_prev[i] * el + _dot_tn(kd_ref[b, rows, cols(h)].astype(BF16), vb[i])
        for i, (b, h) in enumerate(probs):
            attn = at_ref[b, rows, h * chunk:(h + 1) * chunk].astype(BF16)
            o = _dot(qg_ref[b, rows, cols(h)].astype(BF16), sb[i]) + _dot(attn, vb[i])
            ms = jnp.mean(o * o, axis=-1, keepdims=True)
            o_ref[b, rows, cols(h)] = (o * lax.rsqrt(ms + NORM_EPS) * nw * _silu(z_ref[b, rows, cols(h)])).astype(
                o_ref.dtype)
        return carry

    nchunk = qg_ref.shape[1] // chunk
    lax.fori_loop(0, nchunk, step, 0, unroll=min(GDN_SCAN_UNROLL, nchunk))

    @pl.when(tb == pl.num_programs(1) - 1)
    def _():
        so_ref[...] = s_scr[...]


def _gated_delta(cv, conv_done, z, ba, conv_hist, states, layer, conv_w, a_log, dt_bias, gdn_norm, nb, t,
                 new_states):
    chunk = CHUNK if t % CHUNK == 0 else t
    tblk = min(t, 512)
    assert conv_done or tblk == t
    nchunk_blk = tblk // chunk
    hist_pad, cw_pad = _conv_operands(conv_hist, conv_w)
    ab = jnp.zeros((SUBLANES, LANES), F32)
    ab = ab.at[0, GDN_HEADS:2 * GDN_HEADS].set(a_log).at[1, GDN_HEADS:2 * GDN_HEADS].set(dt_bias)
    cv3 = cv.reshape(nb, t, CONV_DIM)
    pblk = _seqs_per_step(nb, t, 2)
    tok = lambda w: pl.BlockSpec((pblk, tblk, w), lambda b, i: (b, i, 0))
    full = lambda a: pl.BlockSpec(a.shape, lambda b, i: (0,) * a.ndim)
    el_spec =pl.BlockSpec((pblk, nchunk_blk * SUBLANES, GDN_W), lambda b, i: (b, i, 0))
    tok_shape = lambda w, dt=F32: jax.ShapeDtypeStruct((nb, t, w), dt)
    el_shape = jax.ShapeDtypeStruct((nb, (t // chunk) * SUBLANES, GDN_W), F32)
    opd = BF16 if chunk % (2 * SUBLANES) == 0 else F32
    qg, kd, u, w, attn, el = pl.pallas_call(
        functools.partial(_gdn_prep_kernel, chunk=chunk, cpi=min(GDN_PREP_CHUNKS_PER_ITER, nchunk_blk)),
        grid=(nb // pblk, t // tblk),
        in_specs=[tok(CONV_DIM), pl.BlockSpec((pblk, SUBLANES, CONV_DIM), lambda b, i: (b, 0, 0)),
                  tok(LANES), full(cw_pad), full(ab)],
        out_specs=[tok(GDN_W), tok(GDN_W), tok(GDN_W), tok(GDN_W), tok(GDN_HEADS * chunk), el_spec],
        out_shape=[tok_shape(GDN_W, opd), tok_shape(GDN_W, opd), tok_shape(GDN_W), tok_shape(GDN_W, opd),
                   tok_shape(GDN_HEADS * chunk, opd), el_shape],
        scratch_shapes=[pltpu.VMEM((pblk, tblk, LANES), F32), pltpu.VMEM((pblk, tblk, LANES), F32)] + (
            [] if conv_done else [pltpu.VMEM((pblk, CONV_DIM // LANES, tblk + SUBLANES, LANES), F32),
                                  pltpu.VMEM((pblk, tblk, CONV_DIM), F32)]),
        compiler_params=_cparams("parallel", "parallel"),
        name="gdn_prep",
    )(cv3, hist_pad, ba.reshape(nb, t, LANES), cw_pad, ab)

    bblk = _seqs_per_step(nb, t, GDN_SCAN_SEQS_PER_STEP)
    stok = lambda w: pl.BlockSpec((bblk, tblk, w), lambda b, i: (b, i, 0))
    sel_spec = pl.BlockSpec((bblk, nchunk_blk * SUBLANES, GDN_W), lambda b, i: (b, i, 0))
    st_in = pl.BlockSpec((None, bblk, GDN_HEADS, LANES, LANES), lambda b, i: (layer, b, 0, 0, 0))
    st = pl.BlockSpec((None, bblk, GDN_HEADS, LANES, LANES), lambda b, i: (layer, b, 0, 0, 0))
    nw = gdn_norm.reshape(1, LANES)
    carried = [] if new_states is None else [new_states]
    o, s_new = pl.pallas_call(
        functools.partial(_gdn_scan_kernel, chunk=chunk),
        grid=(nb // bblk, t // tblk),
        in_specs=[stok(GDN_W), stok(GDN_W), stok(GDN_W), stok(GDN_W), stok(GDN_HEADS * chunk), sel_spec,
                  stok(GDN_W), st_in, full(nw)] + [pl.BlockSpec(memory_space=pl.ANY)] * len(carried),
        out_specs=[stok(GDN_W), st],
        out_shape=[tok_shape(GDN_W, _mixer_out_dtype(t)), jax.ShapeDtypeStruct(states.shape, F32)],
        input_output_aliases={9: 1} if carried else {},
        scratch_shapes=[pltpu.VMEM((bblk, GDN_HEADS, LANES, LANES), F32)],
        compiler_params=_cparams("parallel", "arbitrary"),
        name="gdn_scan",
    )(qg, kd, u, w, attn, el, z.reshape(nb, t, GDN_W), states, nw, *carried)
    return o.reshape(nb * t, GDN_W), s_new


FFN_TILE = 256


def _out_ffn_kernel(x_ref, oa_ref, orr_ref, oc_ref, g1_ref, sh2_ref, sc2_ref, g2_ref, nw_ref, wo_ref, wg_ref,
                    wu_ref, wd_ref, fn_ref, out_ref, *, final):
    mix = (_dot(oa_ref[...].astype(BF16), wo_ref[0:DSA_W, :])
           + _dot(orr_ref[...].astype(BF16), wo_ref[DSA_W:DSA_W + RET_W, :])
           + _dot(oc_ref[...].astype(BF16), wo_ref[DSA_W + RET_W:, :]))
    x1 = x_ref[...] + g1_ref[...] * mix
    hb = _rms_mod(x1, nw_ref[...], sc2_ref[...], sh2_ref[...]).astype(BF16)
    acc = None
    for j in range(wg_ref.shape[1] // FFN_TILE):
        cols = slice(j * FFN_TILE, (j + 1) * FFN_TILE)
        act = (_silu(_dot(hb, wg_ref[:, cols])) * _dot(hb, wu_ref[:, cols])).astype(BF16)
        down = _dot(act, wd_ref[cols, :])
        acc = down if acc is None else acc + down
    x2 = x1 + g2_ref[...] * acc
    if final:
        ms = jnp.mean(x2 * x2, axis=-1, keepdims=True)
        x2 = x2 * lax.rsqrt(ms + NORM_EPS) * fn_ref[...]
    out_ref[...] = x2


def _out_ffn(x2, t, oa, orr, oc, mod3, norm_w, w_out_b, wg_b, wu_b, wd_b, layer, final_norm, final, tm):
    m, d = x2.shape
    row = lambda w: pl.BlockSpec((tm, w), lambda i: (i, 0))
    const = lambda a: pl.BlockSpec(a.shape, lambda i: (0,) * a.ndim, pipeline_mode=pl.Buffered(1))
    wspec = lambda a: pl.BlockSpec((None,) + a.shape[1:], lambda i: (layer,) + (0,) * (a.ndim - 1),
                                   pipeline_mode=pl.Buffered(1))
    nw = norm_w.reshape(1, d)
    fn = final_norm.reshape(1, d)
    return pl.pallas_call(
        functools.partial(_out_ffn_kernel, final=final),
        grid=(m // tm,),
        in_specs=[row(d), row(DSA_W), row(RET_W), row(GDN_W),
                  _mod_spec(tm, t, d, 2), _mod_spec(tm, t, d, 3), _mod_spec(tm, t, d, 4), _mod_spec(tm, t, d, 5),
                  const(nw), wspec(w_out_b), wspec(wg_b), wspec(wu_b), wspec(wd_b), const(fn)],
        out_specs=row(d),
        out_shape=jax.ShapeDtypeStruct((m, d), F32),
        compiler_params=_cparams("parallel"),
        name="out_ffn",
    )(x2, oa, orr, oc, mod3, mod3, mod3, mod3, nw, w_out_b, wg_b, wu_b, wd_b, fn)


ROPE_SPLIT = 64


def _rope_tables(pos0, t, inv_freq):
    reps = LANES // (2 * inv_freq.shape[0])
    inv_lane = jnp.tile(jnp.concatenate([inv_freq, inv_freq]), reps)[None, :]
    sign_lane = jnp.tile(jnp.concatenate([-jnp.ones_like(inv_freq), jnp.ones_like(inv_freq)]), reps)[None, :]
    if t % ROPE_SPLIT:
        ang = (pos0 + jnp.arange(t, dtype=jnp.int32)).astype(F32)[:, None] * inv_lane
        return jnp.cos(ang), jnp.sin(ang) * sign_lane
    coarse = (pos0 + ROPE_SPLIT * jnp.arange(t // ROPE_SPLIT, dtype=jnp.int32)).astype(F32)[:, None] * inv_lane
    fine = jnp.arange(ROPE_SPLIT, dtype=jnp.int32).astype(F32)[:, None] * inv_lane
    cc, sc = jnp.cos(coarse)[:, None, :], jnp.sin(coarse)[:, None, :]
    cf, sf = jnp.cos(fine)[None], jnp.sin(fine)[None]
    cos = (cc * cf - sc * sf).reshape(t, LANES)
    sin = (sc * cf + cc * sf).reshape(t, LANES)
    return cos, sin * sign_lane


def _trunk(x, modp, pos0, k_hist, v_hist, s_ret, s_gdn, conv_hist, wts):
    (norm_mix, norm_ffn, w_in_b, w_ba_b, ret_norm, conv_w, a_log, dt_bias, gdn_norm, w_out_b, wg_b, wu_b, wd_b,
     final_norm) = wts
    nb, t, d = x.shape
    m = nb * t
    depth = w_in_b.shape[0]
    tm = min(TOKEN_TILE, m)
    inv_a = 1.0 / (ROPE_THETA ** (jnp.arange(0, DSA_HEAD_DIM, 2, dtype=F32) / DSA_HEAD_DIM))
    inv_r = 1.0 / (10000.0 ** jnp.linspace(0.0, 1.0, RET_HEAD_DIM // 2, dtype=F32))
    tabs = _rope_tables(pos0, t, inv_a) + _rope_tables(pos0, t, inv_r)
    if t < tm:
        tabs = tuple(jnp.tile(a, (tm // t, 1)) for a in tabs)
    x2 = x.reshape(m, d)
    ks, vs, rs, cs = [], [], [], []
    gs = None
    win_t = []
    for l in range(depth):
        if t >= tm:
            mod3 = modp[l].reshape(nb, 1, 6 * d)
        else:
            mod3 = jnp.repeat(modp[l], t, axis=0).reshape(m // tm, tm, 6 * d)
        (qa, ka, va, qr, kr, vr, gr, cv, z, ba), fused = _inproj(
            x2, t, mod3, norm_mix[l], w_in_b, w_ba_b, l, tabs, tm, conv_hist[l], conv_w[l], win_t)
        nhist = CONV_WIDTH - 1
        keep = min(t, DSA_MAX_WINDOW)
        if fused is None:
            cvn = jnp.concatenate([conv_hist[l], cv.reshape(nb, t, CONV_DIM)], axis=1)[:, -nhist:]
            window = lambda a: a.reshape(nb, t, DSA_W)[:, t - keep:].reshape(nb, keep, DSA_HEADS, DSA_HEAD_DIM)
            kwin, vwin = window(ka), window(va)
        else:
            conv_tail, *win_t = fused
            cvn = conv_tail[:, SUBLANES - nhist:]
        if k_hist is None:
            oa = _dsa_prompt(qa, ka, va, nb, t)
        else:
            oa = _dsa_step(qa, ka, va, k_hist, v_hist, l, nb, t)
        orr, sr = _retention(qr, kr, vr, gr, _to_block_diag(s_ret[l]), ret_norm[l], nb, t)
        oc, gs = _gated_delta(cv, fused is not None, z, ba, conv_hist[l], s_gdn, l, conv_w[l], a_log[l],
                              dt_bias[l], gdn_norm[l], nb, t, gs)
        x2 = _out_ffn(x2, t, oa, orr, oc, mod3, norm_ffn[l], w_out_b, wg_b, wu_b, wd_b, l,
                      final_norm, l == depth - 1, tm)
        if fused is None:
            ks.append(kwin)
            vs.append(vwin)
        rs.append(_from_block_diag(sr))
        cs.append(cvn)
    if win_t:
        untransposed = lambda a: a.reshape(depth, nb, DSA_HEADS, DSA_HEAD_DIM, keep).transpose(0, 1, 4, 2, 3)
        k_out, v_out = untransposed(win_t[0]), untransposed(win_t[1])
    else:
        k_out, v_out = jnp.stack(ks), jnp.stack(vs)
    return (x2.reshape(nb, t, d), k_out, v_out, jnp.stack(rs), gs, jnp.stack(cs))


def kernel(x_prompt, x_sample, cache_win_k, cache_win_v, state_ret, state_gdn, state_conv, c_prompt, c_sample, ada_w, ada_b, norm_mix, norm_ffn, w_in, ret_norm, conv_w, a_log, dt_bias, gdn_norm, w_out, w_gate, w_up, w_down, final_norm):
    nb, t_p, d = x_prompt.shape
    db, t_s, _ = x_sample.shape
    depth = ada_w.shape[0]
    rows = nb + db
    rows_pad = -(-rows // SUBLANES) * SUBLANES
    c_all = jnp.concatenate([c_prompt, c_sample, jnp.zeros((rows_pad - rows, d), F32)], axis=0)
    mod = _modulation(c_all, ada_w, ada_b)
    w_ba_b = jnp.pad(w_in[:, :, _C_BA:], ((0, 0), (0, 0), (0, LANES - (IN_COLS - _C_BA)))).astype(BF16)
    wts = (norm_mix, norm_ffn, w_in.astype(BF16), w_ba_b, ret_norm, conv_w, a_log, dt_bias, gdn_norm, w_out.astype(BF16),
           w_gate.astype(BF16), w_up.astype(BF16), w_down.astype(BF16), final_norm)

    zr = jnp.zeros((depth, nb, RET_HEADS, RET_HEAD_DIM, RET_HEAD_DIM), F32)
    zg = jnp.zeros((depth, nb, GDN_HEADS, GDN_HEAD_DIM, GDN_HEAD_DIM), F32)
    zc = jnp.zeros((depth, nb, CONV_WIDTH - 1, CONV_DIM), F32)
    y_p, kp, vp, rp, gp, cp = _trunk(x_prompt, mod[:, :nb], 0,
                                     None, None, zr, zg, zc, wts)
    wb = cache_win_k.shape[2]
    y_s, ks, vs, rs, gs, cs = _trunk(x_sample, mod[:, nb:rows], PAST_LEN,
                                     cache_win_k.reshape(depth, db, wb, DSA_W).transpose(0, 1, 3, 2),
                                     cache_win_v.reshape(depth, db, wb, DSA_W).transpose(0, 1, 3, 2),
                                     state_ret, state_gdn, state_conv, wts)
    return (y_p, y_s, kp, vp, rp, gp, cp, ks, vs, rs, gs, cs)
```

```python
import functools
import math

import jax
import jax.numpy as jnp
from jax import lax
from jax.experimental import pallas as pl
from jax.experimental.pallas import tpu as pltpu

F32 = jnp.float32
BF16 = jnp.bfloat16
HIGHEST = lax.Precision.HIGHEST

DSA_HEAD_DIM = 64
DSA_HEADS = 4
DSA_PATTERNS = ((128, 1), (512, 4), (2048, 16))
DSA_MAX_WINDOW = 2048
ROPE_THETA = 10000.0
RET_HEAD_DIM = 64
RET_HEADS = 4
GDN_HEAD_DIM = 128
GDN_HEADS = 4
CONV_WIDTH = 4
CHUNK = 64
NORM_EPS = 1e-6
PAST_LEN = 16384
LOG2_E = math.log2(math.e)

DSA_W = DSA_HEADS * DSA_HEAD_DIM
RET_W = RET_HEADS * RET_HEAD_DIM
GDN_W = GDN_HEADS * GDN_HEAD_DIM
CONV_DIM = 3 * GDN_W
LANES = 128
SUBLANES = 8
VMEM_LIMIT = 56 * 1024 * 1024
TOKEN_TILE = 512

_C_QA, _C_KA, _C_VA = 0, DSA_W, 2 * DSA_W
_C_QR = 3 * DSA_W
_C_KR, _C_VR, _C_GR = _C_QR + RET_W, _C_QR + 2 * RET_W, _C_QR + 3 * RET_W
_C_CV = _C_QR + 4 * RET_W
_C_Z = _C_CV + CONV_DIM
_C_BA = _C_Z + GDN_W
IN_COLS = _C_BA + 2 * GDN_HEADS


def _cparams(*sem):
    return pltpu.CompilerParams(dimension_semantics=sem, vmem_limit_bytes=VMEM_LIMIT)


def _dot(a, b, **kw):
    return jnp.dot(a, b, preferred_element_type=F32, **kw)


def _dot_nt(a, b, **kw):
    return lax.dot_general(a, b, (((1,), (1,)), ((), ())), preferred_element_type=F32, **kw)


def _dot_tn(a, b, **kw):
    return lax.dot_general(a, b, (((0,), (0,)), ((), ())), preferred_element_type=F32, **kw)


def _silu(x):
    return x * jax.nn.sigmoid(x)


def _mixer_out_dtype(t):
    return BF16 if t % (2 * SUBLANES) == 0 else F32


def _seqs_per_step(nb, t, base):
    return math.gcd(nb, base * (4 if t < CHUNK else 1))


def _mod_kernel(c_ref, w_ref, b_ref, o_ref):
    a = _silu(c_ref[...]).astype(BF16)
    o_ref[...] = _dot(a, w_ref[...].astype(BF16)) + b_ref[...]


def _modulation(c_all, ada_w, ada_b, tn=1536):
    depth, d, n = ada_w.shape
    bp = c_all.shape[0]
    return pl.pallas_call(
        _mod_kernel,
        grid=(depth, n // tn),
        in_specs=[
            pl.BlockSpec((bp, d), lambda l, j: (0, 0)),
            pl.BlockSpec((None, d, tn), lambda l, j: (l, 0, j)),
            pl.BlockSpec((None, 1, tn), lambda l, j: (l, 0, j)),
        ],
        out_specs=pl.BlockSpec((None, bp, tn), lambda l, j: (l, 0, j)),
        out_shape=jax.ShapeDtypeStruct((depth, bp, n), F32),
        compiler_params=_cparams("parallel", "parallel"),
        name="modulation",
    )(c_all, ada_w, ada_b.reshape(depth, 1, n))


def _rms_mod(x, nw, sc, sh):
    ms = jnp.mean(x * x, axis=-1, keepdims=True)
    return (x * lax.rsqrt(ms + NORM_EPS) * nw) * (1.0 + sc) + sh


def _causal_conv_silu(xp_scr, seq, cw_ref, cg, nrows):
    cols = slice(cg * LANES, (cg + 1) * LANES)
    acc = None
    for i in range(CONV_WIDTH):
        start = SUBLANES - (CONV_WIDTH - 1) + i
        term = xp_scr[seq, cg, pl.ds(start, nrows, stride=1), :] * cw_ref[i:i + 1, cols]
        acc = term if acc is None else acc + term
    return _silu(acc)


def _inproj_kernel(x_ref, nw_ref, sh_ref, sc_ref, w_ref, wba_ref, ca_ref, sa_ref, cr_ref, sr_ref, *rest,
                   tiles_per_seq):
    if tiles_per_seq:
        hist_ref, cw_ref = rest[:2]
        rest = rest[2:]
    qa_ref, ka_ref, va_ref, qr_ref, kr_ref, vr_ref, gr_ref, cv_ref, z_ref, ba_ref = rest[:10]
    tm = x_ref.shape[0]
    if tiles_per_seq:
        tail_ref, kt_ref, vt_ref, xp_scr = rest[10:]

        @pl.when(pl.program_id(0) % tiles_per_seq == 0)
        def _():
            tail_ref[...] = hist_ref[...]

    hb = _rms_mod(x_ref[...], nw_ref[...], sc_ref[...], sh_ref[...]).astype(BF16)

    def proj(c0, width):
        return _dot(hb, w_ref[:, c0:c0 + width])

    lane = lax.broadcasted_iota(jnp.int32, (tm, DSA_W), 1)
    first_half = (lane % DSA_HEAD_DIM) < (DSA_HEAD_DIM // 2)

    def rope(y, cos, sin_signed):
        partner = jnp.where(first_half, pltpu.roll(y, DSA_W - DSA_HEAD_DIM // 2, 1),
                            pltpu.roll(y, DSA_HEAD_DIM // 2, 1))
        return y * cos + partner * sin_signed

    wide = lambda ref: jnp.concatenate([ref[...]] * (DSA_W // LANES), axis=1)
    ca, sa, cr, sr = wide(ca_ref), wide(sa_ref), wide(cr_ref), wide(sr_ref)
    qa_ref[...] = rope(proj(_C_QA, DSA_W), ca, sa) * (DSA_HEAD_DIM ** -0.5 * LOG2_E)
    ka = rope(proj(_C_KA, DSA_W), ca, sa)
    va = proj(_C_VA, DSA_W)
    ka_ref[...] = ka
    va_ref[...] = va
    if tiles_per_seq:
        kt_ref[...] = ka.T
        vt_ref[...] = va.T

    qr_ref[...] = rope(proj(_C_QR, RET_W), cr, sr)
    kr_ref[...] = rope(proj(_C_KR, RET_W), cr, sr) * (RET_HEAD_DIM ** -0.5)
    vr_ref[...] = proj(_C_VR, RET_W)
    gr_ref[...] = proj(_C_GR, RET_W)
    for s in range(CONV_DIM // GDN_W):
        y = proj(_C_CV + s * GDN_W, GDN_W)
        if not tiles_per_seq:
            cv_ref[:, s * GDN_W:(s + 1) * GDN_W] = y
            continue
        for j in range(GDN_W // LANES):
            cg = s * (GDN_W // LANES) + j
            cols = slice(cg * LANES, (cg + 1) * LANES)
            yj = y[:, j * LANES:(j + 1) * LANES]
            xp_scr[0, cg, :SUBLANES, :] = tail_ref[:, cols]
            xp_scr[0, cg, SUBLANES:, :] = yj
            cv_ref[:, cols] = _causal_conv_silu(xp_scr, 0, cw_ref, cg, tm)
            tail_ref[:, cols] = yj[tm - SUBLANES:, :]
    z_ref[...] = proj(_C_Z, GDN_W)
    ba_ref[...] = _dot(hb, wba_ref[...])


def _mod_spec(tm, t, d, col):
    if t >= tm:
        return pl.BlockSpec((None, 1, d), lambda i: ((i * tm) // t, 0, col))
    return pl.BlockSpec((None, tm, d), lambda i: (i, 0, col))


def _conv_operands(conv_hist, conv_w):
    nb = conv_hist.shape[0]
    hist_pad = jnp.concatenate([jnp.zeros((nb, SUBLANES - (CONV_WIDTH - 1), CONV_DIM), F32), conv_hist], axis=1)
    cw_pad = jnp.concatenate([conv_w, jnp.zeros((SUBLANES - CONV_WIDTH, CONV_DIM), F32)], axis=0)
    return hist_pad, cw_pad


def _inproj(x2, t, mod3, norm_w, w_in_b, w_ba_b, layer, tabs, tm, conv_hist, conv_w):
    m, d = x2.shape
    nt = tabs[0].shape[0] // tm
    widths = (DSA_W,) * 3 + (RET_W,) * 4 + (CONV_DIM, GDN_W, LANES)
    tab_spec = pl.BlockSpec((tm, LANES), lambda i: (i % nt, 0))
    tiles_per_seq = t // tm if t % tm == 0 else 0
    window_tiles = min(t, DSA_MAX_WINDOW) // tm
    in_specs = [
        pl.BlockSpec((tm, d), lambda i: (i, 0)),
        pl.BlockSpec((1, d), lambda i: (0, 0)),
        _mod_spec(tm, t, d, 0),
        _mod_spec(tm, t, d, 1),
        pl.BlockSpec((None, d, IN_COLS), lambda i: (layer, 0, 0), pipeline_mode=pl.Buffered(1)),
        pl.BlockSpec((None, d, LANES), lambda i: (layer, 0, 0), pipeline_mode=pl.Buffered(1)),
        tab_spec, tab_spec, tab_spec, tab_spec,
    ]
    operands = [x2, norm_w.reshape(1, d), mod3, mod3, w_in_b, w_ba_b, *tabs]
    out_specs = [pl.BlockSpec((tm, w), lambda i: (i, 0)) for w in widths]
    out_shape = [jax.ShapeDtypeStruct((m, w), F32) for w in widths]
    scratch = []
    if tiles_per_seq:
        hist_pad, cw_pad = _conv_operands(conv_hist, conv_w)
        seq_tail = pl.BlockSpec((None, SUBLANES, CONV_DIM), lambda i: (i // tiles_per_seq, 0, 0))
        in_specs += [seq_tail, pl.BlockSpec(cw_pad.shape, lambda i: (0, 0))]
        operands += [hist_pad, cw_pad]
        out_specs.append(seq_tail)
        out_shape.append(jax.ShapeDtypeStruct((m // t, SUBLANES, CONV_DIM), F32))
        first_win = tiles_per_seq - window_tiles
        win_t = pl.BlockSpec((None, DSA_W, tm), lambda i: (i // tiles_per_seq, 0,
                                                            jnp.maximum(i % tiles_per_seq - first_win, 0)))
        out_specs += [win_t, win_t]
        out_shape += [jax.ShapeDtypeStruct((m // t, DSA_W, window_tiles * tm), F32)] * 2
        scratch.append(pltpu.VMEM((1, CONV_DIM // LANES, tm + SUBLANES, LANES), F32))
    outs = pl.pallas_call(
        functools.partial(_inproj_kernel, tiles_per_seq=tiles_per_seq),
        grid=(m // tm,),
        in_specs=in_specs,
        out_specs=out_specs,
        out_shape=out_shape,
        scratch_shapes=scratch,
        compiler_params=_cparams("arbitrary"),
        name="inproj",
    )(*operands)
    return (outs[:10], outs[10:]) if tiles_per_seq else (outs, None)


DSA_BLK = 128


DSA_QBLK = DSA_MAX_WINDOW
DSA_GROUP = 4


def _dsa_kernel(q_ref, kp_ref, kc_ref, vp_ref, vc_ref, o_ref, acc_scr, m_scr, l_scr):
    blk = pl.program_id(2)
    qblk = q_ref.shape[0]
    row = lax.broadcasted_iota(jnp.int32, (2 * DSA_BLK, 2 * DSA_BLK), 0) % DSA_BLK
    col = lax.broadcasted_iota(jnp.int32, (2 * DSA_BLK, 2 * DSA_BLK), 1)
    ok = jnp.logical_and(col >= row, col <= row + DSA_BLK)
    ok_first = jnp.logical_and(ok, jnp.logical_or(col >= DSA_BLK, blk > 0))
    lo = lax.broadcasted_iota(jnp.int32, (DSA_BLK, LANES), 1) < DSA_HEAD_DIM
    neg = -jnp.inf
    halves = lambda x: jnp.where(lo, x[:DSA_BLK], x[DSA_BLK:])

    def rows_of(dil, start, n):
        return pl.ds(start, n) if dil == 1 else pl.ds(start, n, stride=dil)

    def group_softmax(dil, subs):
        n = range(len(subs))
        idx = [rows_of(dil, ph + dil * DSA_BLK * j, DSA_BLK) for ph, j in subs]
        q = [q_ref[i, :] for i in idx]
        k, v = [], []
        for u, (ph, j) in enumerate(subs):
            if j == 0:
                band = rows_of(dil, qblk - dil * DSA_BLK + ph, DSA_BLK)
                k.append(jnp.concatenate([kp_ref[band, :], kc_ref[idx[u], :]], axis=0).astype(BF16))
                v.append(jnp.concatenate([vp_ref[band, :], vc_ref[idx[u], :]], axis=0).astype(BF16))
            else:
                both = rows_of(dil, ph + dil * DSA_BLK * (j - 1), 2 * DSA_BLK)
                k.append(kc_ref[both, :].astype(BF16))
                v.append(vc_ref[both, :].astype(BF16))
        q2 = [jnp.concatenate([jnp.where(lo, q[u], 0.0), jnp.where(lo, 0.0, q[u])], axis=0).astype(BF16)
              for u in n]
        s = [jnp.where(ok_first if subs[u][1] == 0 else ok, _dot_nt(q2[u], k[u]), neg) for u in n]
        mx = [jnp.max(s[u], axis=-1, keepdims=True) for u in n]
        p = [jnp.exp2(s[u] - mx[u]) for u in n]
        den = [jnp.sum(p[u], axis=-1, keepdims=True) for u in n]
        pv = [_dot(p[u].astype(BF16), v[u]) for u in n]
        return idx, [(halves(pv[u]), halves(jnp.broadcast_to(mx[u], (2 * DSA_BLK, LANES))),
                      halves(jnp.broadcast_to(den[u], (2 * DSA_BLK, LANES)))) for u in n]

    dils = sorted((d for _, d in DSA_PATTERNS), reverse=True)
    for pi, dil in enumerate(dils):
        subs = [(ph, j) for j in range(qblk // (dil * DSA_BLK)) for ph in range(dil)]
        for g0 in range(0, len(subs), DSA_GROUP):
            idx, tiles = group_softmax(dil, subs[g0:g0 + DSA_GROUP])
            for i, (pv, mx, den) in zip(idx, tiles):
                if pi > 0:
                    m_old = m_scr[i, :]
                    m_new = jnp.maximum(m_old, mx)
                    w_old = jnp.exp2(m_old - m_new)
                    w_cur = jnp.exp2(mx - m_new)
                    pv = acc_scr[i, :] * w_old + pv * w_cur
                    den = l_scr[i, :] * w_old + den * w_cur
                    mx = m_new
                if pi < len(dils) - 1:
                    acc_scr[i, :] = pv
                    m_scr[i, :] = mx
                    l_scr[i, :] = den
                else:
                    o_ref[i, :] = (pv / den).astype(o_ref.dtype)


def _dsa_prompt(q, k, v, nb, t):
    assert all(w // d == DSA_BLK for w, d in DSA_PATTERNS) and t % DSA_QBLK == 0
    ngrp = DSA_W // LANES
    r3 = lambda a: a.reshape(nb, t, DSA_W)
    cur = pl.BlockSpec((None, DSA_QBLK, LANES), lambda b, g, i: (b, i, g))
    prv = pl.BlockSpec((None, DSA_QBLK, LANES), lambda b, g, i: (b, jnp.maximum(i - 1, 0), g))
    return pl.pallas_call(
        _dsa_kernel,
        grid=(nb, ngrp, t // DSA_QBLK),
        in_specs=[cur, prv, cur, prv, cur],
        out_specs=cur,
        out_shape=jax.ShapeDtypeStruct((nb, t, DSA_W), _mixer_out_dtype(t)),
        scratch_shapes=[pltpu.VMEM((DSA_QBLK, LANES), F32)] * 3,
        compiler_params=_cparams("parallel", "parallel", "parallel"),
        name="dsa_prompt",
    )(r3(q), r3(k), r3(k), r3(v), r3(v)).reshape(nb * t, DSA_W)


DSA_STEP_SEQS = 2


def _multiplicity(dist):
    total = jnp.zeros(dist.shape, F32)
    for window, dil in DSA_PATTERNS:
        hit = (dist >= 0) & (dist <= window) & ((dist & (dil - 1)) == 0)
        total = total + hit.astype(F32)
    return total


def _dsa_step_kernel(q_ref, kc_ref, vc_ref, kn_ref, vn_ref, o_ref):
    nseq, tq = q_ref.shape[0], q_ref.shape[1]
    wb = kc_ref.shape[2]
    qi = lax.broadcasted_iota(jnp.int32, (2 * tq, wb), 0) % tq
    w_c = _multiplicity(wb + qi - lax.broadcasted_iota(jnp.int32, (2 * tq, wb), 1))
    qn = lax.broadcasted_iota(jnp.int32, (2 * tq, LANES), 0) % tq
    nn = lax.broadcasted_iota(jnp.int32, (2 * tq, LANES), 1)
    w_n = jnp.where(nn < tq, _multiplicity(qn - nn), 0.0)
    lo = lax.broadcasted_iota(jnp.int32, (tq, LANES), 1) < DSA_HEAD_DIM
    pad = jnp.zeros((LANES - tq, LANES), F32)
    neg = -jnp.inf
    for b, g in [(b, g) for b in range(nseq) for g in range(DSA_W // LANES)]:
        cols = slice(g * LANES, (g + 1) * LANES)
        q = q_ref[b, :, cols]
        q2 = jnp.concatenate([jnp.where(lo, q, 0.0), jnp.where(lo, 0.0, q)], axis=0).astype(BF16)
        kt_c, vt_c = kc_ref[b, cols, :].astype(BF16), vc_ref[b, cols, :].astype(BF16)
        k_n = jnp.concatenate([kn_ref[b, :, cols], pad], axis=0).astype(BF16)
        v_n = jnp.concatenate([vn_ref[b, :, cols], pad], axis=0).astype(BF16)
        s_c = jnp.where(w_c > 0, _dot(q2, kt_c), neg)
        s_n = jnp.where(w_n > 0, _dot_nt(q2, k_n), neg)
        mx = jnp.maximum(jnp.max(s_c, axis=-1, keepdims=True), jnp.max(s_n, axis=-1, keepdims=True))
        p_c = w_c * jnp.exp2(s_c - mx)
        p_n = w_n * jnp.exp2(s_n - mx)
        den = jnp.sum(p_c, axis=-1, keepdims=True) + jnp.sum(p_n, axis=-1, keepdims=True)
        o2 = (_dot_nt(p_c.astype(BF16), vt_c) + _dot(p_n.astype(BF16), v_n)) / den
        o_ref[b, :, cols] = jnp.where(lo, o2[:tq], o2[tq:])


def _dsa_step(q, k_new, v_new, k_cache, v_cache, layer, nb, t):
    wb = k_cache.shape[3]
    bblk = math.gcd(nb, DSA_STEP_SEQS)
    new = pl.BlockSpec((bblk, t, DSA_W), lambda b: (b, 0, 0))
    cache = pl.BlockSpec((None, bblk, DSA_W, wb), lambda b: (layer, b, 0, 0))
    r3 = lambda a: a.reshape(nb, t, DSA_W)
    return pl.pallas_call(
        _dsa_step_kernel,
        grid=(nb // bblk,),
        in_specs=[new, cache, cache, new, new],
        out_specs=new,
        out_shape=jax.ShapeDtypeStruct((nb, t, DSA_W), F32),
        compiler_params=_cparams("parallel"),
        name="dsa_step",
    )(r3(q), k_cache, v_cache, r3(k_new), r3(v_new)).reshape(nb * t, DSA_W)


RET_CHUNK = 256
RET_SEQS_PER_STEP = 2
RET_UNROLL = 4


def _ret_kernel(q_ref, k_ref, v_ref, g_ref, s0_ref, dec_ref, qd_ref, kd_ref, cd_ref, bd_ref, nw_ref,
                o_ref, so_ref, s_scr, *, chunk):
    tb = pl.program_id(1)
    ngrp = RET_W // LANES

    @pl.when(tb == 0)
    def _():
        s_scr[...] = s0_ref[...]

    lo = lax.broadcasted_iota(jnp.int32, (chunk, LANES), 1) < RET_HEAD_DIM
    nw = nw_ref[...]
    bd = bd_ref[...]

    def body(c, carry):
        rows = pl.ds(pl.multiple_of(c * chunk, chunk), chunk)
        for b in range(q_ref.shape[0]):
            for g in range(ngrp):
                cols = slice(g * LANES, (g + 1) * LANES)
                q, k, v = q_ref[b, rows, cols], k_ref[b, rows, cols], v_ref[b, rows, cols]
                kb, vb = k.astype(BF16), v.astype(BF16)
                parts = []
                for hh in range(2):
                    qm = jnp.where(lo if hh == 0 else jnp.logical_not(lo), q, 0.0).astype(BF16)
                    inner = _dot_nt(qm, kb) * dec_ref[2 * g + hh]
                    parts.append(_dot(inner.astype(BF16), vb))
                s_prev = s_scr[b, g]
                o = jnp.where(lo, parts[0], parts[1]) + _dot(q.astype(BF16), s_prev.astype(BF16)) * qd_ref[g]
                s_scr[b, g] = s_prev * cd_ref[g] + bd * _dot_tn((k * kd_ref[g]).astype(BF16), vb)
                o2 = o * o
                ms = jnp.where(lo, jnp.sum(jnp.where(lo, o2, 0.0), axis=-1, keepdims=True),
                               jnp.sum(jnp.where(lo, 0.0, o2), axis=-1, keepdims=True)) * (1.0 / RET_HEAD_DIM)
                o_ref[b, rows, cols] = (o * lax.rsqrt(ms + NORM_EPS) * nw * _silu(g_ref[b, rows, cols])).astype(
                    o_ref.dtype)
        return carry

    nchunk = q_ref.shape[1] // chunk
    lax.fori_loop(0, nchunk, body, 0, unroll=min(RET_UNROLL, nchunk))

    @pl.when(tb == pl.num_programs(1) - 1)
    def _():
        so_ref[...] = s_scr[...]


def _ret_tables(chunk):
    log_gamma = jnp.log(1.0 - 2.0 ** (-5.0 - jnp.arange(RET_HEADS, dtype=F32)))
    i = jnp.arange(chunk, dtype=F32)
    diff = i[:, None] - i[None, :]
    causal = diff >= 0
    decay = jnp.where(causal[None], jnp.exp(log_gamma[:, None, None] * jnp.where(causal, diff, 0.0)[None]), 0.0)
    per_lane = lambda a: jnp.repeat(a, RET_HEAD_DIM, axis=0).reshape(RET_W // LANES, LANES, -1)
    q_dec = per_lane(jnp.exp(log_gamma[:, None] * (i[None, :] + 1.0))).transpose(0, 2, 1)
    k_dec = per_lane(jnp.exp(log_gamma[:, None] * (chunk - 1.0 - i)[None, :])).transpose(0, 2, 1)
    c_dec = jnp.broadcast_to(per_lane(jnp.exp(log_gamma * chunk)[:, None]), (RET_W // LANES, LANES, LANES))
    head_of = jnp.arange(LANES) // RET_HEAD_DIM
    block_diag = (head_of[:, None] == head_of[None, :]).astype(F32)
    return decay, q_dec, k_dec, c_dec, block_diag


def _retention(q, k, v, gate, state_bd, ret_norm, nb, t):
    chunk = RET_CHUNK if t % RET_CHUNK == 0 else t
    tblk = min(t, 1024)
    ngrp = RET_W // LANES
    bblk = _seqs_per_step(nb, t, RET_SEQS_PER_STEP)
    decay, q_dec, k_dec, c_dec, block_diag = _ret_tables(chunk)
    r3 = lambda a: a.reshape(nb, t, RET_W)
    tok = pl.BlockSpec((bblk, tblk, RET_W), lambda b, i: (b, i, 0))
    st = pl.BlockSpec((bblk, ngrp, LANES, LANES), lambda b, i: (b, 0, 0, 0))
    full = lambda a: pl.BlockSpec(a.shape, lambda b, i: (0,) * a.ndim)
    nw = jnp.tile(ret_norm, LANES // RET_HEAD_DIM).reshape(1, LANES)
    o, s_new = pl.pallas_call(
        functools.partial(_ret_kernel, chunk=chunk),
        grid=(nb // bblk, t // tblk),
        in_specs=[tok, tok, tok, tok, st, full(decay), full(q_dec), full(k_dec), full(c_dec),
                  full(block_diag), full(nw)],
        out_specs=[tok, st],
        out_shape=[jax.ShapeDtypeStruct((nb, t, RET_W), _mixer_out_dtype(t)),
                   jax.ShapeDtypeStruct((nb, ngrp, LANES, LANES), F32)],
        scratch_shapes=[pltpu.VMEM((bblk, ngrp, LANES, LANES), F32)],
        compiler_params=_cparams("parallel", "arbitrary"),
        name="retention",
    )(r3(q), r3(k), r3(v), r3(gate), state_bd, decay, q_dec, k_dec, c_dec, block_diag, nw)
    return o.reshape(nb * t, RET_W), s_new


def _to_block_diag(s):
    nb = s.shape[0]
    s = s.reshape(nb, 2, 2, RET_HEAD_DIM, RET_HEAD_DIM)
    z = jnp.zeros_like(s[:, :, 0])
    top = jnp.concatenate([s[:, :, 0], z], axis=-1)
    bot = jnp.concatenate([z, s[:, :, 1]], axis=-1)
    return jnp.concatenate([top, bot], axis=-2)


def _from_block_diag(s):
    h = RET_HEAD_DIM
    return jnp.stack([s[:, :, :h, :h], s[:, :, h:, h:]], axis=2).reshape(s.shape[0], RET_HEADS, h, h)


GDN_PREP_CHUNKS_PER_ITER = 8
GDN_SCAN_SEQS_PER_STEP = 2
GDN_SCAN_UNROLL = 4


def _softplus(x):
    return jnp.maximum(x, 0.0) + jnp.log1p(jnp.exp(-jnp.abs(x)))


def _gdn_prep_kernel(cv_ref, hist_ref, ba_ref, cw_ref, ab_ref,
                     qg_ref, kd_ref, u_ref, w_ref, at_ref, el_ref, beta_scr, g_scr, *conv_scr, chunk, cpi):
    nseq, tblk = cv_ref.shape[0], cv_ref.shape[1]
    ncg = CONV_DIM // LANES
    ab = ab_ref[...]
    cs_scr = cv_ref
    for b in range(nseq):
        if conv_scr:
            xp_scr, cs_scr = conv_scr
            for cg in range(ncg):
                cols = slice(cg * LANES, (cg + 1) * LANES)
                xp_scr[b, cg, :SUBLANES, :] = hist_ref[b, :, cols]
                xp_scr[b, cg, SUBLANES:, :] = cv_ref[b, :, cols]
                cs_scr[b, :, cols] = _causal_conv_silu(xp_scr, b, cw_ref, cg, tblk)

        ba = ba_ref[b]
        beta_scr[b] = jax.nn.sigmoid(ba)
        g_scr[b] = -jnp.exp(ab[0:1, :]) * _softplus(ba + ab[1:2, :])

    grp = 2 if 2 * chunk == LANES else 1
    width = grp * chunk
    ri = lax.broadcasted_iota(jnp.int32, (chunk, width), 0)
    lane = lax.broadcasted_iota(jnp.int32, (chunk, width), 1)
    ci = lane % chunk
    first = lane < chunk
    incl = ri >= ci
    strict = ri > ci
    tri = (lax.broadcasted_iota(jnp.int32, (chunk, chunk), 0)
           >= lax.broadcasted_iota(jnp.int32, (chunk, chunk), 1)).astype(F32)
    lane_pick = (lax.broadcasted_iota(jnp.int32, (SUBLANES, LANES), 0)
                 == lax.broadcasted_iota(jnp.int32, (SUBLANES, LANES), 1)).astype(F32)
    base = min(SUBLANES, chunk)
    assert chunk % base == 0 and (chunk // base) & (chunk // base - 1) == 0
    same_blk = [(ri >> sh) == (ci >> sh) for sh in range(int(math.log2(base)), int(math.log2(chunk)) + 1)]

    def l2n(x):
        return x * lax.rsqrt(jnp.sum(x * x, axis=-1, keepdims=True) + NORM_EPS)

    def side_by_side(per_head):
        if grp == 1:
            return per_head[0][:, :width]
        return jnp.where(first, per_head[0][:, :width], per_head[1][:, :width])

    def block_diag(y):
        if grp == 1:
            return y.astype(BF16)
        return jnp.concatenate([jnp.where(first, y, 0.0), jnp.where(first, 0.0, y)], axis=0).astype(BF16)

    def block_rows(per_head):
        if grp == 1:
            return per_head[0].astype(BF16)
        a, b = per_head
        return jnp.concatenate([jnp.concatenate([a, jnp.zeros_like(b)], axis=1),
                                jnp.concatenate([jnp.zeros_like(a), b], axis=1)], axis=0).astype(BF16)

    def step(it, carry):
        probs = []
        for b, cc in [(b, cc) for b in range(nseq) for cc in range(cpi)]:
            c = it * cpi + cc
            rows = pl.ds(pl.multiple_of(c * chunk, chunk), chunk)
            beta_c = beta_scr[b, rows, :]
            gcol = _dot(tri, g_scr[b, rows, :], precision=HIGHEST)
            grow = _dot_nt(lane_pick, jnp.concatenate([gcol] * grp, axis=0), precision=HIGHEST)
            for h0 in range(0, GDN_HEADS, grp):
                hs = range(h0, h0 + grp)
                q = [l2n(cs_scr[b, rows, h * LANES:(h + 1) * LANES]) for h in hs]
                k = [l2n(cs_scr[b, rows, GDN_W + h * LANES:GDN_W + (h + 1) * LANES]) for h in hs]
                v = [cs_scr[b, rows, 2 * GDN_W + h * LANES:2 * GDN_W + (h + 1) * LANES] for h in hs]
                beta = [jnp.broadcast_to(beta_c[:, h:h + 1], (chunk, LANES)) for h in hs]
                gc = [jnp.broadcast_to(gcol[:, GDN_HEADS + h:GDN_HEADS + h + 1], (chunk, LANES)) for h in hs]
                g_row = side_by_side([jnp.broadcast_to(grow[GDN_HEADS + h:GDN_HEADS + h + 1, :], (chunk, width))
                                      for h in hs])
                gdiff = side_by_side(gc) - g_row
                dmask = jnp.where(incl, jnp.exp(jnp.where(incl, gdiff, 0.0)), 0.0)
                kb = [k[i] * beta[i] for i in range(grp)]
                k_rows = block_rows(k)
                lower = jnp.where(strict, _dot_nt(jnp.concatenate(kb, axis=1).astype(BF16), k_rows) * dmask, 0.0)
                npow = jnp.where(same_blk[0], -lower, 0.0)
                probs.append(dict(b=b, c=c, rows=rows, h0=h0, q=q, k=k, v=v, beta=beta, gc=gc, dmask=dmask,
                                  kb=kb, k_rows=k_rows, lower=lower, npow=npow, qmat=npow))
        for _ in range(int(math.log2(base)) - 1):
            for p in probs:
                p["npow"] = _dot(p["npow"].astype(BF16), block_diag(p["npow"]))
            for p in probs:
                p["qmat"] = p["qmat"] + p["npow"] + _dot(p["qmat"].astype(BF16), block_diag(p["npow"]))
        for lvl in range(1, len(same_blk)):
            for p in probs:
                cb = jnp.where(jnp.logical_and(same_blk[lvl], jnp.logical_not(same_blk[lvl - 1])),
                               p["lower"], 0.0)
                p["x"] = cb + _dot(p["qmat"].astype(BF16), block_diag(cb))
            for p in probs:
                p["qmat"] = p["qmat"] - p["x"] - _dot(p["x"].astype(BF16), block_diag(p["qmat"]))
        for p in probs:
            b, rows, h0, gc = p["b"], p["rows"], p["h0"], p["gc"]
            cols = slice(h0 * LANES, (h0 + grp) * LANES)
            idx = range(grp)
            eg = [jnp.exp(gc[i]) for i in idx]
            rhs_u = [p["v"][i] * p["beta"][i] for i in idx]
            rhs_w = [p["kb"][i] * eg[i] for i in idx]
            qb = p["qmat"].astype(BF16)
            qs = [p["q"][i] * (GDN_HEAD_DIM ** -0.5) for i in idx]
            g_last = [gc[i][chunk - 1:chunk, :] for i in idx]
            cat = lambda xs: jnp.concatenate(xs, axis=1)
            qg_ref[b, rows, cols] = cat([qs[i] * eg[i] for i in idx]).astype(qg_ref.dtype)
            kd_ref[b, rows, cols] = cat([p["k"][i] * jnp.exp(g_last[i] - gc[i]) for i in idx]).astype(kd_ref.dtype)
            u_ref[b, rows, cols] = cat(rhs_u) + _dot(qb, block_rows(rhs_u))
            w_ref[b, rows, cols] = (cat(rhs_w) + _dot(qb, block_rows(rhs_w))).astype(w_ref.dtype)
            at_ref[b, rows, h0 * chunk:(h0 + grp) * chunk] = (
                _dot_nt(cat(qs).astype(BF16), p["k_rows"]) * p["dmask"]).astype(at_ref.dtype)
            el_ref[b, pl.ds(pl.multiple_of(p["c"] * SUBLANES, SUBLANES), SUBLANES), cols] = cat(
                [jnp.broadcast_to(jnp.exp(g_last[i]), (SUBLANES, LANES)) for i in idx])
        return carry

    lax.fori_loop(0, tblk // (chunk * cpi), step, 0)


def _gdn_scan_kernel(qg_ref, kd_ref, u_ref, w_ref, at_ref, el_ref, z_ref, s0_ref, nw_ref,
                     o_ref, so_ref, s_scr, *, chunk):
    tb = pl.program_id(1)

    @pl.when(tb == 0)
    def _():
        s_scr[...] = s0_ref[...]

    nw = nw_ref[...]
    probs = [(b, h) for b in range(qg_ref.shape[0]) for h in range(GDN_HEADS)]

    def step(c, carry):
        rows = pl.ds(pl.multiple_of(c * chunk, chunk), chunk)
        erow = pl.ds(pl.multiple_of(c * SUBLANES, SUBLANES), SUBLANES)
        cols = lambda h: slice(h * LANES, (h + 1) * LANES)
        s_prev = [s_scr[b, h] for b, h in probs]
        sb = [s.astype(BF16) for s in s_prev]
        v_new = [u_ref[b, rows, cols(h)] - _dot(w_ref[b, rows, cols(h)].astype(BF16), sb[i])
                 for i, (b, h) in enumerate(probs)]
        vb = [v.astype(BF16) for v in v_new]
        for i, (b, h) in enumerate(probs):
            el = el_ref[b, erow, cols(h)][0:1, :]
            s_scr[b, h] = s_prev[i] * el + _dot_tn(kd_ref[b, rows, cols(h)].astype(BF16), vb[i])
        for i, (b, h) in enumerate(probs):
            attn = at_ref[b, rows, h * chunk:(h + 1) * chunk].astype(BF16)
            o = _dot(qg_ref[b, rows, cols(h)].astype(BF16), sb[i]) + _dot(attn, vb[i])
            ms = jnp.mean(o * o, axis=-1, keepdims=True)
            o_ref[b, rows, cols(h)] = (o * lax.rsqrt(ms + NORM_EPS) * nw * _silu(z_ref[b, rows, cols(h)])).astype(
                o_ref.dtype)
        return carry

    nchunk = qg_ref.shape[1] // chunk
    lax.fori_loop(0, nchunk, step, 0, unroll=min(GDN_SCAN_UNROLL, nchunk))

    @pl.when(tb == pl.num_programs(1) - 1)
    def _():
        so_ref[...] = s_scr[...]


def _gated_delta(cv, conv_done, z, ba, conv_hist, states, layer, conv_w, a_log, dt_bias, gdn_norm, nb, t):
    chunk = CHUNK if t % CHUNK == 0 else t
    tblk = min(t, 512)
    assert conv_done or tblk == t
    nchunk_blk = tblk // chunk
    hist_pad, cw_pad = _conv_operands(conv_hist, conv_w)
    ab = jnp.zeros((SUBLANES, LANES), F32)
    ab = ab.at[0, GDN_HEADS:2 * GDN_HEADS].set(a_log).at[1, GDN_HEADS:2 * GDN_HEADS].set(dt_bias)
    cv3 = cv.reshape(nb, t, CONV_DIM)
    pblk = _seqs_per_step(nb, t, 2)
    tok = lambda w: pl.BlockSpec((pblk, tblk, w), lambda b, i: (b, i, 0))
    full = lambda a: pl.BlockSpec(a.shape, lambda b, i: (0,) * a.ndim)
    el_spec =pl.BlockSpec((pblk, nchunk_blk * SUBLANES, GDN_W), lambda b, i: (b, i, 0))
    tok_shape = lambda w, dt=F32: jax.ShapeDtypeStruct((nb, t, w), dt)
    el_shape = jax.ShapeDtypeStruct((nb, (t // chunk) * SUBLANES, GDN_W), F32)
    opd = BF16 if chunk % (2 * SUBLANES) == 0 else F32
    qg, kd, u, w, attn, el = pl.pallas_call(
        functools.partial(_gdn_prep_kernel, chunk=chunk, cpi=min(GDN_PREP_CHUNKS_PER_ITER, nchunk_blk)),
        grid=(nb // pblk, t // tblk),
        in_specs=[tok(CONV_DIM), pl.BlockSpec((pblk, SUBLANES, CONV_DIM), lambda b, i: (b, 0, 0)),
                  tok(LANES), full(cw_pad), full(ab)],
        out_specs=[tok(GDN_W), tok(GDN_W), tok(GDN_W), tok(GDN_W), tok(GDN_HEADS * chunk), el_spec],
        out_shape=[tok_shape(GDN_W, opd), tok_shape(GDN_W, opd), tok_shape(GDN_W), tok_shape(GDN_W, opd),
                   tok_shape(GDN_HEADS * chunk, opd), el_shape],
        scratch_shapes=[pltpu.VMEM((pblk, tblk, LANES), F32), pltpu.VMEM((pblk, tblk, LANES), F32)] + (
            [] if conv_done else [pltpu.VMEM((pblk, CONV_DIM // LANES, tblk + SUBLANES, LANES), F32),
                                  pltpu.VMEM((pblk, tblk, CONV_DIM), F32)]),
        compiler_params=_cparams("parallel", "parallel"),
        name="gdn_prep",
    )(cv3, hist_pad, ba.reshape(nb, t, LANES), cw_pad, ab)

    bblk = _seqs_per_step(nb, t, GDN_SCAN_SEQS_PER_STEP)
    stok = lambda w: pl.BlockSpec((bblk, tblk, w), lambda b, i: (b, i, 0))
    sel_spec = pl.BlockSpec((bblk, nchunk_blk * SUBLANES, GDN_W), lambda b, i: (b, i, 0))
    st_in = pl.BlockSpec((None, bblk, GDN_HEADS, LANES, LANES), lambda b, i: (layer, b, 0, 0, 0))
    st = pl.BlockSpec((bblk, GDN_HEADS, LANES, LANES), lambda b, i: (b, 0, 0, 0))
    nw = gdn_norm.reshape(1, LANES)
    o, s_new = pl.pallas_call(
        functools.partial(_gdn_scan_kernel, chunk=chunk),
        grid=(nb // bblk, t // tblk),
        in_specs=[stok(GDN_W), stok(GDN_W), stok(GDN_W), stok(GDN_W), stok(GDN_HEADS * chunk), sel_spec,
                  stok(GDN_W), st_in, full(nw)],
        out_specs=[stok(GDN_W), st],
        out_shape=[tok_shape(GDN_W, _mixer_out_dtype(t)), jax.ShapeDtypeStruct((nb, GDN_HEADS, LANES, LANES), F32)],
        scratch_shapes=[pltpu.VMEM((bblk, GDN_HEADS, LANES, LANES), F32)],
        compiler_params=_cparams("parallel", "arbitrary"),
        name="gdn_scan",
    )(qg, kd, u, w, attn, el, z.reshape(nb, t, GDN_W), states, nw)
    return o.reshape(nb * t, GDN_W), s_new


FFN_TILE = 256


def _out_ffn_kernel(x_ref, oa_ref, orr_ref, oc_ref, g1_ref, sh2_ref, sc2_ref, g2_ref, nw_ref, wo_ref, wg_ref,
                    wu_ref, wd_ref, fn_ref, out_ref, *, final):
    mix = (_dot(oa_ref[...].astype(BF16), wo_ref[0:DSA_W, :])
           + _dot(orr_ref[...].astype(BF16), wo_ref[DSA_W:DSA_W + RET_W, :])
           + _dot(oc_ref[...].astype(BF16), wo_ref[DSA_W + RET_W:, :]))
    x1 = x_ref[...] + g1_ref[...] * mix
    hb = _rms_mod(x1, nw_ref[...], sc2_ref[...], sh2_ref[...]).astype(BF16)
    acc = None
    for j in range(wg_ref.shape[1] // FFN_TILE):
        cols = slice(j * FFN_TILE, (j + 1) * FFN_TILE)
        act = (_silu(_dot(hb, wg_ref[:, cols])) * _dot(hb, wu_ref[:, cols])).astype(BF16)
        down = _dot(act, wd_ref[cols, :])
        acc = down if acc is None else acc + down
    x2 = x1 + g2_ref[...] * acc
    if final:
        ms = jnp.mean(x2 * x2, axis=-1, keepdims=True)
        x2 = x2 * lax.rsqrt(ms + NORM_EPS) * fn_ref[...]
    out_ref[...] = x2


def _out_ffn_stream_kernel(x_ref, oa_ref, orr_ref, oc_ref, g1_ref, sh2_ref, sc2_ref, g2_ref, nw_ref, wo_ref,
                           wg_ref, wu_ref, wd_ref, fn_ref, out_ref, x1_scr, hb_scr, acc_scr, *, final):
    j = pl.program_id(0)

    @pl.when(j == 0)
    def _():
        mix = (_dot(oa_ref[...].astype(BF16), wo_ref[0:DSA_W, :])
               + _dot(orr_ref[...].astype(BF16), wo_ref[DSA_W:DSA_W + RET_W, :])
               + _dot(oc_ref[...].astype(BF16), wo_ref[DSA_W + RET_W:, :]))
        x1 = x_ref[...] + g1_ref[...] * mix
        x1_scr[...] = x1
        hb_scr[...] = _rms_mod(x1, nw_ref[...], sc2_ref[...], sh2_ref[...]).astype(BF16)
        acc_scr[...] = jnp.zeros_like(acc_scr)

    hb = hb_scr[...]
    act = (_silu(_dot(hb, wg_ref[...])) * _dot(hb, wu_ref[...])).astype(BF16)
    acc_scr[...] += _dot(act, wd_ref[...])

    @pl.when(j == pl.num_programs(0) - 1)
    def _():
        x2 = x1_scr[...] + g2_ref[...] * acc_scr[...]
        if final:
            ms = jnp.mean(x2 * x2, axis=-1, keepdims=True)
            x2 = x2 * lax.rsqrt(ms + NORM_EPS) * fn_ref[...]
        out_ref[...] = x2


def _out_ffn_stream(x2, oa, orr, oc, mod3, norm_w, w_out_b, wg_b, wu_b, wd_b, layer, final_norm, final):
    m, d = x2.shape
    hid = wg_b.shape[2]
    row = lambda w: pl.BlockSpec((m, w), lambda j: (0, 0))
    mod = lambda col: pl.BlockSpec((None, mod3.shape[1], d), lambda j: (0, 0, col))
    const = lambda a: pl.BlockSpec(a.shape, lambda j: (0,) * a.ndim)
    nw = norm_w.reshape(1, d)
    fn = final_norm.reshape(1, d)
    return pl.pallas_call(
        functools.partial(_out_ffn_stream_kernel, final=final),
        grid=(hid // FFN_TILE,),
        in_specs=[row(d), row(oa.shape[1]), row(orr.shape[1]), row(oc.shape[1]), mod(2), mod(3), mod(4), mod(5),
                  const(nw), pl.BlockSpec((None,) + w_out_b.shape[1:], lambda j: (layer, 0, 0)),
                  pl.BlockSpec((None, d, FFN_TILE), lambda j: (layer, 0, j)),
                  pl.BlockSpec((None, d, FFN_TILE), lambda j: (layer, 0, j)),
                  pl.BlockSpec((None, FFN_TILE, d), lambda j: (layer, j, 0)), const(fn)],
        out_specs=row(d),
        out_shape=jax.ShapeDtypeStruct((m, d), F32),
        scratch_shapes=[pltpu.VMEM((m, d), F32), pltpu.VMEM((m, d), BF16), pltpu.VMEM((m, d), F32)],
        compiler_params=_cparams("arbitrary"),
        name="out_ffn_stream",
    )(x2, oa, orr, oc, mod3, mod3, mod3, mod3, nw, w_out_b, wg_b, wu_b, wd_b, fn)


def _out_ffn(x2, t, oa, orr, oc, mod3, norm_w, w_out_b, wg_b, wu_b, wd_b, layer, final_norm, final, tm):
    m, d = x2.shape
    if m == tm:
        return _out_ffn_stream(x2, oa, orr, oc, mod3, norm_w, w_out_b, wg_b, wu_b, wd_b, layer, final_norm, final)
    row = lambda w: pl.BlockSpec((tm, w), lambda i: (i, 0))
    const = lambda a: pl.BlockSpec(a.shape, lambda i: (0,) * a.ndim, pipeline_mode=pl.Buffered(1))
    wspec = lambda a: pl.BlockSpec((None,) + a.shape[1:], lambda i: (layer,) + (0,) * (a.ndim - 1),
                                   pipeline_mode=pl.Buffered(1))
    nw = norm_w.reshape(1, d)
    fn = final_norm.reshape(1, d)
    return pl.pallas_call(
        functools.partial(_out_ffn_kernel, final=final),
        grid=(m // tm,),
        in_specs=[row(d), row(DSA_W), row(RET_W), row(GDN_W),
                  _mod_spec(tm, t, d, 2), _mod_spec(tm, t, d, 3), _mod_spec(tm, t, d, 4), _mod_spec(tm, t, d, 5),
                  const(nw), wspec(w_out_b), wspec(wg_b), wspec(wu_b), wspec(wd_b), const(fn)],
        out_specs=row(d),
        out_shape=jax.ShapeDtypeStruct((m, d), F32),
        compiler_params=_cparams("parallel"),
        name="out_ffn",
    )(x2, oa, orr, oc, mod3, mod3, mod3, mod3, nw, w_out_b, wg_b, wu_b, wd_b, fn)


ROPE_SPLIT = 64


def _rope_tables(pos0, t, inv_freq):
    reps = LANES // (2 * inv_freq.shape[0])
    inv_lane = jnp.tile(jnp.concatenate([inv_freq, inv_freq]), reps)[None, :]
    sign_lane = jnp.tile(jnp.concatenate([-jnp.ones_like(inv_freq), jnp.ones_like(inv_freq)]), reps)[None, :]
    if t % ROPE_SPLIT:
        ang = (pos0 + jnp.arange(t, dtype=jnp.int32)).astype(F32)[:, None] * inv_lane
        return jnp.cos(ang), jnp.sin(ang) * sign_lane
    coarse = (pos0 + ROPE_SPLIT * jnp.arange(t // ROPE_SPLIT, dtype=jnp.int32)).astype(F32)[:, None] * inv_lane
    fine = jnp.arange(ROPE_SPLIT, dtype=jnp.int32).astype(F32)[:, None] * inv_lane
    cc, sc = jnp.cos(coarse)[:, None, :], jnp.sin(coarse)[:, None, :]
    cf, sf = jnp.cos(fine)[None], jnp.sin(fine)[None]
    cos = (cc * cf - sc * sf).reshape(t, LANES)
    sin = (sc * cf + cc * sf).reshape(t, LANES)
    return cos, sin * sign_lane


def _trunk(x, modp, pos0, k_hist, v_hist, s_ret, s_gdn, conv_hist, wts):
    (norm_mix, norm_ffn, w_in_b, w_ba_b, ret_norm, conv_w, a_log, dt_bias, gdn_norm, w_out_b, wg_b, wu_b, wd_b,
     final_norm) = wts
    nb, t, d = x.shape
    m = nb * t
    depth = w_in_b.shape[0]
    tm = min(TOKEN_TILE, m)
    inv_a = 1.0 / (ROPE_THETA ** (jnp.arange(0, DSA_HEAD_DIM, 2, dtype=F32) / DSA_HEAD_DIM))
    inv_r = 1.0 / (10000.0 ** jnp.linspace(0.0, 1.0, RET_HEAD_DIM // 2, dtype=F32))
    tabs = _rope_tables(pos0, t, inv_a) + _rope_tables(pos0, t, inv_r)
    if t < tm:
        tabs = tuple(jnp.tile(a, (tm // t, 1)) for a in tabs)
    x2 = x.reshape(m, d)
    ks, vs, rs, gs, cs = [], [], [], [], []
    for l in range(depth):
        if t >= tm:
            mod3 = modp[l].reshape(nb, 1, 6 * d)
        else:
            mod3 = jnp.repeat(modp[l], t, axis=0).reshape(m // tm, tm, 6 * d)
        (qa, ka, va, qr, kr, vr, gr, cv, z, ba), fused = _inproj(
            x2, t, mod3, norm_mix[l], w_in_b, w_ba_b, l, tabs, tm, conv_hist[l], conv_w[l])
        nhist = CONV_WIDTH - 1
        keep = min(t, DSA_MAX_WINDOW)
        if fused is None:
            cvn = jnp.concatenate([conv_hist[l], cv.reshape(nb, t, CONV_DIM)], axis=1)[:, -nhist:]
            window = lambda a: a.reshape(nb, t, DSA_W)[:, t - keep:].reshape(nb, keep, DSA_HEADS, DSA_HEAD_DIM)
            kwin, vwin = window(ka), window(va)
        else:
            conv_tail, k_t, v_t = fused
            cvn = conv_tail[:, SUBLANES - nhist:]
            untransposed = lambda a: a.reshape(nb, DSA_HEADS, DSA_HEAD_DIM, keep).transpose(0, 3, 1, 2)
            kwin, vwin = untransposed(k_t), untransposed(v_t)
        if k_hist is None:
            oa = _dsa_prompt(qa, ka, va, nb, t)
        else:
            oa = _dsa_step(qa, ka, va, k_hist, v_hist, l, nb, t)
        orr, sr = _retention(qr, kr, vr, gr, _to_block_diag(s_ret[l]), ret_norm[l], nb, t)
        oc, sg = _gated_delta(cv, fused is not None, z, ba, conv_hist[l], s_gdn, l, conv_w[l], a_log[l],
                              dt_bias[l], gdn_norm[l], nb, t)
        x2 = _out_ffn(x2, t, oa, orr, oc, mod3, norm_ffn[l], w_out_b, wg_b, wu_b, wd_b, l,
                      final_norm, l == depth - 1, tm)
        ks.append(kwin)
        vs.append(vwin)
        rs.append(_from_block_diag(sr))
        gs.append(sg)
        cs.append(cvn)
    return (x2.reshape(nb, t, d), jnp.stack(ks), jnp.stack(vs), jnp.stack(rs), jnp.stack(gs), jnp.stack(cs))


def kernel(x_prompt, x_sample, cache_win_k, cache_win_v, state_ret, state_gdn, state_conv, c_prompt, c_sample, ada_w, ada_b, norm_mix, norm_ffn, w_in, ret_norm, conv_w, a_log, dt_bias, gdn_norm, w_out, w_gate, w_up, w_down, final_norm):
    nb, t_p, d = x_prompt.shape
    db, t_s, _ = x_sample.shape
    depth = ada_w.shape[0]
    rows = nb + db
    rows_pad = -(-rows // SUBLANES) * SUBLANES
    c_all = jnp.concatenate([c_prompt, c_sample, jnp.zeros((rows_pad - rows, d), F32)], axis=0)
    mod = _modulation(c_all, ada_w, ada_b)
    w_ba_b = jnp.pad(w_in[:, :, _C_BA:], ((0, 0), (0, 0), (0, LANES - (IN_COLS - _C_BA)))).astype(BF16)
    wts = (norm_mix, norm_ffn, w_in.astype(BF16), w_ba_b, ret_norm, conv_w, a_log, dt_bias, gdn_norm, w_out.astype(BF16),
           w_gate.astype(BF16), w_up.astype(BF16), w_down.astype(BF16), final_norm)

    zr = jnp.zeros((depth, nb, RET_HEADS, RET_HEAD_DIM, RET_HEAD_DIM), F32)
    zg = jnp.zeros((depth, nb, GDN_HEADS, GDN_HEAD_DIM, GDN_HEAD_DIM), F32)
    zc = jnp.zeros((depth, nb, CONV_WIDTH - 1, CONV_DIM), F32)
    y_p, kp, vp, rp, gp, cp = _trunk(x_prompt, mod[:, :nb], 0,
                                     None, None, zr, zg, zc, wts)
    wb = cache_win_k.shape[2]
    y_s, ks, vs, rs, gs, cs = _trunk(x_sample, mod[:, nb:rows], PAST_LEN,
                                     cache_win_k.reshape(depth, db, wb, DSA_W).transpose(0, 1, 3, 2),
                                     cache_win_v.reshape(depth, db, wb, DSA_W).transpose(0, 1, 3, 2),
                                     state_ret, state_gdn, state_conv, wts)
    return (y_p, y_s, kp, vp, rp, gp, cp, ks, vs, rs, gs, cs)
```

```python
import functools
import math

import jax
import jax.numpy as jnp
from jax import lax
from jax.experimental import pallas as pl
from jax.experimental.pallas import tpu as pltpu

F32 = jnp.float32
BF16 = jnp.bfloat16
HIGHEST = lax.Precision.HIGHEST

DSA_HEAD_DIM = 64
DSA_HEADS = 4
DSA_PATTERNS = ((128, 1), (512, 4), (2048, 16))
DSA_MAX_WINDOW = 2048
ROPE_THETA = 10000.0
RET_HEAD_DIM = 64
RET_HEADS = 4
GDN_HEAD_DIM = 128
GDN_HEADS = 4
CONV_WIDTH = 4
CHUNK = 64
NORM_EPS = 1e-6
PAST_LEN = 16384
LOG2_E = math.log2(math.e)

DSA_W = DSA_HEADS * DSA_HEAD_DIM
RET_W = RET_HEADS * RET_HEAD_DIM
GDN_W = GDN_HEADS * GDN_HEAD_DIM
CONV_DIM = 3 * GDN_W
LANES = 128
SUBLANES = 8
VMEM_LIMIT = 56 * 1024 * 1024
TOKEN_TILE = 512

_C_QA, _C_KA, _C_VA = 0, DSA_W, 2 * DSA_W
_C_QR = 3 * DSA_W
_C_KR, _C_VR, _C_GR = _C_QR + RET_W, _C_QR + 2 * RET_W, _C_QR + 3 * RET_W
_C_CV = _C_QR + 4 * RET_W
_C_Z = _C_CV + CONV_DIM
_C_BA = _C_Z + GDN_W
IN_COLS = _C_BA + 2 * GDN_HEADS


def _cparams(*sem):
    return pltpu.CompilerParams(dimension_semantics=sem, vmem_limit_bytes=VMEM_LIMIT)


def _dot(a, b, **kw):
    return jnp.dot(a, b, preferred_element_type=F32, **kw)


def _dot_nt(a, b, **kw):
    return lax.dot_general(a, b, (((1,), (1,)), ((), ())), preferred_element_type=F32, **kw)


def _dot_tn(a, b, **kw):
    return lax.dot_general(a, b, (((0,), (0,)), ((), ())), preferred_element_type=F32, **kw)


def _silu(x):
    return x * jax.nn.sigmoid(x)


def _mixer_out_dtype(t):
    return BF16 if t % (2 * SUBLANES) == 0 else F32


def _seqs_per_step(nb, t, base):
    return math.gcd(nb, base * (4 if t < CHUNK else 1))


def _mod_kernel(c_ref, w_ref, b_ref, o_ref):
    a = _silu(c_ref[...]).astype(BF16)
    o_ref[...] = _dot(a, w_ref[...].astype(BF16)) + b_ref[...]


def _modulation(c_all, ada_w, ada_b, tn=1536):
    depth, d, n = ada_w.shape
    bp = c_all.shape[0]
    return pl.pallas_call(
        _mod_kernel,
        grid=(depth, n // tn),
        in_specs=[
            pl.BlockSpec((bp, d), lambda l, j: (0, 0)),
            pl.BlockSpec((None, d, tn), lambda l, j: (l, 0, j)),
            pl.BlockSpec((None, 1, tn), lambda l, j: (l, 0, j)),
        ],
        out_specs=pl.BlockSpec((None, bp, tn), lambda l, j: (l, 0, j)),
        out_shape=jax.ShapeDtypeStruct((depth, bp, n), F32),
        compiler_params=_cparams("parallel", "parallel"),
        name="modulation",
    )(c_all, ada_w, ada_b.reshape(depth, 1, n))


def _rms_mod(x, nw, sc, sh):
    ms = jnp.mean(x * x, axis=-1, keepdims=True)
    return (x * lax.rsqrt(ms + NORM_EPS) * nw) * (1.0 + sc) + sh


def _causal_conv_silu(xp_scr, seq, cw_ref, cg, nrows):
    cols = slice(cg * LANES, (cg + 1) * LANES)
    acc = None
    for i in range(CONV_WIDTH):
        start = SUBLANES - (CONV_WIDTH - 1) + i
        term = xp_scr[seq, cg, pl.ds(start, nrows, stride=1), :] * cw_ref[i:i + 1, cols]
        acc = term if acc is None else acc + term
    return _silu(acc)


def _inproj_kernel(x_ref, nw_ref, sh_ref, sc_ref, w_ref, wba_ref, ca_ref, sa_ref, cr_ref, sr_ref, *rest,
                   tiles_per_seq):
    if tiles_per_seq:
        hist_ref, cw_ref = rest[:2]
        rest = rest[2:]
    qa_ref, ka_ref, va_ref, qr_ref, kr_ref, vr_ref, gr_ref, cv_ref, z_ref, ba_ref = rest[:10]
    tm = x_ref.shape[0]
    if tiles_per_seq:
        tail_ref, kt_ref, vt_ref, xp_scr = rest[10:]

        @pl.when(pl.program_id(0) % tiles_per_seq == 0)
        def _():
            tail_ref[...] = hist_ref[...]

    hb = _rms_mod(x_ref[...], nw_ref[...], sc_ref[...], sh_ref[...]).astype(BF16)

    def proj(c0, width):
        return _dot(hb, w_ref[:, c0:c0 + width])

    lane = lax.broadcasted_iota(jnp.int32, (tm, DSA_W), 1)
    first_half = (lane % DSA_HEAD_DIM) < (DSA_HEAD_DIM // 2)

    def rope(y, cos, sin_signed):
        partner = jnp.where(first_half, pltpu.roll(y, DSA_W - DSA_HEAD_DIM // 2, 1),
                            pltpu.roll(y, DSA_HEAD_DIM // 2, 1))
        return y * cos + partner * sin_signed

    wide = lambda ref: jnp.concatenate([ref[...]] * (DSA_W // LANES), axis=1)
    ca, sa, cr, sr = wide(ca_ref), wide(sa_ref), wide(cr_ref), wide(sr_ref)
    qa_ref[...] = rope(proj(_C_QA, DSA_W), ca, sa) * (DSA_HEAD_DIM ** -0.5 * LOG2_E)
    ka = rope(proj(_C_KA, DSA_W), ca, sa)
    va = proj(_C_VA, DSA_W)
    ka_ref[...] = ka
    va_ref[...] = va
    if tiles_per_seq:
        kt_ref[...] = ka.T
        vt_ref[...] = va.T

    qr_ref[...] = rope(proj(_C_QR, RET_W), cr, sr)
    kr_ref[...] = rope(proj(_C_KR, RET_W), cr, sr) * (RET_HEAD_DIM ** -0.5)
    vr_ref[...] = proj(_C_VR, RET_W)
    gr_ref[...] = proj(_C_GR, RET_W)
    for s in range(CONV_DIM // GDN_W):
        y = proj(_C_CV + s * GDN_W, GDN_W)
        if not tiles_per_seq:
            cv_ref[:, s * GDN_W:(s + 1) * GDN_W] = y
            continue
        for j in range(GDN_W // LANES):
            cg = s * (GDN_W // LANES) + j
            cols = slice(cg * LANES, (cg + 1) * LANES)
            yj = y[:, j * LANES:(j + 1) * LANES]
            xp_scr[0, cg, :SUBLANES, :] = tail_ref[:, cols]
            xp_scr[0, cg, SUBLANES:, :] = yj
            cv_ref[:, cols] = _causal_conv_silu(xp_scr, 0, cw_ref, cg, tm)
            tail_ref[:, cols] = yj[tm - SUBLANES:, :]
    z_ref[...] = proj(_C_Z, GDN_W)
    ba_ref[...] = _dot(hb, wba_ref[...])


def _mod_spec(tm, t, d, col):
    if t >= tm:
        return pl.BlockSpec((None, 1, d), lambda i: ((i * tm) // t, 0, col))
    return pl.BlockSpec((None, tm, d), lambda i: (i, 0, col))


def _conv_operands(conv_hist, conv_w):
    nb = conv_hist.shape[0]
    hist_pad = jnp.concatenate([jnp.zeros((nb, SUBLANES - (CONV_WIDTH - 1), CONV_DIM), F32), conv_hist], axis=1)
    cw_pad = jnp.concatenate([conv_w, jnp.zeros((SUBLANES - CONV_WIDTH, CONV_DIM), F32)], axis=0)
    return hist_pad, cw_pad


def _inproj(x2, t, mod3, norm_w, w_in_b, w_ba_b, layer, tabs, tm, conv_hist, conv_w):
    m, d = x2.shape
    nt = tabs[0].shape[0] // tm
    widths = (DSA_W,) * 3 + (RET_W,) * 4 + (CONV_DIM, GDN_W, LANES)
    tab_spec = pl.BlockSpec((tm, LANES), lambda i: (i % nt, 0))
    tiles_per_seq = t // tm if t % tm == 0 else 0
    window_tiles = min(t, DSA_MAX_WINDOW) // tm
    in_specs = [
        pl.BlockSpec((tm, d), lambda i: (i, 0)),
        pl.BlockSpec((1, d), lambda i: (0, 0)),
        _mod_spec(tm, t, d, 0),
        _mod_spec(tm, t, d, 1),
        pl.BlockSpec((None, d, IN_COLS), lambda i: (layer, 0, 0), pipeline_mode=pl.Buffered(1)),
        pl.BlockSpec((None, d, LANES), lambda i: (layer, 0, 0), pipeline_mode=pl.Buffered(1)),
        tab_spec, tab_spec, tab_spec, tab_spec,
    ]
    operands = [x2, norm_w.reshape(1, d), mod3, mod3, w_in_b, w_ba_b, *tabs]
    out_specs = [pl.BlockSpec((tm, w), lambda i: (i, 0)) for w in widths]
    out_shape = [jax.ShapeDtypeStruct((m, w), F32) for w in widths]
    scratch = []
    if tiles_per_seq:
        hist_pad, cw_pad = _conv_operands(conv_hist, conv_w)
        seq_tail = pl.BlockSpec((None, SUBLANES, CONV_DIM), lambda i: (i // tiles_per_seq, 0, 0))
        in_specs += [seq_tail, pl.BlockSpec(cw_pad.shape, lambda i: (0, 0))]
        operands += [hist_pad, cw_pad]
        out_specs.append(seq_tail)
        out_shape.append(jax.ShapeDtypeStruct((m // t, SUBLANES, CONV_DIM), F32))
        first_win = tiles_per_seq - window_tiles
        win_t = pl.BlockSpec((None, DSA_W, tm), lambda i: (i // tiles_per_seq, 0,
                                                            jnp.maximum(i % tiles_per_seq - first_win, 0)))
        out_specs += [win_t, win_t]
        out_shape += [jax.ShapeDtypeStruct((m // t, DSA_W, window_tiles * tm), F32)] * 2
        scratch.append(pltpu.VMEM((1, CONV_DIM // LANES, tm + SUBLANES, LANES), F32))
    outs = pl.pallas_call(
        functools.partial(_inproj_kernel, tiles_per_seq=tiles_per_seq),
        grid=(m // tm,),
        in_specs=in_specs,
        out_specs=out_specs,
        out_shape=out_shape,
        scratch_shapes=scratch,
        compiler_params=_cparams("arbitrary"),
        name="inproj",
    )(*operands)
    return (outs[:10], outs[10:]) if tiles_per_seq else (outs, None)


DSA_BLK = 128


DSA_QBLK = DSA_MAX_WINDOW
DSA_GROUP = 4


def _dsa_kernel(q_ref, kp_ref, kc_ref, vp_ref, vc_ref, o_ref, acc_scr, m_scr, l_scr):
    blk = pl.program_id(2)
    qblk = q_ref.shape[0]
    row = lax.broadcasted_iota(jnp.int32, (2 * DSA_BLK, 2 * DSA_BLK), 0) % DSA_BLK
    col = lax.broadcasted_iota(jnp.int32, (2 * DSA_BLK, 2 * DSA_BLK), 1)
    ok = jnp.logical_and(col >= row, col <= row + DSA_BLK)
    ok_first = jnp.logical_and(ok, jnp.logical_or(col >= DSA_BLK, blk > 0))
    lo = lax.broadcasted_iota(jnp.int32, (DSA_BLK, LANES), 1) < DSA_HEAD_DIM
    neg = -jnp.inf
    halves = lambda x: jnp.where(lo, x[:DSA_BLK], x[DSA_BLK:])

    def rows_of(dil, start, n):
        return pl.ds(start, n) if dil == 1 else pl.ds(start, n, stride=dil)

    def group_softmax(dil, subs):
        n = range(len(subs))
        idx = [rows_of(dil, ph + dil * DSA_BLK * j, DSA_BLK) for ph, j in subs]
        q = [q_ref[i, :] for i in idx]
        k, v = [], []
        for u, (ph, j) in enumerate(subs):
            if j == 0:
                band = rows_of(dil, qblk - dil * DSA_BLK + ph, DSA_BLK)
                k.append(jnp.concatenate([kp_ref[band, :], kc_ref[idx[u], :]], axis=0).astype(BF16))
                v.append(jnp.concatenate([vp_ref[band, :], vc_ref[idx[u], :]], axis=0).astype(BF16))
            else:
                both = rows_of(dil, ph + dil * DSA_BLK * (j - 1), 2 * DSA_BLK)
                k.append(kc_ref[both, :].astype(BF16))
                v.append(vc_ref[both, :].astype(BF16))
        q2 = [jnp.concatenate([jnp.where(lo, q[u], 0.0), jnp.where(lo, 0.0, q[u])], axis=0).astype(BF16)
              for u in n]
        s = [jnp.where(ok_first if subs[u][1] == 0 else ok, _dot_nt(q2[u], k[u]), neg) for u in n]
        mx = [jnp.max(s[u], axis=-1, keepdims=True) for u in n]
        p = [jnp.exp2(s[u] - mx[u]) for u in n]
        den = [jnp.sum(p[u], axis=-1, keepdims=True) for u in n]
        pv = [_dot(p[u].astype(BF16), v[u]) for u in n]
        return idx, [(halves(pv[u]), halves(jnp.broadcast_to(mx[u], (2 * DSA_BLK, LANES))),
                      halves(jnp.broadcast_to(den[u], (2 * DSA_BLK, LANES)))) for u in n]

    dils = sorted((d for _, d in DSA_PATTERNS), reverse=True)
    for pi, dil in enumerate(dils):
        subs = [(ph, j) for j in range(qblk // (dil * DSA_BLK)) for ph in range(dil)]
        for g0 in range(0, len(subs), DSA_GROUP):
            idx, tiles = group_softmax(dil, subs[g0:g0 + DSA_GROUP])
            for i, (pv, mx, den) in zip(idx, tiles):
                if pi > 0:
                    m_old = m_scr[i, :]
                    m_new = jnp.maximum(m_old, mx)
                    w_old = jnp.exp2(m_old - m_new)
                    w_cur = jnp.exp2(mx - m_new)
                    pv = acc_scr[i, :] * w_old + pv * w_cur
                    den = l_scr[i, :] * w_old + den * w_cur
                    mx = m_new
                if pi < len(dils) - 1:
                    acc_scr[i, :] = pv
                    m_scr[i, :] = mx
                    l_scr[i, :] = den
                else:
                    o_ref[i, :] = (pv / den).astype(o_ref.dtype)


def _dsa_prompt(q, k, v, nb, t):
    assert all(w // d == DSA_BLK for w, d in DSA_PATTERNS) and t % DSA_QBLK == 0
    ngrp = DSA_W // LANES
    r3 = lambda a: a.reshape(nb, t, DSA_W)
    cur = pl.BlockSpec((None, DSA_QBLK, LANES), lambda b, g, i: (b, i, g))
    prv = pl.BlockSpec((None, DSA_QBLK, LANES), lambda b, g, i: (b, jnp.maximum(i - 1, 0), g))
    return pl.pallas_call(
        _dsa_kernel,
        grid=(nb, ngrp, t // DSA_QBLK),
        in_specs=[cur, prv, cur, prv, cur],
        out_specs=cur,
        out_shape=jax.ShapeDtypeStruct((nb, t, DSA_W), _mixer_out_dtype(t)),
        scratch_shapes=[pltpu.VMEM((DSA_QBLK, LANES), F32)] * 3,
        compiler_params=_cparams("parallel", "parallel", "parallel"),
        name="dsa_prompt",
    )(r3(q), r3(k), r3(k), r3(v), r3(v)).reshape(nb * t, DSA_W)


DSA_STEP_SEQS = 2


def _multiplicity(dist):
    total = jnp.zeros(dist.shape, F32)
    for window, dil in DSA_PATTERNS:
        hit = (dist >= 0) & (dist <= window) & ((dist & (dil - 1)) == 0)
        total = total + hit.astype(F32)
    return total


def _dsa_step_kernel(q_ref, kc_ref, vc_ref, kn_ref, vn_ref, o_ref):
    nseq, tq = q_ref.shape[0], q_ref.shape[1]
    wb = kc_ref.shape[2]
    qi = lax.broadcasted_iota(jnp.int32, (2 * tq, wb), 0) % tq
    w_c = _multiplicity(wb + qi - lax.broadcasted_iota(jnp.int32, (2 * tq, wb), 1))
    qn = lax.broadcasted_iota(jnp.int32, (2 * tq, LANES), 0) % tq
    nn = lax.broadcasted_iota(jnp.int32, (2 * tq, LANES), 1)
    w_n = jnp.where(nn < tq, _multiplicity(qn - nn), 0.0)
    lo = lax.broadcasted_iota(jnp.int32, (tq, LANES), 1) < DSA_HEAD_DIM
    pad = jnp.zeros((LANES - tq, LANES), F32)
    neg = -jnp.inf
    for b, g in [(b, g) for b in range(nseq) for g in range(DSA_W // LANES)]:
        cols = slice(g * LANES, (g + 1) * LANES)
        q = q_ref[b, :, cols]
        q2 = jnp.concatenate([jnp.where(lo, q, 0.0), jnp.where(lo, 0.0, q)], axis=0).astype(BF16)
        kt_c, vt_c = kc_ref[b, cols, :].astype(BF16), vc_ref[b, cols, :].astype(BF16)
        k_n = jnp.concatenate([kn_ref[b, :, cols], pad], axis=0).astype(BF16)
        v_n = jnp.concatenate([vn_ref[b, :, cols], pad], axis=0).astype(BF16)
        s_c = jnp.where(w_c > 0, _dot(q2, kt_c), neg)
        s_n = jnp.where(w_n > 0, _dot_nt(q2, k_n), neg)
        mx = jnp.maximum(jnp.max(s_c, axis=-1, keepdims=True), jnp.max(s_n, axis=-1, keepdims=True))
        p_c = w_c * jnp.exp2(s_c - mx)
        p_n = w_n * jnp.exp2(s_n - mx)
        den = jnp.sum(p_c, axis=-1, keepdims=True) + jnp.sum(p_n, axis=-1, keepdims=True)
        o2 = (_dot_nt(p_c.astype(BF16), vt_c) + _dot(p_n.astype(BF16), v_n)) / den
        o_ref[b, :, cols] = jnp.where(lo, o2[:tq], o2[tq:])


def _dsa_step(q, k_new, v_new, k_cache, v_cache, layer, nb, t):
    wb = k_cache.shape[3]
    bblk = math.gcd(nb, DSA_STEP_SEQS)
    new = pl.BlockSpec((bblk, t, DSA_W), lambda b: (b, 0, 0))
    cache = pl.BlockSpec((None, bblk, DSA_W, wb), lambda b: (layer, b, 0, 0))
    r3 = lambda a: a.reshape(nb, t, DSA_W)
    return pl.pallas_call(
        _dsa_step_kernel,
        grid=(nb // bblk,),
        in_specs=[new, cache, cache, new, new],
        out_specs=new,
        out_shape=jax.ShapeDtypeStruct((nb, t, DSA_W), F32),
        compiler_params=_cparams("parallel"),
        name="dsa_step",
    )(r3(q), k_cache, v_cache, r3(k_new), r3(v_new)).reshape(nb * t, DSA_W)


RET_CHUNK = 256
RET_SEQS_PER_STEP = 2
RET_UNROLL = 4


def _ret_kernel(q_ref, k_ref, v_ref, g_ref, s0_ref, dec_ref, qd_ref, kd_ref, cd_ref, bd_ref, nw_ref,
                o_ref, so_ref, s_scr, *, chunk):
    tb = pl.program_id(1)
    ngrp = RET_W // LANES

    @pl.when(tb == 0)
    def _():
        s_scr[...] = s0_ref[...]

    lo = lax.broadcasted_iota(jnp.int32, (chunk, LANES), 1) < RET_HEAD_DIM
    nw = nw_ref[...]
    bd = bd_ref[...]

    def body(c, carry):
        rows = pl.ds(pl.multiple_of(c * chunk, chunk), chunk)
        for b in range(q_ref.shape[0]):
            for g in range(ngrp):
                cols = slice(g * LANES, (g + 1) * LANES)
                q, k, v = q_ref[b, rows, cols], k_ref[b, rows, cols], v_ref[b, rows, cols]
                kb, vb = k.astype(BF16), v.astype(BF16)
                parts = []
                for hh in range(2):
                    qm = jnp.where(lo if hh == 0 else jnp.logical_not(lo), q, 0.0).astype(BF16)
                    inner = _dot_nt(qm, kb) * dec_ref[2 * g + hh]
                    parts.append(_dot(inner.astype(BF16), vb))
                s_prev = s_scr[b, g]
                o = jnp.where(lo, parts[0], parts[1]) + _dot(q.astype(BF16), s_prev.astype(BF16)) * qd_ref[g]
                s_scr[b, g] = s_prev * cd_ref[g] + bd * _dot_tn((k * kd_ref[g]).astype(BF16), vb)
                o2 = o * o
                ms = jnp.where(lo, jnp.sum(jnp.where(lo, o2, 0.0), axis=-1, keepdims=True),
                               jnp.sum(jnp.where(lo, 0.0, o2), axis=-1, keepdims=True)) * (1.0 / RET_HEAD_DIM)
                o_ref[b, rows, cols] = (o * lax.rsqrt(ms + NORM_EPS) * nw * _silu(g_ref[b, rows, cols])).astype(
                    o_ref.dtype)
        return carry

    nchunk = q_ref.shape[1] // chunk
    lax.fori_loop(0, nchunk, body, 0, unroll=min(RET_UNROLL, nchunk))

    @pl.when(tb == pl.num_programs(1) - 1)
    def _():
        so_ref[...] = s_scr[...]


def _ret_tables(chunk):
    log_gamma = jnp.log(1.0 - 2.0 ** (-5.0 - jnp.arange(RET_HEADS, dtype=F32)))
    i = jnp.arange(chunk, dtype=F32)
    diff = i[:, None] - i[None, :]
    causal = diff >= 0
    decay = jnp.where(causal[None], jnp.exp(log_gamma[:, None, None] * jnp.where(causal, diff, 0.0)[None]), 0.0)
    per_lane = lambda a: jnp.repeat(a, RET_HEAD_DIM, axis=0).reshape(RET_W // LANES, LANES, -1)
    q_dec = per_lane(jnp.exp(log_gamma[:, None] * (i[None, :] + 1.0))).transpose(0, 2, 1)
    k_dec = per_lane(jnp.exp(log_gamma[:, None] * (chunk - 1.0 - i)[None, :])).transpose(0, 2, 1)
    c_dec = jnp.broadcast_to(per_lane(jnp.exp(log_gamma * chunk)[:, None]), (RET_W // LANES, LANES, LANES))
    head_of = jnp.arange(LANES) // RET_HEAD_DIM
    block_diag = (head_of[:, None] == head_of[None, :]).astype(F32)
    return decay, q_dec, k_dec, c_dec, block_diag


def _retention(q, k, v, gate, state_bd, ret_norm, nb, t):
    chunk = RET_CHUNK if t % RET_CHUNK == 0 else t
    tblk = min(t, 1024)
    ngrp = RET_W // LANES
    bblk = _seqs_per_step(nb, t, RET_SEQS_PER_STEP)
    decay, q_dec, k_dec, c_dec, block_diag = _ret_tables(chunk)
    r3 = lambda a: a.reshape(nb, t, RET_W)
    tok = pl.BlockSpec((bblk, tblk, RET_W), lambda b, i: (b, i, 0))
    st = pl.BlockSpec((bblk, ngrp, LANES, LANES), lambda b, i: (b, 0, 0, 0))
    full = lambda a: pl.BlockSpec(a.shape, lambda b, i: (0,) * a.ndim)
    nw = jnp.tile(ret_norm, LANES // RET_HEAD_DIM).reshape(1, LANES)
    o, s_new = pl.pallas_call(
        functools.partial(_ret_kernel, chunk=chunk),
        grid=(nb // bblk, t // tblk),
        in_specs=[tok, tok, tok, tok, st, full(decay), full(q_dec), full(k_dec), full(c_dec),
                  full(block_diag), full(nw)],
        out_specs=[tok, st],
        out_shape=[jax.ShapeDtypeStruct((nb, t, RET_W), _mixer_out_dtype(t)),
                   jax.ShapeDtypeStruct((nb, ngrp, LANES, LANES), F32)],
        scratch_shapes=[pltpu.VMEM((bblk, ngrp, LANES, LANES), F32)],
        compiler_params=_cparams("parallel", "arbitrary"),
        name="retention",
    )(r3(q), r3(k), r3(v), r3(gate), state_bd, decay, q_dec, k_dec, c_dec, block_diag, nw)
    return o.reshape(nb * t, RET_W), s_new


def _to_block_diag(s):
    nb = s.shape[0]
    s = s.reshape(nb, 2, 2, RET_HEAD_DIM, RET_HEAD_DIM)
    z = jnp.zeros_like(s[:, :, 0])
    top = jnp.concatenate([s[:, :, 0], z], axis=-1)
    bot = jnp.concatenate([z, s[:, :, 1]], axis=-1)
    return jnp.concatenate([top, bot], axis=-2)


def _from_block_diag(s):
    h = RET_HEAD_DIM
    return jnp.stack([s[:, :, :h, :h], s[:, :, h:, h:]], axis=2).reshape(s.shape[0], RET_HEADS, h, h)


GDN_PREP_CHUNKS_PER_ITER = 8
GDN_SCAN_SEQS_PER_STEP = 2
GDN_SCAN_UNROLL = 4


def _softplus(x):
    return jnp.maximum(x, 0.0) + jnp.log1p(jnp.exp(-jnp.abs(x)))


def _gdn_prep_kernel(cv_ref, hist_ref, ba_ref, cw_ref, ab_ref,
                     qg_ref, kd_ref, u_ref, w_ref, at_ref, el_ref, beta_scr, g_scr, *conv_scr, chunk, cpi):
    nseq, tblk = cv_ref.shape[0], cv_ref.shape[1]
    ncg = CONV_DIM // LANES
    ab = ab_ref[...]
    cs_scr = cv_ref
    for b in range(nseq):
        if conv_scr:
            xp_scr, cs_scr = conv_scr
            for cg in range(ncg):
                cols = slice(cg * LANES, (cg + 1) * LANES)
                xp_scr[b, cg, :SUBLANES, :] = hist_ref[b, :, cols]
                xp_scr[b, cg, SUBLANES:, :] = cv_ref[b, :, cols]
                cs_scr[b, :, cols] = _causal_conv_silu(xp_scr, b, cw_ref, cg, tblk)

        ba = ba_ref[b]
        beta_scr[b] = jax.nn.sigmoid(ba)
        g_scr[b] = -jnp.exp(ab[0:1, :]) * _softplus(ba + ab[1:2, :])

    grp = 2 if 2 * chunk == LANES else 1
    width = grp * chunk
    ri = lax.broadcasted_iota(jnp.int32, (chunk, width), 0)
    lane = lax.broadcasted_iota(jnp.int32, (chunk, width), 1)
    ci = lane % chunk
    first = lane < chunk
    incl = ri >= ci
    strict = ri > ci
    tri = (lax.broadcasted_iota(jnp.int32, (chunk, chunk), 0)
           >= lax.broadcasted_iota(jnp.int32, (chunk, chunk), 1)).astype(F32)
    lane_pick = (lax.broadcasted_iota(jnp.int32, (SUBLANES, LANES), 0)
                 == lax.broadcasted_iota(jnp.int32, (SUBLANES, LANES), 1)).astype(F32)
    base = min(SUBLANES, chunk)
    assert chunk % base == 0 and (chunk // base) & (chunk // base - 1) == 0
    same_blk = [(ri >> sh) == (ci >> sh) for sh in range(int(math.log2(base)), int(math.log2(chunk)) + 1)]

    def l2n(x):
        return x * lax.rsqrt(jnp.sum(x * x, axis=-1, keepdims=True) + NORM_EPS)

    def side_by_side(per_head):
        if grp == 1:
            return per_head[0][:, :width]
        return jnp.where(first, per_head[0][:, :width], per_head[1][:, :width])

    def block_diag(y):
        if grp == 1:
            return y.astype(BF16)
        return jnp.concatenate([jnp.where(first, y, 0.0), jnp.where(first, 0.0, y)], axis=0).astype(BF16)

    def block_rows(per_head):
        if grp == 1:
            return per_head[0].astype(BF16)
        a, b = per_head
        return jnp.concatenate([jnp.concatenate([a, jnp.zeros_like(b)], axis=1),
                                jnp.concatenate([jnp.zeros_like(a), b], axis=1)], axis=0).astype(BF16)

    def step(it, carry):
        probs = []
        for b, cc in [(b, cc) for b in range(nseq) for cc in range(cpi)]:
            c = it * cpi + cc
            rows = pl.ds(pl.multiple_of(c * chunk, chunk), chunk)
            beta_c = beta_scr[b, rows, :]
            gcol = _dot(tri, g_scr[b, rows, :], precision=HIGHEST)
            grow = _dot_nt(lane_pick, jnp.concatenate([gcol] * grp, axis=0), precision=HIGHEST)
            for h0 in range(0, GDN_HEADS, grp):
                hs = range(h0, h0 + grp)
                q = [l2n(cs_scr[b, rows, h * LANES:(h + 1) * LANES]) for h in hs]
                k = [l2n(cs_scr[b, rows, GDN_W + h * LANES:GDN_W + (h + 1) * LANES]) for h in hs]
                v = [cs_scr[b, rows, 2 * GDN_W + h * LANES:2 * GDN_W + (h + 1) * LANES] for h in hs]
                beta = [jnp.broadcast_to(beta_c[:, h:h + 1], (chunk, LANES)) for h in hs]
                gc = [jnp.broadcast_to(gcol[:, GDN_HEADS + h:GDN_HEADS + h + 1], (chunk, LANES)) for h in hs]
                g_row = side_by_side([jnp.broadcast_to(grow[GDN_HEADS + h:GDN_HEADS + h + 1, :], (chunk, width))
                                      for h in hs])
                gdiff = side_by_side(gc) - g_row
                dmask = jnp.where(incl, jnp.exp(jnp.where(incl, gdiff, 0.0)), 0.0)
                kb = [k[i] * beta[i] for i in range(grp)]
                k_rows = block_rows(k)
                lower = jnp.where(strict, _dot_nt(jnp.concatenate(kb, axis=1).astype(BF16), k_rows) * dmask, 0.0)
                npow = jnp.where(same_blk[0], -lower, 0.0)
                probs.append(dict(b=b, c=c, rows=rows, h0=h0, q=q, k=k, v=v, beta=beta, gc=gc, dmask=dmask,
                                  kb=kb, k_rows=k_rows, lower=lower, npow=npow, qmat=npow))
        for _ in range(int(math.log2(base)) - 1):
            for p in probs:
                p["npow"] = _dot(p["npow"].astype(BF16), block_diag(p["npow"]))
            for p in probs:
                p["qmat"] = p["qmat"] + p["npow"] + _dot(p["qmat"].astype(BF16), block_diag(p["npow"]))
        for lvl in range(1, len(same_blk)):
            for p in probs:
                cb = jnp.where(jnp.logical_and(same_blk[lvl], jnp.logical_not(same_blk[lvl - 1])),
                               p["lower"], 0.0)
                p["x"] = cb + _dot(p["qmat"].astype(BF16), block_diag(cb))
            for p in probs:
                p["qmat"] = p["qmat"] - p["x"] - _dot(p["x"].astype(BF16), block_diag(p["qmat"]))
        for p in probs:
            b, rows, h0, gc = p["b"], p["rows"], p["h0"], p["gc"]
            cols = slice(h0 * LANES, (h0 + grp) * LANES)
            idx = range(grp)
            eg = [jnp.exp(gc[i]) for i in idx]
            rhs_u = [p["v"][i] * p["beta"][i] for i in idx]
            rhs_w = [p["kb"][i] * eg[i] for i in idx]
            qb = p["qmat"].astype(BF16)
            qs = [p["q"][i] * (GDN_HEAD_DIM ** -0.5) for i in idx]
            g_last = [gc[i][chunk - 1:chunk, :] for i in idx]
            cat = lambda xs: jnp.concatenate(xs, axis=1)
            qg_ref[b, rows, cols] = cat([qs[i] * eg[i] for i in idx]).astype(qg_ref.dtype)
            kd_ref[b, rows, cols] = cat([p["k"][i] * jnp.exp(g_last[i] - gc[i]) for i in idx]).astype(kd_ref.dtype)
            u_ref[b, rows, cols] = cat(rhs_u) + _dot(qb, block_rows(rhs_u))
            w_ref[b, rows, cols] = (cat(rhs_w) + _dot(qb, block_rows(rhs_w))).astype(w_ref.dtype)
            at_ref[b, rows, h0 * chunk:(h0 + grp) * chunk] = (
                _dot_nt(cat(qs).astype(BF16), p["k_rows"]) * p["dmask"]).astype(at_ref.dtype)
            el_ref[b, pl.ds(pl.multiple_of(p["c"] * SUBLANES, SUBLANES), SUBLANES), cols] = cat(
                [jnp.broadcast_to(jnp.exp(g_last[i]), (SUBLANES, LANES)) for i in idx])
        return carry

    lax.fori_loop(0, tblk // (chunk * cpi), step, 0)


def _gdn_scan_kernel(qg_ref, kd_ref, u_ref, w_ref, at_ref, el_ref, z_ref, s0_ref, nw_ref,
                     o_ref, so_ref, s_scr, *, chunk):
    tb = pl.program_id(1)

    @pl.when(tb == 0)
    def _():
        s_scr[...] = s0_ref[...]

    nw = nw_ref[...]
    probs = [(b, h) for b in range(qg_ref.shape[0]) for h in range(GDN_HEADS)]

    def step(c, carry):
        rows = pl.ds(pl.multiple_of(c * chunk, chunk), chunk)
        erow = pl.ds(pl.multiple_of(c * SUBLANES, SUBLANES), SUBLANES)
        cols = lambda h: slice(h * LANES, (h + 1) * LANES)
        s_prev = [s_scr[b, h] for b, h in probs]
        sb = [s.astype(BF16) for s in s_prev]
        v_new = [u_ref[b, rows, cols(h)] - _dot(w_ref[b, rows, cols(h)].astype(BF16), sb[i])
                 for i, (b, h) in enumerate(probs)]
        vb = [v.astype(BF16) for v in v_new]
        for i, (b, h) in enumerate(probs):
            el = el_ref[b, erow, cols(h)][0:1, :]
            s_scr[b, h] = s_prev[i] * el + _dot_tn(kd_ref[b, rows, cols(h)].astype(BF16), vb[i])
        for i, (b, h) in enumerate(probs):
            attn = at_ref[b, rows, h * chunk:(h + 1) * chunk].astype(BF16)
            o = _dot(qg_ref[b, rows, cols(h)].astype(BF16), sb[i]) + _dot(attn, vb[i])
            ms = jnp.mean(o * o, axis=-1, keepdims=True)
            o_ref[b, rows, cols(h)] = (o * lax.rsqrt(ms + NORM_EPS) * nw * _silu(z_ref[b, rows, cols(h)])).astype(
                o_ref.dtype)
        return carry

    nchunk = qg_ref.shape[1] // chunk
    lax.fori_loop(0, nchunk, step, 0, unroll=min(GDN_SCAN_UNROLL, nchunk))

    @pl.when(tb == pl.num_programs(1) - 1)
    def _():
        so_ref[...] = s_scr[...]


def _gated_delta(cv, conv_done, z, ba, conv_hist, states, layer, conv_w, a_log, dt_bias, gdn_norm, nb, t):
    chunk = CHUNK if t % CHUNK == 0 else t
    tblk = min(t, 512)
    assert conv_done or tblk == t
    nchunk_blk = tblk // chunk
    hist_pad, cw_pad = _conv_operands(conv_hist, conv_w)
    ab = jnp.zeros((SUBLANES, LANES), F32)
    ab = ab.at[0, GDN_HEADS:2 * GDN_HEADS].set(a_log).at[1, GDN_HEADS:2 * GDN_HEADS].set(dt_bias)
    cv3 = cv.reshape(nb, t, CONV_DIM)
    pblk = _seqs_per_step(nb, t, 2)
    tok = lambda w: pl.BlockSpec((pblk, tblk, w), lambda b, i: (b, i, 0))
    full = lambda a: pl.BlockSpec(a.shape, lambda b, i: (0,) * a.ndim)
    el_spec =pl.BlockSpec((pblk, nchunk_blk * SUBLANES, GDN_W), lambda b, i: (b, i, 0))
    tok_shape = lambda w, dt=F32: jax.ShapeDtypeStruct((nb, t, w), dt)
    el_shape = jax.ShapeDtypeStruct((nb, (t // chunk) * SUBLANES, GDN_W), F32)
    opd = BF16 if chunk % (2 * SUBLANES) == 0 else F32
    qg, kd, u, w, attn, el = pl.pallas_call(
        functools.partial(_gdn_prep_kernel, chunk=chunk, cpi=min(GDN_PREP_CHUNKS_PER_ITER, nchunk_blk)),
        grid=(nb // pblk, t // tblk),
        in_specs=[tok(CONV_DIM), pl.BlockSpec((pblk, SUBLANES, CONV_DIM), lambda b, i: (b, 0, 0)),
                  tok(LANES), full(cw_pad), full(ab)],
        out_specs=[tok(GDN_W), tok(GDN_W), tok(GDN_W), tok(GDN_W), tok(GDN_HEADS * chunk), el_spec],
        out_shape=[tok_shape(GDN_W, opd), tok_shape(GDN_W, opd), tok_shape(GDN_W), tok_shape(GDN_W, opd),
                   tok_shape(GDN_HEADS * chunk, opd), el_shape],
        scratch_shapes=[pltpu.VMEM((pblk, tblk, LANES), F32), pltpu.VMEM((pblk, tblk, LANES), F32)] + (
            [] if conv_done else [pltpu.VMEM((pblk, CONV_DIM // LANES, tblk + SUBLANES, LANES), F32),
                                  pltpu.VMEM((pblk, tblk, CONV_DIM), F32)]),
        compiler_params=_cparams("parallel", "parallel"),
        name="gdn_prep",
    )(cv3, hist_pad, ba.reshape(nb, t, LANES), cw_pad, ab)

    bblk = _seqs_per_step(nb, t, GDN_SCAN_SEQS_PER_STEP)
    stok = lambda w: pl.BlockSpec((bblk, tblk, w), lambda b, i: (b, i, 0))
    sel_spec = pl.BlockSpec((bblk, nchunk_blk * SUBLANES, GDN_W), lambda b, i: (b, i, 0))
    st_in = pl.BlockSpec((None, bblk, GDN_HEADS, LANES, LANES), lambda b, i: (layer, b, 0, 0, 0))
    st = pl.BlockSpec((bblk, GDN_HEADS, LANES, LANES), lambda b, i: (b, 0, 0, 0))
    nw = gdn_norm.reshape(1, LANES)
    o, s_new = pl.pallas_call(
        functools.partial(_gdn_scan_kernel, chunk=chunk),
        grid=(nb // bblk, t // tblk),
        in_specs=[stok(GDN_W), stok(GDN_W), stok(GDN_W), stok(GDN_W), stok(GDN_HEADS * chunk), sel_spec,
                  stok(GDN_W), st_in, full(nw)],
        out_specs=[stok(GDN_W), st],
        out_shape=[tok_shape(GDN_W, _mixer_out_dtype(t)), jax.ShapeDtypeStruct((nb, GDN_HEADS, LANES, LANES), F32)],
        scratch_shapes=[pltpu.VMEM((bblk, GDN_HEADS, LANES, LANES), F32)],
        compiler_params=_cparams("parallel", "arbitrary"),
        name="gdn_scan",
    )(qg, kd, u, w, attn, el, z.reshape(nb, t, GDN_W), states, nw)
    return o.reshape(nb * t, GDN_W), s_new


FFN_TILE = 256


def _out_ffn_kernel(x_ref, oa_ref, orr_ref, oc_ref, g1_ref, sh2_ref, sc2_ref, g2_ref, nw_ref, wo_ref, wg_ref,
                    wu_ref, wd_ref, fn_ref, out_ref, *, final):
    mix = (_dot(oa_ref[...].astype(BF16), wo_ref[0:DSA_W, :].astype(BF16))
           + _dot(orr_ref[...].astype(BF16), wo_ref[DSA_W:DSA_W + RET_W, :].astype(BF16))
           + _dot(oc_ref[...].astype(BF16), wo_ref[DSA_W + RET_W:, :].astype(BF16)))
    x1 = x_ref[...] + g1_ref[...] * mix
    hb = _rms_mod(x1, nw_ref[...], sc2_ref[...], sh2_ref[...]).astype(BF16)
    acc = None
    for j in range(wg_ref.shape[1] // FFN_TILE):
        cols = slice(j * FFN_TILE, (j + 1) * FFN_TILE)
        act = (_silu(_dot(hb, wg_ref[:, cols])) * _dot(hb, wu_ref[:, cols])).astype(BF16)
        down = _dot(act, wd_ref[cols, :])
        acc = down if acc is None else acc + down
    x2 = x1 + g2_ref[...] * acc
    if final:
        ms = jnp.mean(x2 * x2, axis=-1, keepdims=True)
        x2 = x2 * lax.rsqrt(ms + NORM_EPS) * fn_ref[...]
    out_ref[...] = x2


def _out_ffn(x2, t, oa, orr, oc, mod3, norm_w, w_out_b, wg_b, wu_b, wd_b, layer, final_norm, final, tm):
    m, d = x2.shape
    row = lambda w: pl.BlockSpec((tm, w), lambda i: (i, 0))
    const = lambda a: pl.BlockSpec(a.shape, lambda i: (0,) * a.ndim, pipeline_mode=pl.Buffered(1))
    wspec = lambda a: pl.BlockSpec((None,) + a.shape[1:], lambda i: (layer,) + (0,) * (a.ndim - 1),
                                   pipeline_mode=pl.Buffered(1))
    nw = norm_w.reshape(1, d)
    fn = final_norm.reshape(1, d)
    return pl.pallas_call(
        functools.partial(_out_ffn_kernel, final=final),
        grid=(m // tm,),
        in_specs=[row(d), row(DSA_W), row(RET_W), row(GDN_W),
                  _mod_spec(tm, t, d, 2), _mod_spec(tm, t, d, 3), _mod_spec(tm, t, d, 4), _mod_spec(tm, t, d, 5),
                  const(nw), wspec(w_out_b), wspec(wg_b), wspec(wu_b), wspec(wd_b), const(fn)],
        out_specs=row(d),
        out_shape=jax.ShapeDtypeStruct((m, d), F32),
        compiler_params=_cparams("parallel"),
        name="out_ffn",
    )(x2, oa, orr, oc, mod3, mod3, mod3, mod3, nw, w_out_b, wg_b, wu_b, wd_b, fn)


ROPE_SPLIT = 64


def _rope_tables(pos0, t, inv_freq):
    reps = LANES // (2 * inv_freq.shape[0])
    inv_lane = jnp.tile(jnp.concatenate([inv_freq, inv_freq]), reps)[None, :]
    sign_lane = jnp.tile(jnp.concatenate([-jnp.ones_like(inv_freq), jnp.ones_like(inv_freq)]), reps)[None, :]
    if t % ROPE_SPLIT:
        ang = (pos0 + jnp.arange(t, dtype=jnp.int32)).astype(F32)[:, None] * inv_lane
        return jnp.cos(ang), jnp.sin(ang) * sign_lane
    coarse = (pos0 + ROPE_SPLIT * jnp.arange(t // ROPE_SPLIT, dtype=jnp.int32)).astype(F32)[:, None] * inv_lane
    fine = jnp.arange(ROPE_SPLIT, dtype=jnp.int32).astype(F32)[:, None] * inv_lane
    cc, sc = jnp.cos(coarse)[:, None, :], jnp.sin(coarse)[:, None, :]
    cf, sf = jnp.cos(fine)[None], jnp.sin(fine)[None]
    cos = (cc * cf - sc * sf).reshape(t, LANES)
    sin = (sc * cf + cc * sf).reshape(t, LANES)
    return cos, sin * sign_lane


def _trunk(x, modp, pos0, k_hist, v_hist, s_ret, s_gdn, conv_hist, wts):
    (norm_mix, norm_ffn, w_in_b, w_ba_b, ret_norm, conv_w, a_log, dt_bias, gdn_norm, w_out_b, wg_b, wu_b, wd_b,
     final_norm) = wts
    nb, t, d = x.shape
    m = nb * t
    depth = w_in_b.shape[0]
    tm = min(TOKEN_TILE, m)
    inv_a = 1.0 / (ROPE_THETA ** (jnp.arange(0, DSA_HEAD_DIM, 2, dtype=F32) / DSA_HEAD_DIM))
    inv_r = 1.0 / (10000.0 ** jnp.linspace(0.0, 1.0, RET_HEAD_DIM // 2, dtype=F32))
    tabs = _rope_tables(pos0, t, inv_a) + _rope_tables(pos0, t, inv_r)
    if t < tm:
        tabs = tuple(jnp.tile(a, (tm // t, 1)) for a in tabs)
    x2 = x.reshape(m, d)
    ks, vs, rs, gs, cs = [], [], [], [], []
    for l in range(depth):
        if t >= tm:
            mod3 = modp[l].reshape(nb, 1, 6 * d)
        else:
            mod3 = jnp.repeat(modp[l], t, axis=0).reshape(m // tm, tm, 6 * d)
        (qa, ka, va, qr, kr, vr, gr, cv, z, ba), fused = _inproj(
            x2, t, mod3, norm_mix[l], w_in_b, w_ba_b, l, tabs, tm, conv_hist[l], conv_w[l])
        nhist = CONV_WIDTH - 1
        keep = min(t, DSA_MAX_WINDOW)
        if fused is None:
            cvn = jnp.concatenate([conv_hist[l], cv.reshape(nb, t, CONV_DIM)], axis=1)[:, -nhist:]
            window = lambda a: a.reshape(nb, t, DSA_W)[:, t - keep:].reshape(nb, keep, DSA_HEADS, DSA_HEAD_DIM)
            kwin, vwin = window(ka), window(va)
        else:
            conv_tail, k_t, v_t = fused
            cvn = conv_tail[:, SUBLANES - nhist:]
            untransposed = lambda a: a.reshape(nb, DSA_HEADS, DSA_HEAD_DIM, keep).transpose(0, 3, 1, 2)
            kwin, vwin = untransposed(k_t), untransposed(v_t)
        if k_hist is None:
            oa = _dsa_prompt(qa, ka, va, nb, t)
        else:
            oa = _dsa_step(qa, ka, va, k_hist, v_hist, l, nb, t)
        orr, sr = _retention(qr, kr, vr, gr, _to_block_diag(s_ret[l]), ret_norm[l], nb, t)
        oc, sg = _gated_delta(cv, fused is not None, z, ba, conv_hist[l], s_gdn, l, conv_w[l], a_log[l],
                              dt_bias[l], gdn_norm[l], nb, t)
        x2 = _out_ffn(x2, t, oa, orr, oc, mod3, norm_ffn[l], w_out_b, wg_b, wu_b, wd_b, l,
                      final_norm, l == depth - 1, tm)
        ks.append(kwin)
        vs.append(vwin)
        rs.append(_from_block_diag(sr))
        gs.append(sg)
        cs.append(cvn)
    return (x2.reshape(nb, t, d), jnp.stack(ks), jnp.stack(vs), jnp.stack(rs), jnp.stack(gs), jnp.stack(cs))


def kernel(x_prompt, x_sample, cache_win_k, cache_win_v, state_ret, state_gdn, state_conv, c_prompt, c_sample, ada_w, ada_b, norm_mix, norm_ffn, w_in, ret_norm, conv_w, a_log, dt_bias, gdn_norm, w_out, w_gate, w_up, w_down, final_norm):
    nb, t_p, d = x_prompt.shape
    db, t_s, _ = x_sample.shape
    depth = ada_w.shape[0]
    rows = nb + db
    rows_pad = -(-rows // SUBLANES) * SUBLANES
    c_all = jnp.concatenate([c_prompt, c_sample, jnp.zeros((rows_pad - rows, d), F32)], axis=0)
    mod = _modulation(c_all, ada_w, ada_b)
    w_ba_b = jnp.pad(w_in[:, :, _C_BA:], ((0, 0), (0, 0), (0, LANES - (IN_COLS - _C_BA)))).astype(BF16)
    wts = (norm_mix, norm_ffn, w_in.astype(BF16), w_ba_b, ret_norm, conv_w, a_log, dt_bias, gdn_norm, w_out,
           w_gate.astype(BF16), w_up.astype(BF16), w_down.astype(BF16), final_norm)

    zr = jnp.zeros((depth, nb, RET_HEADS, RET_HEAD_DIM, RET_HEAD_DIM), F32)
    zg = jnp.zeros((depth, nb, GDN_HEADS, GDN_HEAD_DIM, GDN_HEAD_DIM), F32)
    zc = jnp.zeros((depth, nb, CONV_WIDTH - 1, CONV_DIM), F32)
    y_p, kp, vp, rp, gp, cp = _trunk(x_prompt, mod[:, :nb], 0,
                                     None, None, zr, zg, zc, wts)
    wb = cache_win_k.shape[2]
    y_s, ks, vs, rs, gs, cs = _trunk(x_sample, mod[:, nb:rows], PAST_LEN,
                                     cache_win_k.reshape(depth, db, wb, DSA_W).transpose(0, 1, 3, 2),
                                     cache_win_v.reshape(depth, db, wb, DSA_W).transpose(0, 1, 3, 2),
                                     state_ret, state_gdn, state_conv, wts)
    return (y_p, y_s, kp, vp, rp, gp, cp, ks, vs, rs, gs, cs)
```
